```python
import math
import jax
import jax.numpy as jnp
from jax import lax
import numpy as np

D_MODEL = 1024
BATCH = 16
SEQ = 256
DEPTH = 4
DEC_BATCH = 4
DEC_SEQ = 2048
PAST_LEN = 256

GRID_W = 64
N_MIXERS = 2
N_SSD_LAYERS = (DEPTH + 1) // 2
N_ATT_LAYERS = DEPTH // 2
EPS = 1e-6

SSD_INNER = 2 * D_MODEL
SSD_HEAD_DIM = 64
SSD_HEADS = SSD_INNER // SSD_HEAD_DIM
SSD_GROUPS = 4
SSD_STATE = 128
SSD_CONV = 3
SSD_CHUNK = 128
SSD_BC = SSD_GROUPS * SSD_STATE
SSD_CONV_CH = SSD_INNER + 2 * SSD_BC
SSD_IN_DIM = SSD_INNER + SSD_CONV_CH + 2 * SSD_HEADS

DA_HEAD_DIM = 64
DA_HEADS = D_MODEL // (2 * DA_HEAD_DIM)
DA_SCALE = DA_HEAD_DIM ** -0.5
Q_BLOCK = 128
ROPE_BASE = 10000.0
ROPE_AXIS_DIM = DA_HEAD_DIM // 2

FFN_HIDDEN = 2816
FFN_CONV = 3

kernel_name = 'hybrid_ssd_diffattn_dit_step'

F32 = jnp.float32


def _rms(x):
    xf = x.astype(F32)
    return xf * lax.rsqrt(jnp.mean(xf * xf, axis=-1, keepdims=True) + EPS)


def rmsnorm(x, g):
    return (_rms(x) * g.astype(F32)).astype(x.dtype)


def modulation(cvec, w, b):
    m = jax.nn.silu(cvec) @ w + b
    return jnp.split(m[:, None, :], 6, axis=-1)


def modulate(x, g, shift, scale):
    return rmsnorm(x, g) * (1 + scale) + shift


def dwconv(x, w, b):
    k, ch = w.shape
    pad = k // 2
    y = lax.conv_general_dilated(x, w[:, None, :], window_strides=(1,), padding=[(pad, pad)],
                                 dimension_numbers=('NWC', 'WIO', 'NWC'), feature_group_count=ch)
    return y + b


def ssd_scan(x, dt, a, bm, cm, h0):
    b, l, nh, p = x.shape
    g, n = bm.shape[2], bm.shape[3]
    e = nh // g
    nc = l // SSD_CHUNK
    xd = (x.astype(F32) * dt[..., None]).reshape(b, nc, SSD_CHUNK, g, e, p)
    a_cs = jnp.cumsum((dt * a).reshape(b, nc, SSD_CHUNK, g, e), axis=2)
    bm = bm.astype(F32).reshape(b, nc, SSD_CHUNK, g, n)
    cm = cm.astype(F32).reshape(b, nc, SSD_CHUNK, g, n)
    tri = jnp.tril(jnp.ones((SSD_CHUNK, SSD_CHUNK), bool))[:, :, None, None]
    seg = a_cs[:, :, :, None] - a_cs[:, :, None, :]
    decay = jnp.exp(jnp.where(tri, seg, -jnp.inf))
    cb = jnp.einsum('bcign,bcjgn->bcijg', cm, bm)
    y_diag = jnp.einsum('bcijge,bcjgep->bcigep', cb[..., None] * decay, xd)
    xw = xd * jnp.exp(a_cs[:, :, -1:] - a_cs)[..., None]
    states = jnp.einsum('bcqgn,bcqgep->bcgepn', bm, xw)
    chunk_decay = jnp.exp(a_cs[:, :, -1])

    def step(hc, inp):
        dec, st = inp
        return dec[..., None, None] * hc + st, hc

    h_last, h_in = lax.scan(step, h0.astype(F32).reshape(b, g, e, p, n),
                            (jnp.moveaxis(chunk_decay, 1, 0), jnp.moveaxis(states, 1, 0)))
    h_in = jnp.moveaxis(h_in, 0, 1)
    y_off = jnp.einsum('bcqgn,bcgepn->bcqgep', cm, h_in) * jnp.exp(a_cs)[..., None]
    y = (y_diag + y_off).reshape(b, l, nh, p)
    return y.astype(x.dtype), h_last.reshape(b, nh, p, n).astype(x.dtype)


def ssd_mixer(h, h0, in_w, conv_w, conv_b, dt_bias, a_log, d_skip, norm_g, out_w):
    b, l, _ = h.shape
    proj = h @ in_w
    z = proj[..., :SSD_INNER]
    xbc = jax.nn.silu(dwconv(proj[..., SSD_INNER:SSD_INNER + SSD_CONV_CH], conv_w, conv_b))
    dt_raw = proj[..., SSD_INNER + SSD_CONV_CH:].reshape(b, l, 2, SSD_HEADS)
    xs = xbc[..., :SSD_INNER].reshape(b, l, SSD_HEADS, SSD_HEAD_DIM)
    bm = xbc[..., SSD_INNER:SSD_INNER + SSD_BC].reshape(b, l, SSD_GROUPS, SSD_STATE)
    cm = xbc[..., SSD_INNER + SSD_BC:].reshape(b, l, SSD_GROUPS, SSD_STATE)
    dt = jax.nn.softplus(dt_raw.astype(F32) + dt_bias.astype(F32))
    a = -jnp.exp(a_log.astype(F32))
    y_f, h_f = ssd_scan(xs, dt[:, :, 0], a[0], bm, cm, h0[:, 0])
    rev = lambda t: t[:, ::-1]
    y_b, h_b = ssd_scan(rev(xs), rev(dt[:, :, 1]), a[1], rev(bm), rev(cm), h0[:, 1])
    y = y_f + rev(y_b) + d_skip[:, None] * xs
    y = y.reshape(b, l, SSD_INNER) * jax.nn.silu(z)
    y = _rms(y.reshape(b, l, SSD_GROUPS, SSD_INNER // SSD_GROUPS)).reshape(b, l, SSD_INNER)
    y = (y * norm_g.astype(F32)).astype(h.dtype)
    return y @ out_w, jnp.stack([h_f, h_b], axis=1)


def diff_lambda(lam_vecs, lam_init):
    lv = lam_vecs.astype(F32)
    return jnp.exp(jnp.sum(lv[0] * lv[1])) - jnp.exp(jnp.sum(lv[2] * lv[3])) + lam_init


def diff_qkv(h, w):
    b, l, _ = h.shape
    q, k, v = jnp.split(h @ w, 3, axis=-1)
    return (q.reshape(b, l, DA_HEADS, 2, DA_HEAD_DIM),
            k.reshape(b, l, DA_HEADS, 2, DA_HEAD_DIM),
            v.reshape(b, l, DA_HEADS, 2 * DA_HEAD_DIM))


def diff_attend(q, k, v, lam):
    b, lq, nh, _, dh = q.shape
    nb = lq // Q_BLOCK
    qb = jnp.moveaxis(q.reshape(b, nb, Q_BLOCK, nh, 2, dh), 1, 0)

    def one_block(qi):
        s = jnp.einsum('bqhcd,bkhcd->bhcqk', qi, k, preferred_element_type=F32) * DA_SCALE
        p = jax.nn.softmax(s, axis=-1)
        w = p[:, :, 0] - lam * p[:, :, 1]
        return jnp.einsum('bhqk,bkhd->bqhd', w.astype(v.dtype), v)

    o = lax.map(one_block, qb)
    return jnp.moveaxis(o, 0, 1).reshape(b, lq, nh, 2 * dh)


def diff_out(o, subln_g, out_w, lam_init):
    b, l = o.shape[:2]
    o = (_rms(o) * subln_g.astype(F32) * (1.0 - lam_init)).astype(o.dtype)
    return o.reshape(b, l, D_MODEL) @ out_w


def axial_rope_tables(rows):
    row = jnp.repeat(jnp.arange(rows, dtype=F32), GRID_W)
    col = jnp.tile(jnp.arange(GRID_W, dtype=F32), rows)
    inv = 1.0 / (ROPE_BASE ** (jnp.arange(0, ROPE_AXIS_DIM, 2, dtype=F32) / ROPE_AXIS_DIM))
    ang_r = row[:, None] * inv
    ang_c = col[:, None] * inv
    shp = lambda t: t[None, :, None, None, :]
    return (shp(jnp.cos(ang_r)), shp(jnp.sin(ang_r)), shp(jnp.cos(ang_c)), shp(jnp.sin(ang_c)))


def _rotate(x, cos, sin):
    x1, x2 = jnp.split(x, 2, axis=-1)
    return jnp.concatenate([x1 * cos - x2 * sin, x1 * sin + x2 * cos], axis=-1)


def rope_2d(x, tabs):
    cr, sr, cc, sc = (t.astype(x.dtype) for t in tabs)
    xr, xc = jnp.split(x, 2, axis=-1)
    return jnp.concatenate([_rotate(xr, cr, sr), _rotate(xc, cc, sc)], axis=-1)


def conv_ffn(h, up_w, conv_w, conv_b, down_w):
    u = dwconv(h @ up_w, conv_w, conv_b)
    a, v = jnp.split(u, 2, axis=-1)
    return (jax.nn.silu(a) * v) @ down_w


def lambda_init_fn(layer):
    return 0.8 - 0.6 * math.exp(-0.3 * layer)


def setup_inputs(seed: int = 0) -> dict:
    key = jax.random.key(seed)
    ks = iter(jax.random.split(key, 64))
    nrm = lambda shape, scale: jax.random.normal(next(ks), shape, F32) * scale
    gain = lambda shape: 1.0 + 0.02 * jax.random.normal(next(ks), shape, F32)
    u_dt = jax.random.uniform(next(ks), (N_SSD_LAYERS, 2, SSD_HEADS), F32)
    dt0 = jnp.exp(u_dt * (math.log(0.1) - math.log(0.001)) + math.log(0.001))
    dt_bias = dt0 + jnp.log(-jnp.expm1(-dt0))
    a_log = jnp.log(jax.random.uniform(next(ks), (N_SSD_LAYERS, 2, SSD_HEADS), F32, 1.0, 16.0))
    return {
        'x_prompt': nrm((BATCH, SEQ, D_MODEL), 1.0),
        'x_sample': nrm((DEC_BATCH, DEC_SEQ, D_MODEL), 1.0),
        'state_ssd': nrm((DEC_BATCH, N_SSD_LAYERS, 2, SSD_HEADS, SSD_HEAD_DIM, SSD_STATE), 1.0),
        'cache_k': nrm((DEC_BATCH, N_ATT_LAYERS, PAST_LEN, DA_HEADS, 2, DA_HEAD_DIM), 1.0),
        'cache_v': nrm((DEC_BATCH, N_ATT_LAYERS, PAST_LEN, DA_HEADS, 2 * DA_HEAD_DIM), 1.0),
        'c': nrm((DEC_BATCH, D_MODEL), 1.0),
        'c_ctx': nrm((D_MODEL,), 1.0),
        'mod_w': nrm((DEPTH, D_MODEL, 6 * D_MODEL), 0.5 * D_MODEL ** -0.5),
        'mod_b': nrm((DEPTH, 6 * D_MODEL), 0.01),
        'norm_mix_g': gain((DEPTH, D_MODEL)),
        'norm_ffn_g': gain((DEPTH, D_MODEL)),
        'ssd_in_w': nrm((N_SSD_LAYERS, D_MODEL, SSD_IN_DIM), D_MODEL ** -0.5),
        'ssd_conv_w': nrm((N_SSD_LAYERS, SSD_CONV, SSD_CONV_CH), SSD_CONV ** -0.5),
        'ssd_conv_b': nrm((N_SSD_LAYERS, SSD_CONV_CH), 0.01),
        'ssd_dt_bias': dt_bias,
        'ssd_a_log': a_log,
        'ssd_d': gain((N_SSD_LAYERS, SSD_HEADS)),
        'ssd_norm_g': gain((N_SSD_LAYERS, SSD_INNER)),
        'ssd_out_w': nrm((N_SSD_LAYERS, SSD_INNER, D_MODEL), SSD_INNER ** -0.5),
        'att_qkv_w': nrm((N_ATT_LAYERS, D_MODEL, 3 * D_MODEL), D_MODEL ** -0.5),
        'att_lambda': nrm((N_ATT_LAYERS, 4, DA_HEAD_DIM), 0.1),
        'att_subln_g': gain((N_ATT_LAYERS, 2 * DA_HEAD_DIM)),
        'att_out_w': nrm((N_ATT_LAYERS, D_MODEL, D_MODEL), D_MODEL ** -0.5),
        'ffn_up_w': nrm((DEPTH, D_MODEL, 2 * FFN_HIDDEN), D_MODEL ** -0.5),
        'ffn_conv_w': nrm((DEPTH, FFN_CONV, 2 * FFN_HIDDEN), FFN_CONV ** -0.5),
        'ffn_conv_b': nrm((DEPTH, 2 * FFN_HIDDEN), 0.01),
        'ffn_down_w': nrm((DEPTH, FFN_HIDDEN, D_MODEL), FFN_HIDDEN ** -0.5),
        'final_norm_g': gain((D_MODEL,)),
    }


def reference(x_prompt, x_sample, state_ssd, cache_k, cache_v, c, c_ctx,
              mod_w, mod_b, norm_mix_g, norm_ffn_g,
              ssd_in_w, ssd_conv_w, ssd_conv_b, ssd_dt_bias, ssd_a_log, ssd_d, ssd_norm_g, ssd_out_w,
              att_qkv_w, att_lambda, att_subln_g, att_out_w,
              ffn_up_w, ffn_conv_w, ffn_conv_b, ffn_down_w, final_norm_g):
    rows = x_sample.shape[1] // GRID_W
    tabs = axial_rope_tables(rows)
    xp, xs = x_prompt, x_sample
    new_ssd, new_k, new_v = [], [], []
    for i in range(DEPTH):
        slot = i // N_MIXERS
        p_sh1, p_sc1, p_g1, p_sh2, p_sc2, p_g2 = modulation(c_ctx[None], mod_w[i], mod_b[i])
        s_sh1, s_sc1, s_g1, s_sh2, s_sc2, s_g2 = modulation(c, mod_w[i], mod_b[i])
        hp = modulate(xp, norm_mix_g[i], p_sh1, p_sc1)
        hs = modulate(xs, norm_mix_g[i], s_sh1, s_sc1)
        if i % N_MIXERS == 0:
            ssd_p = (ssd_in_w[slot], ssd_conv_w[slot], ssd_conv_b[slot], ssd_dt_bias[slot],
                     ssd_a_log[slot], ssd_d[slot], ssd_norm_g[slot], ssd_out_w[slot])
            h_zero = jnp.zeros((xp.shape[0], 2, SSD_HEADS, SSD_HEAD_DIM, SSD_STATE), xp.dtype)
            op, st = ssd_mixer(hp, h_zero, *ssd_p)
            os_, _ = ssd_mixer(hs, state_ssd[:, slot], *ssd_p)
            new_ssd.append(st)
        else:
            lam_init = lambda_init_fn(i)
            lam = diff_lambda(att_lambda[slot], lam_init)
            qp, kp, vp = diff_qkv(hp, att_qkv_w[slot])
            op = diff_out(diff_attend(qp, kp, vp, lam), att_subln_g[slot], att_out_w[slot], lam_init)
            qs, ks_, vs = diff_qkv(hs, att_qkv_w[slot])
            qs = rope_2d(qs, tabs)
            ks_ = rope_2d(ks_, tabs)
            k_all = jnp.concatenate([cache_k[:, slot], ks_], axis=1)
            v_all = jnp.concatenate([cache_v[:, slot], vs], axis=1)
            os_ = diff_out(diff_attend(qs, k_all, v_all, lam), att_subln_g[slot], att_out_w[slot], lam_init)
            new_k.append(kp)
            new_v.append(vp)
        xp = xp + p_g1 * op
        xs = xs + s_g1 * os_
        ffn_p = (ffn_up_w[i], ffn_conv_w[i], ffn_conv_b[i], ffn_down_w[i])
        xp = xp + p_g2 * conv_ffn(modulate(xp, norm_ffn_g[i], p_sh2, p_sc2), *ffn_p)
        xs = xs + s_g2 * conv_ffn(modulate(xs, norm_ffn_g[i], s_sh2, s_sc2), *ffn_p)
    y_prompt = rmsnorm(xp, final_norm_g)
    y_sample = rmsnorm(xs, final_norm_g)
    new_state_ssd = jnp.stack(new_ssd, axis=1)
    new_cache_k = jnp.stack(new_k, axis=1)
    new_cache_v = jnp.stack(new_v, axis=1)
    return (y_prompt, y_sample, new_state_ssd, new_cache_k, new_cache_v)
```

```python
import functools
import math

import jax
import jax.numpy as jnp
from jax import lax
from jax.experimental import pallas as pl
from jax.experimental.pallas import tpu as pltpu

F32 = jnp.float32
BF16 = jnp.bfloat16

D_MODEL = 1024
BATCH = 16
SEQ = 256
DEPTH = 4
DEC_BATCH = 4
DEC_SEQ = 2048
PAST_LEN = 256
GRID_W = 64
EPS = 1e-6

SSD_INNER = 2048
SSD_HEAD_DIM = 64
SSD_HEADS = 32
SSD_GROUPS = 4
SSD_STATE = 128
SSD_CHUNK = 128
SSD_BC = SSD_GROUPS * SSD_STATE
SSD_CONV_CH = SSD_INNER + 2 * SSD_BC
SSD_MAIN = SSD_INNER + SSD_CONV_CH
HEADS_PER_GROUP = SSD_HEADS // SSD_GROUPS
GROUP_W = HEADS_PER_GROUP * SSD_HEAD_DIM

DA_HEAD_DIM = 64
DA_HEADS = 8
DA_SCALE = DA_HEAD_DIM ** -0.5
HEAD_W = 2 * DA_HEAD_DIM
ROPE_BASE = 10000.0
ROPE_AXIS_DIM = DA_HEAD_DIM // 2

FFN_HIDDEN = 2816

N_CTX = BATCH * SEQ
N_LAT = DEC_BATCH * DEC_SEQ
N_TOK = N_CTX + N_LAT
MOD_ROWS = 8
LANES = 128
SUBLANES = 8
VMEM_LIMIT = 48 * 1024 * 1024


def _params(*sem):
    return pltpu.CompilerParams(dimension_semantics=sem, vmem_limit_bytes=VMEM_LIMIT)


def _mod_row(i, tm):
    nctx = N_CTX // tm
    return jnp.where(i < nctx, 0, 1 + (i - nctx) // (DEC_SEQ // tm))


def _silu(x):
    return x * jax.nn.sigmoid(x)


def _softplus(x):
    return jnp.maximum(x, 0.0) + jnp.log1p(jnp.exp(-jnp.abs(x)))


def _split3(x):
    hi = x.astype(BF16)
    r1 = x - hi.astype(F32)
    mid = r1.astype(BF16)
    lo = (r1 - mid.astype(F32)).astype(BF16)
    return hi, mid, lo


def _dot(a, b):
    return jnp.dot(a, b, preferred_element_type=F32)


def _dot_nt(a, b):
    return lax.dot_general(a, b, (((1,), (1,)), ((), ())), preferred_element_type=F32)


def _dot3(x, sel):
    hi, mid, lo = _split3(x)
    return _dot(hi, sel) + _dot(mid, sel) + _dot(lo, sel)


def _mod_kernel(c_ref, w_ref, b_ref, o_ref):
    s = _silu(c_ref[...]).astype(BF16)
    o_ref[...] = _dot(s, w_ref[...].astype(BF16)) + b_ref[...]


def _modulation(cpad, mod_w, mod_b):
    tn = 1024
    n = 6 * D_MODEL
    return pl.pallas_call(
        _mod_kernel,
        grid=(DEPTH, n // tn),
        in_specs=[
            pl.BlockSpec((MOD_ROWS, D_MODEL), lambda l, j: (0, 0)),
            pl.BlockSpec((None, D_MODEL, tn), lambda l, j: (l, 0, j)),
            pl.BlockSpec((None, 1, tn), lambda l, j: (l, 0, j)),
        ],
        out_specs=pl.BlockSpec((None, MOD_ROWS, tn), lambda l, j: (l, 0, j)),
        out_shape=jax.ShapeDtypeStruct((DEPTH, MOD_ROWS, n), F32),
        compiler_params=_params("parallel", "parallel"),
        name="modulation",
    )(cpad, mod_w, mod_b.reshape(DEPTH, 1, n))


def _mod_spec(layer, k, tm, ngrid):
    base = layer * MOD_ROWS * 6 + k
    if ngrid == 1:
        return pl.BlockSpec((None, 1, D_MODEL), lambda i: (base + 6 * _mod_row(i, tm), 0, 0))
    return pl.BlockSpec((None, 1, D_MODEL), lambda i, j: (base + 6 * _mod_row(i, tm), 0, 0))


def _nmm_kernel(*refs, epilogue, n_rope_blocks):
    if epilogue == "rope":
        x_ref, g_ref, sh_ref, sc_ref, w_ref, cos_ref, sin_ref, o_ref, h_ref = refs
    elif epilogue == "softplus":
        x_ref, g_ref, sh_ref, sc_ref, w_ref, bias_ref, o_ref, h_ref = refs
    else:
        x_ref, g_ref, sh_ref, sc_ref, w_ref, o_ref, h_ref = refs
    j = pl.program_id(1)

    @pl.when(j == 0)
    def _():
        x = x_ref[...]
        r = x * lax.rsqrt(jnp.mean(x * x, axis=-1, keepdims=True) + EPS)
        h = (r * g_ref[...]) * (1.0 + sc_ref[...]) + sh_ref[...]
        h_ref[...] = h.astype(BF16)

    u = _dot(h_ref[...], w_ref[...])
    if epilogue == "softplus":
        o_ref[...] = _softplus(u + bias_ref[...])
    elif epilogue == "rope":
        @pl.when(j < n_rope_blocks)
        def _():
            cos = cos_ref[...]
            sin = sin_ref[...]
            lane = lax.broadcasted_iota(jnp.int32, cos.shape, 1)
            half = ROPE_AXIS_DIM // 2
            first = (lane % ROPE_AXIS_DIM) < half
            for k in range(u.shape[1] // HEAD_W):
                cols = slice(k * HEAD_W, (k + 1) * HEAD_W)
                uk = u[:, cols]
                partner = jnp.where(first, pltpu.roll(uk, HEAD_W - half, 1), pltpu.roll(uk, half, 1))
                o_ref[:, cols] = uk * cos + partner * sin

        @pl.when(j >= n_rope_blocks)
        def _():
            o_ref[...] = u
    else:
        o_ref[...] = u.astype(o_ref.dtype)


def _norm_mod_matmul(x, mod3, layer, kshift, kscale, g, w, *, tn, out_dtype=F32,
                     epilogue="none", extra=(), n_rope_blocks=0, name):
    tm = 1024
    n = w.shape[1]
    in_specs = [
        pl.BlockSpec((tm, D_MODEL), lambda i, j: (i, 0)),
        pl.BlockSpec((1, D_MODEL), lambda i, j: (0, 0)),
        _mod_spec(layer, kshift, tm, 2),
        _mod_spec(layer, kscale, tm, 2),
        pl.BlockSpec((D_MODEL, tn), lambda i, j: (0, j)),
    ]
    if epilogue == "rope":
        in_specs += [pl.BlockSpec((tm, HEAD_W), lambda i, j: (i, 0))] * 2
    elif epilogue == "softplus":
        in_specs += [pl.BlockSpec((1, tn), lambda i, j: (0, j))]
    return pl.pallas_call(
        functools.partial(_nmm_kernel, epilogue=epilogue, n_rope_blocks=n_rope_blocks),
        grid=(N_TOK // tm, n // tn),
        in_specs=in_specs,
        out_specs=pl.BlockSpec((tm, tn), lambda i, j: (i, j)),
        out_shape=jax.ShapeDtypeStruct((N_TOK, n), out_dtype),
        scratch_shapes=[pltpu.VMEM((tm, D_MODEL), BF16)],
        compiler_params=_params("parallel", "arbitrary"),
        name=name,
    )(x, g.reshape(1, D_MODEL), mod3, mod3, w, *extra)


def _conv3(u, prev_row, next_row, w, b, row0, seqlen):
    tm = u.shape[0]
    loc = lax.broadcasted_iota(jnp.int32, (tm, 1), 0)
    pos = (loc + row0) & (seqlen - 1)
    prev = jnp.where(loc == 0, prev_row, pltpu.roll(u, 1, 0))
    prev = jnp.where(pos == 0, 0.0, prev)
    nxt = jnp.where(loc == tm - 1, next_row, pltpu.roll(u, tm - 1, 0))
    nxt = jnp.where(pos == seqlen - 1, 0.0, nxt)
    return prev * w[0:1, :] + u * w[1:2, :] + nxt * w[2:3, :] + b


def _block_seqlen(i, tm):
    return jnp.where(i < N_CTX // tm, SEQ, DEC_SEQ)


def _conv_silu_kernel(u_ref, up_ref, un_ref, w_ref, b_ref, o_ref, *, tm):
    i = pl.program_id(0)
    y = _conv3(u_ref[...], up_ref[SUBLANES - 1:SUBLANES, :], un_ref[0:1, :],
               w_ref[...], b_ref[...], i * tm, _block_seqlen(i, tm))
    o_ref[...] = _silu(y)


def _conv_glu_kernel(a_ref, ap_ref, an_ref, v_ref, vp_ref, vn_ref,
                     wa_ref, ba_ref, wv_ref, bv_ref, o_ref, *, tm):
    i = pl.program_id(0)
    row0 = i * tm
    seqlen = _block_seqlen(i, tm)
    a = _conv3(a_ref[...], ap_ref[SUBLANES - 1:SUBLANES, :], an_ref[0:1, :],
               wa_ref[...], ba_ref[...], row0, seqlen)
    v = _conv3(v_ref[...], vp_ref[SUBLANES - 1:SUBLANES, :], vn_ref[0:1, :],
               wv_ref[...], bv_ref[...], row0, seqlen)
    o_ref[...] = (_silu(a) * v).astype(o_ref.dtype)


def _halo_specs(tm, tc, coff):
    per = tm // SUBLANES
    last = N_TOK // SUBLANES - 1
    return [
        pl.BlockSpec((tm, tc), lambda i, j: (i, j + coff)),
        pl.BlockSpec((SUBLANES, tc), lambda i, j: (jnp.maximum(i * per - 1, 0), j + coff)),
        pl.BlockSpec((SUBLANES, tc), lambda i, j: (jnp.minimum((i + 1) * per, last), j + coff)),
    ]


def _ssd_conv(proj, conv_w, conv_b):
    tm, tc = 1024, 512
    coff = SSD_INNER // tc
    return pl.pallas_call(
        functools.partial(_conv_silu_kernel, tm=tm),
        grid=(N_TOK // tm, SSD_CONV_CH // tc),
        in_specs=_halo_specs(tm, tc, coff) + [
            pl.BlockSpec((3, tc), lambda i, j: (0, j)),
            pl.BlockSpec((1, tc), lambda i, j: (0, j)),
        ],
        out_specs=pl.BlockSpec((tm, tc), lambda i, j: (i, j)),
        out_shape=jax.ShapeDtypeStruct((N_TOK, SSD_CONV_CH), F32),
        compiler_params=_params("parallel", "parallel"),
        name="ssd_conv",
    )(proj, proj, proj, conv_w, conv_b.reshape(1, SSD_CONV_CH))


def _ffn_conv_glu(u, conv_w, conv_b):
    tm, tc = 1024, 256
    nb = FFN_HIDDEN // tc
    cb = conv_b.reshape(1, 2 * FFN_HIDDEN)
    wspec = lambda off: [pl.BlockSpec((3, tc), lambda i, j: (0, j + off)),
                         pl.BlockSpec((1, tc), lambda i, j: (0, j + off))]
    return pl.pallas_call(
        functools.partial(_conv_glu_kernel, tm=tm),
        grid=(N_TOK // tm, nb),
        in_specs=_halo_specs(tm, tc, 0) + _halo_specs(tm, tc, nb) + wspec(0) + wspec(nb),
        out_specs=pl.BlockSpec((tm, tc), lambda i, j: (i, j)),
        out_shape=jax.ShapeDtypeStruct((N_TOK, FFN_HIDDEN), BF16),
        compiler_params=_params("parallel", "parallel"),
        name="ffn_conv_glu",
    )(u, u, u, u, u, u, conv_w, cb, conv_w, cb)


def _mm_res_kernel(a_ref, w_ref, x_ref, gate_ref, o_ref):
    o_ref[...] = x_ref[...] + gate_ref[...] * _dot(a_ref[...], w_ref[...])


def _matmul_residual(a, w, x, mod3, layer, kgate, *, name):
    tm = 512
    k = a.shape[1]
    return pl.pallas_call(
        _mm_res_kernel,
        grid=(N_TOK // tm,),
        in_specs=[
            pl.BlockSpec((tm, k), lambda i: (i, 0)),
            pl.BlockSpec((k, D_MODEL), lambda i: (0, 0)),
            pl.BlockSpec((tm, D_MODEL), lambda i: (i, 0)),
            _mod_spec(layer, kgate, tm, 1),
        ],
        out_specs=pl.BlockSpec((tm, D_MODEL), lambda i: (i, 0)),
        out_shape=jax.ShapeDtypeStruct((N_TOK, D_MODEL), F32),
        compiler_params=_params("parallel"),
        name=name,
    )(a, w, x, mod3)


def _scan_kernel(*refs, nc, zero_init):
    if zero_init:
        x_ref, b_ref, c_ref, dt_ref, alog_ref, e_ref, y_ref, hl_ref, st_ref = refs
    else:
        x_ref, b_ref, c_ref, dt_ref, alog_ref, e_ref, h0_ref, y_ref, hl_ref, st_ref = refs
    d = pl.program_id(1)
    c = pl.program_id(2)
    q = SSD_CHUNK

    @pl.when(c == 0)
    def _():
        if zero_init:
            st_ref[...] = jnp.zeros_like(st_ref)
        else:
            st_ref[...] = h0_ref[...].T

    ii = lax.broadcasted_iota(jnp.int32, (q, q), 0)
    jj = lax.broadcasted_iota(jnp.int32, (q, q), 1)
    tri = ((ii - jj) * (1 - 2 * d)) >= 0
    tri_b = jnp.where(tri, 1.0, 0.0).astype(BF16)

    dtb = dt_ref[...]
    da = dtb * (-jnp.exp(alog_ref[...]))
    hi, mid, lo = _split3(da)
    cs = _dot(tri_b, hi) + _dot(tri_b, mid) + _dot(tri_b, lo)
    cs_t = cs.T
    e = e_ref[...]
    dt_x = _dot3(dtb, e)
    cs_x = _dot3(cs, e)
    tot_x = jnp.where(d == 0, cs_x[q - 1:q, :], cs_x[0:1, :])

    xs = x_ref[...]
    xd = xs * dt_x
    xw = (xd * jnp.exp(tot_x - cs_x)).astype(BF16)
    dec_out = jnp.exp(cs_x)
    dec_chunk = jnp.exp(tot_x)
    lane = lax.broadcasted_iota(jnp.int32, (q, LANES), 1)

    for g in range(SSD_GROUPS):
        gs = slice(g * GROUP_W, (g + 1) * GROUP_W)
        bg = b_ref[:, g * SSD_STATE:(g + 1) * SSD_STATE]
        cg = c_ref[:, g * SSD_STATE:(g + 1) * SSD_STATE].astype(BF16)
        cb = _dot_nt(cg, bg.astype(BF16))
        st_g = st_ref[:, gs]
        y_off = _dot(cg, st_g.astype(BF16))
        new_g = _dot(bg.T.astype(BF16), xw[:, gs])
        st_ref[:, gs] = st_g * dec_chunk[:, gs] + new_g
        for p in range(HEADS_PER_GROUP // 2):
            h = g * HEADS_PER_GROUP + 2 * p
            ms = []
            for hh in (h, h + 1):
                seg = cs[:, hh:hh + 1] - cs_t[hh:hh + 1, :]
                decay = jnp.where(tri, jnp.exp(jnp.minimum(seg, 0.0)), 0.0)
                ms.append((cb * decay).astype(BF16))
            lhs = jnp.concatenate(ms, axis=1)
            cols = slice(h * SSD_HEAD_DIM, (h + 2) * SSD_HEAD_DIM)
            xd2 = xd[:, cols]
            rhs = jnp.concatenate([jnp.where(lane < SSD_HEAD_DIM, xd2, 0.0),
                                   jnp.where(lane >= SSD_HEAD_DIM, xd2, 0.0)], axis=0).astype(BF16)
            loc = slice(2 * p * SSD_HEAD_DIM, (2 * p + 2) * SSD_HEAD_DIM)
            y_ref[:, cols] = _dot(lhs, rhs) + y_off[:, loc] * dec_out[:, cols]

    @pl.when(c == nc - 1)
    def _():
        hl_ref[...] = st_ref[...].T


def _ssd_scan(xbc, dt, a_log_pad, expand, h0, slot, *, nseq, nc, chunk0, name):
    q = SSD_CHUNK
    zero_init = h0 is None

    def chunk(b, d, c):
        return chunk0 + b * nc + c + d * (nc - 1 - 2 * c)

    def ochunk(b, d, c):
        return b * nc + c + d * (nc - 1 - 2 * c)

    in_specs = [
        pl.BlockSpec((q, SSD_INNER), lambda b, d, c: (chunk(b, d, c), 0)),
        pl.BlockSpec((q, SSD_BC), lambda b, d, c: (chunk(b, d, c), SSD_INNER // SSD_BC)),
        pl.BlockSpec((q, SSD_BC), lambda b, d, c: (chunk(b, d, c), SSD_INNER // SSD_BC + 1)),
        pl.BlockSpec((q, LANES), lambda b, d, c: (chunk(b, d, c), d)),
        pl.BlockSpec((None, 1, LANES), lambda b, d, c: (d, 0, 0)),
        pl.BlockSpec((LANES, SSD_INNER), lambda b, d, c: (0, 0)),
    ]
    args = [xbc, xbc, xbc, dt, a_log_pad, expand]
    if not zero_init:
        in_specs.append(pl.BlockSpec((None, None, None, SSD_INNER, SSD_STATE),
                                     lambda b, d, c: (b, slot, d, 0, 0)))
        args.append(h0)
    return pl.pallas_call(
        functools.partial(_scan_kernel, nc=nc, zero_init=zero_init),
        grid=(nseq, 2, nc),
        in_specs=in_specs,
        out_specs=[
            pl.BlockSpec((None, q, SSD_INNER), lambda b, d, c: (d, ochunk(b, d, c), 0)),
            pl.BlockSpec((None, None, SSD_INNER, SSD_STATE), lambda b, d, c: (b, d, 0, 0)),
        ],
        out_shape=[
            jax.ShapeDtypeStruct((2, nseq * nc * q, SSD_INNER), F32),
            jax.ShapeDtypeStruct((nseq, 2, SSD_INNER, SSD_STATE), F32),
        ],
        scratch_shapes=[pltpu.VMEM((SSD_STATE, SSD_INNER), F32)],
        compiler_params=_params("parallel", "arbitrary", "arbitrary"),
        name=name,
    )(*args)


def _ssd_out_kernel(yf_ref, yb_ref, xs_ref, z_ref, dx_ref, ng_ref, w_ref, x_ref, gate_ref, o_ref):
    y = yf_ref[...] + yb_ref[...] + dx_ref[...] * xs_ref[...]
    y = y * _silu(z_ref[...])
    gw = SSD_INNER // SSD_GROUPS
    parts = []
    for g in range(SSD_GROUPS):
        yg = y[:, g * gw:(g + 1) * gw]
        parts.append(yg * lax.rsqrt(jnp.mean(yg * yg, axis=-1, keepdims=True) + EPS))
    yn = (jnp.concatenate(parts, axis=1) * ng_ref[...]).astype(BF16)
    o_ref[...] = x_ref[...] + gate_ref[...] * _dot(yn, w_ref[...])


def _ssd_out(y, xbc, proj, d_x, norm_g, out_w, x, mod3, layer):
    tm = 256
    return pl.pallas_call(
        _ssd_out_kernel,
        grid=(N_TOK // tm,),
        in_specs=[
            pl.BlockSpec((None, tm, SSD_INNER), lambda i: (0, i, 0)),
            pl.BlockSpec((None, tm, SSD_INNER), lambda i: (1, i, 0)),
            pl.BlockSpec((tm, SSD_INNER), lambda i: (i, 0)),
            pl.BlockSpec((tm, SSD_INNER), lambda i: (i, 0)),
            pl.BlockSpec((1, SSD_INNER), lambda i: (0, 0)),
            pl.BlockSpec((1, SSD_INNER), lambda i: (0, 0)),
            pl.BlockSpec((SSD_INNER, D_MODEL), lambda i: (0, 0)),
            pl.BlockSpec((tm, D_MODEL), lambda i: (i, 0)),
            _mod_spec(layer, 2, tm, 1),
        ],
        out_specs=pl.BlockSpec((tm, D_MODEL), lambda i: (i, 0)),
        out_shape=jax.ShapeDtypeStruct((N_TOK, D_MODEL), F32),
        compiler_params=_params("parallel"),
        name="ssd_out",
    )(y, y, xbc, proj, d_x, norm_g.reshape(1, SSD_INNER), out_w, x, mod3)


def _attn_kernel(*refs, has_cache, lam_init):
    if has_cache:
        q_ref, k_ref, v_ref, kc_ref, vc_ref, lam_ref, g_ref, o_ref = refs
    else:
        q_ref, k_ref, v_ref, lam_ref, g_ref, o_ref = refs
    lv = lam_ref[...]
    lam = (jnp.exp(jnp.sum(lv[0:1] * lv[1:2], axis=-1, keepdims=True))
           - jnp.exp(jnp.sum(lv[2:3] * lv[3:4], axis=-1, keepdims=True)) + lam_init)

    q = q_ref[...] * DA_SCALE
    lane = lax.broadcasted_iota(jnp.int32, q.shape, 1)
    q1 = jnp.where(lane < DA_HEAD_DIM, q, 0.0).astype(BF16)
    q2 = jnp.where(lane >= DA_HEAD_DIM, q, 0.0).astype(BF16)
    keys = [k_ref[...].astype(BF16)]
    vals = [v_ref[...].astype(BF16)]
    if has_cache:
        keys.append(kc_ref[...].astype(BF16))
        vals.append(vc_ref[...].astype(BF16))

    probs = []
    for qm in (q1, q2):
        s = [_dot_nt(qm, k) for k in keys]
        m = functools.reduce(jnp.maximum, [jnp.max(t, axis=-1, keepdims=True) for t in s])
        p = [jnp.exp(t - m) for t in s]
        l = functools.reduce(jnp.add, [jnp.sum(t, axis=-1, keepdims=True) for t in p])
        probs.append((p, 1.0 / l))
    (p1, r1), (p2, r2) = probs
    r2 = lam * r2
    o = None
    for a, b, v in zip(p1, p2, vals):
        w = (a * r1 - b * r2).astype(BF16)
        t = _dot(w, v)
        o = t if o is None else o + t
    o = o * lax.rsqrt(jnp.mean(o * o, axis=-1, keepdims=True) + EPS)
    o_ref[...] = (o * g_ref[...] * (1.0 - lam_init)).astype(o_ref.dtype)


def _attention(qkv, cache_k4, cache_v4, slot, lam_vecs, subln_g, lam_init):
    kcol, vcol = DA_HEADS, 2 * DA_HEADS
    common = [pl.BlockSpec((4, DA_HEAD_DIM), lambda *_: (0, 0)),
              pl.BlockSpec((1, HEAD_W), lambda *_: (0, 0))]
    g2 = subln_g.reshape(1, HEAD_W)

    ctx = pl.pallas_call(
        functools.partial(_attn_kernel, has_cache=False, lam_init=lam_init),
        grid=(BATCH, DA_HEADS),
        in_specs=[
            pl.BlockSpec((SEQ, HEAD_W), lambda b, h: (b, h)),
            pl.BlockSpec((SEQ, HEAD_W), lambda b, h: (b, kcol + h)),
            pl.BlockSpec((SEQ, HEAD_W), lambda b, h: (b, vcol + h)),
        ] + common,
        out_specs=pl.BlockSpec((SEQ, HEAD_W), lambda b, h: (b, h)),
        out_shape=jax.ShapeDtypeStruct((N_CTX, D_MODEL), BF16),
        compiler_params=_params("parallel", "parallel"),
        name="attn_ctx",
    )(qkv, qkv, qkv, lam_vecs, g2)

    tq = 256
    nq = DEC_SEQ // tq
    q0 = N_CTX // tq
    s0 = N_CTX // DEC_SEQ
    lat = pl.pallas_call(
        functools.partial(_attn_kernel, has_cache=True, lam_init=lam_init),
        grid=(DEC_BATCH, DA_HEADS, nq),
        in_specs=[
            pl.BlockSpec((tq, HEAD_W), lambda b, h, i: (q0 + b * nq + i, h)),
            pl.BlockSpec((DEC_SEQ, HEAD_W), lambda b, h, i: (s0 + b, kcol + h)),
            pl.BlockSpec((DEC_SEQ, HEAD_W), lambda b, h, i: (s0 + b, vcol + h)),
            pl.BlockSpec((None, None, PAST_LEN, HEAD_W), lambda b, h, i: (b, slot, 0, h)),
            pl.BlockSpec((None, None, PAST_LEN, HEAD_W), lambda b, h, i: (b, slot, 0, h)),
        ] + common,
        out_specs=pl.BlockSpec((tq, HEAD_W), lambda b, h, i: (b * nq + i, h)),
        out_shape=jax.ShapeDtypeStruct((N_LAT, D_MODEL), BF16),
        compiler_params=_params("parallel", "parallel", "parallel"),
        name="attn_lat",
    )(qkv, qkv, qkv, cache_k4, cache_v4, lam_vecs, g2)
    return jnp.concatenate([ctx, lat], axis=0)


def _final_kernel(x_ref, g_ref, o_ref):
    x = x_ref[...]
    o_ref[...] = x * lax.rsqrt(jnp.mean(x * x, axis=-1, keepdims=True) + EPS) * g_ref[...]


def _final_norm(x, g):
    tm = 1024
    return pl.pallas_call(
        _final_kernel,
        grid=(N_TOK // tm,),
        in_specs=[pl.BlockSpec((tm, D_MODEL), lambda i: (i, 0)),
                  pl.BlockSpec((1, D_MODEL), lambda i: (0, 0))],
        out_specs=pl.BlockSpec((tm, D_MODEL), lambda i: (i, 0)),
        out_shape=jax.ShapeDtypeStruct((N_TOK, D_MODEL), F32),
        compiler_params=_params("parallel"),
        name="final_norm",
    )(x, g.reshape(1, D_MODEL))


def _rope_token_tables():
    rows = DEC_SEQ // GRID_W
    row = jnp.repeat(jnp.arange(rows, dtype=F32), GRID_W)
    col = jnp.tile(jnp.arange(GRID_W, dtype=F32), rows)
    inv = 1.0 / (ROPE_BASE ** (jnp.arange(0, ROPE_AXIS_DIM, 2, dtype=F32) / ROPE_AXIS_DIM))
    ang_r = row[:, None] * inv
    ang_c = col[:, None] * inv
    cr, sr, cc, sc = jnp.cos(ang_r), jnp.sin(ang_r), jnp.cos(ang_c), jnp.sin(ang_c)
    cos64 = jnp.concatenate([cr, cr, cc, cc], axis=1)
    sin64 = jnp.concatenate([-sr, sr, -sc, sc], axis=1)
    cos_lat = jnp.tile(cos64, (DEC_BATCH, 2))
    sin_lat = jnp.tile(sin64, (DEC_BATCH, 2))
    cos = jnp.concatenate([jnp.ones((N_CTX, HEAD_W), F32), cos_lat], axis=0)
    sin = jnp.concatenate([jnp.zeros((N_CTX, HEAD_W), F32), sin_lat], axis=0)
    return cos, sin


def _lambda_init(layer):
    return 0.8 - 0.6 * math.exp(-0.3 * layer)


def kernel(x_prompt, x_sample, state_ssd, cache_k, cache_v, c, c_ctx, mod_w, mod_b, norm_mix_g, norm_ffn_g, ssd_in_w, ssd_conv_w, ssd_conv_b, ssd_dt_bias, ssd_a_log, ssd_d, ssd_norm_g, ssd_out_w, att_qkv_w, att_lambda, att_subln_g, att_out_w, ffn_up_w, ffn_conv_w, ffn_conv_b, ffn_down_w, final_norm_g):
    n_ssd = ssd_in_w.shape[0]
    n_att = att_qkv_w.shape[0]
    x = jnp.concatenate([x_prompt.reshape(N_CTX, D_MODEL), x_sample.reshape(N_LAT, D_MODEL)], axis=0)

    cpad = jnp.concatenate([c_ctx[None], c, jnp.zeros((MOD_ROWS - 1 - DEC_BATCH, D_MODEL), F32)], axis=0)
    mod = _modulation(cpad, mod_w, mod_b)
    mod3 = mod.reshape(DEPTH * MOD_ROWS * 6, 1, D_MODEL)

    cos_t, sin_t = _rope_token_tables()
    expand = (jnp.arange(LANES)[:, None] == (jnp.arange(SSD_INNER)[None, :] // SSD_HEAD_DIM)).astype(BF16)
    h0_lat = state_ssd.reshape(DEC_BATCH, n_ssd, 2, SSD_INNER, SSD_STATE)
    cache_k4 = cache_k.reshape(DEC_BATCH, n_att, PAST_LEN, D_MODEL)
    cache_v4 = cache_v.reshape(DEC_BATCH, n_att, PAST_LEN, D_MODEL)

    new_ssd, new_k, new_v = [], [], []
    for i in range(DEPTH):
        slot = i // 2
        if i % 2 == 0:
            in_w = ssd_in_w[slot]
            w_main = in_w[:, :SSD_MAIN].astype(BF16)
            pad = jnp.zeros((D_MODEL, LANES - SSD_HEADS), F32)
            w_dt = jnp.concatenate([in_w[:, SSD_MAIN:SSD_MAIN + SSD_HEADS], pad,
                                    in_w[:, SSD_MAIN + SSD_HEADS:], pad], axis=1).astype(BF16)
            zpad = jnp.zeros((LANES - SSD_HEADS,), F32)
            dt_bias = jnp.concatenate([ssd_dt_bias[slot, 0], zpad, ssd_dt_bias[slot, 1], zpad]).reshape(1, 2 * LANES)
            a_log_pad = jnp.concatenate([ssd_a_log[slot], jnp.zeros((2, LANES - SSD_HEADS), F32)],
                                        axis=1).reshape(2, 1, LANES)
            proj = _norm_mod_matmul(x, mod3, i, 0, 1, norm_mix_g[i], w_main, tn=1280, name="ssd_in")
            dt = _norm_mod_matmul(x, mod3, i, 0, 1, norm_mix_g[i], w_dt, tn=2 * LANES,
                                  epilogue="softplus", extra=(dt_bias,), name="ssd_dt")
            xbc = _ssd_conv(proj, ssd_conv_w[slot], ssd_conv_b[slot])
            y_ctx, st_ctx = _ssd_scan(xbc, dt, a_log_pad, expand, None, slot,
                                      nseq=BATCH, nc=SEQ // SSD_CHUNK, chunk0=0, name="ssd_scan_ctx")
            y_lat, _ = _ssd_scan(xbc, dt, a_log_pad, expand, h0_lat, slot,
                                 nseq=DEC_BATCH, nc=DEC_SEQ // SSD_CHUNK, chunk0=N_CTX // SSD_CHUNK,
                                 name="ssd_scan_lat")
            y = jnp.concatenate([y_ctx, y_lat], axis=1)
            new_ssd.append(st_ctx.reshape(BATCH, 2, SSD_HEADS, SSD_HEAD_DIM, SSD_STATE))
            d_x = jnp.repeat(ssd_d[slot], SSD_HEAD_DIM).reshape(1, SSD_INNER)
            x = _ssd_out(y, xbc, proj, d_x, ssd_norm_g[slot], ssd_out_w[slot].astype(BF16), x, mod3, i)
        else:
            lam_init = _lambda_init(i)
            qkv = _norm_mod_matmul(x, mod3, i, 0, 1, norm_mix_g[i], att_qkv_w[slot].astype(BF16), tn=1024,
                                   epilogue="rope", extra=(cos_t, sin_t),
                                   n_rope_blocks=2 * D_MODEL // 1024, name="att_qkv")
            new_k.append(qkv[:N_CTX, D_MODEL:2 * D_MODEL].reshape(BATCH, SEQ, DA_HEADS, 2, DA_HEAD_DIM))
            new_v.append(qkv[:N_CTX, 2 * D_MODEL:].reshape(BATCH, SEQ, DA_HEADS, HEAD_W))
            o = _attention(qkv, cache_k4, cache_v4, slot, att_lambda[slot], att_subln_g[slot], lam_init)
            x = _matmul_residual(o, att_out_w[slot].astype(BF16), x, mod3, i, 2, name="att_out")
        u = _norm_mod_matmul(x, mod3, i, 3, 4, norm_ffn_g[i], ffn_up_w[i].astype(BF16), tn=512, name="ffn_up")
        a = _ffn_conv_glu(u, ffn_conv_w[i], ffn_conv_b[i])
        x = _matmul_residual(a, ffn_down_w[i].astype(BF16), x, mod3, i, 5, name="ffn_down")

    y = _final_norm(x, final_norm_g)
    y_prompt = y[:N_CTX].reshape(BATCH, SEQ, D_MODEL)
    y_sample = y[N_CTX:].reshape(DEC_BATCH, DEC_SEQ, D_MODEL)
    return (y_prompt, y_sample, jnp.stack(new_ssd, axis=1), jnp.stack(new_k, axis=1), jnp.stack(new_v, axis=1))
```

```python
import functools
import math

import jax
import jax.numpy as jnp
from jax import lax
from jax.experimental import pallas as pl
from jax.experimental.pallas import tpu as pltpu

F32 = jnp.float32
BF16 = jnp.bfloat16

D_MODEL = 1024
BATCH = 16
SEQ = 256
DEPTH = 4
DEC_BATCH = 4
DEC_SEQ = 2048
PAST_LEN = 256
GRID_W = 64
EPS = 1e-6

SSD_INNER = 2048
SSD_HEAD_DIM = 64
SSD_HEADS = 32
SSD_GROUPS = 4
SSD_STATE = 128
SSD_CHUNK = 128
SSD_BC = SSD_GROUPS * SSD_STATE
SSD_CONV_CH = SSD_INNER + 2 * SSD_BC
SSD_MAIN = SSD_INNER + SSD_CONV_CH
HEADS_PER_GROUP = SSD_HEADS // SSD_GROUPS
GROUP_W = HEADS_PER_GROUP * SSD_HEAD_DIM

DA_HEAD_DIM = 64
DA_HEADS = 8
DA_SCALE = DA_HEAD_DIM ** -0.5
HEAD_W = 2 * DA_HEAD_DIM
ROPE_BASE = 10000.0
ROPE_AXIS_DIM = DA_HEAD_DIM // 2

FFN_HIDDEN = 2816

N_CTX = BATCH * SEQ
N_LAT = DEC_BATCH * DEC_SEQ
N_TOK = N_CTX + N_LAT
MOD_ROWS = 8
LANES = 128
SUBLANES = 8
VMEM_LIMIT = 48 * 1024 * 1024


def _params(*sem):
    return pltpu.CompilerParams(dimension_semantics=sem, vmem_limit_bytes=VMEM_LIMIT)


def _mod_row(i, tm):
    nctx = N_CTX // tm
    return jnp.where(i < nctx, 0, 1 + (i - nctx) // (DEC_SEQ // tm))


def _silu(x):
    return x * jax.nn.sigmoid(x)


def _softplus(x):
    return jnp.maximum(x, 0.0) + jnp.log1p(jnp.exp(-jnp.abs(x)))


def _split3(x):
    hi = x.astype(BF16)
    r1 = x - hi.astype(F32)
    mid = r1.astype(BF16)
    lo = (r1 - mid.astype(F32)).astype(BF16)
    return hi, mid, lo


def _dot(a, b):
    return jnp.dot(a, b, preferred_element_type=F32)


def _dot_nt(a, b):
    return lax.dot_general(a, b, (((1,), (1,)), ((), ())), preferred_element_type=F32)


def _dot3(x, sel):
    hi, mid, lo = _split3(x)
    return _dot(hi, sel) + _dot(mid, sel) + _dot(lo, sel)


def _mod_kernel(c_ref, w_ref, b_ref, o_ref):
    s = _silu(c_ref[...]).astype(BF16)
    o_ref[...] = _dot(s, w_ref[...].astype(BF16)) + b_ref[...]


def _modulation(cpad, mod_w, mod_b):
    tn = 1024
    n = 6 * D_MODEL
    return pl.pallas_call(
        _mod_kernel,
        grid=(DEPTH, n // tn),
        in_specs=[
            pl.BlockSpec((MOD_ROWS, D_MODEL), lambda l, j: (0, 0)),
            pl.BlockSpec((None, D_MODEL, tn), lambda l, j: (l, 0, j)),
            pl.BlockSpec((None, 1, tn), lambda l, j: (l, 0, j)),
        ],
        out_specs=pl.BlockSpec((None, MOD_ROWS, tn), lambda l, j: (l, 0, j)),
        out_shape=jax.ShapeDtypeStruct((DEPTH, MOD_ROWS, n), F32),
        compiler_params=_params("parallel", "parallel"),
        name="modulation",
    )(cpad, mod_w, mod_b.reshape(DEPTH, 1, n))


def _mod_spec(layer, k, tm, ngrid):
    base = layer * MOD_ROWS * 6 + k
    if ngrid == 1:
        return pl.BlockSpec((None, 1, D_MODEL), lambda i: (base + 6 * _mod_row(i, tm), 0, 0))
    return pl.BlockSpec((None, 1, D_MODEL), lambda i, j: (base + 6 * _mod_row(i, tm), 0, 0))


def _nmm_kernel(*refs, epilogue, n_rope_blocks):
    if epilogue == "rope":
        x_ref, g_ref, sh_ref, sc_ref, w_ref, cos_ref, sin_ref, o_ref, h_ref = refs
    elif epilogue == "softplus":
        x_ref, g_ref, sh_ref, sc_ref, w_ref, bias_ref, o_ref, h_ref = refs
    else:
        x_ref, g_ref, sh_ref, sc_ref, w_ref, o_ref, h_ref = refs
    j = pl.program_id(1)

    @pl.when(j == 0)
    def _():
        x = x_ref[...]
        r = x * lax.rsqrt(jnp.mean(x * x, axis=-1, keepdims=True) + EPS)
        h = (r * g_ref[...]) * (1.0 + sc_ref[...]) + sh_ref[...]
        h_ref[...] = h.astype(BF16)

    u = _dot(h_ref[...], w_ref[...])
    if epilogue == "softplus":
        o_ref[...] = _softplus(u + bias_ref[...])
    elif epilogue == "rope":
        @pl.when(j < n_rope_blocks)
        def _():
            cos = cos_ref[...]
            sin = sin_ref[...]
            lane = lax.broadcasted_iota(jnp.int32, cos.shape, 1)
            half = ROPE_AXIS_DIM // 2
            first = (lane % ROPE_AXIS_DIM) < half
            for k in range(u.shape[1] // HEAD_W):
                cols = slice(k * HEAD_W, (k + 1) * HEAD_W)
                uk = u[:, cols]
                partner = jnp.where(first, pltpu.roll(uk, HEAD_W - half, 1), pltpu.roll(uk, half, 1))
                o_ref[:, cols] = uk * cos + partner * sin

        @pl.when(j >= n_rope_blocks)
        def _():
            o_ref[...] = u
    else:
        o_ref[...] = u.astype(o_ref.dtype)


def _norm_mod_matmul(x, mod3, layer, kshift, kscale, g, w, *, tn, out_dtype=F32,
                     epilogue="none", extra=(), n_rope_blocks=0, name):
    tm = 1024
    n = w.shape[1]
    in_specs = [
        pl.BlockSpec((tm, D_MODEL), lambda i, j: (i, 0)),
        pl.BlockSpec((1, D_MODEL), lambda i, j: (0, 0)),
        _mod_spec(layer, kshift, tm, 2),
        _mod_spec(layer, kscale, tm, 2),
        pl.BlockSpec((D_MODEL, tn), lambda i, j: (0, j)),
    ]
    if epilogue == "rope":
        in_specs += [pl.BlockSpec((tm, HEAD_W), lambda i, j: (i, 0))] * 2
    elif epilogue == "softplus":
        in_specs += [pl.BlockSpec((1, tn), lambda i, j: (0, j))]
    return pl.pallas_call(
        functools.partial(_nmm_kernel, epilogue=epilogue, n_rope_blocks=n_rope_blocks),
        grid=(N_TOK // tm, n // tn),
        in_specs=in_specs,
        out_specs=pl.BlockSpec((tm, tn), lambda i, j: (i, j)),
        out_shape=jax.ShapeDtypeStruct((N_TOK, n), out_dtype),
        scratch_shapes=[pltpu.VMEM((tm, D_MODEL), BF16)],
        compiler_params=_params("parallel", "arbitrary"),
        name=name,
    )(x, g.reshape(1, D_MODEL), mod3, mod3, w, *extra)


def _conv3(u, prev_row, next_row, w, b, row0, seqlen):
    tm = u.shape[0]
    loc = lax.broadcasted_iota(jnp.int32, (tm, 1), 0)
    pos = (loc + row0) & (seqlen - 1)
    prev = jnp.where(loc == 0, prev_row, pltpu.roll(u, 1, 0))
    prev = jnp.where(pos == 0, 0.0, prev)
    nxt = jnp.where(loc == tm - 1, next_row, pltpu.roll(u, tm - 1, 0))
    nxt = jnp.where(pos == seqlen - 1, 0.0, nxt)
    return prev * w[0:1, :] + u * w[1:2, :] + nxt * w[2:3, :] + b


def _block_seqlen(i, tm):
    return jnp.where(i < N_CTX // tm, SEQ, DEC_SEQ)


def _conv_silu_kernel(u_ref, up_ref, un_ref, w_ref, b_ref, o_ref, *, tm):
    i = pl.program_id(0)
    y = _conv3(u_ref[...], up_ref[SUBLANES - 1:SUBLANES, :], un_ref[0:1, :],
               w_ref[...], b_ref[...], i * tm, _block_seqlen(i, tm))
    o_ref[...] = _silu(y)


def _halo_specs(tm, tc, coff):
    per = tm // SUBLANES
    last = N_TOK // SUBLANES - 1
    return [
        pl.BlockSpec((tm, tc), lambda i, j: (i, j + coff)),
        pl.BlockSpec((SUBLANES, tc), lambda i, j: (jnp.maximum(i * per - 1, 0), j + coff)),
        pl.BlockSpec((SUBLANES, tc), lambda i, j: (jnp.minimum((i + 1) * per, last), j + coff)),
    ]


def _ssd_conv(proj, conv_w, conv_b):
    tm, tc = 1024, 512
    coff = SSD_INNER // tc
    return pl.pallas_call(
        functools.partial(_conv_silu_kernel, tm=tm),
        grid=(N_TOK // tm, SSD_CONV_CH // tc),
        in_specs=_halo_specs(tm, tc, coff) + [
            pl.BlockSpec((3, tc), lambda i, j: (0, j)),
            pl.BlockSpec((1, tc), lambda i, j: (0, j)),
        ],
        out_specs=pl.BlockSpec((tm, tc), lambda i, j: (i, j)),
        out_shape=jax.ShapeDtypeStruct((N_TOK, SSD_CONV_CH), F32),
        compiler_params=_params("parallel", "parallel"),
        name="ssd_conv",
    )(proj, proj, proj, conv_w, conv_b.reshape(1, SSD_CONV_CH))


def _norm_mod(x, g, sc, sh):
    r = x * lax.rsqrt(jnp.mean(x * x, axis=-1, keepdims=True) + EPS)
    return ((r * g) * (1.0 + sc) + sh).astype(BF16)


def _ffn_kernel(x_ref, xp_ref, xn_ref, g_ref, sh_ref, sc_ref, gate_ref, wa_ref, wv_ref,
                cwa_ref, cba_ref, cwv_ref, cbv_ref, wd_ref, o_ref, h_ref, hh_ref, *, tm):
    i = pl.program_id(0)
    j = pl.program_id(1)

    @pl.when(j == 0)
    def _():
        g, sc, sh = g_ref[...], sc_ref[...], sh_ref[...]
        h_ref[...] = _norm_mod(x_ref[...], g, sc, sh)
        hh_ref[...] = _norm_mod(jnp.concatenate([xp_ref[...], xn_ref[...]], axis=0), g, sc, sh)

    h = h_ref[...]
    hh = hh_ref[...]
    row0 = i * tm
    seqlen = _block_seqlen(i, tm)
    halves = []
    for w_ref, cw_ref, cb_ref in ((wa_ref, cwa_ref, cba_ref), (wv_ref, cwv_ref, cbv_ref)):
        w = w_ref[...]
        u = _dot(h, w)
        uh = _dot(hh, w)
        halves.append(_conv3(u, uh[SUBLANES - 1:SUBLANES, :], uh[SUBLANES:SUBLANES + 1, :],
                             cw_ref[...], cb_ref[...], row0, seqlen))
    act = (_silu(halves[0]) * halves[1]).astype(BF16)
    contrib = _dot(act, wd_ref[...])

    @pl.when(j == 0)
    def _():
        o_ref[...] = contrib

    @pl.when(j > 0)
    def _():
        o_ref[...] += contrib

    @pl.when(j == pl.num_programs(1) - 1)
    def _():
        o_ref[...] = x_ref[...] + gate_ref[...] * o_ref[...]


def _ffn(x, mod3, layer, g, up_w, conv_w, conv_b, down_w):
    tm, th = 1024, 256
    nb = FFN_HIDDEN // th
    per = tm // SUBLANES
    last = N_TOK // SUBLANES - 1
    cb = conv_b.reshape(1, 2 * FFN_HIDDEN)
    return pl.pallas_call(
        functools.partial(_ffn_kernel, tm=tm),
        grid=(N_TOK // tm, nb),
        in_specs=[
            pl.BlockSpec((tm, D_MODEL), lambda i, j: (i, 0)),
            pl.BlockSpec((SUBLANES, D_MODEL), lambda i, j: (jnp.maximum(i * per - 1, 0), 0)),
            pl.BlockSpec((SUBLANES, D_MODEL), lambda i, j: (jnp.minimum((i + 1) * per, last), 0)),
            pl.BlockSpec((1, D_MODEL), lambda i, j: (0, 0)),
            _mod_spec(layer, 3, tm, 2),
            _mod_spec(layer, 4, tm, 2),
            _mod_spec(layer, 5, tm, 2),
            pl.BlockSpec((D_MODEL, th), lambda i, j: (0, j)),
            pl.BlockSpec((D_MODEL, th), lambda i, j: (0, j + nb)),
            pl.BlockSpec((3, th), lambda i, j: (0, j)),
            pl.BlockSpec((1, th), lambda i, j: (0, j)),
            pl.BlockSpec((3, th), lambda i, j: (0, j + nb)),
            pl.BlockSpec((1, th), lambda i, j: (0, j + nb)),
            pl.BlockSpec((th, D_MODEL), lambda i, j: (j, 0)),
        ],
        out_specs=pl.BlockSpec((tm, D_MODEL), lambda i, j: (i, 0)),
        out_shape=jax.ShapeDtypeStruct((N_TOK, D_MODEL), F32),
        scratch_shapes=[pltpu.VMEM((tm, D_MODEL), BF16), pltpu.VMEM((2 * SUBLANES, D_MODEL), BF16)],
        compiler_params=_params("parallel", "arbitrary"),
        name="ffn",
    )(x, x, x, g.reshape(1, D_MODEL), mod3, mod3, mod3, up_w, up_w, conv_w, cb, conv_w, cb, down_w)


def _mm_res_kernel(ac_ref, al_ref, w_ref, x_ref, gate_ref, o_ref, *, n_ctx_blocks):
    a = jnp.where(pl.program_id(0) < n_ctx_blocks, ac_ref[...], al_ref[...])
    o_ref[...] = x_ref[...] + gate_ref[...] * _dot(a, w_ref[...])


def _two_source_specs(block, tm, lead):
    nctx = N_CTX // tm
    return [pl.BlockSpec(block, lambda i: lead + (jnp.minimum(i, nctx - 1), 0)),
            pl.BlockSpec(block, lambda i: lead + (jnp.maximum(i - nctx, 0), 0))]


def _matmul_residual(a_ctx, a_lat, w, x, mod3, layer, kgate, *, name):
    tm = 512
    k = w.shape[0]
    return pl.pallas_call(
        functools.partial(_mm_res_kernel, n_ctx_blocks=N_CTX // tm),
        grid=(N_TOK // tm,),
        in_specs=_two_source_specs((tm, k), tm, ()) + [
            pl.BlockSpec((k, D_MODEL), lambda i: (0, 0)),
            pl.BlockSpec((tm, D_MODEL), lambda i: (i, 0)),
            _mod_spec(layer, kgate, tm, 1),
        ],
        out_specs=pl.BlockSpec((tm, D_MODEL), lambda i: (i, 0)),
        out_shape=jax.ShapeDtypeStruct((N_TOK, D_MODEL), F32),
        compiler_params=_params("parallel"),
        name=name,
    )(a_ctx, a_lat, w, x, mod3)


def _scan_kernel(*refs, nc, zero_init):
    if zero_init:
        x_ref, b_ref, c_ref, dt_ref, alog_ref, e_ref, y_ref, hl_ref, st_ref = refs
    else:
        x_ref, b_ref, c_ref, dt_ref, alog_ref, e_ref, h0_ref, y_ref, hl_ref, st_ref = refs
    d = pl.program_id(1)
    c = pl.program_id(2)
    q = SSD_CHUNK

    @pl.when(c == 0)
    def _():
        if zero_init:
            st_ref[...] = jnp.zeros_like(st_ref)
        else:
            st_ref[...] = h0_ref[...].T

    ii = lax.broadcasted_iota(jnp.int32, (q, q), 0)
    jj = lax.broadcasted_iota(jnp.int32, (q, q), 1)
    tri = ((ii - jj) * (1 - 2 * d)) >= 0
    tri_b = jnp.where(tri, 1.0, 0.0).astype(BF16)

    dtb = dt_ref[...]
    da = dtb * (-jnp.exp(alog_ref[...]))
    hi, mid, lo = _split3(da)
    cs = _dot(tri_b, hi) + _dot(tri_b, mid) + _dot(tri_b, lo)
    cs_t = cs.T
    e = e_ref[...]
    dt_x = _dot3(dtb, e)
    cs_x = _dot3(cs, e)
    tot_x = jnp.where(d == 0, cs_x[q - 1:q, :], cs_x[0:1, :])

    xs = x_ref[...]
    xd = xs * dt_x
    xw = (xd * jnp.exp(tot_x - cs_x)).astype(BF16)
    dec_out = jnp.exp(cs_x)
    dec_chunk = jnp.exp(tot_x)
    lane = lax.broadcasted_iota(jnp.int32, (q, LANES), 1)

    for g in range(SSD_GROUPS):
        gs = slice(g * GROUP_W, (g + 1) * GROUP_W)
        bg = b_ref[:, g * SSD_STATE:(g + 1) * SSD_STATE]
        cg = c_ref[:, g * SSD_STATE:(g + 1) * SSD_STATE].astype(BF16)
        cb = _dot_nt(cg, bg.astype(BF16))
        st_g = st_ref[:, gs]
        y_off = _dot(cg, st_g.astype(BF16))
        new_g = _dot(bg.T.astype(BF16), xw[:, gs])
        st_ref[:, gs] = st_g * dec_chunk[:, gs] + new_g
        for p in range(HEADS_PER_GROUP // 2):
            h = g * HEADS_PER_GROUP + 2 * p
            ms = []
            for hh in (h, h + 1):
                seg = cs[:, hh:hh + 1] - cs_t[hh:hh + 1, :]
                decay = jnp.where(tri, jnp.exp(jnp.minimum(seg, 0.0)), 0.0)
                ms.append((cb * decay).astype(BF16))
            lhs = jnp.concatenate(ms, axis=1)
            cols = slice(h * SSD_HEAD_DIM, (h + 2) * SSD_HEAD_DIM)
            xd2 = xd[:, cols]
            rhs = jnp.concatenate([jnp.where(lane < SSD_HEAD_DIM, xd2, 0.0),
                                   jnp.where(lane >= SSD_HEAD_DIM, xd2, 0.0)], axis=0).astype(BF16)
            loc = slice(2 * p * SSD_HEAD_DIM, (2 * p + 2) * SSD_HEAD_DIM)
            y_ref[:, cols] = _dot(lhs, rhs) + y_off[:, loc] * dec_out[:, cols]

    @pl.when(c == nc - 1)
    def _():
        hl_ref[...] = st_ref[...].T


def _ssd_scan(xbc, dt, a_log_pad, expand, h0, slot, *, nseq, nc, chunk0, name):
    q = SSD_CHUNK
    zero_init = h0 is None

    def chunk(b, d, c):
        return chunk0 + b * nc + c + d * (nc - 1 - 2 * c)

    def ochunk(b, d, c):
        return b * nc + c + d * (nc - 1 - 2 * c)

    in_specs = [
        pl.BlockSpec((q, SSD_INNER), lambda b, d, c: (chunk(b, d, c), 0)),
        pl.BlockSpec((q, SSD_BC), lambda b, d, c: (chunk(b, d, c), SSD_INNER // SSD_BC)),
        pl.BlockSpec((q, SSD_BC), lambda b, d, c: (chunk(b, d, c), SSD_INNER // SSD_BC + 1)),
        pl.BlockSpec((q, LANES), lambda b, d, c: (chunk(b, d, c), d)),
        pl.BlockSpec((None, 1, LANES), lambda b, d, c: (d, 0, 0)),
        pl.BlockSpec((LANES, SSD_INNER), lambda b, d, c: (0, 0)),
    ]
    args = [xbc, xbc, xbc, dt, a_log_pad, expand]
    if not zero_init:
        in_specs.append(pl.BlockSpec((None, None, None, SSD_INNER, SSD_STATE),
                                     lambda b, d, c: (b, slot, d, 0, 0)))
        args.append(h0)
    return pl.pallas_call(
        functools.partial(_scan_kernel, nc=nc, zero_init=zero_init),
        grid=(nseq, 2, nc),
        in_specs=in_specs,
        out_specs=[
            pl.BlockSpec((None, q, SSD_INNER), lambda b, d, c: (d, ochunk(b, d, c), 0)),
            pl.BlockSpec((None, None, SSD_INNER, SSD_STATE), lambda b, d, c: (b, d, 0, 0)),
        ],
        out_shape=[
            jax.ShapeDtypeStruct((2, nseq * nc * q, SSD_INNER), F32),
            jax.ShapeDtypeStruct((nseq, 2, SSD_INNER, SSD_STATE), F32),
        ],
        scratch_shapes=[pltpu.VMEM((SSD_STATE, SSD_INNER), F32)],
        compiler_params=_params("parallel", "arbitrary", "arbitrary"),
        name=name,
    )(*args)


def _ssd_out_kernel(yfc_ref, ybc_ref, yfl_ref, ybl_ref, xs_ref, z_ref, dx_ref, ng_ref, w_ref,
                    x_ref, gate_ref, o_ref, *, n_ctx_blocks):
    is_ctx = pl.program_id(0) < n_ctx_blocks
    yf = jnp.where(is_ctx, yfc_ref[...], yfl_ref[...])
    yb = jnp.where(is_ctx, ybc_ref[...], ybl_ref[...])
    y = yf + yb + dx_ref[...] * xs_ref[...]
    y = y * _silu(z_ref[...])
    gw = SSD_INNER // SSD_GROUPS
    parts = []
    for g in range(SSD_GROUPS):
        yg = y[:, g * gw:(g + 1) * gw]
        parts.append(yg * lax.rsqrt(jnp.mean(yg * yg, axis=-1, keepdims=True) + EPS))
    yn = (jnp.concatenate(parts, axis=1) * ng_ref[...]).astype(BF16)
    o_ref[...] = x_ref[...] + gate_ref[...] * _dot(yn, w_ref[...])


def _ssd_out(y_ctx, y_lat, xbc, proj, d_x, norm_g, out_w, x, mod3, layer):
    tm = 256
    yblock = (None, tm, SSD_INNER)
    fwd = _two_source_specs(yblock, tm, (0,))
    bwd = _two_source_specs(yblock, tm, (1,))
    return pl.pallas_call(
        functools.partial(_ssd_out_kernel, n_ctx_blocks=N_CTX // tm),
        grid=(N_TOK // tm,),
        in_specs=[
            fwd[0], bwd[0], fwd[1], bwd[1],
            pl.BlockSpec((tm, SSD_INNER), lambda i: (i, 0)),
            pl.BlockSpec((tm, SSD_INNER), lambda i: (i, 0)),
            pl.BlockSpec((1, SSD_INNER), lambda i: (0, 0)),
            pl.BlockSpec((1, SSD_INNER), lambda i: (0, 0)),
            pl.BlockSpec((SSD_INNER, D_MODEL), lambda i: (0, 0)),
            pl.BlockSpec((tm, D_MODEL), lambda i: (i, 0)),
            _mod_spec(layer, 2, tm, 1),
        ],
        out_specs=pl.BlockSpec((tm, D_MODEL), lambda i: (i, 0)),
        out_shape=jax.ShapeDtypeStruct((N_TOK, D_MODEL), F32),
        compiler_params=_params("parallel"),
        name="ssd_out",
    )(y_ctx, y_ctx, y_lat, y_lat, xbc, proj, d_x, norm_g.reshape(1, SSD_INNER), out_w, x, mod3)


def _attn_kernel(*refs, has_cache, lam_init):
    if has_cache:
        q_ref, k_ref, v_ref, kc_ref, vc_ref, lam_ref, g_ref, o_ref = refs
    else:
        q_ref, k_ref, v_ref, lam_ref, g_ref, o_ref = refs
    lv = lam_ref[...]
    lam = (jnp.exp(jnp.sum(lv[0:1] * lv[1:2], axis=-1, keepdims=True))
           - jnp.exp(jnp.sum(lv[2:3] * lv[3:4], axis=-1, keepdims=True)) + lam_init)

    q = q_ref[...] * DA_SCALE
    lane = lax.broadcasted_iota(jnp.int32, q.shape, 1)
    q1 = jnp.where(lane < DA_HEAD_DIM, q, 0.0).astype(BF16)
    q2 = jnp.where(lane >= DA_HEAD_DIM, q, 0.0).astype(BF16)
    keys = [k_ref[...].astype(BF16)]
    vals = [v_ref[...].astype(BF16)]
    if has_cache:
        keys.append(kc_ref[...].astype(BF16))
        vals.append(vc_ref[...].astype(BF16))

    probs = []
    for qm in (q1, q2):
        s = [_dot_nt(qm, k) for k in keys]
        m = functools.reduce(jnp.maximum, [jnp.max(t, axis=-1, keepdims=True) for t in s])
        p = [jnp.exp(t - m) for t in s]
        l = functools.reduce(jnp.add, [jnp.sum(t, axis=-1, keepdims=True) for t in p])
        probs.append((p, 1.0 / l))
    (p1, r1), (p2, r2) = probs
    r2 = lam * r2
    o = None
    for a, b, v in zip(p1, p2, vals):
        w = (a * r1 - b * r2).astype(BF16)
        t = _dot(w, v)
        o = t if o is None else o + t
    o = o * lax.rsqrt(jnp.mean(o * o, axis=-1, keepdims=True) + EPS)
    o_ref[...] = (o * g_ref[...] * (1.0 - lam_init)).astype(o_ref.dtype)


def _attention(qkv, cache_k4, cache_v4, slot, lam_vecs, subln_g, lam_init):
    kcol, vcol = DA_HEADS, 2 * DA_HEADS
    common = [pl.BlockSpec((4, DA_HEAD_DIM), lambda *_: (0, 0)),
              pl.BlockSpec((1, HEAD_W), lambda *_: (0, 0))]
    g2 = subln_g.reshape(1, HEAD_W)

    ctx = pl.pallas_call(
        functools.partial(_attn_kernel, has_cache=False, lam_init=lam_init),
        grid=(BATCH, DA_HEADS),
        in_specs=[
            pl.BlockSpec((SEQ, HEAD_W), lambda b, h: (b, h)),
            pl.BlockSpec((SEQ, HEAD_W), lambda b, h: (b, kcol + h)),
            pl.BlockSpec((SEQ, HEAD_W), lambda b, h: (b, vcol + h)),
        ] + common,
        out_specs=pl.BlockSpec((SEQ, HEAD_W), lambda b, h: (b, h)),
        out_shape=jax.ShapeDtypeStruct((N_CTX, D_MODEL), BF16),
        compiler_params=_params("parallel", "parallel"),
        name="attn_ctx",
    )(qkv, qkv, qkv, lam_vecs, g2)

    tq = 256
    nq = DEC_SEQ // tq
    q0 = N_CTX // tq
    s0 = N_CTX // DEC_SEQ
    lat = pl.pallas_call(
        functools.partial(_attn_kernel, has_cache=True, lam_init=lam_init),
        grid=(DEC_BATCH, DA_HEADS, nq),
        in_specs=[
            pl.BlockSpec((tq, HEAD_W), lambda b, h, i: (q0 + b * nq + i, h)),
            pl.BlockSpec((DEC_SEQ, HEAD_W), lambda b, h, i: (s0 + b, kcol + h)),
            pl.BlockSpec((DEC_SEQ, HEAD_W), lambda b, h, i: (s0 + b, vcol + h)),
            pl.BlockSpec((None, None, PAST_LEN, HEAD_W), lambda b, h, i: (b, slot, 0, h)),
            pl.BlockSpec((None, None, PAST_LEN, HEAD_W), lambda b, h, i: (b, slot, 0, h)),
        ] + common,
        out_specs=pl.BlockSpec((tq, HEAD_W), lambda b, h, i: (b * nq + i, h)),
        out_shape=jax.ShapeDtypeStruct((N_LAT, D_MODEL), BF16),
        compiler_params=_params("parallel", "parallel", "parallel"),
        name="attn_lat",
    )(qkv, qkv, qkv, cache_k4, cache_v4, lam_vecs, g2)
    return ctx, lat


def _final_kernel(x_ref, g_ref, o_ref):
    x = x_ref[...]
    o_ref[...] = x * lax.rsqrt(jnp.mean(x * x, axis=-1, keepdims=True) + EPS) * g_ref[...]


def _final_norm(x, g):
    tm = 1024
    return pl.pallas_call(
        _final_kernel,
        grid=(N_TOK // tm,),
        in_specs=[pl.BlockSpec((tm, D_MODEL), lambda i: (i, 0)),
                  pl.BlockSpec((1, D_MODEL), lambda i: (0, 0))],
        out_specs=pl.BlockSpec((tm, D_MODEL), lambda i: (i, 0)),
        out_shape=jax.ShapeDtypeStruct((N_TOK, D_MODEL), F32),
        compiler_params=_params("parallel"),
        name="final_norm",
    )(x, g.reshape(1, D_MODEL))


def _rope_token_tables():
    rows = DEC_SEQ // GRID_W
    row = jnp.repeat(jnp.arange(rows, dtype=F32), GRID_W)
    col = jnp.tile(jnp.arange(GRID_W, dtype=F32), rows)
    inv = 1.0 / (ROPE_BASE ** (jnp.arange(0, ROPE_AXIS_DIM, 2, dtype=F32) / ROPE_AXIS_DIM))
    ang_r = row[:, None] * inv
    ang_c = col[:, None] * inv
    cr, sr, cc, sc = jnp.cos(ang_r), jnp.sin(ang_r), jnp.cos(ang_c), jnp.sin(ang_c)
    cos64 = jnp.concatenate([cr, cr, cc, cc], axis=1)
    sin64 = jnp.concatenate([-sr, sr, -sc, sc], axis=1)
    cos_lat = jnp.tile(cos64, (DEC_BATCH, 2))
    sin_lat = jnp.tile(sin64, (DEC_BATCH, 2))
    cos = jnp.concatenate([jnp.ones((N_CTX, HEAD_W), F32), cos_lat], axis=0)
    sin = jnp.concatenate([jnp.zeros((N_CTX, HEAD_W), F32), sin_lat], axis=0)
    return cos, sin


def _lambda_init(layer):
    return 0.8 - 0.6 * math.exp(-0.3 * layer)


def kernel(x_prompt, x_sample, state_ssd, cache_k, cache_v, c, c_ctx, mod_w, mod_b, norm_mix_g, norm_ffn_g, ssd_in_w, ssd_conv_w, ssd_conv_b, ssd_dt_bias, ssd_a_log, ssd_d, ssd_norm_g, ssd_out_w, att_qkv_w, att_lambda, att_subln_g, att_out_w, ffn_up_w, ffn_conv_w, ffn_conv_b, ffn_down_w, final_norm_g):
    n_ssd = ssd_in_w.shape[0]
    n_att = att_qkv_w.shape[0]
    x = jnp.concatenate([x_prompt.reshape(N_CTX, D_MODEL), x_sample.reshape(N_LAT, D_MODEL)], axis=0)

    cpad = jnp.concatenate([c_ctx[None], c, jnp.zeros((MOD_ROWS - 1 - DEC_BATCH, D_MODEL), F32)], axis=0)
    mod = _modulation(cpad, mod_w, mod_b)
    mod3 = mod.reshape(DEPTH * MOD_ROWS * 6, 1, D_MODEL)

    cos_t, sin_t = _rope_token_tables()
    expand = (jnp.arange(LANES)[:, None] == (jnp.arange(SSD_INNER)[None, :] // SSD_HEAD_DIM)).astype(BF16)
    h0_lat = state_ssd.reshape(DEC_BATCH, n_ssd, 2, SSD_INNER, SSD_STATE)
    cache_k4 = cache_k.reshape(DEC_BATCH, n_att, PAST_LEN, D_MODEL)
    cache_v4 = cache_v.reshape(DEC_BATCH, n_att, PAST_LEN, D_MODEL)

    new_ssd, new_k, new_v = [], [], []
    for i in range(DEPTH):
        slot = i // 2
        if i % 2 == 0:
            in_w = ssd_in_w[slot]
            w_main = in_w[:, :SSD_MAIN].astype(BF16)
            pad = jnp.zeros((D_MODEL, LANES - SSD_HEADS), F32)
            w_dt = jnp.concatenate([in_w[:, SSD_MAIN:SSD_MAIN + SSD_HEADS], pad,
                                    in_w[:, SSD_MAIN + SSD_HEADS:], pad], axis=1).astype(BF16)
            zpad = jnp.zeros((LANES - SSD_HEADS,), F32)
            dt_bias = jnp.concatenate([ssd_dt_bias[slot, 0], zpad, ssd_dt_bias[slot, 1], zpad]).reshape(1, 2 * LANES)
            a_log_pad = jnp.concatenate([ssd_a_log[slot], jnp.zeros((2, LANES - SSD_HEADS), F32)],
                                        axis=1).reshape(2, 1, LANES)
            proj = _norm_mod_matmul(x, mod3, i, 0, 1, norm_mix_g[i], w_main, tn=1280, name="ssd_in")
            dt = _norm_mod_matmul(x, mod3, i, 0, 1, norm_mix_g[i], w_dt, tn=2 * LANES,
                                  epilogue="softplus", extra=(dt_bias,), name="ssd_dt")
            xbc = _ssd_conv(proj, ssd_conv_w[slot], ssd_conv_b[slot])
            y_ctx, st_ctx = _ssd_scan(xbc, dt, a_log_pad, expand, None, slot,
                                      nseq=BATCH, nc=SEQ // SSD_CHUNK, chunk0=0, name="ssd_scan_ctx")
            y_lat, _ = _ssd_scan(xbc, dt, a_log_pad, expand, h0_lat, slot,
                                 nseq=DEC_BATCH, nc=DEC_SEQ // SSD_CHUNK, chunk0=N_CTX // SSD_CHUNK,
                                 name="ssd_scan_lat")
            new_ssd.append(st_ctx.reshape(BATCH, 2, SSD_HEADS, SSD_HEAD_DIM, SSD_STATE))
            d_x = jnp.repeat(ssd_d[slot], SSD_HEAD_DIM).reshape(1, SSD_INNER)
            x = _ssd_out(y_ctx, y_lat, xbc, proj, d_x, ssd_norm_g[slot], ssd_out_w[slot].astype(BF16), x, mod3, i)
        else:
            lam_init = _lambda_init(i)
            qkv = _norm_mod_matmul(x, mod3, i, 0, 1, norm_mix_g[i], att_qkv_w[slot].astype(BF16), tn=1024,
                                   epilogue="rope", extra=(cos_t, sin_t),
                                   n_rope_blocks=2 * D_MODEL // 1024, name="att_qkv")
            new_k.append(qkv[:N_CTX, D_MODEL:2 * D_MODEL].reshape(BATCH, SEQ, DA_HEADS, 2, DA_HEAD_DIM))
            new_v.append(qkv[:N_CTX, 2 * D_MODEL:].reshape(BATCH, SEQ, DA_HEADS, HEAD_W))
            o_ctx, o_lat = _attention(qkv, cache_k4, cache_v4, slot, att_lambda[slot], att_subln_g[slot], lam_init)
            x = _matmul_residual(o_ctx, o_lat, att_out_w[slot].astype(BF16), x, mod3, i, 2, name="att_out")
        x = _ffn(x, mod3, i, norm_ffn_g[i], ffn_up_w[i].astype(BF16), ffn_conv_w[i], ffn_conv_b[i],
                 ffn_down_w[i].astype(BF16))

    y = _final_norm(x, final_norm_g)
    y_prompt = y[:N_CTX].reshape(BATCH, SEQ, D_MODEL)
    y_sample = y[N_CTX:].reshape(DEC_BATCH, DEC_SEQ, D_MODEL)
    return (y_prompt, y_sample, jnp.stack(new_ssd, axis=1), jnp.stack(new_k, axis=1), jnp.stack(new_v, axis=1))
```

```python
import functools
import math

import jax
import jax.numpy as jnp
from jax import lax
from jax.experimental import pallas as pl
from jax.experimental.pallas import tpu as pltpu

F32 = jnp.float32
BF16 = jnp.bfloat16

D_MODEL = 1024
BATCH = 16
SEQ = 256
DEPTH = 4
DEC_BATCH = 4
DEC_SEQ = 2048
PAST_LEN = 256
GRID_W = 64
EPS = 1e-6

SSD_INNER = 2048
SSD_HEAD_DIM = 64
SSD_HEADS = 32
SSD_GROUPS = 4
SSD_STATE = 128
SSD_CHUNK = 128
SSD_BC = SSD_GROUPS * SSD_STATE
SSD_CONV_CH = SSD_INNER + 2 * SSD_BC
SSD_MAIN = SSD_INNER + SSD_CONV_CH
HEADS_PER_GROUP = SSD_HEADS // SSD_GROUPS
GROUP_W = HEADS_PER_GROUP * SSD_HEAD_DIM

DA_HEAD_DIM = 64
DA_HEADS = 8
DA_SCALE = DA_HEAD_DIM ** -0.5
HEAD_W = 2 * DA_HEAD_DIM
ROPE_BASE = 10000.0
ROPE_AXIS_DIM = DA_HEAD_DIM // 2

FFN_HIDDEN = 2816

N_CTX = BATCH * SEQ
N_LAT = DEC_BATCH * DEC_SEQ
N_TOK = N_CTX + N_LAT
MOD_ROWS = 8
LANES = 128
SUBLANES = 8
VMEM_LIMIT = 48 * 1024 * 1024
SCAN_VMEM_LIMIT = 56 * 1024 * 1024


def _params(*sem):
    return pltpu.CompilerParams(dimension_semantics=sem, vmem_limit_bytes=VMEM_LIMIT)


def _mod_row(i, tm):
    nctx = N_CTX // tm
    return jnp.where(i < nctx, 0, 1 + (i - nctx) // (DEC_SEQ // tm))


def _silu(x):
    return x * jax.nn.sigmoid(x)


def _softplus(x):
    return jnp.maximum(x, 0.0) + jnp.log1p(jnp.exp(-jnp.abs(x)))


def _split3(x):
    hi = x.astype(BF16)
    r1 = x - hi.astype(F32)
    mid = r1.astype(BF16)
    lo = (r1 - mid.astype(F32)).astype(BF16)
    return hi, mid, lo


def _dot(a, b):
    return jnp.dot(a, b, preferred_element_type=F32)


def _dot_nt(a, b):
    return lax.dot_general(a, b, (((1,), (1,)), ((), ())), preferred_element_type=F32)


def _dot3(x, sel):
    hi, mid, lo = _split3(x)
    return _dot(hi, sel) + _dot(mid, sel) + _dot(lo, sel)


def _norm_mod(x, g, sc, sh):
    r = x * lax.rsqrt(jnp.mean(x * x, axis=-1, keepdims=True) + EPS)
    return ((r * g) * (1.0 + sc) + sh).astype(BF16)


def _mod_kernel(c_ref, w_ref, b_ref, o_ref):
    s = _silu(c_ref[...]).astype(BF16)
    o_ref[...] = _dot(s, w_ref[...].astype(BF16)) + b_ref[...]


def _modulation(cpad, mod_w, mod_b):
    tn = 1024
    n = 6 * D_MODEL
    return pl.pallas_call(
        _mod_kernel,
        grid=(DEPTH, n // tn),
        in_specs=[
            pl.BlockSpec((MOD_ROWS, D_MODEL), lambda l, j: (0, 0)),
            pl.BlockSpec((None, D_MODEL, tn), lambda l, j: (l, 0, j)),
            pl.BlockSpec((None, 1, tn), lambda l, j: (l, 0, j)),
        ],
        out_specs=pl.BlockSpec((None, MOD_ROWS, tn), lambda l, j: (l, 0, j)),
        out_shape=jax.ShapeDtypeStruct((DEPTH, MOD_ROWS, n), F32),
        compiler_params=_params("parallel", "parallel"),
        name="modulation",
    )(cpad, mod_w, mod_b.reshape(DEPTH, 1, n))


def _mod_spec(layer, k, tm, ngrid):
    base = layer * MOD_ROWS * 6 + k
    if ngrid == 1:
        return pl.BlockSpec((None, 1, D_MODEL), lambda i: (base + 6 * _mod_row(i, tm), 0, 0))
    return pl.BlockSpec((None, 1, D_MODEL), lambda i, j: (base + 6 * _mod_row(i, tm), 0, 0))


def _nmm_kernel(*refs, epilogue, n_rope_blocks):
    if epilogue == "rope":
        x_ref, g_ref, sh_ref, sc_ref, w_ref, cos_ref, sin_ref, o_ref, h_ref = refs
    else:
        x_ref, g_ref, sh_ref, sc_ref, w_ref, o_ref, h_ref = refs
    j = pl.program_id(1)

    @pl.when(j == 0)
    def _():
        h_ref[...] = _norm_mod(x_ref[...], g_ref[...], sc_ref[...], sh_ref[...])

    u = _dot(h_ref[...], w_ref[...])
    if epilogue == "rope":
        @pl.when(j < n_rope_blocks)
        def _():
            cos = cos_ref[...]
            sin = sin_ref[...]
            lane = lax.broadcasted_iota(jnp.int32, cos.shape, 1)
            half = ROPE_AXIS_DIM // 2
            first = (lane % ROPE_AXIS_DIM) < half
            for k in range(u.shape[1] // HEAD_W):
                cols = slice(k * HEAD_W, (k + 1) * HEAD_W)
                uk = u[:, cols]
                partner = jnp.where(first, pltpu.roll(uk, HEAD_W - half, 1), pltpu.roll(uk, half, 1))
                o_ref[:, cols] = uk * cos + partner * sin

        @pl.when(j >= n_rope_blocks)
        def _():
            o_ref[...] = u
    else:
        o_ref[...] = u.astype(o_ref.dtype)


def _norm_mod_matmul(x, mod3, layer, kshift, kscale, g, w, *, tn, out_dtype=F32,
                     epilogue="none", extra=(), n_rope_blocks=0, name):
    tm = 1024
    n = w.shape[1]
    in_specs = [
        pl.BlockSpec((tm, D_MODEL), lambda i, j: (i, 0)),
        pl.BlockSpec((1, D_MODEL), lambda i, j: (0, 0)),
        _mod_spec(layer, kshift, tm, 2),
        _mod_spec(layer, kscale, tm, 2),
        pl.BlockSpec((D_MODEL, tn), lambda i, j: (0, j)),
    ]
    if epilogue == "rope":
        in_specs += [pl.BlockSpec((tm, HEAD_W), lambda i, j: (i, 0))] * 2
    return pl.pallas_call(
        functools.partial(_nmm_kernel, epilogue=epilogue, n_rope_blocks=n_rope_blocks),
        grid=(N_TOK // tm, n // tn),
        in_specs=in_specs,
        out_specs=pl.BlockSpec((tm, tn), lambda i, j: (i, j)),
        out_shape=jax.ShapeDtypeStruct((N_TOK, n), out_dtype),
        scratch_shapes=[pltpu.VMEM((tm, D_MODEL), BF16)],
        compiler_params=_params("parallel", "arbitrary"),
        name=name,
    )(x, g.reshape(1, D_MODEL), mod3, mod3, w, *extra)


PAD = SUBLANES


def _block_seqlen(i, tm):
    return jnp.where(i < N_CTX // tm, SEQ, DEC_SEQ)


def _project_with_halo(h, hh, w, u_ref, tm):
    u_ref[PAD:PAD + tm, :] = _dot(h, w)
    uh = _dot(hh, w)
    u_ref[PAD - 1:PAD, :] = uh[SUBLANES - 1:SUBLANES, :]
    u_ref[PAD + tm:PAD + tm + 1, :] = uh[SUBLANES:SUBLANES + 1, :]


def _conv_chunks(u_refs, cws, cbs, row0, seqmask, tm, rows, emit):
    ridx = lax.broadcasted_iota(jnp.int32, (rows, 1), 0)
    for r0 in range(0, tm, rows):
        outs = []
        for u_ref, cw, cb in zip(u_refs, cws, cbs):
            prev = u_ref[pl.ds(PAD - 1 + r0, rows), :]
            cur = u_ref[pl.ds(PAD + r0, rows), :]
            nxt = u_ref[pl.ds(PAD + 1 + r0, rows), :]
            if r0 % SEQ == 0:
                starts = ((row0 + r0) & seqmask) == 0
                prev = jnp.where(jnp.logical_and(ridx == 0, starts), 0.0, prev)
            if (r0 + rows) % SEQ == 0:
                ends = ((row0 + r0 + rows) & seqmask) == 0
                nxt = jnp.where(jnp.logical_and(ridx == rows - 1, ends), 0.0, nxt)
            outs.append(prev * cw[0:1, :] + cur * cw[1:2, :] + nxt * cw[2:3, :] + cb)
        emit(r0, outs)


def _halo_x_specs(tm):
    per = tm // SUBLANES
    last = N_TOK // SUBLANES - 1
    return [
        pl.BlockSpec((tm, D_MODEL), lambda i, j: (i, 0)),
        pl.BlockSpec((SUBLANES, D_MODEL), lambda i, j: (jnp.maximum(i * per - 1, 0), 0)),
        pl.BlockSpec((SUBLANES, D_MODEL), lambda i, j: (jnp.minimum((i + 1) * per, last), 0)),
    ]


def _xbc_kernel(x_ref, xp_ref, xn_ref, g_ref, sh_ref, sc_ref, w_ref, cw_ref, cb_ref, wdt_ref, bdt_ref,
                o_ref, dt_ref, h_ref, hh_ref, u_ref, *, tm, rows):
    i = pl.program_id(0)

    @pl.when(pl.program_id(1) == 0)
    def _():
        g, sc, sh = g_ref[...], sc_ref[...], sh_ref[...]
        h_ref[...] = _norm_mod(x_ref[...], g, sc, sh)
        hh_ref[...] = _norm_mod(jnp.concatenate([xp_ref[...], xn_ref[...]], axis=0), g, sc, sh)
        dt_ref[...] = _softplus(_dot(h_ref[...], wdt_ref[...]) + bdt_ref[...])

    _project_with_halo(h_ref[...], hh_ref[...], w_ref[...], u_ref, tm)

    def emit(r0, outs):
        o_ref[pl.ds(r0, rows), :] = _silu(outs[0])

    _conv_chunks([u_ref], [cw_ref[...]], [cb_ref[...]], i * tm, _block_seqlen(i, tm) - 1, tm, rows, emit)


def _ssd_xbc(x, mod3, layer, g, w_xbc, conv_w, conv_b, w_dt, dt_bias):
    tm, tc, rows = 1024, 512, 32
    ndt = w_dt.shape[1]
    return pl.pallas_call(
        functools.partial(_xbc_kernel, tm=tm, rows=rows),
        grid=(N_TOK // tm, SSD_CONV_CH // tc),
        in_specs=_halo_x_specs(tm) + [
            pl.BlockSpec((1, D_MODEL), lambda i, j: (0, 0)),
            _mod_spec(layer, 0, tm, 2),
            _mod_spec(layer, 1, tm, 2),
            pl.BlockSpec((D_MODEL, tc), lambda i, j: (0, j)),
            pl.BlockSpec((3, tc), lambda i, j: (0, j)),
            pl.BlockSpec((1, tc), lambda i, j: (0, j)),
            pl.BlockSpec((D_MODEL, ndt), lambda i, j: (0, 0)),
            pl.BlockSpec((1, ndt), lambda i, j: (0, 0)),
        ],
        out_specs=[
            pl.BlockSpec((tm, tc), lambda i, j: (i, j)),
            pl.BlockSpec((tm, ndt), lambda i, j: (i, 0)),
        ],
        out_shape=[
            jax.ShapeDtypeStruct((N_TOK, SSD_CONV_CH), F32),
            jax.ShapeDtypeStruct((N_TOK, ndt), F32),
        ],
        scratch_shapes=[pltpu.VMEM((tm, D_MODEL), BF16), pltpu.VMEM((2 * SUBLANES, D_MODEL), BF16),
                        pltpu.VMEM((tm + 2 * SUBLANES, tc), F32)],
        compiler_params=_params("parallel", "arbitrary"),
        name="ssd_xbc",
    )(x, x, x, g.reshape(1, D_MODEL), mod3, mod3, w_xbc, conv_w, conv_b.reshape(1, SSD_CONV_CH), w_dt, dt_bias)


def _ffn_kernel(x_ref, xp_ref, xn_ref, g_ref, sh_ref, sc_ref, gate_ref, wa_ref, wv_ref,
                cwa_ref, cba_ref, cwv_ref, cbv_ref, wd_ref, o_ref,
                h_ref, hh_ref, ua_ref, uv_ref, act_ref, *, tm, th, rows):
    i = pl.program_id(0)
    j = pl.program_id(1)

    @pl.when(j == 0)
    def _():
        g, sc, sh = g_ref[...], sc_ref[...], sh_ref[...]
        h_ref[...] = _norm_mod(x_ref[...], g, sc, sh)
        hh_ref[...] = _norm_mod(jnp.concatenate([xp_ref[...], xn_ref[...]], axis=0), g, sc, sh)

    h = h_ref[...]
    hh = hh_ref[...]
    _project_with_halo(h, hh, wa_ref[...], ua_ref, tm)
    _project_with_halo(h, hh, wv_ref[...], uv_ref, tm)

    col0 = pl.multiple_of(j * th, th)

    def emit(r0, outs):
        act_ref[pl.ds(r0, rows), pl.ds(col0, th)] = (_silu(outs[0]) * outs[1]).astype(BF16)

    _conv_chunks([ua_ref, uv_ref], [cwa_ref[...], cwv_ref[...]], [cba_ref[...], cbv_ref[...]],
                 i * tm, _block_seqlen(i, tm) - 1, tm, rows, emit)

    @pl.when(j == pl.num_programs(1) - 1)
    def _():
        o_ref[...] = x_ref[...] + gate_ref[...] * _dot(act_ref[...], wd_ref[...])


def _ffn(x, mod3, layer, g, up_w, conv_w, conv_b, down_w):
    tm, th, rows = 1024, 256, 64
    nb = FFN_HIDDEN // th
    cb = conv_b.reshape(1, 2 * FFN_HIDDEN)
    return pl.pallas_call(
        functools.partial(_ffn_kernel, tm=tm, th=th, rows=rows),
        grid=(N_TOK // tm, nb),
        in_specs=_halo_x_specs(tm) + [
            pl.BlockSpec((1, D_MODEL), lambda i, j: (0, 0)),
            _mod_spec(layer, 3, tm, 2),
            _mod_spec(layer, 4, tm, 2),
            _mod_spec(layer, 5, tm, 2),
            pl.BlockSpec((D_MODEL, th), lambda i, j: (0, j)),
            pl.BlockSpec((D_MODEL, th), lambda i, j: (0, j + nb)),
            pl.BlockSpec((3, th), lambda i, j: (0, j)),
            pl.BlockSpec((1, th), lambda i, j: (0, j)),
            pl.BlockSpec((3, th), lambda i, j: (0, j + nb)),
            pl.BlockSpec((1, th), lambda i, j: (0, j + nb)),
            pl.BlockSpec((FFN_HIDDEN, D_MODEL), lambda i, j: (0, 0)),
        ],
        out_specs=pl.BlockSpec((tm, D_MODEL), lambda i, j: (i, 0)),
        out_shape=jax.ShapeDtypeStruct((N_TOK, D_MODEL), F32),
        scratch_shapes=[pltpu.VMEM((tm, D_MODEL), BF16), pltpu.VMEM((2 * SUBLANES, D_MODEL), BF16),
                        pltpu.VMEM((tm + 2 * SUBLANES, th), F32), pltpu.VMEM((tm + 2 * SUBLANES, th), F32),
                        pltpu.VMEM((tm, FFN_HIDDEN), BF16)],
        compiler_params=_params("parallel", "arbitrary"),
        name="ffn",
    )(x, x, x, g.reshape(1, D_MODEL), mod3, mod3, mod3, up_w, up_w, conv_w, cb, conv_w, cb, down_w)


def _mm_res_kernel(ac_ref, al_ref, w_ref, x_ref, gate_ref, o_ref, *, n_ctx_blocks):
    a = jnp.where(pl.program_id(0) < n_ctx_blocks, ac_ref[...], al_ref[...])
    o_ref[...] = x_ref[...] + gate_ref[...] * _dot(a, w_ref[...])


def _two_source_specs(block, tm):
    nctx = N_CTX // tm
    return [pl.BlockSpec(block, lambda i: (jnp.minimum(i, nctx - 1), 0)),
            pl.BlockSpec(block, lambda i: (jnp.maximum(i - nctx, 0), 0))]


def _matmul_residual(a_ctx, a_lat, w, x, mod3, layer, kgate, *, name):
    tm = 512
    k = w.shape[0]
    return pl.pallas_call(
        functools.partial(_mm_res_kernel, n_ctx_blocks=N_CTX // tm),
        grid=(N_TOK // tm,),
        in_specs=_two_source_specs((tm, k), tm) + [
            pl.BlockSpec((k, D_MODEL), lambda i: (0, 0)),
            pl.BlockSpec((tm, D_MODEL), lambda i: (i, 0)),
            _mod_spec(layer, kgate, tm, 1),
        ],
        out_specs=pl.BlockSpec((tm, D_MODEL), lambda i: (i, 0)),
        out_shape=jax.ShapeDtypeStruct((N_TOK, D_MODEL), F32),
        compiler_params=_params("parallel"),
        name=name,
    )(a_ctx, a_lat, w, x, mod3)


def _scan_kernel(*refs, nc, zero_init):
    if zero_init:
        x_ref, b_ref, c_ref, dt_ref, alog_ref, e_ref, dx_ref, y_ref, hl_ref, st_ref, yt_ref = refs
    else:
        x_ref, b_ref, c_ref, dt_ref, alog_ref, e_ref, dx_ref, h0_ref, y_ref, hl_ref, st_ref, yt_ref = refs
    d = pl.program_id(1)
    c = pl.program_id(2)
    q = SSD_CHUNK

    @pl.when(c == 0)
    def _():
        if zero_init:
            st_ref[...] = jnp.zeros_like(st_ref)
        else:
            st_ref[...] = h0_ref[...].T

    ii = lax.broadcasted_iota(jnp.int32, (q, q), 0)
    jj = lax.broadcasted_iota(jnp.int32, (q, q), 1)
    tri = ((ii - jj) * (1 - 2 * d)) >= 0
    tri_b = jnp.where(tri, 1.0, 0.0).astype(BF16)

    dtb = dt_ref[...]
    da = dtb * (-jnp.exp(alog_ref[...]))
    hi, mid, lo = _split3(da)
    cs = _dot(tri_b, hi) + _dot(tri_b, mid) + _dot(tri_b, lo)
    cs_t = cs.T
    e = e_ref[...]
    dt_x = _dot3(dtb, e)
    cs_x = _dot3(cs, e)
    tot_x = jnp.where(d == 0, cs_x[q - 1:q, :], cs_x[0:1, :])

    xs = x_ref[...]
    xd = xs * dt_x
    xw = (xd * jnp.exp(tot_x - cs_x)).astype(BF16)
    dec_out = jnp.exp(cs_x)
    dec_chunk = jnp.exp(tot_x)
    lane = lax.broadcasted_iota(jnp.int32, (q, LANES), 1)

    for g in range(SSD_GROUPS):
        gs = slice(g * GROUP_W, (g + 1) * GROUP_W)
        bg = b_ref[:, g * SSD_STATE:(g + 1) * SSD_STATE]
        cg = c_ref[:, g * SSD_STATE:(g + 1) * SSD_STATE].astype(BF16)
        cb = _dot_nt(cg, bg.astype(BF16))
        st_g = st_ref[:, gs]
        y_off = _dot(cg, st_g.astype(BF16))
        new_g = _dot(bg.T.astype(BF16), xw[:, gs])
        st_ref[:, gs] = st_g * dec_chunk[:, gs] + new_g
        for p in range(HEADS_PER_GROUP // 2):
            h = g * HEADS_PER_GROUP + 2 * p
            ms = []
            for hh in (h, h + 1):
                seg = cs[:, hh:hh + 1] - cs_t[hh:hh + 1, :]
                decay = jnp.where(tri, jnp.exp(jnp.minimum(seg, 0.0)), 0.0)
                ms.append((cb * decay).astype(BF16))
            lhs = jnp.concatenate(ms, axis=1)
            cols = slice(h * SSD_HEAD_DIM, (h + 2) * SSD_HEAD_DIM)
            xd2 = xd[:, cols]
            rhs = jnp.concatenate([jnp.where(lane < SSD_HEAD_DIM, xd2, 0.0),
                                   jnp.where(lane >= SSD_HEAD_DIM, xd2, 0.0)], axis=0).astype(BF16)
            loc = slice(2 * p * SSD_HEAD_DIM, (2 * p + 2) * SSD_HEAD_DIM)
            yt_ref[:, cols] = _dot(lhs, rhs) + y_off[:, loc] * dec_out[:, cols]

    r0 = pl.multiple_of((c + d * (nc - 1 - 2 * c)) * q, q)

    @pl.when(d == 0)
    def _():
        y_ref[pl.ds(r0, q), :] = yt_ref[...] + dx_ref[...] * xs

    @pl.when(d == 1)
    def _():
        y_ref[pl.ds(r0, q), :] += yt_ref[...]

    @pl.when(c == nc - 1)
    def _():
        hl_ref[...] = st_ref[...].T


def _ssd_scan(xbc, dt, a_log_pad, expand, d_x, h0, slot, *, nseq, nc, chunk0, name):
    q = SSD_CHUNK
    zero_init = h0 is None

    def chunk(b, d, c):
        return chunk0 + b * nc + c + d * (nc - 1 - 2 * c)

    in_specs = [
        pl.BlockSpec((q, SSD_INNER), lambda b, d, c: (chunk(b, d, c), 0)),
        pl.BlockSpec((q, SSD_BC), lambda b, d, c: (chunk(b, d, c), SSD_INNER // SSD_BC)),
        pl.BlockSpec((q, SSD_BC), lambda b, d, c: (chunk(b, d, c), SSD_INNER // SSD_BC + 1)),
        pl.BlockSpec((q, LANES), lambda b, d, c: (chunk(b, d, c), d)),
        pl.BlockSpec((None, 1, LANES), lambda b, d, c: (d, 0, 0)),
        pl.BlockSpec((LANES, SSD_INNER), lambda b, d, c: (0, 0)),
        pl.BlockSpec((1, SSD_INNER), lambda b, d, c: (0, 0)),
    ]
    args = [xbc, xbc, xbc, dt, a_log_pad, expand, d_x]
    if not zero_init:
        in_specs.append(pl.BlockSpec((None, None, None, SSD_INNER, SSD_STATE),
                                     lambda b, d, c: (b, slot, d, 0, 0)))
        args.append(h0)
    return pl.pallas_call(
        functools.partial(_scan_kernel, nc=nc, zero_init=zero_init),
        grid=(nseq, 2, nc),
        in_specs=in_specs,
        out_specs=[
            pl.BlockSpec((nc * q, SSD_INNER), lambda b, d, c: (b, 0)),
            pl.BlockSpec((None, None, SSD_INNER, SSD_STATE), lambda b, d, c: (b, d, 0, 0)),
        ],
        out_shape=[
            jax.ShapeDtypeStruct((nseq * nc * q, SSD_INNER), F32),
            jax.ShapeDtypeStruct((nseq, 2, SSD_INNER, SSD_STATE), F32),
        ],
        scratch_shapes=[pltpu.VMEM((SSD_STATE, SSD_INNER), F32), pltpu.VMEM((q, SSD_INNER), F32)],
        compiler_params=pltpu.CompilerParams(dimension_semantics=("parallel", "arbitrary", "arbitrary"),
                                             vmem_limit_bytes=SCAN_VMEM_LIMIT),
        name=name,
    )(*args)


def _ssd_out_kernel(yc_ref, yl_ref, z_ref, ng_ref, w_ref, x_ref, gate_ref, o_ref, *, n_ctx_blocks):
    y = jnp.where(pl.program_id(0) < n_ctx_blocks, yc_ref[...], yl_ref[...])
    y = y * _silu(z_ref[...])
    gw = SSD_INNER // SSD_GROUPS
    parts = []
    for g in range(SSD_GROUPS):
        yg = y[:, g * gw:(g + 1) * gw]
        parts.append(yg * lax.rsqrt(jnp.mean(yg * yg, axis=-1, keepdims=True) + EPS))
    yn = (jnp.concatenate(parts, axis=1) * ng_ref[...]).astype(BF16)
    o_ref[...] = x_ref[...] + gate_ref[...] * _dot(yn, w_ref[...])


def _ssd_out(y_ctx, y_lat, z, norm_g, out_w, x, mod3, layer):
    tm = 512
    return pl.pallas_call(
        functools.partial(_ssd_out_kernel, n_ctx_blocks=N_CTX // tm),
        grid=(N_TOK // tm,),
        in_specs=_two_source_specs((tm, SSD_INNER), tm) + [
            pl.BlockSpec((tm, SSD_INNER), lambda i: (i, 0)),
            pl.BlockSpec((1, SSD_INNER), lambda i: (0, 0)),
            pl.BlockSpec((SSD_INNER, D_MODEL), lambda i: (0, 0)),
            pl.BlockSpec((tm, D_MODEL), lambda i: (i, 0)),
            _mod_spec(layer, 2, tm, 1),
        ],
        out_specs=pl.BlockSpec((tm, D_MODEL), lambda i: (i, 0)),
        out_shape=jax.ShapeDtypeStruct((N_TOK, D_MODEL), F32),
        compiler_params=_params("parallel"),
        name="ssd_out",
    )(y_ctx, y_lat, z, norm_g.reshape(1, SSD_INNER), out_w, x, mod3)


def _attn_kernel(*refs, has_cache, lam_init):
    if has_cache:
        q_ref, k_ref, v_ref, kc_ref, vc_ref, lam_ref, g_ref, o_ref = refs
    else:
        q_ref, k_ref, v_ref, lam_ref, g_ref, o_ref = refs
    lv = lam_ref[...]
    lam = (jnp.exp(jnp.sum(lv[0:1] * lv[1:2], axis=-1, keepdims=True))
           - jnp.exp(jnp.sum(lv[2:3] * lv[3:4], axis=-1, keepdims=True)) + lam_init)

    q = q_ref[...] * DA_SCALE
    lane = lax.broadcasted_iota(jnp.int32, q.shape, 1)
    q1 = jnp.where(lane < DA_HEAD_DIM, q, 0.0).astype(BF16)
    q2 = jnp.where(lane >= DA_HEAD_DIM, q, 0.0).astype(BF16)
    keys = [k_ref[...].astype(BF16)]
    vals = [v_ref[...].astype(BF16)]
    if has_cache:
        keys.append(kc_ref[...].astype(BF16))
        vals.append(vc_ref[...].astype(BF16))

    probs = []
    for qm in (q1, q2):
        s = [_dot_nt(qm, k) for k in keys]
        m = functools.reduce(jnp.maximum, [jnp.max(t, axis=-1, keepdims=True) for t in s])
        p = [jnp.exp(t - m) for t in s]
        l = functools.reduce(jnp.add, [jnp.sum(t, axis=-1, keepdims=True) for t in p])
        probs.append((p, 1.0 / l))
    (p1, r1), (p2, r2) = probs
    r2 = lam * r2
    o = None
    for a, b, v in zip(p1, p2, vals):
        w = (a * r1 - b * r2).astype(BF16)
        t = _dot(w, v)
        o = t if o is None else o + t
    o = o * lax.rsqrt(jnp.mean(o * o, axis=-1, keepdims=True) + EPS)
    o_ref[...] = (o * g_ref[...] * (1.0 - lam_init)).astype(o_ref.dtype)


def _attention(qkv, cache_k4, cache_v4, slot, lam_vecs, subln_g, lam_init):
    kcol, vcol = DA_HEADS, 2 * DA_HEADS
    common = [pl.BlockSpec((4, DA_HEAD_DIM), lambda *_: (0, 0)),
              pl.BlockSpec((1, HEAD_W), lambda *_: (0, 0))]
    g2 = subln_g.reshape(1, HEAD_W)

    ctx = pl.pallas_call(
        functools.partial(_attn_kernel, has_cache=False, lam_init=lam_init),
        grid=(BATCH, DA_HEADS),
        in_specs=[
            pl.BlockSpec((SEQ, HEAD_W), lambda b, h: (b, h)),
            pl.BlockSpec((SEQ, HEAD_W), lambda b, h: (b, kcol + h)),
            pl.BlockSpec((SEQ, HEAD_W), lambda b, h: (b, vcol + h)),
        ] + common,
        out_specs=pl.BlockSpec((SEQ, HEAD_W), lambda b, h: (b, h)),
        out_shape=jax.ShapeDtypeStruct((N_CTX, D_MODEL), BF16),
        compiler_params=_params("parallel", "parallel"),
        name="attn_ctx",
    )(qkv, qkv, qkv, lam_vecs, g2)

    tq = 256
    nq = DEC_SEQ // tq
    q0 = N_CTX // tq
    s0 = N_CTX // DEC_SEQ
    lat = pl.pallas_call(
        functools.partial(_attn_kernel, has_cache=True, lam_init=lam_init),
        grid=(DEC_BATCH, DA_HEADS, nq),
        in_specs=[
            pl.BlockSpec((tq, HEAD_W), lambda b, h, i: (q0 + b * nq + i, h)),
            pl.BlockSpec((DEC_SEQ, HEAD_W), lambda b, h, i: (s0 + b, kcol + h)),
            pl.BlockSpec((DEC_SEQ, HEAD_W), lambda b, h, i: (s0 + b, vcol + h)),
            pl.BlockSpec((None, None, PAST_LEN, HEAD_W), lambda b, h, i: (b, slot, 0, h)),
            pl.BlockSpec((None, None, PAST_LEN, HEAD_W), lambda b, h, i: (b, slot, 0, h)),
        ] + common,
        out_specs=pl.BlockSpec((tq, HEAD_W), lambda b, h, i: (b * nq + i, h)),
        out_shape=jax.ShapeDtypeStruct((N_LAT, D_MODEL), BF16),
        compiler_params=_params("parallel", "parallel", "parallel"),
        name="attn_lat",
    )(qkv, qkv, qkv, cache_k4, cache_v4, lam_vecs, g2)
    return ctx, lat


def _final_kernel(x_ref, g_ref, o_ref):
    x = x_ref[...]
    o_ref[...] = x * lax.rsqrt(jnp.mean(x * x, axis=-1, keepdims=True) + EPS) * g_ref[...]


def _final_norm(x, g):
    tm = 1024
    return pl.pallas_call(
        _final_kernel,
        grid=(N_TOK // tm,),
        in_specs=[pl.BlockSpec((tm, D_MODEL), lambda i: (i, 0)),
                  pl.BlockSpec((1, D_MODEL), lambda i: (0, 0))],
        out_specs=pl.BlockSpec((tm, D_MODEL), lambda i: (i, 0)),
        out_shape=jax.ShapeDtypeStruct((N_TOK, D_MODEL), F32),
        compiler_params=_params("parallel"),
        name="final_norm",
    )(x, g.reshape(1, D_MODEL))


def _rope_token_tables():
    rows = DEC_SEQ // GRID_W
    row = jnp.repeat(jnp.arange(rows, dtype=F32), GRID_W)
    col = jnp.tile(jnp.arange(GRID_W, dtype=F32), rows)
    inv = 1.0 / (ROPE_BASE ** (jnp.arange(0, ROPE_AXIS_DIM, 2, dtype=F32) / ROPE_AXIS_DIM))
    ang_r = row[:, None] * inv
    ang_c = col[:, None] * inv
    cr, sr, cc, sc = jnp.cos(ang_r), jnp.sin(ang_r), jnp.cos(ang_c), jnp.sin(ang_c)
    cos64 = jnp.concatenate([cr, cr, cc, cc], axis=1)
    sin64 = jnp.concatenate([-sr, sr, -sc, sc], axis=1)
    cos_lat = jnp.tile(cos64, (DEC_BATCH, 2))
    sin_lat = jnp.tile(sin64, (DEC_BATCH, 2))
    cos = jnp.concatenate([jnp.ones((N_CTX, HEAD_W), F32), cos_lat], axis=0)
    sin = jnp.concatenate([jnp.zeros((N_CTX, HEAD_W), F32), sin_lat], axis=0)
    return cos, sin


def _lambda_init(layer):
    return 0.8 - 0.6 * math.exp(-0.3 * layer)


def kernel(x_prompt, x_sample, state_ssd, cache_k, cache_v, c, c_ctx, mod_w, mod_b, norm_mix_g, norm_ffn_g, ssd_in_w, ssd_conv_w, ssd_conv_b, ssd_dt_bias, ssd_a_log, ssd_d, ssd_norm_g, ssd_out_w, att_qkv_w, att_lambda, att_subln_g, att_out_w, ffn_up_w, ffn_conv_w, ffn_conv_b, ffn_down_w, final_norm_g):
    n_ssd = ssd_in_w.shape[0]
    n_att = att_qkv_w.shape[0]
    x = jnp.concatenate([x_prompt.reshape(N_CTX, D_MODEL), x_sample.reshape(N_LAT, D_MODEL)], axis=0)

    cpad = jnp.concatenate([c_ctx[None], c, jnp.zeros((MOD_ROWS - 1 - DEC_BATCH, D_MODEL), F32)], axis=0)
    mod = _modulation(cpad, mod_w, mod_b)
    mod3 = mod.reshape(DEPTH * MOD_ROWS * 6, 1, D_MODEL)

    cos_t, sin_t = _rope_token_tables()
    expand = (jnp.arange(LANES)[:, None] == (jnp.arange(SSD_INNER)[None, :] // SSD_HEAD_DIM)).astype(BF16)
    h0_lat = state_ssd.reshape(DEC_BATCH, n_ssd, 2, SSD_INNER, SSD_STATE)
    cache_k4 = cache_k.reshape(DEC_BATCH, n_att, PAST_LEN, D_MODEL)
    cache_v4 = cache_v.reshape(DEC_BATCH, n_att, PAST_LEN, D_MODEL)

    new_ssd, new_k, new_v = [], [], []
    for i in range(DEPTH):
        slot = i // 2
        if i % 2 == 0:
            in_w = ssd_in_w[slot]
            w_z = in_w[:, :SSD_INNER].astype(BF16)
            w_xbc = in_w[:, SSD_INNER:SSD_MAIN].astype(BF16)
            pad = jnp.zeros((D_MODEL, LANES - SSD_HEADS), F32)
            w_dt = jnp.concatenate([in_w[:, SSD_MAIN:SSD_MAIN + SSD_HEADS], pad,
                                    in_w[:, SSD_MAIN + SSD_HEADS:], pad], axis=1).astype(BF16)
            zpad = jnp.zeros((LANES - SSD_HEADS,), F32)
            dt_bias = jnp.concatenate([ssd_dt_bias[slot, 0], zpad, ssd_dt_bias[slot, 1], zpad]).reshape(1, 2 * LANES)
            a_log_pad = jnp.concatenate([ssd_a_log[slot], jnp.zeros((2, LANES - SSD_HEADS), F32)],
                                        axis=1).reshape(2, 1, LANES)
            d_x = jnp.repeat(ssd_d[slot], SSD_HEAD_DIM).reshape(1, SSD_INNER)
            z = _norm_mod_matmul(x, mod3, i, 0, 1, norm_mix_g[i], w_z, tn=1024, name="ssd_z")
            xbc, dt = _ssd_xbc(x, mod3, i, norm_mix_g[i], w_xbc, ssd_conv_w[slot], ssd_conv_b[slot], w_dt, dt_bias)
            y_ctx, st_ctx = _ssd_scan(xbc, dt, a_log_pad, expand, d_x, None, slot,
                                      nseq=BATCH, nc=SEQ // SSD_CHUNK, chunk0=0, name="ssd_scan_ctx")
            y_lat, _ = _ssd_scan(xbc, dt, a_log_pad, expand, d_x, h0_lat, slot,
                                 nseq=DEC_BATCH, nc=DEC_SEQ // SSD_CHUNK, chunk0=N_CTX // SSD_CHUNK,
                                 name="ssd_scan_lat")
            new_ssd.append(st_ctx.reshape(BATCH, 2, SSD_HEADS, SSD_HEAD_DIM, SSD_STATE))
            x = _ssd_out(y_ctx, y_lat, z, ssd_norm_g[slot], ssd_out_w[slot].astype(BF16), x, mod3, i)
        else:
            lam_init = _lambda_init(i)
            qkv = _norm_mod_matmul(x, mod3, i, 0, 1, norm_mix_g[i], att_qkv_w[slot].astype(BF16), tn=1024,
                                   epilogue="rope", extra=(cos_t, sin_t),
                                   n_rope_blocks=2 * D_MODEL // 1024, name="att_qkv")
            new_k.append(qkv[:N_CTX, D_MODEL:2 * D_MODEL].reshape(BATCH, SEQ, DA_HEADS, 2, DA_HEAD_DIM))
            new_v.append(qkv[:N_CTX, 2 * D_MODEL:].reshape(BATCH, SEQ, DA_HEADS, HEAD_W))
            o_ctx, o_lat = _attention(qkv, cache_k4, cache_v4, slot, att_lambda[slot], att_subln_g[slot], lam_init)
            x = _matmul_residual(o_ctx, o_lat, att_out_w[slot].astype(BF16), x, mod3, i, 2, name="att_out")
        x = _ffn(x, mod3, i, norm_ffn_g[i], ffn_up_w[i].astype(BF16), ffn_conv_w[i], ffn_conv_b[i],
                 ffn_down_w[i].astype(BF16))

    y = _final_norm(x, final_norm_g)
    y_prompt = y[:N_CTX].reshape(BATCH, SEQ, D_MODEL)
    y_sample = y[N_CTX:].reshape(DEC_BATCH, DEC_SEQ, D_MODEL)
    return (y_prompt, y_sample, jnp.stack(new_ssd, axis=1), jnp.stack(new_k, axis=1), jnp.stack(new_v, axis=1))
```

```python
import functools
import math

import jax
import jax.numpy as jnp
from jax import lax
from jax.experimental import pallas as pl
from jax.experimental.pallas import tpu as pltpu

F32 = jnp.float32
BF16 = jnp.bfloat16

D_MODEL = 1024
BATCH = 16
SEQ = 256
DEPTH = 4
DEC_BATCH = 4
DEC_SEQ = 2048
PAST_LEN = 256
GRID_W = 64
EPS = 1e-6

SSD_INNER = 2048
SSD_HEAD_DIM = 64
SSD_HEADS = 32
SSD_GROUPS = 4
SSD_STATE = 128
SSD_CHUNK = 128
SSD_BC = SSD_GROUPS * SSD_STATE
SSD_CONV_CH = SSD_INNER + 2 * SSD_BC
SSD_MAIN = SSD_INNER + SSD_CONV_CH
HEADS_PER_GROUP = SSD_HEADS // SSD_GROUPS
GROUP_W = HEADS_PER_GROUP * SSD_HEAD_DIM

DA_HEAD_DIM = 64
DA_HEADS = 8
DA_SCALE = DA_HEAD_DIM ** -0.5
LOG2E = math.log2(math.e)
Q_PRESCALE = DA_SCALE * LOG2E
HEAD_W = 2 * DA_HEAD_DIM
ROPE_BASE = 10000.0
ROPE_AXIS_DIM = DA_HEAD_DIM // 2

FFN_HIDDEN = 2816

N_CTX = BATCH * SEQ
N_LAT = DEC_BATCH * DEC_SEQ
N_TOK = N_CTX + N_LAT
MOD_ROWS = 8
LANES = 128
SUBLANES = 8
VMEM_LIMIT = 48 * 1024 * 1024
SCAN_VMEM_LIMIT = 56 * 1024 * 1024


def _params(*sem):
    return pltpu.CompilerParams(dimension_semantics=sem, vmem_limit_bytes=VMEM_LIMIT)


def _mod_row(i, tm):
    nctx = N_CTX // tm
    return jnp.where(i < nctx, 0, 1 + (i - nctx) // (DEC_SEQ // tm))


def _silu(x):
    return x * jax.nn.sigmoid(x)


def _softplus(x):
    return jnp.maximum(x, 0.0) + jnp.log1p(jnp.exp(-jnp.abs(x)))


def _split3(x):
    hi = x.astype(BF16)
    r1 = x - hi.astype(F32)
    mid = r1.astype(BF16)
    lo = (r1 - mid.astype(F32)).astype(BF16)
    return hi, mid, lo


def _dot(a, b):
    return jnp.dot(a, b, preferred_element_type=F32)


def _dot_nt(a, b):
    return lax.dot_general(a, b, (((1,), (1,)), ((), ())), preferred_element_type=F32)


def _norm_mod(x, g, sc, sh):
    r = x * lax.rsqrt(jnp.mean(x * x, axis=-1, keepdims=True) + EPS)
    return ((r * g) * (1.0 + sc) + sh).astype(BF16)


def _mod_kernel(c_ref, w_ref, b_ref, o_ref):
    s = _silu(c_ref[...]).astype(BF16)
    o_ref[...] = _dot(s, w_ref[...].astype(BF16)) + b_ref[...]


def _modulation(cpad, mod_w, mod_b):
    tn = 1024
    n = 6 * D_MODEL
    return pl.pallas_call(
        _mod_kernel,
        grid=(DEPTH, n // tn),
        in_specs=[
            pl.BlockSpec((MOD_ROWS, D_MODEL), lambda l, j: (0, 0)),
            pl.BlockSpec((None, D_MODEL, tn), lambda l, j: (l, 0, j)),
            pl.BlockSpec((None, 1, tn), lambda l, j: (l, 0, j)),
        ],
        out_specs=pl.BlockSpec((None, MOD_ROWS, tn), lambda l, j: (l, 0, j)),
        out_shape=jax.ShapeDtypeStruct((DEPTH, MOD_ROWS, n), F32),
        compiler_params=_params("parallel", "parallel"),
        name="modulation",
    )(cpad, mod_w, mod_b.reshape(DEPTH, 1, n))


def _mod_spec(layer, k, tm, ngrid):
    base = layer * MOD_ROWS * 6 + k
    if ngrid == 1:
        return pl.BlockSpec((None, 1, D_MODEL), lambda i: (base + 6 * _mod_row(i, tm), 0, 0))
    return pl.BlockSpec((None, 1, D_MODEL), lambda i, j: (base + 6 * _mod_row(i, tm), 0, 0))


def _nmm_kernel(x_ref, g_ref, sh_ref, sc_ref, w_ref, o_ref, h_ref):
    @pl.when(pl.program_id(1) == 0)
    def _():
        h_ref[...] = _norm_mod(x_ref[...], g_ref[...], sc_ref[...], sh_ref[...])

    o_ref[...] = _dot(h_ref[...], w_ref[...]).astype(o_ref.dtype)


def _nmm_in_specs(layer, kshift, kscale, tm, tn):
    return [
        pl.BlockSpec((tm, D_MODEL), lambda i, j: (i, 0)),
        pl.BlockSpec((1, D_MODEL), lambda i, j: (0, 0)),
        _mod_spec(layer, kshift, tm, 2),
        _mod_spec(layer, kscale, tm, 2),
        pl.BlockSpec((D_MODEL, tn), lambda i, j: (0, j)),
    ]


def _norm_mod_matmul(x, mod3, layer, kshift, kscale, g, w, *, tn, out_dtype, name):
    tm = 1024
    n = w.shape[1]
    return pl.pallas_call(
        _nmm_kernel,
        grid=(N_TOK // tm, n // tn),
        in_specs=_nmm_in_specs(layer, kshift, kscale, tm, tn),
        out_specs=pl.BlockSpec((tm, tn), lambda i, j: (i, j)),
        out_shape=jax.ShapeDtypeStruct((N_TOK, n), out_dtype),
        scratch_shapes=[pltpu.VMEM((tm, D_MODEL), BF16)],
        compiler_params=_params("parallel", "arbitrary"),
        name=name,
    )(x, g.reshape(1, D_MODEL), mod3, mod3, w)


def _qkv_kernel(x_ref, g_ref, sh_ref, sc_ref, w_ref, cos_ref, sin_ref, q_ref, k_ref, v_ref, h_ref):
    j = pl.program_id(1)

    @pl.when(j == 0)
    def _():
        h_ref[...] = _norm_mod(x_ref[...], g_ref[...], sc_ref[...], sh_ref[...])

    u = _dot(h_ref[...], w_ref[...])

    def rope_into(o_ref, scale):
        cos = cos_ref[...] * scale
        sin = sin_ref[...] * scale
        lane = lax.broadcasted_iota(jnp.int32, cos.shape, 1)
        half = ROPE_AXIS_DIM // 2
        first = (lane % ROPE_AXIS_DIM) < half
        for k in range(u.shape[1] // HEAD_W):
            cols = slice(k * HEAD_W, (k + 1) * HEAD_W)
            uk = u[:, cols]
            partner = jnp.where(first, pltpu.roll(uk, HEAD_W - half, 1), pltpu.roll(uk, half, 1))
            o_ref[:, cols] = (uk * cos + partner * sin).astype(o_ref.dtype)

    @pl.when(j == 0)
    def _():
        rope_into(q_ref, Q_PRESCALE)

    @pl.when(j == 1)
    def _():
        rope_into(k_ref, 1.0)

    @pl.when(j == 2)
    def _():
        v_ref[...] = u


def _qkv_proj(x, mod3, layer, g, w, cos_t, sin_t):
    tm = 1024
    out = pl.BlockSpec((tm, D_MODEL), lambda i, j: (i, 0))
    return pl.pallas_call(
        _qkv_kernel,
        grid=(N_TOK // tm, 3),
        in_specs=_nmm_in_specs(layer, 0, 1, tm, D_MODEL) + [pl.BlockSpec((tm, HEAD_W), lambda i, j: (i, 0))] * 2,
        out_specs=[out, out, out],
        out_shape=[jax.ShapeDtypeStruct((N_TOK, D_MODEL), BF16),
                   jax.ShapeDtypeStruct((N_TOK, D_MODEL), F32),
                   jax.ShapeDtypeStruct((N_TOK, D_MODEL), F32)],
        scratch_shapes=[pltpu.VMEM((tm, D_MODEL), BF16)],
        compiler_params=_params("parallel", "arbitrary"),
        name="att_qkv",
    )(x, g.reshape(1, D_MODEL), mod3, mod3, w, cos_t, sin_t)


PAD = SUBLANES


def _block_seqlen(i, tm):
    return jnp.where(i < N_CTX // tm, SEQ, DEC_SEQ)


def _project_with_halo(h, hh, w, u_ref, tm):
    u_ref[PAD:PAD + tm, :] = _dot(h, w)
    uh = _dot(hh, w)
    u_ref[PAD - 1:PAD, :] = uh[SUBLANES - 1:SUBLANES, :]
    u_ref[PAD + tm:PAD + tm + 1, :] = uh[SUBLANES:SUBLANES + 1, :]


def _conv_chunks(u_refs, cws, cbs, row0, seqmask, tm, rows, emit):
    ridx = lax.broadcasted_iota(jnp.int32, (rows, 1), 0)
    for r0 in range(0, tm, rows):
        outs = []
        for u_ref, cw, cb in zip(u_refs, cws, cbs):
            prev = u_ref[pl.ds(PAD - 1 + r0, rows), :]
            cur = u_ref[pl.ds(PAD + r0, rows), :]
            nxt = u_ref[pl.ds(PAD + 1 + r0, rows), :]
            if r0 % SEQ == 0:
                starts = ((row0 + r0) & seqmask) == 0
                prev = jnp.where(jnp.logical_and(ridx == 0, starts), 0.0, prev)
            if (r0 + rows) % SEQ == 0:
                ends = ((row0 + r0 + rows) & seqmask) == 0
                nxt = jnp.where(jnp.logical_and(ridx == rows - 1, ends), 0.0, nxt)
            outs.append(prev * cw[0:1, :] + cur * cw[1:2, :] + nxt * cw[2:3, :] + cb)
        emit(r0, outs)


def _halo_x_specs(tm):
    per = tm // SUBLANES
    last = N_TOK // SUBLANES - 1
    return [
        pl.BlockSpec((tm, D_MODEL), lambda i, j: (i, 0)),
        pl.BlockSpec((SUBLANES, D_MODEL), lambda i, j: (jnp.maximum(i * per - 1, 0), 0)),
        pl.BlockSpec((SUBLANES, D_MODEL), lambda i, j: (jnp.minimum((i + 1) * per, last), 0)),
    ]


def _xbc_kernel(x_ref, xp_ref, xn_ref, g_ref, sh_ref, sc_ref, w_ref, cw_ref, cb_ref, wdt_ref, bdt_ref,
                o_ref, dt_ref, h_ref, hh_ref, u_ref, *, tm, rows):
    i = pl.program_id(0)

    @pl.when(pl.program_id(1) == 0)
    def _():
        g, sc, sh = g_ref[...], sc_ref[...], sh_ref[...]
        h_ref[...] = _norm_mod(x_ref[...], g, sc, sh)
        hh_ref[...] = _norm_mod(jnp.concatenate([xp_ref[...], xn_ref[...]], axis=0), g, sc, sh)
        dt_ref[...] = _softplus(_dot(h_ref[...], wdt_ref[...]) + bdt_ref[...])

    _project_with_halo(h_ref[...], hh_ref[...], w_ref[...], u_ref, tm)

    def emit(r0, outs):
        o_ref[pl.ds(r0, rows), :] = _silu(outs[0])

    _conv_chunks([u_ref], [cw_ref[...]], [cb_ref[...]], i * tm, _block_seqlen(i, tm) - 1, tm, rows, emit)


def _ssd_xbc(x, mod3, layer, g, w_xbc, conv_w, conv_b, w_dt, dt_bias):
    tm, tc, rows = 1024, 512, 32
    ndt = w_dt.shape[1]
    return pl.pallas_call(
        functools.partial(_xbc_kernel, tm=tm, rows=rows),
        grid=(N_TOK // tm, SSD_CONV_CH // tc),
        in_specs=_halo_x_specs(tm) + [
            pl.BlockSpec((1, D_MODEL), lambda i, j: (0, 0)),
            _mod_spec(layer, 0, tm, 2),
            _mod_spec(layer, 1, tm, 2),
            pl.BlockSpec((D_MODEL, tc), lambda i, j: (0, j)),
            pl.BlockSpec((3, tc), lambda i, j: (0, j)),
            pl.BlockSpec((1, tc), lambda i, j: (0, j)),
            pl.BlockSpec((D_MODEL, ndt), lambda i, j: (0, 0)),
            pl.BlockSpec((1, ndt), lambda i, j: (0, 0)),
        ],
        out_specs=[
            pl.BlockSpec((tm, tc), lambda i, j: (i, j)),
            pl.BlockSpec((tm, ndt), lambda i, j: (i, 0)),
        ],
        out_shape=[
            jax.ShapeDtypeStruct((N_TOK, SSD_CONV_CH), F32),
            jax.ShapeDtypeStruct((N_TOK, ndt), F32),
        ],
        scratch_shapes=[pltpu.VMEM((tm, D_MODEL), BF16), pltpu.VMEM((2 * SUBLANES, D_MODEL), BF16),
                        pltpu.VMEM((tm + 2 * SUBLANES, tc), F32)],
        compiler_params=_params("parallel", "arbitrary"),
        name="ssd_xbc",
    )(x, x, x, g.reshape(1, D_MODEL), mod3, mod3, w_xbc, conv_w, conv_b.reshape(1, SSD_CONV_CH), w_dt, dt_bias)


def _ffn_kernel(x_ref, xp_ref, xn_ref, g_ref, sh_ref, sc_ref, gate_ref, wa_ref, wv_ref,
                cwa_ref, cba_ref, cwv_ref, cbv_ref, wd_ref, o_ref,
                h_ref, hh_ref, ua_ref, uv_ref, act_ref, *, tm, th, rows):
    i = pl.program_id(0)
    j = pl.program_id(1)

    @pl.when(j == 0)
    def _():
        g, sc, sh = g_ref[...], sc_ref[...], sh_ref[...]
        h_ref[...] = _norm_mod(x_ref[...], g, sc, sh)
        hh_ref[...] = _norm_mod(jnp.concatenate([xp_ref[...], xn_ref[...]], axis=0), g, sc, sh)

    h = h_ref[...]
    hh = hh_ref[...]
    _project_with_halo(h, hh, wa_ref[...], ua_ref, tm)
    _project_with_halo(h, hh, wv_ref[...], uv_ref, tm)

    col0 = pl.multiple_of(j * th, th)

    def emit(r0, outs):
        act_ref[pl.ds(r0, rows), pl.ds(col0, th)] = (_silu(outs[0]) * outs[1]).astype(BF16)

    _conv_chunks([ua_ref, uv_ref], [cwa_ref[...], cwv_ref[...]], [cba_ref[...], cbv_ref[...]],
                 i * tm, _block_seqlen(i, tm) - 1, tm, rows, emit)

    @pl.when(j == pl.num_programs(1) - 1)
    def _():
        o_ref[...] = x_ref[...] + gate_ref[...] * _dot(act_ref[...], wd_ref[...])


def _ffn(x, mod3, layer, g, up_w, conv_w, conv_b, down_w):
    tm, th, rows = 1024, 256, 64
    nb = FFN_HIDDEN // th
    cb = conv_b.reshape(1, 2 * FFN_HIDDEN)
    return pl.pallas_call(
        functools.partial(_ffn_kernel, tm=tm, th=th, rows=rows),
        grid=(N_TOK // tm, nb),
        in_specs=_halo_x_specs(tm) + [
            pl.BlockSpec((1, D_MODEL), lambda i, j: (0, 0)),
            _mod_spec(layer, 3, tm, 2),
            _mod_spec(layer, 4, tm, 2),
            _mod_spec(layer, 5, tm, 2),
            pl.BlockSpec((D_MODEL, th), lambda i, j: (0, j)),
            pl.BlockSpec((D_MODEL, th), lambda i, j: (0, j + nb)),
            pl.BlockSpec((3, th), lambda i, j: (0, j)),
            pl.BlockSpec((1, th), lambda i, j: (0, j)),
            pl.BlockSpec((3, th), lambda i, j: (0, j + nb)),
            pl.BlockSpec((1, th), lambda i, j: (0, j + nb)),
            pl.BlockSpec((FFN_HIDDEN, D_MODEL), lambda i, j: (0, 0)),
        ],
        out_specs=pl.BlockSpec((tm, D_MODEL), lambda i, j: (i, 0)),
        out_shape=jax.ShapeDtypeStruct((N_TOK, D_MODEL), F32),
        scratch_shapes=[pltpu.VMEM((tm, D_MODEL), BF16), pltpu.VMEM((2 * SUBLANES, D_MODEL), BF16),
                        pltpu.VMEM((tm + 2 * SUBLANES, th), F32), pltpu.VMEM((tm + 2 * SUBLANES, th), F32),
                        pltpu.VMEM((tm, FFN_HIDDEN), BF16)],
        compiler_params=_params("parallel", "arbitrary"),
        name="ffn",
    )(x, x, x, g.reshape(1, D_MODEL), mod3, mod3, mod3, up_w, up_w, conv_w, cb, conv_w, cb, down_w)


def _mm_res_kernel(ac_ref, al_ref, w_ref, x_ref, gate_ref, o_ref, *, n_ctx_blocks):
    a = jnp.where(pl.program_id(0) < n_ctx_blocks, ac_ref[...], al_ref[...])
    o_ref[...] = x_ref[...] + gate_ref[...] * _dot(a, w_ref[...])


def _two_source_specs(block, tm):
    nctx = N_CTX // tm
    return [pl.BlockSpec(block, lambda i: (jnp.minimum(i, nctx - 1), 0)),
            pl.BlockSpec(block, lambda i: (jnp.maximum(i - nctx, 0), 0))]


def _matmul_residual(a_ctx, a_lat, w, x, mod3, layer, kgate, *, name):
    tm = 512
    k = w.shape[0]
    return pl.pallas_call(
        functools.partial(_mm_res_kernel, n_ctx_blocks=N_CTX // tm),
        grid=(N_TOK // tm,),
        in_specs=_two_source_specs((tm, k), tm) + [
            pl.BlockSpec((k, D_MODEL), lambda i: (0, 0)),
            pl.BlockSpec((tm, D_MODEL), lambda i: (i, 0)),
            _mod_spec(layer, kgate, tm, 1),
        ],
        out_specs=pl.BlockSpec((tm, D_MODEL), lambda i: (i, 0)),
        out_shape=jax.ShapeDtypeStruct((N_TOK, D_MODEL), F32),
        compiler_params=_params("parallel"),
        name=name,
    )(a_ctx, a_lat, w, x, mod3)


def _scan_kernel(*refs, nc, zero_init):
    if zero_init:
        x_ref, b_ref, c_ref, dt_ref, alog_ref, e_ref, dx_ref, y_ref, hl_ref, st_ref, yt_ref = refs
    else:
        x_ref, b_ref, c_ref, dt_ref, alog_ref, e_ref, dx_ref, h0_ref, y_ref, hl_ref, st_ref, yt_ref = refs
    d = pl.program_id(1)
    c = pl.program_id(2)
    q = SSD_CHUNK

    @pl.when(c == 0)
    def _():
        if zero_init:
            st_ref[...] = jnp.zeros_like(st_ref)
        else:
            st_ref[...] = h0_ref[...].T

    ii = lax.broadcasted_iota(jnp.int32, (q, q), 0)
    jj = lax.broadcasted_iota(jnp.int32, (q, q), 1)
    tri = ((ii - jj) * (1 - 2 * d)) >= 0
    tri_b = jnp.where(tri, 1.0, 0.0).astype(BF16)

    dtb = dt_ref[...]
    da = dtb * (-jnp.exp(alog_ref[...]) * LOG2E)
    hi, mid, lo = _split3(da)
    cs = _dot(tri_b, hi) + _dot(tri_b, mid) + _dot(tri_b, lo)
    cs_t = cs.T
    e = e_ref[...]
    ex = _dot(jnp.concatenate(_split3(dtb) + _split3(cs), axis=0), e)
    dt_x = ex[0:q] + ex[q:2 * q] + ex[2 * q:3 * q]
    cs_x = ex[3 * q:4 * q] + ex[4 * q:5 * q] + ex[5 * q:6 * q]
    tot_x = jnp.where(d == 0, cs_x[q - 1:q, :], cs_x[0:1, :])

    xs = x_ref[...]
    xd = xs * dt_x
    xw = (xd * jnp.exp2(tot_x - cs_x)).astype(BF16)
    dec_out = jnp.exp2(cs_x)
    dec_chunk = jnp.exp2(tot_x)
    lane = lax.broadcasted_iota(jnp.int32, (q, LANES), 1)

    for g in range(SSD_GROUPS):
        gs = slice(g * GROUP_W, (g + 1) * GROUP_W)
        bg = b_ref[:, g * SSD_STATE:(g + 1) * SSD_STATE]
        cg = c_ref[:, g * SSD_STATE:(g + 1) * SSD_STATE].astype(BF16)
        cb = _dot_nt(cg, bg.astype(BF16))
        st_g = st_ref[:, gs]
        y_off = _dot(cg, st_g.astype(BF16))
        new_g = _dot(bg.T.astype(BF16), xw[:, gs])
        st_ref[:, gs] = st_g * dec_chunk[:, gs] + new_g
        for p in range(HEADS_PER_GROUP // 2):
            h = g * HEADS_PER_GROUP + 2 * p
            ms = []
            for hh in (h, h + 1):
                seg = cs[:, hh:hh + 1] - cs_t[hh:hh + 1, :]
                decay = jnp.where(tri, jnp.exp2(seg), 0.0)
                ms.append((cb * decay).astype(BF16))
            lhs = jnp.concatenate(ms, axis=1)
            cols = slice(h * SSD_HEAD_DIM, (h + 2) * SSD_HEAD_DIM)
            xd2 = xd[:, cols]
            rhs = jnp.concatenate([jnp.where(lane < SSD_HEAD_DIM, xd2, 0.0),
                                   jnp.where(lane >= SSD_HEAD_DIM, xd2, 0.0)], axis=0).astype(BF16)
            loc = slice(2 * p * SSD_HEAD_DIM, (2 * p + 2) * SSD_HEAD_DIM)
            yt_ref[:, cols] = _dot(lhs, rhs) + y_off[:, loc] * dec_out[:, cols]

    r0 = pl.multiple_of((c + d * (nc - 1 - 2 * c)) * q, q)

    @pl.when(d == 0)
    def _():
        y_ref[pl.ds(r0, q), :] = yt_ref[...] + dx_ref[...] * xs

    @pl.when(d == 1)
    def _():
        y_ref[pl.ds(r0, q), :] += yt_ref[...]

    @pl.when(c == nc - 1)
    def _():
        hl_ref[...] = st_ref[...].T


def _ssd_scan(xbc, dt, a_log_pad, expand, d_x, h0, slot, *, nseq, nc, chunk0, name):
    q = SSD_CHUNK
    zero_init = h0 is None

    def chunk(b, d, c):
        return chunk0 + b * nc + c + d * (nc - 1 - 2 * c)

    in_specs = [
        pl.BlockSpec((q, SSD_INNER), lambda b, d, c: (chunk(b, d, c), 0)),
        pl.BlockSpec((q, SSD_BC), lambda b, d, c: (chunk(b, d, c), SSD_INNER // SSD_BC)),
        pl.BlockSpec((q, SSD_BC), lambda b, d, c: (chunk(b, d, c), SSD_INNER // SSD_BC + 1)),
        pl.BlockSpec((q, LANES), lambda b, d, c: (chunk(b, d, c), d)),
        pl.BlockSpec((None, 1, LANES), lambda b, d, c: (d, 0, 0)),
        pl.BlockSpec((LANES, SSD_INNER), lambda b, d, c: (0, 0)),
        pl.BlockSpec((1, SSD_INNER), lambda b, d, c: (0, 0)),
    ]
    args = [xbc, xbc, xbc, dt, a_log_pad, expand, d_x]
    if not zero_init:
        in_specs.append(pl.BlockSpec((None, None, None, SSD_INNER, SSD_STATE),
                                     lambda b, d, c: (b, slot, d, 0, 0)))
        args.append(h0)
    return pl.pallas_call(
        functools.partial(_scan_kernel, nc=nc, zero_init=zero_init),
        grid=(nseq, 2, nc),
        in_specs=in_specs,
        out_specs=[
            pl.BlockSpec((nc * q, SSD_INNER), lambda b, d, c: (b, 0)),
            pl.BlockSpec((None, None, SSD_INNER, SSD_STATE), lambda b, d, c: (b, d, 0, 0)),
        ],
        out_shape=[
            jax.ShapeDtypeStruct((nseq * nc * q, SSD_INNER), F32),
            jax.ShapeDtypeStruct((nseq, 2, SSD_INNER, SSD_STATE), F32),
        ],
        scratch_shapes=[pltpu.VMEM((SSD_STATE, SSD_INNER), F32), pltpu.VMEM((q, SSD_INNER), F32)],
        compiler_params=pltpu.CompilerParams(dimension_semantics=("parallel", "arbitrary", "arbitrary"),
                                             vmem_limit_bytes=SCAN_VMEM_LIMIT),
        name=name,
    )(*args)


def _ssd_out_kernel(yc_ref, yl_ref, z_ref, ng_ref, w_ref, x_ref, gate_ref, o_ref, *, n_ctx_blocks):
    y = jnp.where(pl.program_id(0) < n_ctx_blocks, yc_ref[...], yl_ref[...])
    y = y * _silu(z_ref[...].astype(F32))
    gw = SSD_INNER // SSD_GROUPS
    parts = []
    for g in range(SSD_GROUPS):
        yg = y[:, g * gw:(g + 1) * gw]
        parts.append(yg * lax.rsqrt(jnp.mean(yg * yg, axis=-1, keepdims=True) + EPS))
    yn = (jnp.concatenate(parts, axis=1) * ng_ref[...]).astype(BF16)
    o_ref[...] = x_ref[...] + gate_ref[...] * _dot(yn, w_ref[...])


def _ssd_out(y_ctx, y_lat, z, norm_g, out_w, x, mod3, layer):
    tm = 512
    return pl.pallas_call(
        functools.partial(_ssd_out_kernel, n_ctx_blocks=N_CTX // tm),
        grid=(N_TOK // tm,),
        in_specs=_two_source_specs((tm, SSD_INNER), tm) + [
            pl.BlockSpec((tm, SSD_INNER), lambda i: (i, 0)),
            pl.BlockSpec((1, SSD_INNER), lambda i: (0, 0)),
            pl.BlockSpec((SSD_INNER, D_MODEL), lambda i: (0, 0)),
            pl.BlockSpec((tm, D_MODEL), lambda i: (i, 0)),
            _mod_spec(layer, 2, tm, 1),
        ],
        out_specs=pl.BlockSpec((tm, D_MODEL), lambda i: (i, 0)),
        out_shape=jax.ShapeDtypeStruct((N_TOK, D_MODEL), F32),
        compiler_params=_params("parallel"),
        name="ssd_out",
    )(y_ctx, y_lat, z, norm_g.reshape(1, SSD_INNER), out_w, x, mod3)


def _diff_lambda(lam_ref, lam_init):
    lv = lam_ref[...]
    return (jnp.exp(jnp.sum(lv[0:1] * lv[1:2], axis=-1, keepdims=True))
            - jnp.exp(jnp.sum(lv[2:3] * lv[3:4], axis=-1, keepdims=True)) + lam_init)


def _attend_head(q, kb, vb, lam, g, lam_init):
    lane = lax.broadcasted_iota(jnp.int32, q.shape, 1)
    zero = jnp.zeros_like(q)
    probs = []
    for qm in (jnp.where(lane < DA_HEAD_DIM, q, zero), jnp.where(lane >= DA_HEAD_DIM, q, zero)):
        s = _dot_nt(qm, kb)
        p = jnp.exp2(s - jnp.max(s, axis=-1, keepdims=True))
        probs.append((p, jnp.sum(p, axis=-1, keepdims=True)))
    (p1, l1), (p2, l2) = probs
    w = (p1 - p2 * (lam * l1 / l2)).astype(BF16)
    o = _dot(w, vb) * (1.0 / l1)
    o = o * lax.rsqrt(jnp.mean(o * o, axis=-1, keepdims=True) + EPS)
    return o * g * (1.0 - lam_init)


def _attn_ctx_kernel(q_ref, k_ref, v_ref, lam_ref, g_ref, o_ref, *, lam_init):
    lam = _diff_lambda(lam_ref, lam_init)
    g = g_ref[...]
    for h in range(DA_HEADS):
        cols = slice(h * HEAD_W, (h + 1) * HEAD_W)
        o = _attend_head(q_ref[:, cols], k_ref[:, cols].astype(BF16), v_ref[:, cols].astype(BF16),
                         lam, g, lam_init)
        o_ref[:, cols] = o.astype(o_ref.dtype)


def _attn_lat_kernel(q_ref, k_ref, v_ref, kc_ref, vc_ref, lam_ref, g_ref, o_ref, kb_ref, vb_ref, *, lam_init):
    @pl.when(pl.program_id(2) == 0)
    def _():
        kb_ref[0:DEC_SEQ, :] = k_ref[...].astype(BF16)
        kb_ref[DEC_SEQ:, :] = kc_ref[...].astype(BF16)
        vb_ref[0:DEC_SEQ, :] = v_ref[...].astype(BF16)
        vb_ref[DEC_SEQ:, :] = vc_ref[...].astype(BF16)

    o = _attend_head(q_ref[...], kb_ref[...], vb_ref[...], _diff_lambda(lam_ref, lam_init), g_ref[...], lam_init)
    o_ref[...] = o.astype(o_ref.dtype)


def _attention(q, k, v, cache_k4, cache_v4, slot, lam_vecs, subln_g, lam_init):
    common = [pl.BlockSpec((4, DA_HEAD_DIM), lambda *_: (0, 0)),
              pl.BlockSpec((1, HEAD_W), lambda *_: (0, 0))]
    g2 = subln_g.reshape(1, HEAD_W)

    seq_block = pl.BlockSpec((SEQ, D_MODEL), lambda b: (b, 0))
    ctx = pl.pallas_call(
        functools.partial(_attn_ctx_kernel, lam_init=lam_init),
        grid=(BATCH,),
        in_specs=[seq_block, seq_block, seq_block] + common,
        out_specs=seq_block,
        out_shape=jax.ShapeDtypeStruct((N_CTX, D_MODEL), BF16),
        compiler_params=_params("parallel"),
        name="attn_ctx",
    )(q, k, v, lam_vecs, g2)

    tq = 256
    nq = DEC_SEQ // tq
    q0 = N_CTX // tq
    s0 = N_CTX // DEC_SEQ
    keys = DEC_SEQ + PAST_LEN
    lat = pl.pallas_call(
        functools.partial(_attn_lat_kernel, lam_init=lam_init),
        grid=(DEC_BATCH, DA_HEADS, nq),
        in_specs=[
            pl.BlockSpec((tq, HEAD_W), lambda b, h, i: (q0 + b * nq + i, h)),
            pl.BlockSpec((DEC_SEQ, HEAD_W), lambda b, h, i: (s0 + b, h)),
            pl.BlockSpec((DEC_SEQ, HEAD_W), lambda b, h, i: (s0 + b, h)),
            pl.BlockSpec((None, None, PAST_LEN, HEAD_W), lambda b, h, i: (b, slot, 0, h)),
            pl.BlockSpec((None, None, PAST_LEN, HEAD_W), lambda b, h, i: (b, slot, 0, h)),
        ] + common,
        out_specs=pl.BlockSpec((tq, HEAD_W), lambda b, h, i: (b * nq + i, h)),
        out_shape=jax.ShapeDtypeStruct((N_LAT, D_MODEL), BF16),
        scratch_shapes=[pltpu.VMEM((keys, HEAD_W), BF16), pltpu.VMEM((keys, HEAD_W), BF16)],
        compiler_params=_params("parallel", "parallel", "arbitrary"),
        name="attn_lat",
    )(q, k, v, cache_k4, cache_v4, lam_vecs, g2)
    return ctx, lat


def _final_kernel(x_ref, g_ref, o_ref):
    x = x_ref[...]
    o_ref[...] = x * lax.rsqrt(jnp.mean(x * x, axis=-1, keepdims=True) + EPS) * g_ref[...]


def _final_norm(x, g, row0, nrows):
    tm = 1024
    b0 = row0 // tm
    return pl.pallas_call(
        _final_kernel,
        grid=(nrows // tm,),
        in_specs=[pl.BlockSpec((tm, D_MODEL), lambda i: (b0 + i, 0)),
                  pl.BlockSpec((1, D_MODEL), lambda i: (0, 0))],
        out_specs=pl.BlockSpec((tm, D_MODEL), lambda i: (i, 0)),
        out_shape=jax.ShapeDtypeStruct((nrows, D_MODEL), F32),
        compiler_params=_params("parallel"),
        name="final_norm",
    )(x, g.reshape(1, D_MODEL))


def _rope_token_tables():
    rows = DEC_SEQ // GRID_W
    row = jnp.repeat(jnp.arange(rows, dtype=F32), GRID_W)
    col = jnp.tile(jnp.arange(GRID_W, dtype=F32), rows)
    inv = 1.0 / (ROPE_BASE ** (jnp.arange(0, ROPE_AXIS_DIM, 2, dtype=F32) / ROPE_AXIS_DIM))
    ang_r = row[:, None] * inv
    ang_c = col[:, None] * inv
    cr, sr, cc, sc = jnp.cos(ang_r), jnp.sin(ang_r), jnp.cos(ang_c), jnp.sin(ang_c)
    cos64 = jnp.concatenate([cr, cr, cc, cc], axis=1)
    sin64 = jnp.concatenate([-sr, sr, -sc, sc], axis=1)
    cos_lat = jnp.tile(cos64, (DEC_BATCH, 2))
    sin_lat = jnp.tile(sin64, (DEC_BATCH, 2))
    cos = jnp.concatenate([jnp.ones((N_CTX, HEAD_W), F32), cos_lat], axis=0)
    sin = jnp.concatenate([jnp.zeros((N_CTX, HEAD_W), F32), sin_lat], axis=0)
    return cos, sin


def _lambda_init(layer):
    return 0.8 - 0.6 * math.exp(-0.3 * layer)


def kernel(x_prompt, x_sample, state_ssd, cache_k, cache_v, c, c_ctx, mod_w, mod_b, norm_mix_g, norm_ffn_g, ssd_in_w, ssd_conv_w, ssd_conv_b, ssd_dt_bias, ssd_a_log, ssd_d, ssd_norm_g, ssd_out_w, att_qkv_w, att_lambda, att_subln_g, att_out_w, ffn_up_w, ffn_conv_w, ffn_conv_b, ffn_down_w, final_norm_g):
    n_ssd = ssd_in_w.shape[0]
    n_att = att_qkv_w.shape[0]
    x = jnp.concatenate([x_prompt.reshape(N_CTX, D_MODEL), x_sample.reshape(N_LAT, D_MODEL)], axis=0)

    cpad = jnp.concatenate([c_ctx[None], c, jnp.zeros((MOD_ROWS - 1 - DEC_BATCH, D_MODEL), F32)], axis=0)
    mod = _modulation(cpad, mod_w, mod_b)
    mod3 = mod.reshape(DEPTH * MOD_ROWS * 6, 1, D_MODEL)

    cos_t, sin_t = _rope_token_tables()
    expand = (jnp.arange(LANES)[:, None] == (jnp.arange(SSD_INNER)[None, :] // SSD_HEAD_DIM)).astype(BF16)
    h0_lat = state_ssd.reshape(DEC_BATCH, n_ssd, 2, SSD_INNER, SSD_STATE)
    cache_k4 = cache_k.reshape(DEC_BATCH, n_att, PAST_LEN, D_MODEL)
    cache_v4 = cache_v.reshape(DEC_BATCH, n_att, PAST_LEN, D_MODEL)

    new_ssd, new_k, new_v = [], [], []
    for i in range(DEPTH):
        slot = i // 2
        if i % 2 == 0:
            in_w = ssd_in_w[slot]
            w_z = in_w[:, :SSD_INNER].astype(BF16)
            w_xbc = in_w[:, SSD_INNER:SSD_MAIN].astype(BF16)
            pad = jnp.zeros((D_MODEL, LANES - SSD_HEADS), F32)
            w_dt = jnp.concatenate([in_w[:, SSD_MAIN:SSD_MAIN + SSD_HEADS], pad,
                                    in_w[:, SSD_MAIN + SSD_HEADS:], pad], axis=1).astype(BF16)
            zpad = jnp.zeros((LANES - SSD_HEADS,), F32)
            dt_bias = jnp.concatenate([ssd_dt_bias[slot, 0], zpad, ssd_dt_bias[slot, 1], zpad]).reshape(1, 2 * LANES)
            a_log_pad = jnp.concatenate([ssd_a_log[slot], jnp.zeros((2, LANES - SSD_HEADS), F32)],
                                        axis=1).reshape(2, 1, LANES)
            d_x = jnp.repeat(ssd_d[slot], SSD_HEAD_DIM).reshape(1, SSD_INNER)
            z = _norm_mod_matmul(x, mod3, i, 0, 1, norm_mix_g[i], w_z, tn=1024, out_dtype=BF16, name="ssd_z")
            xbc, dt = _ssd_xbc(x, mod3, i, norm_mix_g[i], w_xbc, ssd_conv_w[slot], ssd_conv_b[slot], w_dt, dt_bias)
            y_ctx, st_ctx = _ssd_scan(xbc, dt, a_log_pad, expand, d_x, None, slot,
                                      nseq=BATCH, nc=SEQ // SSD_CHUNK, chunk0=0, name="ssd_scan_ctx")
            y_lat, _ = _ssd_scan(xbc, dt, a_log_pad, expand, d_x, h0_lat, slot,
                                 nseq=DEC_BATCH, nc=DEC_SEQ // SSD_CHUNK, chunk0=N_CTX // SSD_CHUNK,
                                 name="ssd_scan_lat")
            new_ssd.append(st_ctx.reshape(BATCH, 2, SSD_HEADS, SSD_HEAD_DIM, SSD_STATE))
            x = _ssd_out(y_ctx, y_lat, z, ssd_norm_g[slot], ssd_out_w[slot].astype(BF16), x, mod3, i)
        else:
            lam_init = _lambda_init(i)
            q, k, v = _qkv_proj(x, mod3, i, norm_mix_g[i], att_qkv_w[slot].astype(BF16), cos_t, sin_t)
            new_k.append(k[:N_CTX].reshape(BATCH, SEQ, DA_HEADS, 2, DA_HEAD_DIM))
            new_v.append(v[:N_CTX].reshape(BATCH, SEQ, DA_HEADS, HEAD_W))
            o_ctx, o_lat = _attention(q, k, v, cache_k4, cache_v4, slot, att_lambda[slot], att_subln_g[slot],
                                      lam_init)
            x = _matmul_residual(o_ctx, o_lat, att_out_w[slot].astype(BF16), x, mod3, i, 2, name="att_out")
        x = _ffn(x, mod3, i, norm_ffn_g[i], ffn_up_w[i].astype(BF16), ffn_conv_w[i], ffn_conv_b[i],
                 ffn_down_w[i].astype(BF16))

    y_prompt = _final_norm(x, final_norm_g, 0, N_CTX).reshape(BATCH, SEQ, D_MODEL)
    y_sample = _final_norm(x, final_norm_g, N_CTX, N_LAT).reshape(DEC_BATCH, DEC_SEQ, D_MODEL)
    return (y_prompt, y_sample, jnp.stack(new_ssd, axis=1), jnp.stack(new_k, axis=1), jnp.stack(new_v, axis=1))
```

```python
import functools
import math

import jax
import jax.numpy as jnp
from jax import lax
from jax.experimental import pallas as pl
from jax.experimental.pallas import tpu as pltpu

F32 = jnp.float32
BF16 = jnp.bfloat16

D_MODEL = 1024
BATCH = 16
SEQ = 256
DEPTH = 4
DEC_BATCH = 4
DEC_SEQ = 2048
PAST_LEN = 256
GRID_W = 64
EPS = 1e-6

SSD_INNER = 2048
SSD_HEAD_DIM = 64
SSD_HEADS = 32
SSD_GROUPS = 4
SSD_STATE = 128
SSD_CHUNK = 128
SSD_BC = SSD_GROUPS * SSD_STATE
SSD_CONV_CH = SSD_INNER + 2 * SSD_BC
SSD_MAIN = SSD_INNER + SSD_CONV_CH
HEADS_PER_GROUP = SSD_HEADS // SSD_GROUPS
GROUP_W = HEADS_PER_GROUP * SSD_HEAD_DIM

DA_HEAD_DIM = 64
DA_HEADS = 8
DA_SCALE = DA_HEAD_DIM ** -0.5
LOG2E = math.log2(math.e)
Q_PRESCALE = DA_SCALE * LOG2E
HEAD_W = 2 * DA_HEAD_DIM
ROPE_BASE = 10000.0
ROPE_AXIS_DIM = DA_HEAD_DIM // 2

FFN_HIDDEN = 2816

N_CTX = BATCH * SEQ
N_LAT = DEC_BATCH * DEC_SEQ
N_TOK = N_CTX + N_LAT
MOD_ROWS = 8
LANES = 128
SUBLANES = 8
VMEM_LIMIT = 48 * 1024 * 1024
SCAN_VMEM_LIMIT = 56 * 1024 * 1024


def _params(*sem):
    return pltpu.CompilerParams(dimension_semantics=sem, vmem_limit_bytes=VMEM_LIMIT)


def _mod_row(i, tm):
    nctx = N_CTX // tm
    return jnp.where(i < nctx, 0, 1 + (i - nctx) // (DEC_SEQ // tm))


def _silu(x):
    return x * jax.nn.sigmoid(x)


def _softplus(x):
    return jnp.maximum(x, 0.0) + jnp.log1p(jnp.exp(-jnp.abs(x)))


def _split3(x):
    hi = x.astype(BF16)
    r1 = x - hi.astype(F32)
    mid = r1.astype(BF16)
    lo = (r1 - mid.astype(F32)).astype(BF16)
    return hi, mid, lo


def _dot(a, b):
    return jnp.dot(a, b, preferred_element_type=F32)


def _dot_nt(a, b):
    return lax.dot_general(a, b, (((1,), (1,)), ((), ())), preferred_element_type=F32)


def _norm_mod(x, g, sc, sh):
    r = x * lax.rsqrt(jnp.mean(x * x, axis=-1, keepdims=True) + EPS)
    return ((r * g) * (1.0 + sc) + sh).astype(BF16)


def _mod_kernel(c_ref, w_ref, b_ref, o_ref):
    s = _silu(c_ref[...]).astype(BF16)
    o_ref[...] = _dot(s, w_ref[...].astype(BF16)) + b_ref[...]


def _modulation(cpad, mod_w, mod_b):
    tn = 1024
    n = 6 * D_MODEL
    return pl.pallas_call(
        _mod_kernel,
        grid=(DEPTH, n // tn),
        in_specs=[
            pl.BlockSpec((MOD_ROWS, D_MODEL), lambda l, j: (0, 0)),
            pl.BlockSpec((None, D_MODEL, tn), lambda l, j: (l, 0, j)),
            pl.BlockSpec((None, 1, tn), lambda l, j: (l, 0, j)),
        ],
        out_specs=pl.BlockSpec((None, MOD_ROWS, tn), lambda l, j: (l, 0, j)),
        out_shape=jax.ShapeDtypeStruct((DEPTH, MOD_ROWS, n), F32),
        compiler_params=_params("parallel", "parallel"),
        name="modulation",
    )(cpad, mod_w, mod_b.reshape(DEPTH, 1, n))


def _mod_spec(layer, k, tm, ngrid):
    base = layer * MOD_ROWS * 6 + k
    if ngrid == 1:
        return pl.BlockSpec((None, 1, D_MODEL), lambda i: (base + 6 * _mod_row(i, tm), 0, 0))
    return pl.BlockSpec((None, 1, D_MODEL), lambda i, j: (base + 6 * _mod_row(i, tm), 0, 0))


def _nmm_kernel(x_ref, g_ref, sh_ref, sc_ref, w_ref, o_ref, h_ref):
    @pl.when(pl.program_id(1) == 0)
    def _():
        h_ref[...] = _norm_mod(x_ref[...], g_ref[...], sc_ref[...], sh_ref[...])

    o_ref[...] = _dot(h_ref[...], w_ref[...]).astype(o_ref.dtype)


def _nmm_in_specs(layer, kshift, kscale, tm, tn):
    return [
        pl.BlockSpec((tm, D_MODEL), lambda i, j: (i, 0)),
        pl.BlockSpec((1, D_MODEL), lambda i, j: (0, 0)),
        _mod_spec(layer, kshift, tm, 2),
        _mod_spec(layer, kscale, tm, 2),
        pl.BlockSpec((D_MODEL, tn), lambda i, j: (0, j)),
    ]


def _norm_mod_matmul(x, mod3, layer, kshift, kscale, g, w, *, tn, out_dtype, name):
    tm = 1024
    n = w.shape[1]
    return pl.pallas_call(
        _nmm_kernel,
        grid=(N_TOK // tm, n // tn),
        in_specs=_nmm_in_specs(layer, kshift, kscale, tm, tn),
        out_specs=pl.BlockSpec((tm, tn), lambda i, j: (i, j)),
        out_shape=jax.ShapeDtypeStruct((N_TOK, n), out_dtype),
        scratch_shapes=[pltpu.VMEM((tm, D_MODEL), BF16)],
        compiler_params=_params("parallel", "arbitrary"),
        name=name,
    )(x, g.reshape(1, D_MODEL), mod3, mod3, w)


def _qkv_kernel(x_ref, g_ref, sh_ref, sc_ref, w_ref, cos_ref, sin_ref, q_ref, k_ref, v_ref, kctx_ref, vctx_ref,
                h_ref, *, n_ctx_blocks):
    j = pl.program_id(1)
    is_ctx = pl.program_id(0) < n_ctx_blocks

    @pl.when(j == 0)
    def _():
        h_ref[...] = _norm_mod(x_ref[...], g_ref[...], sc_ref[...], sh_ref[...])

    u = _dot(h_ref[...], w_ref[...])

    def rope_into(o_ref, scale):
        cos = cos_ref[...] * scale
        sin = sin_ref[...] * scale
        lane = lax.broadcasted_iota(jnp.int32, cos.shape, 1)
        half = ROPE_AXIS_DIM // 2
        first = (lane % ROPE_AXIS_DIM) < half
        for k in range(u.shape[1] // HEAD_W):
            cols = slice(k * HEAD_W, (k + 1) * HEAD_W)
            uk = u[:, cols]
            partner = jnp.where(first, pltpu.roll(uk, HEAD_W - half, 1), pltpu.roll(uk, half, 1))
            o_ref[:, cols] = (uk * cos + partner * sin).astype(o_ref.dtype)

    @pl.when(j == 0)
    def _():
        rope_into(q_ref, Q_PRESCALE)

    @pl.when(j == 1)
    def _():
        rope_into(k_ref, 1.0)

    @pl.when(j == 2)
    def _():
        v_ref[...] = u

    @pl.when(jnp.logical_and(j == 1, is_ctx))
    def _():
        kctx_ref[...] = k_ref[...]

    @pl.when(jnp.logical_and(j == 2, is_ctx))
    def _():
        vctx_ref[...] = u


def _qkv_proj(x, mod3, layer, g, w, cos_t, sin_t):
    tm = 512
    nctx = N_CTX // tm
    out = pl.BlockSpec((tm, D_MODEL), lambda i, j: (i, 0))
    ctx_out = pl.BlockSpec((tm, D_MODEL), lambda i, j: (jnp.minimum(i, nctx - 1), 0))
    return pl.pallas_call(
        functools.partial(_qkv_kernel, n_ctx_blocks=nctx),
        grid=(N_TOK // tm, 3),
        in_specs=_nmm_in_specs(layer, 0, 1, tm, D_MODEL) + [pl.BlockSpec((tm, HEAD_W), lambda i, j: (i, 0))] * 2,
        out_specs=[out, out, out, ctx_out, ctx_out],
        out_shape=[jax.ShapeDtypeStruct((N_TOK, D_MODEL), BF16),
                   jax.ShapeDtypeStruct((N_TOK, D_MODEL), F32),
                   jax.ShapeDtypeStruct((N_TOK, D_MODEL), F32),
                   jax.ShapeDtypeStruct((N_CTX, D_MODEL), F32),
                   jax.ShapeDtypeStruct((N_CTX, D_MODEL), F32)],
        scratch_shapes=[pltpu.VMEM((tm, D_MODEL), BF16)],
        compiler_params=_params("arbitrary", "arbitrary"),
        name="att_qkv",
    )(x, g.reshape(1, D_MODEL), mod3, mod3, w, cos_t, sin_t)


PAD = SUBLANES


def _block_seqlen(i, tm):
    return jnp.where(i < N_CTX // tm, SEQ, DEC_SEQ)


def _project_with_halo(h, hh, w, u_ref, tm):
    u_ref[PAD:PAD + tm, :] = _dot(h, w)
    uh = _dot(hh, w)
    u_ref[PAD - 1:PAD, :] = uh[SUBLANES - 1:SUBLANES, :]
    u_ref[PAD + tm:PAD + tm + 1, :] = uh[SUBLANES:SUBLANES + 1, :]


def _conv_chunks(u_refs, cws, cbs, row0, seqmask, tm, rows, emit):
    ridx = lax.broadcasted_iota(jnp.int32, (rows, 1), 0)
    for r0 in range(0, tm, rows):
        outs = []
        for u_ref, cw, cb in zip(u_refs, cws, cbs):
            prev = u_ref[pl.ds(PAD - 1 + r0, rows), :]
            cur = u_ref[pl.ds(PAD + r0, rows), :]
            nxt = u_ref[pl.ds(PAD + 1 + r0, rows), :]
            if r0 % SEQ == 0:
                starts = ((row0 + r0) & seqmask) == 0
                prev = jnp.where(jnp.logical_and(ridx == 0, starts), 0.0, prev)
            if (r0 + rows) % SEQ == 0:
                ends = ((row0 + r0 + rows) & seqmask) == 0
                nxt = jnp.where(jnp.logical_and(ridx == rows - 1, ends), 0.0, nxt)
            outs.append(prev * cw[0:1, :] + cur * cw[1:2, :] + nxt * cw[2:3, :] + cb)
        emit(r0, outs)


def _halo_x_specs(tm):
    per = tm // SUBLANES
    last = N_TOK // SUBLANES - 1
    return [
        pl.BlockSpec((tm, D_MODEL), lambda i, j: (i, 0)),
        pl.BlockSpec((SUBLANES, D_MODEL), lambda i, j: (jnp.maximum(i * per - 1, 0), 0)),
        pl.BlockSpec((SUBLANES, D_MODEL), lambda i, j: (jnp.minimum((i + 1) * per, last), 0)),
    ]


def _xbc_kernel(x_ref, xp_ref, xn_ref, g_ref, sh_ref, sc_ref, w_ref, cw_ref, cb_ref, wdt_ref, bdt_ref,
                o_ref, dt_ref, h_ref, hh_ref, u_ref, *, tm, rows):
    i = pl.program_id(0)

    @pl.when(pl.program_id(1) == 0)
    def _():
        g, sc, sh = g_ref[...], sc_ref[...], sh_ref[...]
        h_ref[...] = _norm_mod(x_ref[...], g, sc, sh)
        hh_ref[...] = _norm_mod(jnp.concatenate([xp_ref[...], xn_ref[...]], axis=0), g, sc, sh)
        dt_ref[...] = _softplus(_dot(h_ref[...], wdt_ref[...]) + bdt_ref[...])

    _project_with_halo(h_ref[...], hh_ref[...], w_ref[...], u_ref, tm)

    def emit(r0, outs):
        o_ref[pl.ds(r0, rows), :] = _silu(outs[0])

    _conv_chunks([u_ref], [cw_ref[...]], [cb_ref[...]], i * tm, _block_seqlen(i, tm) - 1, tm, rows, emit)


def _ssd_xbc(x, mod3, layer, g, w_xbc, conv_w, conv_b, w_dt, dt_bias):
    tm, tc, rows = 1024, 512, 32
    ndt = w_dt.shape[1]
    return pl.pallas_call(
        functools.partial(_xbc_kernel, tm=tm, rows=rows),
        grid=(N_TOK // tm, SSD_CONV_CH // tc),
        in_specs=_halo_x_specs(tm) + [
            pl.BlockSpec((1, D_MODEL), lambda i, j: (0, 0)),
            _mod_spec(layer, 0, tm, 2),
            _mod_spec(layer, 1, tm, 2),
            pl.BlockSpec((D_MODEL, tc), lambda i, j: (0, j)),
            pl.BlockSpec((3, tc), lambda i, j: (0, j)),
            pl.BlockSpec((1, tc), lambda i, j: (0, j)),
            pl.BlockSpec((D_MODEL, ndt), lambda i, j: (0, 0)),
            pl.BlockSpec((1, ndt), lambda i, j: (0, 0)),
        ],
        out_specs=[
            pl.BlockSpec((tm, tc), lambda i, j: (i, j)),
            pl.BlockSpec((tm, ndt), lambda i, j: (i, 0)),
        ],
        out_shape=[
            jax.ShapeDtypeStruct((N_TOK, SSD_CONV_CH), F32),
            jax.ShapeDtypeStruct((N_TOK, ndt), F32),
        ],
        scratch_shapes=[pltpu.VMEM((tm, D_MODEL), BF16), pltpu.VMEM((2 * SUBLANES, D_MODEL), BF16),
                        pltpu.VMEM((tm + 2 * SUBLANES, tc), F32)],
        compiler_params=_params("parallel", "arbitrary"),
        name="ssd_xbc",
    )(x, x, x, g.reshape(1, D_MODEL), mod3, mod3, w_xbc, conv_w, conv_b.reshape(1, SSD_CONV_CH), w_dt, dt_bias)


def _ffn_kernel(x_ref, xp_ref, xn_ref, g_ref, sh_ref, sc_ref, gate_ref, wa_ref, wv_ref,
                cwa_ref, cba_ref, cwv_ref, cbv_ref, wd_ref, o_ref,
                h_ref, hh_ref, ua_ref, uv_ref, act_ref, *, tm, th, rows):
    i = pl.program_id(0)
    j = pl.program_id(1)

    @pl.when(j == 0)
    def _():
        g, sc, sh = g_ref[...], sc_ref[...], sh_ref[...]
        h_ref[...] = _norm_mod(x_ref[...], g, sc, sh)
        hh_ref[...] = _norm_mod(jnp.concatenate([xp_ref[...], xn_ref[...]], axis=0), g, sc, sh)

    h = h_ref[...]
    hh = hh_ref[...]
    _project_with_halo(h, hh, wa_ref[...], ua_ref, tm)
    _project_with_halo(h, hh, wv_ref[...], uv_ref, tm)

    col0 = pl.multiple_of(j * th, th)

    def emit(r0, outs):
        act_ref[pl.ds(r0, rows), pl.ds(col0, th)] = (_silu(outs[0]) * outs[1]).astype(BF16)

    _conv_chunks([ua_ref, uv_ref], [cwa_ref[...], cwv_ref[...]], [cba_ref[...], cbv_ref[...]],
                 i * tm, _block_seqlen(i, tm) - 1, tm, rows, emit)

    @pl.when(j == pl.num_programs(1) - 1)
    def _():
        o_ref[...] = x_ref[...] + gate_ref[...] * _dot(act_ref[...], wd_ref[...])


def _ffn(x, mod3, layer, g, up_w, conv_w, conv_b, down_w):
    tm, th, rows = 1024, 256, 64
    nb = FFN_HIDDEN // th
    cb = conv_b.reshape(1, 2 * FFN_HIDDEN)
    return pl.pallas_call(
        functools.partial(_ffn_kernel, tm=tm, th=th, rows=rows),
        grid=(N_TOK // tm, nb),
        in_specs=_halo_x_specs(tm) + [
            pl.BlockSpec((1, D_MODEL), lambda i, j: (0, 0)),
            _mod_spec(layer, 3, tm, 2),
            _mod_spec(layer, 4, tm, 2),
            _mod_spec(layer, 5, tm, 2),
            pl.BlockSpec((D_MODEL, th), lambda i, j: (0, j)),
            pl.BlockSpec((D_MODEL, th), lambda i, j: (0, j + nb)),
            pl.BlockSpec((3, th), lambda i, j: (0, j)),
            pl.BlockSpec((1, th), lambda i, j: (0, j)),
            pl.BlockSpec((3, th), lambda i, j: (0, j + nb)),
            pl.BlockSpec((1, th), lambda i, j: (0, j + nb)),
            pl.BlockSpec((FFN_HIDDEN, D_MODEL), lambda i, j: (0, 0)),
        ],
        out_specs=pl.BlockSpec((tm, D_MODEL), lambda i, j: (i, 0)),
        out_shape=jax.ShapeDtypeStruct((N_TOK, D_MODEL), F32),
        scratch_shapes=[pltpu.VMEM((tm, D_MODEL), BF16), pltpu.VMEM((2 * SUBLANES, D_MODEL), BF16),
                        pltpu.VMEM((tm + 2 * SUBLANES, th), F32), pltpu.VMEM((tm + 2 * SUBLANES, th), F32),
                        pltpu.VMEM((tm, FFN_HIDDEN), BF16)],
        compiler_params=_params("parallel", "arbitrary"),
        name="ffn",
    )(x, x, x, g.reshape(1, D_MODEL), mod3, mod3, mod3, up_w, up_w, conv_w, cb, conv_w, cb, down_w)


def _mm_res_kernel(ac_ref, al_ref, w_ref, x_ref, gate_ref, o_ref, *, n_ctx_blocks):
    a = jnp.where(pl.program_id(0) < n_ctx_blocks, ac_ref[...], al_ref[...])
    o_ref[...] = x_ref[...] + gate_ref[...] * _dot(a, w_ref[...])


def _two_source_specs(block, tm):
    nctx = N_CTX // tm
    return [pl.BlockSpec(block, lambda i: (jnp.minimum(i, nctx - 1), 0)),
            pl.BlockSpec(block, lambda i: (jnp.maximum(i - nctx, 0), 0))]


def _matmul_residual(a_ctx, a_lat, w, x, mod3, layer, kgate, *, name):
    tm = 512
    k = w.shape[0]
    return pl.pallas_call(
        functools.partial(_mm_res_kernel, n_ctx_blocks=N_CTX // tm),
        grid=(N_TOK // tm,),
        in_specs=_two_source_specs((tm, k), tm) + [
            pl.BlockSpec((k, D_MODEL), lambda i: (0, 0)),
            pl.BlockSpec((tm, D_MODEL), lambda i: (i, 0)),
            _mod_spec(layer, kgate, tm, 1),
        ],
        out_specs=pl.BlockSpec((tm, D_MODEL), lambda i: (i, 0)),
        out_shape=jax.ShapeDtypeStruct((N_TOK, D_MODEL), F32),
        compiler_params=_params("parallel"),
        name=name,
    )(a_ctx, a_lat, w, x, mod3)


def _scan_kernel(*refs, nc, zero_init):
    if zero_init:
        x_ref, b_ref, c_ref, dt_ref, alog_ref, e_ref, dx_ref, y_ref, hl_ref, st_ref, yt_ref = refs
    else:
        x_ref, b_ref, c_ref, dt_ref, alog_ref, e_ref, dx_ref, h0_ref, y_ref, hl_ref, st_ref, yt_ref = refs
    d = pl.program_id(1)
    c = pl.program_id(2)
    q = SSD_CHUNK

    @pl.when(c == 0)
    def _():
        if zero_init:
            st_ref[...] = jnp.zeros_like(st_ref)
        else:
            st_ref[...] = h0_ref[...].T

    half = SSD_HEAD_DIM
    sgn = 1 - 2 * d
    ii = lax.broadcasted_iota(jnp.int32, (q, q), 0)
    jj = lax.broadcasted_iota(jnp.int32, (q, q), 1)
    tri_b = jnp.where(((ii - jj) * sgn) >= 0, 1.0, 0.0).astype(BF16)
    jl = jj & (half - 1)
    tri_mix = [((ii - (half * jh + jl)) * sgn) >= 0 for jh in range(2)]
    lane = lax.broadcasted_iota(jnp.int32, (half, LANES), 1)
    low = lane < half

    def split_cat(v):
        return jnp.concatenate(_split3(v), axis=1)

    dtb = dt_ref[...]
    da = dtb * (-jnp.exp(alog_ref[...]) * LOG2E)
    cs3 = _dot(tri_b, split_cat(da))
    cs = cs3[:, 0:LANES] + cs3[:, LANES:2 * LANES] + cs3[:, 2 * LANES:]
    e3 = e_ref[...]
    dt_x = _dot(split_cat(dtb), e3)
    cs_x = _dot(split_cat(cs), e3)
    tot_x = jnp.where(d == 0, cs_x[q - 1:q, :], cs_x[0:1, :])
    cs_rows = [jnp.concatenate([cs[half * jh:half * (jh + 1), :]] * 2, axis=0).T for jh in range(2)]

    xs = x_ref[...]
    xd = xs * dt_x
    xw = (xd * jnp.exp2(tot_x - cs_x)).astype(BF16)
    dec_out = jnp.exp2(cs_x)
    dec_chunk = jnp.exp2(tot_x)

    for g in range(SSD_GROUPS):
        gs = slice(g * GROUP_W, (g + 1) * GROUP_W)
        bg = b_ref[:, g * SSD_STATE:(g + 1) * SSD_STATE]
        cg = c_ref[:, g * SSD_STATE:(g + 1) * SSD_STATE].astype(BF16)
        bgb = bg.astype(BF16)
        cb_mix = [_dot_nt(cg, jnp.concatenate([bgb[half * jh:half * (jh + 1), :]] * 2, axis=0)) for jh in range(2)]
        st_g = st_ref[:, gs]
        y_off = _dot(cg, st_g.astype(BF16))
        new_g = _dot(bg.T.astype(BF16), xw[:, gs])
        st_ref[:, gs] = st_g * dec_chunk[:, gs] + new_g
        for p in range(HEADS_PER_GROUP // 2):
            h = g * HEADS_PER_GROUP + 2 * p
            cols = slice(h * SSD_HEAD_DIM, (h + 2) * SSD_HEAD_DIM)
            cs_col = cs_x[:, cols]
            xd2 = xd[:, cols]
            ms, rs = [], []
            for jh in range(2):
                row = jnp.where(low[0:1, :], cs_rows[jh][h:h + 1, :], cs_rows[jh][h + 1:h + 2, :])
                decay = jnp.where(tri_mix[jh], jnp.exp2(cs_col - row), 0.0)
                ms.append((cb_mix[jh] * decay).astype(BF16))
                xj = xd2[half * jh:half * (jh + 1), :]
                rs += [jnp.where(low, xj, 0.0), jnp.where(low, 0.0, xj)]
            lhs = jnp.concatenate(ms, axis=1)
            rhs = jnp.concatenate(rs, axis=0).astype(BF16)
            loc = slice(2 * p * SSD_HEAD_DIM, (2 * p + 2) * SSD_HEAD_DIM)
            yt_ref[:, cols] = _dot(lhs, rhs) + y_off[:, loc] * dec_out[:, cols]

    r0 = pl.multiple_of((c + d * (nc - 1 - 2 * c)) * q, q)

    @pl.when(d == 0)
    def _():
        y_ref[pl.ds(r0, q), :] = yt_ref[...] + dx_ref[...] * xs

    @pl.when(d == 1)
    def _():
        y_ref[pl.ds(r0, q), :] += yt_ref[...]

    @pl.when(c == nc - 1)
    def _():
        hl_ref[...] = st_ref[...].T


def _ssd_scan(xbc, dt, a_log_pad, expand, d_x, h0, slot, *, nseq, nc, chunk0, name):
    q = SSD_CHUNK
    zero_init = h0 is None

    def chunk(b, d, c):
        return chunk0 + b * nc + c + d * (nc - 1 - 2 * c)

    in_specs = [
        pl.BlockSpec((q, SSD_INNER), lambda b, d, c: (chunk(b, d, c), 0)),
        pl.BlockSpec((q, SSD_BC), lambda b, d, c: (chunk(b, d, c), SSD_INNER // SSD_BC)),
        pl.BlockSpec((q, SSD_BC), lambda b, d, c: (chunk(b, d, c), SSD_INNER // SSD_BC + 1)),
        pl.BlockSpec((q, LANES), lambda b, d, c: (chunk(b, d, c), d)),
        pl.BlockSpec((None, 1, LANES), lambda b, d, c: (d, 0, 0)),
        pl.BlockSpec((3 * LANES, SSD_INNER), lambda b, d, c: (0, 0)),
        pl.BlockSpec((1, SSD_INNER), lambda b, d, c: (0, 0)),
    ]
    args = [xbc, xbc, xbc, dt, a_log_pad, expand, d_x]
    if not zero_init:
        in_specs.append(pl.BlockSpec((None, None, None, SSD_INNER, SSD_STATE),
                                     lambda b, d, c: (b, slot, d, 0, 0)))
        args.append(h0)
    return pl.pallas_call(
        functools.partial(_scan_kernel, nc=nc, zero_init=zero_init),
        grid=(nseq, 2, nc),
        in_specs=in_specs,
        out_specs=[
            pl.BlockSpec((nc * q, SSD_INNER), lambda b, d, c: (b, 0)),
            pl.BlockSpec((None, None, SSD_INNER, SSD_STATE), lambda b, d, c: (b, d, 0, 0)),
        ],
        out_shape=[
            jax.ShapeDtypeStruct((nseq * nc * q, SSD_INNER), F32),
            jax.ShapeDtypeStruct((nseq, 2, SSD_INNER, SSD_STATE), F32),
        ],
        scratch_shapes=[pltpu.VMEM((SSD_STATE, SSD_INNER), F32), pltpu.VMEM((q, SSD_INNER), F32)],
        compiler_params=pltpu.CompilerParams(dimension_semantics=("parallel", "arbitrary", "arbitrary"),
                                             vmem_limit_bytes=SCAN_VMEM_LIMIT),
        name=name,
    )(*args)


def _ssd_out_kernel(yc_ref, yl_ref, z_ref, ng_ref, w_ref, x_ref, gate_ref, o_ref, *, n_ctx_blocks):
    y = jnp.where(pl.program_id(0) < n_ctx_blocks, yc_ref[...], yl_ref[...])
    y = y * _silu(z_ref[...].astype(F32))
    gw = SSD_INNER // SSD_GROUPS
    parts = []
    for g in range(SSD_GROUPS):
        yg = y[:, g * gw:(g + 1) * gw]
        parts.append(yg * lax.rsqrt(jnp.mean(yg * yg, axis=-1, keepdims=True) + EPS))
    yn = (jnp.concatenate(parts, axis=1) * ng_ref[...]).astype(BF16)
    o_ref[...] = x_ref[...] + gate_ref[...] * _dot(yn, w_ref[...])


def _ssd_out(y_ctx, y_lat, z, norm_g, out_w, x, mod3, layer):
    tm = 512
    return pl.pallas_call(
        functools.partial(_ssd_out_kernel, n_ctx_blocks=N_CTX // tm),
        grid=(N_TOK // tm,),
        in_specs=_two_source_specs((tm, SSD_INNER), tm) + [
            pl.BlockSpec((tm, SSD_INNER), lambda i: (i, 0)),
            pl.BlockSpec((1, SSD_INNER), lambda i: (0, 0)),
            pl.BlockSpec((SSD_INNER, D_MODEL), lambda i: (0, 0)),
            pl.BlockSpec((tm, D_MODEL), lambda i: (i, 0)),
            _mod_spec(layer, 2, tm, 1),
        ],
        out_specs=pl.BlockSpec((tm, D_MODEL), lambda i: (i, 0)),
        out_shape=jax.ShapeDtypeStruct((N_TOK, D_MODEL), F32),
        compiler_params=_params("parallel"),
        name="ssd_out",
    )(y_ctx, y_lat, z, norm_g.reshape(1, SSD_INNER), out_w, x, mod3)


def _diff_lambda(lam_ref, lam_init):
    lv = lam_ref[...]
    return (jnp.exp(jnp.sum(lv[0:1] * lv[1:2], axis=-1, keepdims=True))
            - jnp.exp(jnp.sum(lv[2:3] * lv[3:4], axis=-1, keepdims=True)) + lam_init)


def _attend_head(q, kb, vb, lam, g, lam_init):
    lane = lax.broadcasted_iota(jnp.int32, q.shape, 1)
    zero = jnp.zeros_like(q)
    probs = []
    for qm in (jnp.where(lane < DA_HEAD_DIM, q, zero), jnp.where(lane >= DA_HEAD_DIM, q, zero)):
        s = _dot_nt(qm, kb)
        p = jnp.exp2(s - jnp.max(s, axis=-1, keepdims=True))
        probs.append((p, jnp.sum(p, axis=-1, keepdims=True)))
    (p1, l1), (p2, l2) = probs
    w = (p1 - p2 * (lam * l1 / l2)).astype(BF16)
    o = _dot(w, vb) * (1.0 / l1)
    o = o * lax.rsqrt(jnp.mean(o * o, axis=-1, keepdims=True) + EPS)
    return o * g * (1.0 - lam_init)


def _attn_ctx_kernel(q_ref, k_ref, v_ref, lam_ref, g_ref, o_ref, *, lam_init):
    lam = _diff_lambda(lam_ref, lam_init)
    g = g_ref[...]
    for h in range(DA_HEADS):
        cols = slice(h * HEAD_W, (h + 1) * HEAD_W)
        o = _attend_head(q_ref[:, cols], k_ref[:, cols].astype(BF16), v_ref[:, cols].astype(BF16),
                         lam, g, lam_init)
        o_ref[:, cols] = o.astype(o_ref.dtype)


def _attn_lat_kernel(q_ref, k_ref, v_ref, kc_ref, vc_ref, lam_ref, g_ref, o_ref, kb_ref, vb_ref, *, lam_init):
    @pl.when(pl.program_id(2) == 0)
    def _():
        kb_ref[0:DEC_SEQ, :] = k_ref[...].astype(BF16)
        kb_ref[DEC_SEQ:, :] = kc_ref[...].astype(BF16)
        vb_ref[0:DEC_SEQ, :] = v_ref[...].astype(BF16)
        vb_ref[DEC_SEQ:, :] = vc_ref[...].astype(BF16)

    o = _attend_head(q_ref[...], kb_ref[...], vb_ref[...], _diff_lambda(lam_ref, lam_init), g_ref[...], lam_init)
    o_ref[...] = o.astype(o_ref.dtype)


def _attention(q, k, v, cache_k4, cache_v4, slot, lam_vecs, subln_g, lam_init):
    common = [pl.BlockSpec((4, DA_HEAD_DIM), lambda *_: (0, 0)),
              pl.BlockSpec((1, HEAD_W), lambda *_: (0, 0))]
    g2 = subln_g.reshape(1, HEAD_W)

    seq_block = pl.BlockSpec((SEQ, D_MODEL), lambda b: (b, 0))
    ctx = pl.pallas_call(
        functools.partial(_attn_ctx_kernel, lam_init=lam_init),
        grid=(BATCH,),
        in_specs=[seq_block, seq_block, seq_block] + common,
        out_specs=seq_block,
        out_shape=jax.ShapeDtypeStruct((N_CTX, D_MODEL), BF16),
        compiler_params=_params("parallel"),
        name="attn_ctx",
    )(q, k, v, lam_vecs, g2)

    tq = 256
    nq = DEC_SEQ // tq
    q0 = N_CTX // tq
    s0 = N_CTX // DEC_SEQ
    keys = DEC_SEQ + PAST_LEN
    lat = pl.pallas_call(
        functools.partial(_attn_lat_kernel, lam_init=lam_init),
        grid=(DEC_BATCH, DA_HEADS, nq),
        in_specs=[
            pl.BlockSpec((tq, HEAD_W), lambda b, h, i: (q0 + b * nq + i, h)),
            pl.BlockSpec((DEC_SEQ, HEAD_W), lambda b, h, i: (s0 + b, h)),
            pl.BlockSpec((DEC_SEQ, HEAD_W), lambda b, h, i: (s0 + b, h)),
            pl.BlockSpec((None, None, PAST_LEN, HEAD_W), lambda b, h, i: (b, slot, 0, h)),
            pl.BlockSpec((None, None, PAST_LEN, HEAD_W), lambda b, h, i: (b, slot, 0, h)),
        ] + common,
        out_specs=pl.BlockSpec((tq, HEAD_W), lambda b, h, i: (b * nq + i, h)),
        out_shape=jax.ShapeDtypeStruct((N_LAT, D_MODEL), BF16),
        scratch_shapes=[pltpu.VMEM((keys, HEAD_W), BF16), pltpu.VMEM((keys, HEAD_W), BF16)],
        compiler_params=_params("parallel", "parallel", "arbitrary"),
        name="attn_lat",
    )(q, k, v, cache_k4, cache_v4, lam_vecs, g2)
    return ctx, lat


def _final_kernel(x_ref, g_ref, o_ref):
    x = x_ref[...]
    o_ref[...] = x * lax.rsqrt(jnp.mean(x * x, axis=-1, keepdims=True) + EPS) * g_ref[...]


def _final_norm(x, g, row0, nrows):
    tm = 1024
    b0 = row0 // tm
    return pl.pallas_call(
        _final_kernel,
        grid=(nrows // tm,),
        in_specs=[pl.BlockSpec((tm, D_MODEL), lambda i: (b0 + i, 0)),
                  pl.BlockSpec((1, D_MODEL), lambda i: (0, 0))],
        out_specs=pl.BlockSpec((tm, D_MODEL), lambda i: (i, 0)),
        out_shape=jax.ShapeDtypeStruct((nrows, D_MODEL), F32),
        compiler_params=_params("parallel"),
        name="final_norm",
    )(x, g.reshape(1, D_MODEL))


def _rope_token_tables():
    rows = DEC_SEQ // GRID_W
    row = jnp.repeat(jnp.arange(rows, dtype=F32), GRID_W)
    col = jnp.tile(jnp.arange(GRID_W, dtype=F32), rows)
    inv = 1.0 / (ROPE_BASE ** (jnp.arange(0, ROPE_AXIS_DIM, 2, dtype=F32) / ROPE_AXIS_DIM))
    ang_r = row[:, None] * inv
    ang_c = col[:, None] * inv
    cr, sr, cc, sc = jnp.cos(ang_r), jnp.sin(ang_r), jnp.cos(ang_c), jnp.sin(ang_c)
    cos64 = jnp.concatenate([cr, cr, cc, cc], axis=1)
    sin64 = jnp.concatenate([-sr, sr, -sc, sc], axis=1)
    cos_lat = jnp.tile(cos64, (DEC_BATCH, 2))
    sin_lat = jnp.tile(sin64, (DEC_BATCH, 2))
    cos = jnp.concatenate([jnp.ones((N_CTX, HEAD_W), F32), cos_lat], axis=0)
    sin = jnp.concatenate([jnp.zeros((N_CTX, HEAD_W), F32), sin_lat], axis=0)
    return cos, sin


def _lambda_init(layer):
    return 0.8 - 0.6 * math.exp(-0.3 * layer)


def kernel(x_prompt, x_sample, state_ssd, cache_k, cache_v, c, c_ctx, mod_w, mod_b, norm_mix_g, norm_ffn_g, ssd_in_w, ssd_conv_w, ssd_conv_b, ssd_dt_bias, ssd_a_log, ssd_d, ssd_norm_g, ssd_out_w, att_qkv_w, att_lambda, att_subln_g, att_out_w, ffn_up_w, ffn_conv_w, ffn_conv_b, ffn_down_w, final_norm_g):
    n_ssd = ssd_in_w.shape[0]
    n_att = att_qkv_w.shape[0]
    x = jnp.concatenate([x_prompt.reshape(N_CTX, D_MODEL), x_sample.reshape(N_LAT, D_MODEL)], axis=0)

    cpad = jnp.concatenate([c_ctx[None], c, jnp.zeros((MOD_ROWS - 1 - DEC_BATCH, D_MODEL), F32)], axis=0)
    mod = _modulation(cpad, mod_w, mod_b)
    mod3 = mod.reshape(DEPTH * MOD_ROWS * 6, 1, D_MODEL)

    cos_t, sin_t = _rope_token_tables()
    expand = (jnp.arange(LANES)[:, None] == (jnp.arange(SSD_INNER)[None, :] // SSD_HEAD_DIM)).astype(BF16)
    expand = jnp.concatenate([expand] * 3, axis=0)
    h0_lat = state_ssd.reshape(DEC_BATCH, n_ssd, 2, SSD_INNER, SSD_STATE)
    cache_k4 = cache_k.reshape(DEC_BATCH, n_att, PAST_LEN, D_MODEL)
    cache_v4 = cache_v.reshape(DEC_BATCH, n_att, PAST_LEN, D_MODEL)

    new_ssd, new_k, new_v = [], [], []
    for i in range(DEPTH):
        slot = i // 2
        if i % 2 == 0:
            in_w = ssd_in_w[slot]
            w_z = in_w[:, :SSD_INNER].astype(BF16)
            w_xbc = in_w[:, SSD_INNER:SSD_MAIN].astype(BF16)
            pad = jnp.zeros((D_MODEL, LANES - SSD_HEADS), F32)
            w_dt = jnp.concatenate([in_w[:, SSD_MAIN:SSD_MAIN + SSD_HEADS], pad,
                                    in_w[:, SSD_MAIN + SSD_HEADS:], pad], axis=1).astype(BF16)
            zpad = jnp.zeros((LANES - SSD_HEADS,), F32)
            dt_bias = jnp.concatenate([ssd_dt_bias[slot, 0], zpad, ssd_dt_bias[slot, 1], zpad]).reshape(1, 2 * LANES)
            a_log_pad = jnp.concatenate([ssd_a_log[slot], jnp.zeros((2, LANES - SSD_HEADS), F32)],
                                        axis=1).reshape(2, 1, LANES)
            d_x = jnp.repeat(ssd_d[slot], SSD_HEAD_DIM).reshape(1, SSD_INNER)
            z = _norm_mod_matmul(x, mod3, i, 0, 1, norm_mix_g[i], w_z, tn=1024, out_dtype=BF16, name="ssd_z")
            xbc, dt = _ssd_xbc(x, mod3, i, norm_mix_g[i], w_xbc, ssd_conv_w[slot], ssd_conv_b[slot], w_dt, dt_bias)
            y_ctx, st_ctx = _ssd_scan(xbc, dt, a_log_pad, expand, d_x, None, slot,
                                      nseq=BATCH, nc=SEQ // SSD_CHUNK, chunk0=0, name="ssd_scan_ctx")
            y_lat, _ = _ssd_scan(xbc, dt, a_log_pad, expand, d_x, h0_lat, slot,
                                 nseq=DEC_BATCH, nc=DEC_SEQ // SSD_CHUNK, chunk0=N_CTX // SSD_CHUNK,
                                 name="ssd_scan_lat")
            new_ssd.append(st_ctx.reshape(BATCH, 2, SSD_HEADS, SSD_HEAD_DIM, SSD_STATE))
            x = _ssd_out(y_ctx, y_lat, z, ssd_norm_g[slot], ssd_out_w[slot].astype(BF16), x, mod3, i)
        else:
            lam_init = _lambda_init(i)
            q, k, v, k_ctx, v_ctx = _qkv_proj(x, mod3, i, norm_mix_g[i], att_qkv_w[slot].astype(BF16),
                                              cos_t, sin_t)
            new_k.append(k_ctx.reshape(BATCH, SEQ, DA_HEADS, 2, DA_HEAD_DIM))
            new_v.append(v_ctx.reshape(BATCH, SEQ, DA_HEADS, HEAD_W))
            o_ctx, o_lat = _attention(q, k, v, cache_k4, cache_v4, slot, att_lambda[slot], att_subln_g[slot],
                                      lam_init)
            x = _matmul_residual(o_ctx, o_lat, att_out_w[slot].astype(BF16), x, mod3, i, 2, name="att_out")
        x = _ffn(x, mod3, i, norm_ffn_g[i], ffn_up_w[i].astype(BF16), ffn_conv_w[i], ffn_conv_b[i],
                 ffn_down_w[i].astype(BF16))

    y_prompt = _final_norm(x, final_norm_g, 0, N_CTX).reshape(BATCH, SEQ, D_MODEL)
    y_sample = _final_norm(x, final_norm_g, N_CTX, N_LAT).reshape(DEC_BATCH, DEC_SEQ, D_MODEL)
    return (y_prompt, y_sample, jnp.stack(new_ssd, axis=1), jnp.stack(new_k, axis=1), jnp.stack(new_v, axis=1))
```

```python
import functools
import math

import jax
import jax.numpy as jnp
from jax import lax
from jax.experimental import pallas as pl
from jax.experimental.pallas import tpu as pltpu

F32 = jnp.float32
BF16 = jnp.bfloat16

D_MODEL = 1024
BATCH = 16
SEQ = 256
DEPTH = 4
DEC_BATCH = 4
DEC_SEQ = 2048
PAST_LEN = 256
GRID_W = 64
EPS = 1e-6

SSD_INNER = 2048
SSD_HEAD_DIM = 64
SSD_HEADS = 32
SSD_GROUPS = 4
SSD_STATE = 128
SSD_CHUNK = 128
SSD_BC = SSD_GROUPS * SSD_STATE
SSD_CONV_CH = SSD_INNER + 2 * SSD_BC
SSD_MAIN = SSD_INNER + SSD_CONV_CH
HEADS_PER_GROUP = SSD_HEADS // SSD_GROUPS
GROUP_W = HEADS_PER_GROUP * SSD_HEAD_DIM

DA_HEAD_DIM = 64
DA_HEADS = 8
DA_SCALE = DA_HEAD_DIM ** -0.5
LOG2E = math.log2(math.e)
Q_PRESCALE = DA_SCALE * LOG2E
HEAD_W = 2 * DA_HEAD_DIM
ROPE_BASE = 10000.0
ROPE_AXIS_DIM = DA_HEAD_DIM // 2

FFN_HIDDEN = 2816

N_CTX = BATCH * SEQ
N_LAT = DEC_BATCH * DEC_SEQ
N_TOK = N_CTX + N_LAT
MOD_ROWS = 8
LANES = 128
SUBLANES = 8
VMEM_LIMIT = 48 * 1024 * 1024
BIG_VMEM_LIMIT = 56 * 1024 * 1024


def _params(*sem):
    return pltpu.CompilerParams(dimension_semantics=sem, vmem_limit_bytes=VMEM_LIMIT)


def _mod_row(i, tm):
    nctx = N_CTX // tm
    return jnp.where(i < nctx, 0, 1 + (i - nctx) // (DEC_SEQ // tm))


def _silu(x):
    return x * jax.nn.sigmoid(x)


def _softplus(x):
    return jnp.maximum(x, 0.0) + jnp.log1p(jnp.exp(-jnp.abs(x)))


def _split3(x):
    hi = x.astype(BF16)
    r1 = x - hi.astype(F32)
    mid = r1.astype(BF16)
    lo = (r1 - mid.astype(F32)).astype(BF16)
    return hi, mid, lo


def _dot(a, b):
    return jnp.dot(a, b, preferred_element_type=F32)


def _dot_nt(a, b):
    return lax.dot_general(a, b, (((1,), (1,)), ((), ())), preferred_element_type=F32)


def _norm_mod(x, g, sc, sh):
    r = x * lax.rsqrt(jnp.mean(x * x, axis=-1, keepdims=True) + EPS)
    return ((r * g) * (1.0 + sc) + sh).astype(BF16)


def _mod_kernel(c_ref, w_ref, b_ref, o_ref):
    s = _silu(c_ref[...]).astype(BF16)
    o_ref[...] = _dot(s, w_ref[...].astype(BF16)) + b_ref[...]


def _modulation(cpad, mod_w, mod_b):
    tn = 1024
    n = 6 * D_MODEL
    return pl.pallas_call(
        _mod_kernel,
        grid=(DEPTH, n // tn),
        in_specs=[
            pl.BlockSpec((MOD_ROWS, D_MODEL), lambda l, j: (0, 0)),
            pl.BlockSpec((None, D_MODEL, tn), lambda l, j: (l, 0, j)),
            pl.BlockSpec((None, 1, tn), lambda l, j: (l, 0, j)),
        ],
        out_specs=pl.BlockSpec((None, MOD_ROWS, tn), lambda l, j: (l, 0, j)),
        out_shape=jax.ShapeDtypeStruct((DEPTH, MOD_ROWS, n), F32),
        compiler_params=_params("parallel", "parallel"),
        name="modulation",
    )(cpad, mod_w, mod_b.reshape(DEPTH, 1, n))


def _mod_spec(layer, k, tm, ngrid):
    base = layer * MOD_ROWS * 6 + k
    if ngrid == 1:
        return pl.BlockSpec((None, 1, D_MODEL), lambda i: (base + 6 * _mod_row(i, tm), 0, 0))
    return pl.BlockSpec((None, 1, D_MODEL), lambda i, j: (base + 6 * _mod_row(i, tm), 0, 0))


def _nmm_kernel(x_ref, g_ref, sh_ref, sc_ref, w_ref, o_ref, h_ref):
    @pl.when(pl.program_id(1) == 0)
    def _():
        h_ref[...] = _norm_mod(x_ref[...], g_ref[...], sc_ref[...], sh_ref[...])

    o_ref[...] = _dot(h_ref[...], w_ref[...]).astype(o_ref.dtype)


def _nmm_in_specs(layer, kshift, kscale, tm, tn):
    return [
        pl.BlockSpec((tm, D_MODEL), lambda i, j: (i, 0)),
        pl.BlockSpec((1, D_MODEL), lambda i, j: (0, 0)),
        _mod_spec(layer, kshift, tm, 2),
        _mod_spec(layer, kscale, tm, 2),
        pl.BlockSpec((D_MODEL, tn), lambda i, j: (0, j)),
    ]


def _norm_mod_matmul(x, mod3, layer, kshift, kscale, g, w, *, tn, out_dtype, name):
    tm = 1024
    n = w.shape[1]
    return pl.pallas_call(
        _nmm_kernel,
        grid=(N_TOK // tm, n // tn),
        in_specs=_nmm_in_specs(layer, kshift, kscale, tm, tn),
        out_specs=pl.BlockSpec((tm, tn), lambda i, j: (i, j)),
        out_shape=jax.ShapeDtypeStruct((N_TOK, n), out_dtype),
        scratch_shapes=[pltpu.VMEM((tm, D_MODEL), BF16)],
        compiler_params=_params("parallel", "arbitrary"),
        name=name,
    )(x, g.reshape(1, D_MODEL), mod3, mod3, w)


def _qkv_kernel(x_ref, g_ref, sh_ref, sc_ref, w_ref, cos_ref, sin_ref, q_ref, k_ref, v_ref, kctx_ref, vctx_ref,
                h_ref, *, n_ctx_blocks):
    j = pl.program_id(1)
    is_ctx = pl.program_id(0) < n_ctx_blocks

    @pl.when(j == 0)
    def _():
        h_ref[...] = _norm_mod(x_ref[...], g_ref[...], sc_ref[...], sh_ref[...])

    u = _dot(h_ref[...], w_ref[...])

    def rope_into(o_refs, scale):
        cos = cos_ref[...] * scale
        sin = sin_ref[...] * scale
        lane = lax.broadcasted_iota(jnp.int32, cos.shape, 1)
        half = ROPE_AXIS_DIM // 2
        first = (lane % ROPE_AXIS_DIM) < half
        for k in range(u.shape[1] // HEAD_W):
            cols = slice(k * HEAD_W, (k + 1) * HEAD_W)
            uk = u[:, cols]
            partner = jnp.where(first, pltpu.roll(uk, HEAD_W - half, 1), pltpu.roll(uk, half, 1))
            r = uk * cos + partner * sin
            for o_ref in o_refs:
                o_ref[:, cols] = r.astype(o_ref.dtype)

    @pl.when(j == 0)
    def _():
        rope_into([q_ref], Q_PRESCALE)

    @pl.when(jnp.logical_and(j == 1, is_ctx))
    def _():
        rope_into([k_ref, kctx_ref], 1.0)

    @pl.when(jnp.logical_and(j == 1, jnp.logical_not(is_ctx)))
    def _():
        rope_into([k_ref], 1.0)

    @pl.when(j == 2)
    def _():
        v_ref[...] = u.astype(v_ref.dtype)

    @pl.when(jnp.logical_and(j == 2, is_ctx))
    def _():
        vctx_ref[...] = u


def _qkv_proj(x, mod3, layer, g, w, cos_t, sin_t):
    tm = 1024
    nctx = N_CTX // tm
    out = pl.BlockSpec((tm, D_MODEL), lambda i, j: (i, 0))
    ctx_out = pl.BlockSpec((tm, D_MODEL), lambda i, j: (jnp.minimum(i, nctx - 1), 0))
    return pl.pallas_call(
        functools.partial(_qkv_kernel, n_ctx_blocks=nctx),
        grid=(N_TOK // tm, 3),
        in_specs=_nmm_in_specs(layer, 0, 1, tm, D_MODEL) + [pl.BlockSpec((tm, HEAD_W), lambda i, j: (i, 0))] * 2,
        out_specs=[out, out, out, ctx_out, ctx_out],
        out_shape=[jax.ShapeDtypeStruct((N_TOK, D_MODEL), BF16),
                   jax.ShapeDtypeStruct((N_TOK, D_MODEL), BF16),
                   jax.ShapeDtypeStruct((N_TOK, D_MODEL), BF16),
                   jax.ShapeDtypeStruct((N_CTX, D_MODEL), F32),
                   jax.ShapeDtypeStruct((N_CTX, D_MODEL), F32)],
        scratch_shapes=[pltpu.VMEM((tm, D_MODEL), BF16)],
        compiler_params=pltpu.CompilerParams(dimension_semantics=("arbitrary", "arbitrary"),
                                             vmem_limit_bytes=BIG_VMEM_LIMIT),
        name="att_qkv",
    )(x, g.reshape(1, D_MODEL), mod3, mod3, w, cos_t, sin_t)


PAD = SUBLANES


def _block_seqlen(i, tm):
    return jnp.where(i < N_CTX // tm, SEQ, DEC_SEQ)


def _project_with_halo(h, hh, w, u_ref, tm):
    u_ref[PAD:PAD + tm, :] = _dot(h, w)
    uh = _dot(hh, w)
    u_ref[PAD - 1:PAD, :] = uh[SUBLANES - 1:SUBLANES, :]
    u_ref[PAD + tm:PAD + tm + 1, :] = uh[SUBLANES:SUBLANES + 1, :]


def _conv_chunks(u_refs, cws, cbs, row0, seqmask, tm, rows, emit):
    ridx = lax.broadcasted_iota(jnp.int32, (rows, 1), 0)
    for r0 in range(0, tm, rows):
        outs = []
        for u_ref, cw, cb in zip(u_refs, cws, cbs):
            prev = u_ref[pl.ds(PAD - 1 + r0, rows), :]
            cur = u_ref[pl.ds(PAD + r0, rows), :]
            nxt = u_ref[pl.ds(PAD + 1 + r0, rows), :]
            if r0 % SEQ == 0:
                starts = ((row0 + r0) & seqmask) == 0
                prev = jnp.where(jnp.logical_and(ridx == 0, starts), 0.0, prev)
            if (r0 + rows) % SEQ == 0:
                ends = ((row0 + r0 + rows) & seqmask) == 0
                nxt = jnp.where(jnp.logical_and(ridx == rows - 1, ends), 0.0, nxt)
            outs.append(prev * cw[0:1, :] + cur * cw[1:2, :] + nxt * cw[2:3, :] + cb)
        emit(r0, outs)


def _halo_x_specs(tm):
    per = tm // SUBLANES
    last = N_TOK // SUBLANES - 1
    return [
        pl.BlockSpec((tm, D_MODEL), lambda i, j: (i, 0)),
        pl.BlockSpec((SUBLANES, D_MODEL), lambda i, j: (jnp.maximum(i * per - 1, 0), 0)),
        pl.BlockSpec((SUBLANES, D_MODEL), lambda i, j: (jnp.minimum((i + 1) * per, last), 0)),
    ]


def _xbc_kernel(x_ref, xp_ref, xn_ref, g_ref, sh_ref, sc_ref, w_ref, cw_ref, cb_ref, wdt_ref, bdt_ref,
                o_ref, dt_ref, h_ref, hh_ref, u_ref, *, tm, rows):
    i = pl.program_id(0)

    @pl.when(pl.program_id(1) == 0)
    def _():
        g, sc, sh = g_ref[...], sc_ref[...], sh_ref[...]
        h_ref[...] = _norm_mod(x_ref[...], g, sc, sh)
        hh_ref[...] = _norm_mod(jnp.concatenate([xp_ref[...], xn_ref[...]], axis=0), g, sc, sh)
        dt_ref[...] = _softplus(_dot(h_ref[...], wdt_ref[...]) + bdt_ref[...])

    _project_with_halo(h_ref[...], hh_ref[...], w_ref[...], u_ref, tm)

    def emit(r0, outs):
        o_ref[pl.ds(r0, rows), :] = _silu(outs[0])

    _conv_chunks([u_ref], [cw_ref[...]], [cb_ref[...]], i * tm, _block_seqlen(i, tm) - 1, tm, rows, emit)


def _ssd_xbc(x, mod3, layer, g, w_xbc, conv_w, conv_b, w_dt, dt_bias):
    tm, tc, rows = 1024, 512, 32
    ndt = w_dt.shape[1]
    return pl.pallas_call(
        functools.partial(_xbc_kernel, tm=tm, rows=rows),
        grid=(N_TOK // tm, SSD_CONV_CH // tc),
        in_specs=_halo_x_specs(tm) + [
            pl.BlockSpec((1, D_MODEL), lambda i, j: (0, 0)),
            _mod_spec(layer, 0, tm, 2),
            _mod_spec(layer, 1, tm, 2),
            pl.BlockSpec((D_MODEL, tc), lambda i, j: (0, j)),
            pl.BlockSpec((3, tc), lambda i, j: (0, j)),
            pl.BlockSpec((1, tc), lambda i, j: (0, j)),
            pl.BlockSpec((D_MODEL, ndt), lambda i, j: (0, 0)),
            pl.BlockSpec((1, ndt), lambda i, j: (0, 0)),
        ],
        out_specs=[
            pl.BlockSpec((tm, tc), lambda i, j: (i, j)),
            pl.BlockSpec((tm, ndt), lambda i, j: (i, 0)),
        ],
        out_shape=[
            jax.ShapeDtypeStruct((N_TOK, SSD_CONV_CH), F32),
            jax.ShapeDtypeStruct((N_TOK, ndt), F32),
        ],
        scratch_shapes=[pltpu.VMEM((tm, D_MODEL), BF16), pltpu.VMEM((2 * SUBLANES, D_MODEL), BF16),
                        pltpu.VMEM((tm + 2 * SUBLANES, tc), F32)],
        compiler_params=_params("parallel", "arbitrary"),
        name="ssd_xbc",
    )(x, x, x, g.reshape(1, D_MODEL), mod3, mod3, w_xbc, conv_w, conv_b.reshape(1, SSD_CONV_CH), w_dt, dt_bias)


def _ffn_kernel(x_ref, xp_ref, xn_ref, g_ref, sh_ref, sc_ref, gate_ref, wa_ref, wv_ref,
                cwa_ref, cba_ref, cwv_ref, cbv_ref, wd_ref, o_ref,
                h_ref, hh_ref, ua_ref, uv_ref, act_ref, *, tm, th, rows):
    i = pl.program_id(0)
    j = pl.program_id(1)

    @pl.when(j == 0)
    def _():
        g, sc, sh = g_ref[...], sc_ref[...], sh_ref[...]
        h_ref[...] = _norm_mod(x_ref[...], g, sc, sh)
        hh_ref[...] = _norm_mod(jnp.concatenate([xp_ref[...], xn_ref[...]], axis=0), g, sc, sh)

    h = h_ref[...]
    hh = hh_ref[...]
    _project_with_halo(h, hh, wa_ref[...], ua_ref, tm)
    _project_with_halo(h, hh, wv_ref[...], uv_ref, tm)

    col0 = pl.multiple_of(j * th, th)

    def emit(r0, outs):
        act_ref[pl.ds(r0, rows), pl.ds(col0, th)] = (_silu(outs[0]) * outs[1]).astype(BF16)

    _conv_chunks([ua_ref, uv_ref], [cwa_ref[...], cwv_ref[...]], [cba_ref[...], cbv_ref[...]],
                 i * tm, _block_seqlen(i, tm) - 1, tm, rows, emit)

    @pl.when(j == pl.num_programs(1) - 1)
    def _():
        o_ref[...] = x_ref[...] + gate_ref[...] * _dot(act_ref[...], wd_ref[...])


def _ffn(x, mod3, layer, g, up_w, conv_w, conv_b, down_w):
    tm, th, rows = 1024, 256, 64
    nb = FFN_HIDDEN // th
    cb = conv_b.reshape(1, 2 * FFN_HIDDEN)
    return pl.pallas_call(
        functools.partial(_ffn_kernel, tm=tm, th=th, rows=rows),
        grid=(N_TOK // tm, nb),
        in_specs=_halo_x_specs(tm) + [
            pl.BlockSpec((1, D_MODEL), lambda i, j: (0, 0)),
            _mod_spec(layer, 3, tm, 2),
            _mod_spec(layer, 4, tm, 2),
            _mod_spec(layer, 5, tm, 2),
            pl.BlockSpec((D_MODEL, th), lambda i, j: (0, j)),
            pl.BlockSpec((D_MODEL, th), lambda i, j: (0, j + nb)),
            pl.BlockSpec((3, th), lambda i, j: (0, j)),
            pl.BlockSpec((1, th), lambda i, j: (0, j)),
            pl.BlockSpec((3, th), lambda i, j: (0, j + nb)),
            pl.BlockSpec((1, th), lambda i, j: (0, j + nb)),
            pl.BlockSpec((FFN_HIDDEN, D_MODEL), lambda i, j: (0, 0)),
        ],
        out_specs=pl.BlockSpec((tm, D_MODEL), lambda i, j: (i, 0)),
        out_shape=jax.ShapeDtypeStruct((N_TOK, D_MODEL), F32),
        scratch_shapes=[pltpu.VMEM((tm, D_MODEL), BF16), pltpu.VMEM((2 * SUBLANES, D_MODEL), BF16),
                        pltpu.VMEM((tm + 2 * SUBLANES, th), F32), pltpu.VMEM((tm + 2 * SUBLANES, th), F32),
                        pltpu.VMEM((tm, FFN_HIDDEN), BF16)],
        compiler_params=_params("parallel", "arbitrary"),
        name="ffn",
    )(x, x, x, g.reshape(1, D_MODEL), mod3, mod3, mod3, up_w, up_w, conv_w, cb, conv_w, cb, down_w)


def _mm_res_kernel(ac_ref, al_ref, w_ref, x_ref, gate_ref, o_ref, *, n_ctx_blocks):
    a = jnp.where(pl.program_id(0) < n_ctx_blocks, ac_ref[...], al_ref[...])
    o_ref[...] = x_ref[...] + gate_ref[...] * _dot(a, w_ref[...])


def _two_source_specs(block, tm):
    nctx = N_CTX // tm
    return [pl.BlockSpec(block, lambda i: (jnp.minimum(i, nctx - 1), 0)),
            pl.BlockSpec(block, lambda i: (jnp.maximum(i - nctx, 0), 0))]


def _matmul_residual(a_ctx, a_lat, w, x, mod3, layer, kgate, *, name):
    tm = 512
    k = w.shape[0]
    return pl.pallas_call(
        functools.partial(_mm_res_kernel, n_ctx_blocks=N_CTX // tm),
        grid=(N_TOK // tm,),
        in_specs=_two_source_specs((tm, k), tm) + [
            pl.BlockSpec((k, D_MODEL), lambda i: (0, 0)),
            pl.BlockSpec((tm, D_MODEL), lambda i: (i, 0)),
            _mod_spec(layer, kgate, tm, 1),
        ],
        out_specs=pl.BlockSpec((tm, D_MODEL), lambda i: (i, 0)),
        out_shape=jax.ShapeDtypeStruct((N_TOK, D_MODEL), F32),
        compiler_params=_params("parallel"),
        name=name,
    )(a_ctx, a_lat, w, x, mod3)


def _scan_kernel(*refs, nc, zero_init):
    if zero_init:
        x_ref, b_ref, c_ref, dt_ref, alog_ref, e_ref, dx_ref, y_ref, hl_ref, st_ref, yt_ref = refs
    else:
        x_ref, b_ref, c_ref, dt_ref, alog_ref, e_ref, dx_ref, h0_ref, y_ref, hl_ref, st_ref, yt_ref = refs
    d = pl.program_id(1)
    c = pl.program_id(2)
    q = SSD_CHUNK

    @pl.when(c == 0)
    def _():
        if zero_init:
            st_ref[...] = jnp.zeros_like(st_ref)
        else:
            st_ref[...] = h0_ref[...].T

    half = SSD_HEAD_DIM
    sgn = 1 - 2 * d
    ii = lax.broadcasted_iota(jnp.int32, (q, q), 0)
    jj = lax.broadcasted_iota(jnp.int32, (q, q), 1)
    tri_b = jnp.where(((ii - jj) * sgn) >= 0, 1.0, 0.0).astype(BF16)
    jl = jj & (half - 1)
    tri_mix = [((ii - (half * jh + jl)) * sgn) >= 0 for jh in range(2)]
    lane = lax.broadcasted_iota(jnp.int32, (half, LANES), 1)
    low = lane < half

    def split_cat(v):
        return jnp.concatenate(_split3(v), axis=1)

    dtb = dt_ref[...]
    da = dtb * (-jnp.exp(alog_ref[...]) * LOG2E)
    cs3 = _dot(tri_b, split_cat(da))
    cs = cs3[:, 0:LANES] + cs3[:, LANES:2 * LANES] + cs3[:, 2 * LANES:]
    e3 = e_ref[...]
    dt_x = _dot(split_cat(dtb), e3)
    cs_x = _dot(split_cat(cs), e3)
    tot_x = jnp.where(d == 0, cs_x[q - 1:q, :], cs_x[0:1, :])
    cs_rows = [jnp.concatenate([cs[half * jh:half * (jh + 1), :]] * 2, axis=0).T for jh in range(2)]

    xs = x_ref[...]
    xd = xs * dt_x
    xw = (xd * jnp.exp2(tot_x - cs_x)).astype(BF16)
    dec_out = jnp.exp2(cs_x)
    dec_chunk = jnp.exp2(tot_x)

    for g in range(SSD_GROUPS):
        gs = slice(g * GROUP_W, (g + 1) * GROUP_W)
        bg = b_ref[:, g * SSD_STATE:(g + 1) * SSD_STATE]
        cg = c_ref[:, g * SSD_STATE:(g + 1) * SSD_STATE].astype(BF16)
        bgb = bg.astype(BF16)
        cb_mix = [_dot_nt(cg, jnp.concatenate([bgb[half * jh:half * (jh + 1), :]] * 2, axis=0)) for jh in range(2)]
        st_g = st_ref[:, gs]
        y_off = _dot(cg, st_g.astype(BF16))
        new_g = _dot(bg.T.astype(BF16), xw[:, gs])
        st_ref[:, gs] = st_g * dec_chunk[:, gs] + new_g
        for p in range(HEADS_PER_GROUP // 2):
            h = g * HEADS_PER_GROUP + 2 * p
            cols = slice(h * SSD_HEAD_DIM, (h + 2) * SSD_HEAD_DIM)
            cs_col = cs_x[:, cols]
            xd2 = xd[:, cols]
            ms, rs = [], []
            for jh in range(2):
                row = jnp.where(low[0:1, :], cs_rows[jh][h:h + 1, :], cs_rows[jh][h + 1:h + 2, :])
                decay = jnp.where(tri_mix[jh], jnp.exp2(cs_col - row), 0.0)
                ms.append((cb_mix[jh] * decay).astype(BF16))
                xj = xd2[half * jh:half * (jh + 1), :]
                rs += [jnp.where(low, xj, 0.0), jnp.where(low, 0.0, xj)]
            lhs = jnp.concatenate(ms, axis=1)
            rhs = jnp.concatenate(rs, axis=0).astype(BF16)
            loc = slice(2 * p * SSD_HEAD_DIM, (2 * p + 2) * SSD_HEAD_DIM)
            yt_ref[:, cols] = _dot(lhs, rhs) + y_off[:, loc] * dec_out[:, cols]

    r0 = pl.multiple_of((c + d * (nc - 1 - 2 * c)) * q, q)

    @pl.when(d == 0)
    def _():
        y_ref[pl.ds(r0, q), :] = yt_ref[...] + dx_ref[...] * xs

    @pl.when(d == 1)
    def _():
        y_ref[pl.ds(r0, q), :] += yt_ref[...]

    @pl.when(c == nc - 1)
    def _():
        hl_ref[...] = st_ref[...].T


def _ssd_scan(xbc, dt, a_log_pad, expand, d_x, h0, slot, *, nseq, nc, chunk0, name):
    q = SSD_CHUNK
    zero_init = h0 is None

    def chunk(b, d, c):
        return chunk0 + b * nc + c + d * (nc - 1 - 2 * c)

    in_specs = [
        pl.BlockSpec((q, SSD_INNER), lambda b, d, c: (chunk(b, d, c), 0)),
        pl.BlockSpec((q, SSD_BC), lambda b, d, c: (chunk(b, d, c), SSD_INNER // SSD_BC)),
        pl.BlockSpec((q, SSD_BC), lambda b, d, c: (chunk(b, d, c), SSD_INNER // SSD_BC + 1)),
        pl.BlockSpec((q, LANES), lambda b, d, c: (chunk(b, d, c), d)),
        pl.BlockSpec((None, 1, LANES), lambda b, d, c: (d, 0, 0)),
        pl.BlockSpec((3 * LANES, SSD_INNER), lambda b, d, c: (0, 0)),
        pl.BlockSpec((1, SSD_INNER), lambda b, d, c: (0, 0)),
    ]
    args = [xbc, xbc, xbc, dt, a_log_pad, expand, d_x]
    if not zero_init:
        in_specs.append(pl.BlockSpec((None, None, None, SSD_INNER, SSD_STATE),
                                     lambda b, d, c: (b, slot, d, 0, 0)))
        args.append(h0)
    return pl.pallas_call(
        functools.partial(_scan_kernel, nc=nc, zero_init=zero_init),
        grid=(nseq, 2, nc),
        in_specs=in_specs,
        out_specs=[
            pl.BlockSpec((nc * q, SSD_INNER), lambda b, d, c: (b, 0)),
            pl.BlockSpec((None, None, SSD_INNER, SSD_STATE), lambda b, d, c: (b, d, 0, 0)),
        ],
        out_shape=[
            jax.ShapeDtypeStruct((nseq * nc * q, SSD_INNER), F32),
            jax.ShapeDtypeStruct((nseq, 2, SSD_INNER, SSD_STATE), F32),
        ],
        scratch_shapes=[pltpu.VMEM((SSD_STATE, SSD_INNER), F32), pltpu.VMEM((q, SSD_INNER), F32)],
        compiler_params=pltpu.CompilerParams(dimension_semantics=("parallel", "arbitrary", "arbitrary"),
                                             vmem_limit_bytes=BIG_VMEM_LIMIT),
        name=name,
    )(*args)


def _ssd_out_kernel(yc_ref, yl_ref, z_ref, ng_ref, w_ref, x_ref, gate_ref, o_ref, *, n_ctx_blocks):
    y = jnp.where(pl.program_id(0) < n_ctx_blocks, yc_ref[...], yl_ref[...])
    y = y * _silu(z_ref[...].astype(F32))
    gw = SSD_INNER // SSD_GROUPS
    parts = []
    for g in range(SSD_GROUPS):
        yg = y[:, g * gw:(g + 1) * gw]
        parts.append(yg * lax.rsqrt(jnp.mean(yg * yg, axis=-1, keepdims=True) + EPS))
    yn = (jnp.concatenate(parts, axis=1) * ng_ref[...]).astype(BF16)
    o_ref[...] = x_ref[...] + gate_ref[...] * _dot(yn, w_ref[...])


def _ssd_out(y_ctx, y_lat, z, norm_g, out_w, x, mod3, layer):
    tm = 512
    return pl.pallas_call(
        functools.partial(_ssd_out_kernel, n_ctx_blocks=N_CTX // tm),
        grid=(N_TOK // tm,),
        in_specs=_two_source_specs((tm, SSD_INNER), tm) + [
            pl.BlockSpec((tm, SSD_INNER), lambda i: (i, 0)),
            pl.BlockSpec((1, SSD_INNER), lambda i: (0, 0)),
            pl.BlockSpec((SSD_INNER, D_MODEL), lambda i: (0, 0)),
            pl.BlockSpec((tm, D_MODEL), lambda i: (i, 0)),
            _mod_spec(layer, 2, tm, 1),
        ],
        out_specs=pl.BlockSpec((tm, D_MODEL), lambda i: (i, 0)),
        out_shape=jax.ShapeDtypeStruct((N_TOK, D_MODEL), F32),
        compiler_params=_params("parallel"),
        name="ssd_out",
    )(y_ctx, y_lat, z, norm_g.reshape(1, SSD_INNER), out_w, x, mod3)


def _diff_lambda(lam_ref, lam_init):
    lv = lam_ref[...]
    return (jnp.exp(jnp.sum(lv[0:1] * lv[1:2], axis=-1, keepdims=True))
            - jnp.exp(jnp.sum(lv[2:3] * lv[3:4], axis=-1, keepdims=True)) + lam_init)


def _attend_head(q, kb, vb, lam, g, lam_init):
    lane = lax.broadcasted_iota(jnp.int32, q.shape, 1)
    zero = jnp.zeros_like(q)
    probs = []
    for qm in (jnp.where(lane < DA_HEAD_DIM, q, zero), jnp.where(lane >= DA_HEAD_DIM, q, zero)):
        s = _dot_nt(qm, kb)
        p = jnp.exp2(s - jnp.max(s, axis=-1, keepdims=True))
        probs.append((p, jnp.sum(p, axis=-1, keepdims=True)))
    (p1, l1), (p2, l2) = probs
    w = (p1 - p2 * (lam * l1 / l2)).astype(BF16)
    o = _dot(w, vb) * (1.0 / l1)
    o = o * lax.rsqrt(jnp.mean(o * o, axis=-1, keepdims=True) + EPS)
    return o * g * (1.0 - lam_init)


def _attn_ctx_kernel(q_ref, k_ref, v_ref, lam_ref, g_ref, o_ref, *, lam_init):
    lam = _diff_lambda(lam_ref, lam_init)
    g = g_ref[...]
    for h in range(DA_HEADS):
        cols = slice(h * HEAD_W, (h + 1) * HEAD_W)
        o = _attend_head(q_ref[:, cols], k_ref[:, cols], v_ref[:, cols], lam, g, lam_init)
        o_ref[:, cols] = o.astype(o_ref.dtype)


def _attn_lat_kernel(q_ref, k_ref, v_ref, kc_ref, vc_ref, lam_ref, g_ref, o_ref, kb_ref, vb_ref, *, lam_init):
    @pl.when(pl.program_id(2) == 0)
    def _():
        kb_ref[0:DEC_SEQ, :] = k_ref[...]
        kb_ref[DEC_SEQ:, :] = kc_ref[...].astype(BF16)
        vb_ref[0:DEC_SEQ, :] = v_ref[...]
        vb_ref[DEC_SEQ:, :] = vc_ref[...].astype(BF16)

    o = _attend_head(q_ref[...], kb_ref[...], vb_ref[...], _diff_lambda(lam_ref, lam_init), g_ref[...], lam_init)
    o_ref[...] = o.astype(o_ref.dtype)


def _attention(q, k, v, cache_k4, cache_v4, slot, lam_vecs, subln_g, lam_init):
    common = [pl.BlockSpec((4, DA_HEAD_DIM), lambda *_: (0, 0)),
              pl.BlockSpec((1, HEAD_W), lambda *_: (0, 0))]
    g2 = subln_g.reshape(1, HEAD_W)

    seq_block = pl.BlockSpec((SEQ, D_MODEL), lambda b: (b, 0))
    ctx = pl.pallas_call(
        functools.partial(_attn_ctx_kernel, lam_init=lam_init),
        grid=(BATCH,),
        in_specs=[seq_block, seq_block, seq_block] + common,
        out_specs=seq_block,
        out_shape=jax.ShapeDtypeStruct((N_CTX, D_MODEL), BF16),
        compiler_params=_params("parallel"),
        name="attn_ctx",
    )(q, k, v, lam_vecs, g2)

    tq = 256
    nq = DEC_SEQ // tq
    q0 = N_CTX // tq
    s0 = N_CTX // DEC_SEQ
    keys = DEC_SEQ + PAST_LEN
    lat = pl.pallas_call(
        functools.partial(_attn_lat_kernel, lam_init=lam_init),
        grid=(DEC_BATCH, DA_HEADS, nq),
        in_specs=[
            pl.BlockSpec((tq, HEAD_W), lambda b, h, i: (q0 + b * nq + i, h)),
            pl.BlockSpec((DEC_SEQ, HEAD_W), lambda b, h, i: (s0 + b, h)),
            pl.BlockSpec((DEC_SEQ, HEAD_W), lambda b, h, i: (s0 + b, h)),
            pl.BlockSpec((None, None, PAST_LEN, HEAD_W), lambda b, h, i: (b, slot, 0, h)),
            pl.BlockSpec((None, None, PAST_LEN, HEAD_W), lambda b, h, i: (b, slot, 0, h)),
        ] + common,
        out_specs=pl.BlockSpec((tq, HEAD_W), lambda b, h, i: (b * nq + i, h)),
        out_shape=jax.ShapeDtypeStruct((N_LAT, D_MODEL), BF16),
        scratch_shapes=[pltpu.VMEM((keys, HEAD_W), BF16), pltpu.VMEM((keys, HEAD_W), BF16)],
        compiler_params=_params("parallel", "parallel", "arbitrary"),
        name="attn_lat",
    )(q, k, v, cache_k4, cache_v4, lam_vecs, g2)
    return ctx, lat


def _final_kernel(x_ref, g_ref, o_ref):
    x = x_ref[...]
    o_ref[...] = x * lax.rsqrt(jnp.mean(x * x, axis=-1, keepdims=True) + EPS) * g_ref[...]


def _final_norm(x, g, row0, nrows):
    tm = 1024
    b0 = row0 // tm
    return pl.pallas_call(
        _final_kernel,
        grid=(nrows // tm,),
        in_specs=[pl.BlockSpec((tm, D_MODEL), lambda i: (b0 + i, 0)),
                  pl.BlockSpec((1, D_MODEL), lambda i: (0, 0))],
        out_specs=pl.BlockSpec((tm, D_MODEL), lambda i: (i, 0)),
        out_shape=jax.ShapeDtypeStruct((nrows, D_MODEL), F32),
        compiler_params=_params("parallel"),
        name="final_norm",
    )(x, g.reshape(1, D_MODEL))


def _rope_token_tables():
    rows = DEC_SEQ // GRID_W
    row = jnp.repeat(jnp.arange(rows, dtype=F32), GRID_W)
    col = jnp.tile(jnp.arange(GRID_W, dtype=F32), rows)
    inv = 1.0 / (ROPE_BASE ** (jnp.arange(0, ROPE_AXIS_DIM, 2, dtype=F32) / ROPE_AXIS_DIM))
    ang_r = row[:, None] * inv
    ang_c = col[:, None] * inv
    cr, sr, cc, sc = jnp.cos(ang_r), jnp.sin(ang_r), jnp.cos(ang_c), jnp.sin(ang_c)
    cos64 = jnp.concatenate([cr, cr, cc, cc], axis=1)
    sin64 = jnp.concatenate([-sr, sr, -sc, sc], axis=1)
    cos_lat = jnp.tile(cos64, (DEC_BATCH, 2))
    sin_lat = jnp.tile(sin64, (DEC_BATCH, 2))
    cos = jnp.concatenate([jnp.ones((N_CTX, HEAD_W), F32), cos_lat], axis=0)
    sin = jnp.concatenate([jnp.zeros((N_CTX, HEAD_W), F32), sin_lat], axis=0)
    return cos, sin


def _lambda_init(layer):
    return 0.8 - 0.6 * math.exp(-0.3 * layer)


def kernel(x_prompt, x_sample, state_ssd, cache_k, cache_v, c, c_ctx, mod_w, mod_b, norm_mix_g, norm_ffn_g, ssd_in_w, ssd_conv_w, ssd_conv_b, ssd_dt_bias, ssd_a_log, ssd_d, ssd_norm_g, ssd_out_w, att_qkv_w, att_lambda, att_subln_g, att_out_w, ffn_up_w, ffn_conv_w, ffn_conv_b, ffn_down_w, final_norm_g):
    n_ssd = ssd_in_w.shape[0]
    n_att = att_qkv_w.shape[0]
    x = jnp.concatenate([x_prompt.reshape(N_CTX, D_MODEL), x_sample.reshape(N_LAT, D_MODEL)], axis=0)

    cpad = jnp.concatenate([c_ctx[None], c, jnp.zeros((MOD_ROWS - 1 - DEC_BATCH, D_MODEL), F32)], axis=0)
    mod = _modulation(cpad, mod_w, mod_b)
    mod3 = mod.reshape(DEPTH * MOD_ROWS * 6, 1, D_MODEL)

    cos_t, sin_t = _rope_token_tables()
    expand = (jnp.arange(LANES)[:, None] == (jnp.arange(SSD_INNER)[None, :] // SSD_HEAD_DIM)).astype(BF16)
    expand = jnp.concatenate([expand] * 3, axis=0)
    h0_lat = state_ssd.reshape(DEC_BATCH, n_ssd, 2, SSD_INNER, SSD_STATE)
    cache_k4 = cache_k.reshape(DEC_BATCH, n_att, PAST_LEN, D_MODEL)
    cache_v4 = cache_v.reshape(DEC_BATCH, n_att, PAST_LEN, D_MODEL)

    new_ssd, new_k, new_v = [], [], []
    for i in range(DEPTH):
        slot = i // 2
        if i % 2 == 0:
            in_w = ssd_in_w[slot]
            w_z = in_w[:, :SSD_INNER].astype(BF16)
            w_xbc = in_w[:, SSD_INNER:SSD_MAIN].astype(BF16)
            pad = jnp.zeros((D_MODEL, LANES - SSD_HEADS), F32)
            w_dt = jnp.concatenate([in_w[:, SSD_MAIN:SSD_MAIN + SSD_HEADS], pad,
                                    in_w[:, SSD_MAIN + SSD_HEADS:], pad], axis=1).astype(BF16)
            zpad = jnp.zeros((LANES - SSD_HEADS,), F32)
            dt_bias = jnp.concatenate([ssd_dt_bias[slot, 0], zpad, ssd_dt_bias[slot, 1], zpad]).reshape(1, 2 * LANES)
            a_log_pad = jnp.concatenate([ssd_a_log[slot], jnp.zeros((2, LANES - SSD_HEADS), F32)],
                                        axis=1).reshape(2, 1, LANES)
            d_x = jnp.repeat(ssd_d[slot], SSD_HEAD_DIM).reshape(1, SSD_INNER)
            z = _norm_mod_matmul(x, mod3, i, 0, 1, norm_mix_g[i], w_z, tn=1024, out_dtype=BF16, name="ssd_z")
            xbc, dt = _ssd_xbc(x, mod3, i, norm_mix_g[i], w_xbc, ssd_conv_w[slot], ssd_conv_b[slot], w_dt, dt_bias)
            y_ctx, st_ctx = _ssd_scan(xbc, dt, a_log_pad, expand, d_x, None, slot,
                                      nseq=BATCH, nc=SEQ // SSD_CHUNK, chunk0=0, name="ssd_scan_ctx")
            y_lat, _ = _ssd_scan(xbc, dt, a_log_pad, expand, d_x, h0_lat, slot,
                                 nseq=DEC_BATCH, nc=DEC_SEQ // SSD_CHUNK, chunk0=N_CTX // SSD_CHUNK,
                                 name="ssd_scan_lat")
            new_ssd.append(st_ctx.reshape(BATCH, 2, SSD_HEADS, SSD_HEAD_DIM, SSD_STATE))
            x = _ssd_out(y_ctx, y_lat, z, ssd_norm_g[slot], ssd_out_w[slot].astype(BF16), x, mod3, i)
        else:
            lam_init = _lambda_init(i)
            q, k, v, k_ctx, v_ctx = _qkv_proj(x, mod3, i, norm_mix_g[i], att_qkv_w[slot].astype(BF16),
                                              cos_t, sin_t)
            new_k.append(k_ctx.reshape(BATCH, SEQ, DA_HEADS, 2, DA_HEAD_DIM))
            new_v.append(v_ctx.reshape(BATCH, SEQ, DA_HEADS, HEAD_W))
            o_ctx, o_lat = _attention(q, k, v, cache_k4, cache_v4, slot, att_lambda[slot], att_subln_g[slot],
                                      lam_init)
            x = _matmul_residual(o_ctx, o_lat, att_out_w[slot].astype(BF16), x, mod3, i, 2, name="att_out")
        x = _ffn(x, mod3, i, norm_ffn_g[i], ffn_up_w[i].astype(BF16), ffn_conv_w[i], ffn_conv_b[i],
                 ffn_down_w[i].astype(BF16))

    y_prompt = _final_norm(x, final_norm_g, 0, N_CTX).reshape(BATCH, SEQ, D_MODEL)
    y_sample = _final_norm(x, final_norm_g, N_CTX, N_LAT).reshape(DEC_BATCH, DEC_SEQ, D_MODEL)
    return (y_prompt, y_sample, jnp.stack(new_ssd, axis=1), jnp.stack(new_k, axis=1), jnp.stack(new_v, axis=1))
```

```python
import functools
import math

import jax
import jax.numpy as jnp
from jax import lax
from jax.experimental import pallas as pl
from jax.experimental.pallas import tpu as pltpu

F32 = jnp.float32
BF16 = jnp.bfloat16

D_MODEL = 1024
BATCH = 16
SEQ = 256
DEPTH = 4
DEC_BATCH = 4
DEC_SEQ = 2048
PAST_LEN = 256
GRID_W = 64
EPS = 1e-6

SSD_INNER = 2048
SSD_HEAD_DIM = 64
SSD_HEADS = 32
SSD_GROUPS = 4
SSD_STATE = 128
SSD_CHUNK = 128
SSD_BC = SSD_GROUPS * SSD_STATE
SSD_CONV_CH = SSD_INNER + 2 * SSD_BC
SSD_MAIN = SSD_INNER + SSD_CONV_CH
HEADS_PER_GROUP = SSD_HEADS // SSD_GROUPS
GROUP_W = HEADS_PER_GROUP * SSD_HEAD_DIM

DA_HEAD_DIM = 64
DA_HEADS = 8
DA_SCALE = DA_HEAD_DIM ** -0.5
LOG2E = math.log2(math.e)
Q_PRESCALE = DA_SCALE * LOG2E
HEAD_W = 2 * DA_HEAD_DIM
ROPE_BASE = 10000.0
ROPE_AXIS_DIM = DA_HEAD_DIM // 2

FFN_HIDDEN = 2816

N_CTX = BATCH * SEQ
N_LAT = DEC_BATCH * DEC_SEQ
N_TOK = N_CTX + N_LAT
MOD_ROWS = 8
LANES = 128
SUBLANES = 8
VMEM_LIMIT = 48 * 1024 * 1024
BIG_VMEM_LIMIT = 56 * 1024 * 1024


def _params(*sem):
    return pltpu.CompilerParams(dimension_semantics=sem, vmem_limit_bytes=VMEM_LIMIT)


def _mod_row(i, tm):
    nctx = N_CTX // tm
    return jnp.where(i < nctx, 0, 1 + (i - nctx) // (DEC_SEQ // tm))


def _silu(x):
    return x * jax.nn.sigmoid(x)


def _softplus(x):
    return jnp.maximum(x, 0.0) + jnp.log1p(jnp.exp(-jnp.abs(x)))


def _split3(x):
    hi = x.astype(BF16)
    r1 = x - hi.astype(F32)
    mid = r1.astype(BF16)
    lo = (r1 - mid.astype(F32)).astype(BF16)
    return hi, mid, lo


def _dot(a, b):
    return jnp.dot(a, b, preferred_element_type=F32)


def _dot_nt(a, b):
    return lax.dot_general(a, b, (((1,), (1,)), ((), ())), preferred_element_type=F32)


def _norm_mod(x, g, sc, sh):
    r = x * lax.rsqrt(jnp.mean(x * x, axis=-1, keepdims=True) + EPS)
    return ((r * g) * (1.0 + sc) + sh).astype(BF16)


def _mod_kernel(c_ref, w_ref, b_ref, o_ref):
    s = _silu(c_ref[...]).astype(BF16)
    o_ref[...] = _dot(s, w_ref[...].astype(BF16)) + b_ref[...]


def _modulation(cpad, mod_w, mod_b):
    tn = 1024
    n = 6 * D_MODEL
    return pl.pallas_call(
        _mod_kernel,
        grid=(DEPTH, n // tn),
        in_specs=[
            pl.BlockSpec((MOD_ROWS, D_MODEL), lambda l, j: (0, 0)),
            pl.BlockSpec((None, D_MODEL, tn), lambda l, j: (l, 0, j)),
            pl.BlockSpec((None, 1, tn), lambda l, j: (l, 0, j)),
        ],
        out_specs=pl.BlockSpec((None, MOD_ROWS, tn), lambda l, j: (l, 0, j)),
        out_shape=jax.ShapeDtypeStruct((DEPTH, MOD_ROWS, n), F32),
        compiler_params=_params("parallel", "parallel"),
        name="modulation",
    )(cpad, mod_w, mod_b.reshape(DEPTH, 1, n))


def _mod_spec(layer, k, tm, ngrid):
    base = layer * MOD_ROWS * 6 + k
    if ngrid == 1:
        return pl.BlockSpec((None, 1, D_MODEL), lambda i: (base + 6 * _mod_row(i, tm), 0, 0))
    return pl.BlockSpec((None, 1, D_MODEL), lambda i, j: (base + 6 * _mod_row(i, tm), 0, 0))


def _nmm_kernel(x_ref, g_ref, sh_ref, sc_ref, w_ref, o_ref, h_ref):
    @pl.when(pl.program_id(1) == 0)
    def _():
        h_ref[...] = _norm_mod(x_ref[...], g_ref[...], sc_ref[...], sh_ref[...])

    o_ref[...] = _dot(h_ref[...], w_ref[...]).astype(o_ref.dtype)


def _nmm_in_specs(layer, kshift, kscale, tm, tn):
    return [
        pl.BlockSpec((tm, D_MODEL), lambda i, j: (i, 0)),
        pl.BlockSpec((1, D_MODEL), lambda i, j: (0, 0)),
        _mod_spec(layer, kshift, tm, 2),
        _mod_spec(layer, kscale, tm, 2),
        pl.BlockSpec((D_MODEL, tn), lambda i, j: (0, j)),
    ]


def _norm_mod_matmul(x, mod3, layer, kshift, kscale, g, w, *, tn, out_dtype, name):
    tm = 1024
    n = w.shape[1]
    return pl.pallas_call(
        _nmm_kernel,
        grid=(N_TOK // tm, n // tn),
        in_specs=_nmm_in_specs(layer, kshift, kscale, tm, tn),
        out_specs=pl.BlockSpec((tm, tn), lambda i, j: (i, j)),
        out_shape=jax.ShapeDtypeStruct((N_TOK, n), out_dtype),
        scratch_shapes=[pltpu.VMEM((tm, D_MODEL), BF16)],
        compiler_params=_params("parallel", "arbitrary"),
        name=name,
    )(x, g.reshape(1, D_MODEL), mod3, mod3, w)


def _nmm_join_kernel(xc_ref, xl_ref, g_ref, sh_ref, sc_ref, w_ref, o_ref, xo_ref, h_ref, *, n_ctx_blocks):
    @pl.when(pl.program_id(1) == 0)
    def _():
        x = jnp.where(pl.program_id(0) < n_ctx_blocks, xc_ref[...], xl_ref[...])
        xo_ref[...] = x
        h_ref[...] = _norm_mod(x, g_ref[...], sc_ref[...], sh_ref[...])

    o_ref[...] = _dot(h_ref[...], w_ref[...]).astype(o_ref.dtype)


def _norm_mod_matmul_join(x_ctx, x_lat, mod3, layer, kshift, kscale, g, w, *, tn, out_dtype, name):
    tm = 1024
    n = w.shape[1]
    nctx = N_CTX // tm
    specs = _nmm_in_specs(layer, kshift, kscale, tm, tn)
    return pl.pallas_call(
        functools.partial(_nmm_join_kernel, n_ctx_blocks=nctx),
        grid=(N_TOK // tm, n // tn),
        in_specs=[pl.BlockSpec((tm, D_MODEL), lambda i, j: (jnp.minimum(i, nctx - 1), 0)),
                  pl.BlockSpec((tm, D_MODEL), lambda i, j: (jnp.maximum(i - nctx, 0), 0))] + specs[1:],
        out_specs=[pl.BlockSpec((tm, tn), lambda i, j: (i, j)),
                   pl.BlockSpec((tm, D_MODEL), lambda i, j: (i, 0))],
        out_shape=[jax.ShapeDtypeStruct((N_TOK, n), out_dtype),
                   jax.ShapeDtypeStruct((N_TOK, D_MODEL), F32)],
        scratch_shapes=[pltpu.VMEM((tm, D_MODEL), BF16)],
        compiler_params=_params("parallel", "arbitrary"),
        name=name,
    )(x_ctx, x_lat, g.reshape(1, D_MODEL), mod3, mod3, w)


def _qkv_kernel(*refs, n_ctx_blocks, n_unused):
    x_ref, g_ref, sh_ref, sc_ref, w_ref, cos_ref, sin_ref = refs[:7]
    q_ref, k_ref, v_ref, kctx_ref, vctx_ref, h_ref = refs[7 + n_unused:]
    j = pl.program_id(1)
    is_ctx = pl.program_id(0) < n_ctx_blocks

    @pl.when(j == 0)
    def _():
        h_ref[...] = _norm_mod(x_ref[...], g_ref[...], sc_ref[...], sh_ref[...])

    u = _dot(h_ref[...], w_ref[...])

    def rope_into(o_refs, scale):
        cos = cos_ref[...] * scale
        sin = sin_ref[...] * scale
        lane = lax.broadcasted_iota(jnp.int32, cos.shape, 1)
        half = ROPE_AXIS_DIM // 2
        first = (lane % ROPE_AXIS_DIM) < half
        for k in range(u.shape[1] // HEAD_W):
            cols = slice(k * HEAD_W, (k + 1) * HEAD_W)
            uk = u[:, cols]
            partner = jnp.where(first, pltpu.roll(uk, HEAD_W - half, 1), pltpu.roll(uk, half, 1))
            r = uk * cos + partner * sin
            for o_ref in o_refs:
                if len(o_ref.shape) == 3:
                    o_ref[:, :, cols] = r.reshape(o_ref.shape[0], SEQ, HEAD_W)
                else:
                    o_ref[:, cols] = r.astype(o_ref.dtype)

    @pl.when(j == 0)
    def _():
        rope_into([q_ref], Q_PRESCALE)

    @pl.when(jnp.logical_and(j == 1, is_ctx))
    def _():
        rope_into([k_ref, kctx_ref], 1.0)

    @pl.when(jnp.logical_and(j == 1, jnp.logical_not(is_ctx)))
    def _():
        rope_into([k_ref], 1.0)

    @pl.when(j == 2)
    def _():
        v_ref[...] = u.astype(v_ref.dtype)

    @pl.when(jnp.logical_and(j == 2, is_ctx))
    def _():
        vctx_ref[...] = u.reshape(vctx_ref.shape)


def _qkv_proj(x, mod3, layer, g, w, cos_t, sin_t, slot, n_slots, caches):
    tm = 1024
    nctx = N_CTX // tm
    nseq = tm // SEQ
    out = pl.BlockSpec((tm, D_MODEL), lambda i, j: (i, 0))
    ctx_out = pl.BlockSpec((nseq, None, SEQ, D_MODEL), lambda i, j: (jnp.minimum(i, nctx - 1), slot, 0, 0))
    ctx_shape = jax.ShapeDtypeStruct((BATCH, n_slots, SEQ, D_MODEL), F32)
    in_specs = _nmm_in_specs(layer, 0, 1, tm, D_MODEL) + [pl.BlockSpec((tm, HEAD_W), lambda i, j: (i, 0))] * 2
    args = [x, g.reshape(1, D_MODEL), mod3, mod3, w, cos_t, sin_t]
    aliases = {}
    if caches is not None:
        in_specs += [pl.BlockSpec(memory_space=pl.ANY)] * 2
        aliases = {len(args): 3, len(args) + 1: 4}
        args += list(caches)
    return pl.pallas_call(
        functools.partial(_qkv_kernel, n_ctx_blocks=nctx, n_unused=len(aliases)),
        grid=(N_TOK // tm, 3),
        in_specs=in_specs,
        out_specs=[out, out, out, ctx_out, ctx_out],
        out_shape=[jax.ShapeDtypeStruct((N_TOK, D_MODEL), BF16),
                   jax.ShapeDtypeStruct((N_TOK, D_MODEL), BF16),
                   jax.ShapeDtypeStruct((N_TOK, D_MODEL), BF16),
                   ctx_shape, ctx_shape],
        scratch_shapes=[pltpu.VMEM((tm, D_MODEL), BF16)],
        input_output_aliases=aliases,
        compiler_params=pltpu.CompilerParams(dimension_semantics=("arbitrary", "arbitrary"),
                                             vmem_limit_bytes=BIG_VMEM_LIMIT),
        name="att_qkv",
    )(*args)


PAD = SUBLANES


def _block_seqlen(i, tm):
    return jnp.where(i < N_CTX // tm, SEQ, DEC_SEQ)


def _project_with_halo(h, hh, w, u_ref, tm):
    u_ref[PAD:PAD + tm, :] = _dot(h, w)
    uh = _dot(hh, w)
    u_ref[PAD - 1:PAD, :] = uh[SUBLANES - 1:SUBLANES, :]
    u_ref[PAD + tm:PAD + tm + 1, :] = uh[SUBLANES:SUBLANES + 1, :]


def _conv_chunks(u_refs, cws, cbs, row0, seqmask, tm, rows, emit):
    ridx = lax.broadcasted_iota(jnp.int32, (rows, 1), 0)
    for r0 in range(0, tm, rows):
        outs = []
        for u_ref, cw, cb in zip(u_refs, cws, cbs):
            prev = u_ref[pl.ds(PAD - 1 + r0, rows), :]
            cur = u_ref[pl.ds(PAD + r0, rows), :]
            nxt = u_ref[pl.ds(PAD + 1 + r0, rows), :]
            if r0 % SEQ == 0:
                starts = ((row0 + r0) & seqmask) == 0
                prev = jnp.where(jnp.logical_and(ridx == 0, starts), 0.0, prev)
            if (r0 + rows) % SEQ == 0:
                ends = ((row0 + r0 + rows) & seqmask) == 0
                nxt = jnp.where(jnp.logical_and(ridx == rows - 1, ends), 0.0, nxt)
            outs.append(prev * cw[0:1, :] + cur * cw[1:2, :] + nxt * cw[2:3, :] + cb)
        emit(r0, outs)


def _halo_x_specs(tm):
    per = tm // SUBLANES
    last = N_TOK // SUBLANES - 1
    return [
        pl.BlockSpec((tm, D_MODEL), lambda i, j: (i, 0)),
        pl.BlockSpec((SUBLANES, D_MODEL), lambda i, j: (jnp.maximum(i * per - 1, 0), 0)),
        pl.BlockSpec((SUBLANES, D_MODEL), lambda i, j: (jnp.minimum((i + 1) * per, last), 0)),
    ]


def _xbc_kernel(x_ref, xp_ref, xn_ref, g_ref, sh_ref, sc_ref, w_ref, cw_ref, cb_ref, wdt_ref, bdt_ref,
                o_ref, dt_ref, h_ref, hh_ref, u_ref, *, tm, rows):
    i = pl.program_id(0)

    @pl.when(pl.program_id(1) == 0)
    def _():
        g, sc, sh = g_ref[...], sc_ref[...], sh_ref[...]
        h_ref[...] = _norm_mod(x_ref[...], g, sc, sh)
        hh_ref[...] = _norm_mod(jnp.concatenate([xp_ref[...], xn_ref[...]], axis=0), g, sc, sh)
        dt_ref[...] = _softplus(_dot(h_ref[...], wdt_ref[...]) + bdt_ref[...])

    _project_with_halo(h_ref[...], hh_ref[...], w_ref[...], u_ref, tm)

    def emit(r0, outs):
        o_ref[pl.ds(r0, rows), :] = _silu(outs[0])

    _conv_chunks([u_ref], [cw_ref[...]], [cb_ref[...]], i * tm, _block_seqlen(i, tm) - 1, tm, rows, emit)


def _ssd_xbc(x, mod3, layer, g, w_xbc, conv_w, conv_b, w_dt, dt_bias):
    tm, tc, rows = 1024, 512, 32
    ndt = w_dt.shape[1]
    return pl.pallas_call(
        functools.partial(_xbc_kernel, tm=tm, rows=rows),
        grid=(N_TOK // tm, SSD_CONV_CH // tc),
        in_specs=_halo_x_specs(tm) + [
            pl.BlockSpec((1, D_MODEL), lambda i, j: (0, 0)),
            _mod_spec(layer, 0, tm, 2),
            _mod_spec(layer, 1, tm, 2),
            pl.BlockSpec((D_MODEL, tc), lambda i, j: (0, j)),
            pl.BlockSpec((3, tc), lambda i, j: (0, j)),
            pl.BlockSpec((1, tc), lambda i, j: (0, j)),
            pl.BlockSpec((D_MODEL, ndt), lambda i, j: (0, 0)),
            pl.BlockSpec((1, ndt), lambda i, j: (0, 0)),
        ],
        out_specs=[
            pl.BlockSpec((tm, tc), lambda i, j: (i, j)),
            pl.BlockSpec((tm, ndt), lambda i, j: (i, 0)),
        ],
        out_shape=[
            jax.ShapeDtypeStruct((N_TOK, SSD_CONV_CH), F32),
            jax.ShapeDtypeStruct((N_TOK, ndt), F32),
        ],
        scratch_shapes=[pltpu.VMEM((tm, D_MODEL), BF16), pltpu.VMEM((2 * SUBLANES, D_MODEL), BF16),
                        pltpu.VMEM((tm + 2 * SUBLANES, tc), F32)],
        compiler_params=_params("parallel", "arbitrary"),
        name="ssd_xbc",
    )(x, x, x, g.reshape(1, D_MODEL), mod3, mod3, w_xbc, conv_w, conv_b.reshape(1, SSD_CONV_CH), w_dt, dt_bias)


def _ffn_kernel(x_ref, xp_ref, xn_ref, g_ref, sh_ref, sc_ref, gate_ref, wa_ref, wv_ref,
                cwa_ref, cba_ref, cwv_ref, cbv_ref, wd_ref, o_ref,
                h_ref, hh_ref, ua_ref, uv_ref, act_ref, *, tm, th, rows):
    i = pl.program_id(0)
    j = pl.program_id(1)

    @pl.when(j == 0)
    def _():
        g, sc, sh = g_ref[...], sc_ref[...], sh_ref[...]
        h_ref[...] = _norm_mod(x_ref[...], g, sc, sh)
        hh_ref[...] = _norm_mod(jnp.concatenate([xp_ref[...], xn_ref[...]], axis=0), g, sc, sh)

    h = h_ref[...]
    hh = hh_ref[...]
    _project_with_halo(h, hh, wa_ref[...], ua_ref, tm)
    _project_with_halo(h, hh, wv_ref[...], uv_ref, tm)

    col0 = pl.multiple_of(j * th, th)

    def emit(r0, outs):
        act_ref[pl.ds(r0, rows), pl.ds(col0, th)] = (_silu(outs[0]) * outs[1]).astype(BF16)

    _conv_chunks([ua_ref, uv_ref], [cwa_ref[...], cwv_ref[...]], [cba_ref[...], cbv_ref[...]],
                 i * tm, _block_seqlen(i, tm) - 1, tm, rows, emit)

    @pl.when(j == pl.num_programs(1) - 1)
    def _():
        o_ref[...] = x_ref[...] + gate_ref[...] * _dot(act_ref[...], wd_ref[...])


def _ffn(x, mod3, layer, g, up_w, conv_w, conv_b, down_w):
    tm, th, rows = 1024, 256, 64
    nb = FFN_HIDDEN // th
    cb = conv_b.reshape(1, 2 * FFN_HIDDEN)
    return pl.pallas_call(
        functools.partial(_ffn_kernel, tm=tm, th=th, rows=rows),
        grid=(N_TOK // tm, nb),
        in_specs=_halo_x_specs(tm) + [
            pl.BlockSpec((1, D_MODEL), lambda i, j: (0, 0)),
            _mod_spec(layer, 3, tm, 2),
            _mod_spec(layer, 4, tm, 2),
            _mod_spec(layer, 5, tm, 2),
            pl.BlockSpec((D_MODEL, th), lambda i, j: (0, j)),
            pl.BlockSpec((D_MODEL, th), lambda i, j: (0, j + nb)),
            pl.BlockSpec((3, th), lambda i, j: (0, j)),
            pl.BlockSpec((1, th), lambda i, j: (0, j)),
            pl.BlockSpec((3, th), lambda i, j: (0, j + nb)),
            pl.BlockSpec((1, th), lambda i, j: (0, j + nb)),
            pl.BlockSpec((FFN_HIDDEN, D_MODEL), lambda i, j: (0, 0)),
        ],
        out_specs=pl.BlockSpec((tm, D_MODEL), lambda i, j: (i, 0)),
        out_shape=jax.ShapeDtypeStruct((N_TOK, D_MODEL), F32),
        scratch_shapes=[pltpu.VMEM((tm, D_MODEL), BF16), pltpu.VMEM((2 * SUBLANES, D_MODEL), BF16),
                        pltpu.VMEM((tm + 2 * SUBLANES, th), F32), pltpu.VMEM((tm + 2 * SUBLANES, th), F32),
                        pltpu.VMEM((tm, FFN_HIDDEN), BF16)],
        compiler_params=_params("parallel", "arbitrary"),
        name="ffn",
    )(x, x, x, g.reshape(1, D_MODEL), mod3, mod3, mod3, up_w, up_w, conv_w, cb, conv_w, cb, down_w)


def _mm_res_kernel(ac_ref, al_ref, w_ref, x_ref, gate_ref, o_ref, *, n_ctx_blocks):
    a = jnp.where(pl.program_id(0) < n_ctx_blocks, ac_ref[...], al_ref[...])
    o_ref[...] = x_ref[...] + gate_ref[...] * _dot(a, w_ref[...])


def _two_source_specs(block, tm):
    nctx = N_CTX // tm
    return [pl.BlockSpec(block, lambda i: (jnp.minimum(i, nctx - 1), 0)),
            pl.BlockSpec(block, lambda i: (jnp.maximum(i - nctx, 0), 0))]


def _matmul_residual(a_ctx, a_lat, w, x, mod3, layer, kgate, *, name):
    tm = 512
    k = w.shape[0]
    return pl.pallas_call(
        functools.partial(_mm_res_kernel, n_ctx_blocks=N_CTX // tm),
        grid=(N_TOK // tm,),
        in_specs=_two_source_specs((tm, k), tm) + [
            pl.BlockSpec((k, D_MODEL), lambda i: (0, 0)),
            pl.BlockSpec((tm, D_MODEL), lambda i: (i, 0)),
            _mod_spec(layer, kgate, tm, 1),
        ],
        out_specs=pl.BlockSpec((tm, D_MODEL), lambda i: (i, 0)),
        out_shape=jax.ShapeDtypeStruct((N_TOK, D_MODEL), F32),
        compiler_params=_params("parallel"),
        name=name,
    )(a_ctx, a_lat, w, x, mod3)


def _scan_kernel(*refs, nc, zero_init, n_unused):
    x_ref, b_ref, c_ref, dt_ref, alog_ref, e_ref, dx_ref = refs[:7]
    h0_ref = None if zero_init else refs[7]
    y_ref, hl_ref, st_ref, yt_ref = refs[7 + (not zero_init) + n_unused:]
    d = pl.program_id(1)
    c = pl.program_id(2)
    q = SSD_CHUNK

    @pl.when(c == 0)
    def _():
        if zero_init:
            st_ref[...] = jnp.zeros_like(st_ref)
        else:
            st_ref[...] = h0_ref[...].T

    half = SSD_HEAD_DIM
    sgn = 1 - 2 * d
    ii = lax.broadcasted_iota(jnp.int32, (q, q), 0)
    jj = lax.broadcasted_iota(jnp.int32, (q, q), 1)
    tri_b = jnp.where(((ii - jj) * sgn) >= 0, 1.0, 0.0).astype(BF16)
    jl = jj & (half - 1)
    tri_mix = [((ii - (half * jh + jl)) * sgn) >= 0 for jh in range(2)]
    lane = lax.broadcasted_iota(jnp.int32, (half, LANES), 1)
    low = lane < half

    def split_cat(v):
        return jnp.concatenate(_split3(v), axis=1)

    dtb = dt_ref[...]
    da = dtb * (-jnp.exp(alog_ref[...]) * LOG2E)
    cs3 = _dot(tri_b, split_cat(da))
    cs = cs3[:, 0:LANES] + cs3[:, LANES:2 * LANES] + cs3[:, 2 * LANES:]
    e3 = e_ref[...]
    dt_x = _dot(split_cat(dtb), e3)
    cs_x = _dot(split_cat(cs), e3)
    tot_x = jnp.where(d == 0, cs_x[q - 1:q, :], cs_x[0:1, :])
    cs_rows = [jnp.concatenate([cs[half * jh:half * (jh + 1), :]] * 2, axis=0).T for jh in range(2)]

    xs = x_ref[...]
    xd = xs * dt_x
    xw = (xd * jnp.exp2(tot_x - cs_x)).astype(BF16)
    dec_out = jnp.exp2(cs_x)
    dec_chunk = jnp.exp2(tot_x)

    for g in range(SSD_GROUPS):
        gs = slice(g * GROUP_W, (g + 1) * GROUP_W)
        bg = b_ref[:, g * SSD_STATE:(g + 1) * SSD_STATE]
        cg = c_ref[:, g * SSD_STATE:(g + 1) * SSD_STATE].astype(BF16)
        bgb = bg.astype(BF16)
        cb_mix = [_dot_nt(cg, jnp.concatenate([bgb[half * jh:half * (jh + 1), :]] * 2, axis=0)) for jh in range(2)]
        st_g = st_ref[:, gs]
        y_off = _dot(cg, st_g.astype(BF16))
        new_g = _dot(bg.T.astype(BF16), xw[:, gs])
        st_ref[:, gs] = st_g * dec_chunk[:, gs] + new_g
        for p in range(HEADS_PER_GROUP // 2):
            h = g * HEADS_PER_GROUP + 2 * p
            cols = slice(h * SSD_HEAD_DIM, (h + 2) * SSD_HEAD_DIM)
            cs_col = cs_x[:, cols]
            xd2 = xd[:, cols]
            ms, rs = [], []
            for jh in range(2):
                row = jnp.where(low[0:1, :], cs_rows[jh][h:h + 1, :], cs_rows[jh][h + 1:h + 2, :])
                decay = jnp.where(tri_mix[jh], jnp.exp2(cs_col - row), 0.0)
                ms.append((cb_mix[jh] * decay).astype(BF16))
                xj = xd2[half * jh:half * (jh + 1), :]
                rs += [jnp.where(low, xj, 0.0), jnp.where(low, 0.0, xj)]
            lhs = jnp.concatenate(ms, axis=1)
            rhs = jnp.concatenate(rs, axis=0).astype(BF16)
            loc = slice(2 * p * SSD_HEAD_DIM, (2 * p + 2) * SSD_HEAD_DIM)
            yt_ref[:, cols] = _dot(lhs, rhs) + y_off[:, loc] * dec_out[:, cols]

    r0 = pl.multiple_of((c + d * (nc - 1 - 2 * c)) * q, q)

    @pl.when(d == 0)
    def _():
        y_ref[pl.ds(r0, q), :] = yt_ref[...] + dx_ref[...] * xs

    @pl.when(d == 1)
    def _():
        y_ref[pl.ds(r0, q), :] += yt_ref[...]

    @pl.when(c == nc - 1)
    def _():
        hl_ref[...] = st_ref[...].T


def _ssd_scan(xbc, dt, a_log_pad, expand, d_x, h0, slot, *, nseq, nc, chunk0, name, n_slots=1, states=None):
    q = SSD_CHUNK
    zero_init = h0 is None
    out_slot = slot if n_slots > 1 else 0

    def chunk(b, d, c):
        return chunk0 + b * nc + c + d * (nc - 1 - 2 * c)

    in_specs = [
        pl.BlockSpec((q, SSD_INNER), lambda b, d, c: (chunk(b, d, c), 0)),
        pl.BlockSpec((q, SSD_BC), lambda b, d, c: (chunk(b, d, c), SSD_INNER // SSD_BC)),
        pl.BlockSpec((q, SSD_BC), lambda b, d, c: (chunk(b, d, c), SSD_INNER // SSD_BC + 1)),
        pl.BlockSpec((q, LANES), lambda b, d, c: (chunk(b, d, c), d)),
        pl.BlockSpec((None, 1, LANES), lambda b, d, c: (d, 0, 0)),
        pl.BlockSpec((3 * LANES, SSD_INNER), lambda b, d, c: (0, 0)),
        pl.BlockSpec((1, SSD_INNER), lambda b, d, c: (0, 0)),
    ]
    args = [xbc, xbc, xbc, dt, a_log_pad, expand, d_x]
    if not zero_init:
        in_specs.append(pl.BlockSpec((None, None, None, SSD_INNER, SSD_STATE),
                                     lambda b, d, c: (b, slot, d, 0, 0)))
        args.append(h0)
    aliases = {}
    if states is not None:
        in_specs.append(pl.BlockSpec(memory_space=pl.ANY))
        args.append(states)
        aliases = {len(args) - 1: 1}
    return pl.pallas_call(
        functools.partial(_scan_kernel, nc=nc, zero_init=zero_init, n_unused=len(aliases)),
        grid=(nseq, 2, nc),
        in_specs=in_specs,
        out_specs=[
            pl.BlockSpec((nc * q, SSD_INNER), lambda b, d, c: (b, 0)),
            pl.BlockSpec((None, None, None, SSD_INNER, SSD_STATE), lambda b, d, c: (b, out_slot, d, 0, 0)),
        ],
        out_shape=[
            jax.ShapeDtypeStruct((nseq * nc * q, SSD_INNER), F32),
            jax.ShapeDtypeStruct((nseq, n_slots, 2, SSD_INNER, SSD_STATE), F32),
        ],
        scratch_shapes=[pltpu.VMEM((SSD_STATE, SSD_INNER), F32), pltpu.VMEM((q, SSD_INNER), F32)],
        input_output_aliases=aliases,
        compiler_params=pltpu.CompilerParams(dimension_semantics=("parallel", "arbitrary", "arbitrary"),
                                             vmem_limit_bytes=BIG_VMEM_LIMIT),
        name=name,
    )(*args)


def _ssd_out_kernel(yc_ref, yl_ref, z_ref, ng_ref, w_ref, x_ref, gate_ref, o_ref, *, n_ctx_blocks):
    y = jnp.where(pl.program_id(0) < n_ctx_blocks, yc_ref[...], yl_ref[...])
    y = y * _silu(z_ref[...].astype(F32))
    gw = SSD_INNER // SSD_GROUPS
    parts = []
    for g in range(SSD_GROUPS):
        yg = y[:, g * gw:(g + 1) * gw]
        parts.append(yg * lax.rsqrt(jnp.mean(yg * yg, axis=-1, keepdims=True) + EPS))
    yn = (jnp.concatenate(parts, axis=1) * ng_ref[...]).astype(BF16)
    o_ref[...] = x_ref[...] + gate_ref[...] * _dot(yn, w_ref[...])


def _ssd_out(y_ctx, y_lat, z, norm_g, out_w, x, mod3, layer):
    tm = 512
    return pl.pallas_call(
        functools.partial(_ssd_out_kernel, n_ctx_blocks=N_CTX // tm),
        grid=(N_TOK // tm,),
        in_specs=_two_source_specs((tm, SSD_INNER), tm) + [
            pl.BlockSpec((tm, SSD_INNER), lambda i: (i, 0)),
            pl.BlockSpec((1, SSD_INNER), lambda i: (0, 0)),
            pl.BlockSpec((SSD_INNER, D_MODEL), lambda i: (0, 0)),
            pl.BlockSpec((tm, D_MODEL), lambda i: (i, 0)),
            _mod_spec(layer, 2, tm, 1),
        ],
        out_specs=pl.BlockSpec((tm, D_MODEL), lambda i: (i, 0)),
        out_shape=jax.ShapeDtypeStruct((N_TOK, D_MODEL), F32),
        compiler_params=_params("parallel"),
        name="ssd_out",
    )(y_ctx, y_lat, z, norm_g.reshape(1, SSD_INNER), out_w, x, mod3)


def _diff_lambda(lam_ref, lam_init):
    lv = lam_ref[...]
    return (jnp.exp(jnp.sum(lv[0:1] * lv[1:2], axis=-1, keepdims=True))
            - jnp.exp(jnp.sum(lv[2:3] * lv[3:4], axis=-1, keepdims=True)) + lam_init)


def _attend_head(q, kb, vb, lam, g, lam_init):
    lane = lax.broadcasted_iota(jnp.int32, q.shape, 1)
    zero = jnp.zeros_like(q)
    probs = []
    for qm in (jnp.where(lane < DA_HEAD_DIM, q, zero), jnp.where(lane >= DA_HEAD_DIM, q, zero)):
        s = _dot_nt(qm, kb)
        p = jnp.exp2(s - jnp.max(s, axis=-1, keepdims=True))
        probs.append((p, jnp.sum(p, axis=-1, keepdims=True)))
    (p1, l1), (p2, l2) = probs
    w = (p1 - p2 * (lam * l1 / l2)).astype(BF16)
    o = _dot(w, vb) * (1.0 / l1)
    o = o * lax.rsqrt(jnp.mean(o * o, axis=-1, keepdims=True) + EPS)
    return o * g * (1.0 - lam_init)


def _attn_ctx_kernel(q_ref, k_ref, v_ref, lam_ref, g_ref, o_ref, *, lam_init):
    lam = _diff_lambda(lam_ref, lam_init)
    g = g_ref[...]
    for h in range(DA_HEADS):
        cols = slice(h * HEAD_W, (h + 1) * HEAD_W)
        o = _attend_head(q_ref[:, cols], k_ref[:, cols], v_ref[:, cols], lam, g, lam_init)
        o_ref[:, cols] = o.astype(o_ref.dtype)


def _attn_lat_kernel(q_ref, k_ref, v_ref, kc_ref, vc_ref, lam_ref, g_ref, o_ref, kb_ref, vb_ref, *, lam_init):
    @pl.when(pl.program_id(2) == 0)
    def _():
        kb_ref[0:DEC_SEQ, :] = k_ref[...]
        kb_ref[DEC_SEQ:, :] = kc_ref[...].astype(BF16)
        vb_ref[0:DEC_SEQ, :] = v_ref[...]
        vb_ref[DEC_SEQ:, :] = vc_ref[...].astype(BF16)

    o = _attend_head(q_ref[...], kb_ref[...], vb_ref[...], _diff_lambda(lam_ref, lam_init), g_ref[...], lam_init)
    o_ref[...] = o.astype(o_ref.dtype)


def _attention(q, k, v, cache_k4, cache_v4, slot, lam_vecs, subln_g, lam_init):
    common = [pl.BlockSpec((4, DA_HEAD_DIM), lambda *_: (0, 0)),
              pl.BlockSpec((1, HEAD_W), lambda *_: (0, 0))]
    g2 = subln_g.reshape(1, HEAD_W)

    seq_block = pl.BlockSpec((SEQ, D_MODEL), lambda b: (b, 0))
    ctx = pl.pallas_call(
        functools.partial(_attn_ctx_kernel, lam_init=lam_init),
        grid=(BATCH,),
        in_specs=[seq_block, seq_block, seq_block] + common,
        out_specs=seq_block,
        out_shape=jax.ShapeDtypeStruct((N_CTX, D_MODEL), BF16),
        compiler_params=_params("parallel"),
        name="attn_ctx",
    )(q, k, v, lam_vecs, g2)

    tq = 256
    nq = DEC_SEQ // tq
    q0 = N_CTX // tq
    s0 = N_CTX // DEC_SEQ
    keys = DEC_SEQ + PAST_LEN
    lat = pl.pallas_call(
        functools.partial(_attn_lat_kernel, lam_init=lam_init),
        grid=(DEC_BATCH, DA_HEADS, nq),
        in_specs=[
            pl.BlockSpec((tq, HEAD_W), lambda b, h, i: (q0 + b * nq + i, h)),
            pl.BlockSpec((DEC_SEQ, HEAD_W), lambda b, h, i: (s0 + b, h)),
            pl.BlockSpec((DEC_SEQ, HEAD_W), lambda b, h, i: (s0 + b, h)),
            pl.BlockSpec((None, None, PAST_LEN, HEAD_W), lambda b, h, i: (b, slot, 0, h)),
            pl.BlockSpec((None, None, PAST_LEN, HEAD_W), lambda b, h, i: (b, slot, 0, h)),
        ] + common,
        out_specs=pl.BlockSpec((tq, HEAD_W), lambda b, h, i: (b * nq + i, h)),
        out_shape=jax.ShapeDtypeStruct((N_LAT, D_MODEL), BF16),
        scratch_shapes=[pltpu.VMEM((keys, HEAD_W), BF16), pltpu.VMEM((keys, HEAD_W), BF16)],
        compiler_params=_params("parallel", "parallel", "arbitrary"),
        name="attn_lat",
    )(q, k, v, cache_k4, cache_v4, lam_vecs, g2)
    return ctx, lat


def _final_kernel(x_ref, g_ref, o_ref):
    x = x_ref[...]
    o_ref[...] = x * lax.rsqrt(jnp.mean(x * x, axis=-1, keepdims=True) + EPS) * g_ref[...]


def _final_norm(x, g, row0, nrows):
    tm = 1024
    b0 = row0 // tm
    return pl.pallas_call(
        _final_kernel,
        grid=(nrows // tm,),
        in_specs=[pl.BlockSpec((tm, D_MODEL), lambda i: (b0 + i, 0)),
                  pl.BlockSpec((1, D_MODEL), lambda i: (0, 0))],
        out_specs=pl.BlockSpec((tm, D_MODEL), lambda i: (i, 0)),
        out_shape=jax.ShapeDtypeStruct((nrows, D_MODEL), F32),
        compiler_params=_params("parallel"),
        name="final_norm",
    )(x, g.reshape(1, D_MODEL))


def _rope_token_tables():
    rows = DEC_SEQ // GRID_W
    row = jnp.repeat(jnp.arange(rows, dtype=F32), GRID_W)
    col = jnp.tile(jnp.arange(GRID_W, dtype=F32), rows)
    inv = 1.0 / (ROPE_BASE ** (jnp.arange(0, ROPE_AXIS_DIM, 2, dtype=F32) / ROPE_AXIS_DIM))
    ang_r = row[:, None] * inv
    ang_c = col[:, None] * inv
    cr, sr, cc, sc = jnp.cos(ang_r), jnp.sin(ang_r), jnp.cos(ang_c), jnp.sin(ang_c)
    cos64 = jnp.concatenate([cr, cr, cc, cc], axis=1)
    sin64 = jnp.concatenate([-sr, sr, -sc, sc], axis=1)
    cos_lat = jnp.tile(cos64, (DEC_BATCH, 2))
    sin_lat = jnp.tile(sin64, (DEC_BATCH, 2))
    cos = jnp.concatenate([jnp.ones((N_CTX, HEAD_W), F32), cos_lat], axis=0)
    sin = jnp.concatenate([jnp.zeros((N_CTX, HEAD_W), F32), sin_lat], axis=0)
    return cos, sin


def _lambda_init(layer):
    return 0.8 - 0.6 * math.exp(-0.3 * layer)


def kernel(x_prompt, x_sample, state_ssd, cache_k, cache_v, c, c_ctx, mod_w, mod_b, norm_mix_g, norm_ffn_g, ssd_in_w, ssd_conv_w, ssd_conv_b, ssd_dt_bias, ssd_a_log, ssd_d, ssd_norm_g, ssd_out_w, att_qkv_w, att_lambda, att_subln_g, att_out_w, ffn_up_w, ffn_conv_w, ffn_conv_b, ffn_down_w, final_norm_g):
    n_ssd = ssd_in_w.shape[0]
    n_att = att_qkv_w.shape[0]
    x = None

    cpad = jnp.concatenate([c_ctx[None], c, jnp.zeros((MOD_ROWS - 1 - DEC_BATCH, D_MODEL), F32)], axis=0)
    mod = _modulation(cpad, mod_w, mod_b)
    mod3 = mod.reshape(DEPTH * MOD_ROWS * 6, 1, D_MODEL)

    cos_t, sin_t = _rope_token_tables()
    expand = (jnp.arange(LANES)[:, None] == (jnp.arange(SSD_INNER)[None, :] // SSD_HEAD_DIM)).astype(BF16)
    expand = jnp.concatenate([expand] * 3, axis=0)
    h0_lat = state_ssd.reshape(DEC_BATCH, n_ssd, 2, SSD_INNER, SSD_STATE)
    cache_k4 = cache_k.reshape(DEC_BATCH, n_att, PAST_LEN, D_MODEL)
    cache_v4 = cache_v.reshape(DEC_BATCH, n_att, PAST_LEN, D_MODEL)

    new_ssd, new_kv = None, None
    for i in range(DEPTH):
        slot = i // 2
        if i % 2 == 0:
            in_w = ssd_in_w[slot]
            w_z = in_w[:, :SSD_INNER].astype(BF16)
            w_xbc = in_w[:, SSD_INNER:SSD_MAIN].astype(BF16)
            pad = jnp.zeros((D_MODEL, LANES - SSD_HEADS), F32)
            w_dt = jnp.concatenate([in_w[:, SSD_MAIN:SSD_MAIN + SSD_HEADS], pad,
                                    in_w[:, SSD_MAIN + SSD_HEADS:], pad], axis=1).astype(BF16)
            zpad = jnp.zeros((LANES - SSD_HEADS,), F32)
            dt_bias = jnp.concatenate([ssd_dt_bias[slot, 0], zpad, ssd_dt_bias[slot, 1], zpad]).reshape(1, 2 * LANES)
            a_log_pad = jnp.concatenate([ssd_a_log[slot], jnp.zeros((2, LANES - SSD_HEADS), F32)],
                                        axis=1).reshape(2, 1, LANES)
            d_x = jnp.repeat(ssd_d[slot], SSD_HEAD_DIM).reshape(1, SSD_INNER)
            if x is None:
                z, x = _norm_mod_matmul_join(x_prompt.reshape(N_CTX, D_MODEL), x_sample.reshape(N_LAT, D_MODEL),
                                             mod3, i, 0, 1, norm_mix_g[i], w_z, tn=1024, out_dtype=BF16,
                                             name="ssd_z_join")
            else:
                z = _norm_mod_matmul(x, mod3, i, 0, 1, norm_mix_g[i], w_z, tn=1024, out_dtype=BF16, name="ssd_z")
            xbc, dt = _ssd_xbc(x, mod3, i, norm_mix_g[i], w_xbc, ssd_conv_w[slot], ssd_conv_b[slot], w_dt, dt_bias)
            y_ctx, new_ssd = _ssd_scan(xbc, dt, a_log_pad, expand, d_x, None, slot,
                                       nseq=BATCH, nc=SEQ // SSD_CHUNK, chunk0=0, name="ssd_scan_ctx",
                                       n_slots=n_ssd, states=new_ssd)
            y_lat, _ = _ssd_scan(xbc, dt, a_log_pad, expand, d_x, h0_lat, slot,
                                 nseq=DEC_BATCH, nc=DEC_SEQ // SSD_CHUNK, chunk0=N_CTX // SSD_CHUNK,
                                 name="ssd_scan_lat")
            x = _ssd_out(y_ctx, y_lat, z, ssd_norm_g[slot], ssd_out_w[slot].astype(BF16), x, mod3, i)
        else:
            lam_init = _lambda_init(i)
            q, k, v, *new_kv = _qkv_proj(x, mod3, i, norm_mix_g[i], att_qkv_w[slot].astype(BF16),
                                         cos_t, sin_t, slot, n_att, new_kv)
            o_ctx, o_lat = _attention(q, k, v, cache_k4, cache_v4, slot, att_lambda[slot], att_subln_g[slot],
                                      lam_init)
            x = _matmul_residual(o_ctx, o_lat, att_out_w[slot].astype(BF16), x, mod3, i, 2, name="att_out")
        x = _ffn(x, mod3, i, norm_ffn_g[i], ffn_up_w[i].astype(BF16), ffn_conv_w[i], ffn_conv_b[i],
                 ffn_down_w[i].astype(BF16))

    y_prompt = _final_norm(x, final_norm_g, 0, N_CTX).reshape(BATCH, SEQ, D_MODEL)
    y_sample = _final_norm(x, final_norm_g, N_CTX, N_LAT).reshape(DEC_BATCH, DEC_SEQ, D_MODEL)
    return (y_prompt, y_sample,
            new_ssd.reshape(BATCH, n_ssd, 2, SSD_HEADS, SSD_HEAD_DIM, SSD_STATE),
            new_kv[0].reshape(BATCH, n_att, SEQ, DA_HEADS, 2, DA_HEAD_DIM),
            new_kv[1].reshape(BATCH, n_att, SEQ, DA_HEADS, HEAD_W))
```

```python
import functools
import math

import jax
import jax.numpy as jnp
from jax import lax
from jax.experimental import pallas as pl
from jax.experimental.pallas import tpu as pltpu

F32 = jnp.float32
BF16 = jnp.bfloat16

D_MODEL = 1024
BATCH = 16
SEQ = 256
DEPTH = 4
DEC_BATCH = 4
DEC_SEQ = 2048
PAST_LEN = 256
GRID_W = 64
EPS = 1e-6

SSD_INNER = 2048
SSD_HEAD_DIM = 64
SSD_HEADS = 32
SSD_GROUPS = 4
SSD_STATE = 128
SSD_CHUNK = 128
SSD_BC = SSD_GROUPS * SSD_STATE
SSD_CONV_CH = SSD_INNER + 2 * SSD_BC
SSD_MAIN = SSD_INNER + SSD_CONV_CH
HEADS_PER_GROUP = SSD_HEADS // SSD_GROUPS
GROUP_W = HEADS_PER_GROUP * SSD_HEAD_DIM

DA_HEAD_DIM = 64
DA_HEADS = 8
DA_SCALE = DA_HEAD_DIM ** -0.5
LOG2E = math.log2(math.e)
Q_PRESCALE = DA_SCALE * LOG2E
HEAD_W = 2 * DA_HEAD_DIM
ROPE_BASE = 10000.0
ROPE_AXIS_DIM = DA_HEAD_DIM // 2

FFN_HIDDEN = 2816

N_CTX = BATCH * SEQ
N_LAT = DEC_BATCH * DEC_SEQ
N_TOK = N_CTX + N_LAT
MOD_ROWS = 8
LANES = 128
SUBLANES = 8
VMEM_LIMIT = 48 * 1024 * 1024
BIG_VMEM_LIMIT = 56 * 1024 * 1024


def _params(*sem):
    return pltpu.CompilerParams(dimension_semantics=sem, vmem_limit_bytes=VMEM_LIMIT)


def _mod_row(i, tm):
    nctx = N_CTX // tm
    return jnp.where(i < nctx, 0, 1 + (i - nctx) // (DEC_SEQ // tm))


def _silu(x):
    return x * jax.nn.sigmoid(x)


def _softplus(x):
    return jnp.maximum(x, 0.0) + jnp.log1p(jnp.exp(-jnp.abs(x)))


def _split3(x):
    hi = x.astype(BF16)
    r1 = x - hi.astype(F32)
    mid = r1.astype(BF16)
    lo = (r1 - mid.astype(F32)).astype(BF16)
    return hi, mid, lo


def _dot(a, b):
    return jnp.dot(a, b, preferred_element_type=F32)


def _dot_nt(a, b):
    return lax.dot_general(a, b, (((1,), (1,)), ((), ())), preferred_element_type=F32)


def _norm_mod(x, g, sc, sh):
    r = x * lax.rsqrt(jnp.mean(x * x, axis=-1, keepdims=True) + EPS)
    return ((r * g) * (1.0 + sc) + sh).astype(BF16)


def _mod_kernel(c_ref, w_ref, b_ref, o_ref):
    s = _silu(c_ref[...]).astype(BF16)
    o_ref[...] = _dot(s, w_ref[...].astype(BF16)) + b_ref[...]


def _modulation(cpad, mod_w, mod_b):
    tn = 1024
    n = 6 * D_MODEL
    return pl.pallas_call(
        _mod_kernel,
        grid=(DEPTH, n // tn),
        in_specs=[
            pl.BlockSpec((MOD_ROWS, D_MODEL), lambda l, j: (0, 0)),
            pl.BlockSpec((None, D_MODEL, tn), lambda l, j: (l, 0, j)),
            pl.BlockSpec((None, 1, tn), lambda l, j: (l, 0, j)),
        ],
        out_specs=pl.BlockSpec((None, MOD_ROWS, tn), lambda l, j: (l, 0, j)),
        out_shape=jax.ShapeDtypeStruct((DEPTH, MOD_ROWS, n), F32),
        compiler_params=_params("parallel", "parallel"),
        name="modulation",
    )(cpad, mod_w, mod_b.reshape(DEPTH, 1, n))


def _mod_spec(layer, k, tm, ngrid):
    base = layer * MOD_ROWS * 6 + k
    if ngrid == 1:
        return pl.BlockSpec((None, 1, D_MODEL), lambda i: (base + 6 * _mod_row(i, tm), 0, 0))
    return pl.BlockSpec((None, 1, D_MODEL), lambda i, j: (base + 6 * _mod_row(i, tm), 0, 0))


def _nmm_kernel(x_ref, g_ref, sh_ref, sc_ref, w_ref, o_ref, h_ref):
    @pl.when(pl.program_id(1) == 0)
    def _():
        h_ref[...] = _norm_mod(x_ref[...], g_ref[...], sc_ref[...], sh_ref[...])

    o_ref[...] = _dot(h_ref[...], w_ref[...]).astype(o_ref.dtype)


def _nmm_in_specs(layer, kshift, kscale, tm, tn):
    return [
        pl.BlockSpec((tm, D_MODEL), lambda i, j: (i, 0)),
        pl.BlockSpec((1, D_MODEL), lambda i, j: (0, 0)),
        _mod_spec(layer, kshift, tm, 2),
        _mod_spec(layer, kscale, tm, 2),
        pl.BlockSpec((D_MODEL, tn), lambda i, j: (0, j)),
    ]


def _norm_mod_matmul(x, mod3, layer, kshift, kscale, g, w, *, tn, out_dtype, name):
    tm = 1024
    n = w.shape[1]
    return pl.pallas_call(
        _nmm_kernel,
        grid=(N_TOK // tm, n // tn),
        in_specs=_nmm_in_specs(layer, kshift, kscale, tm, tn),
        out_specs=pl.BlockSpec((tm, tn), lambda i, j: (i, j)),
        out_shape=jax.ShapeDtypeStruct((N_TOK, n), out_dtype),
        scratch_shapes=[pltpu.VMEM((tm, D_MODEL), BF16)],
        compiler_params=_params("parallel", "arbitrary"),
        name=name,
    )(x, g.reshape(1, D_MODEL), mod3, mod3, w)


def _nmm_join_kernel(xc_ref, xl_ref, g_ref, sh_ref, sc_ref, w_ref, o_ref, xo_ref, h_ref, *, n_ctx_blocks):
    @pl.when(pl.program_id(1) == 0)
    def _():
        x = jnp.where(pl.program_id(0) < n_ctx_blocks, xc_ref[...], xl_ref[...])
        xo_ref[...] = x
        h_ref[...] = _norm_mod(x, g_ref[...], sc_ref[...], sh_ref[...])

    o_ref[...] = _dot(h_ref[...], w_ref[...]).astype(o_ref.dtype)


def _norm_mod_matmul_join(x_ctx, x_lat, mod3, layer, kshift, kscale, g, w, *, tn, out_dtype, name):
    tm = 1024
    n = w.shape[1]
    nctx = N_CTX // tm
    specs = _nmm_in_specs(layer, kshift, kscale, tm, tn)
    return pl.pallas_call(
        functools.partial(_nmm_join_kernel, n_ctx_blocks=nctx),
        grid=(N_TOK // tm, n // tn),
        in_specs=[pl.BlockSpec((tm, D_MODEL), lambda i, j: (jnp.minimum(i, nctx - 1), 0)),
                  pl.BlockSpec((tm, D_MODEL), lambda i, j: (jnp.maximum(i - nctx, 0), 0))] + specs[1:],
        out_specs=[pl.BlockSpec((tm, tn), lambda i, j: (i, j)),
                   pl.BlockSpec((tm, D_MODEL), lambda i, j: (i, 0))],
        out_shape=[jax.ShapeDtypeStruct((N_TOK, n), out_dtype),
                   jax.ShapeDtypeStruct((N_TOK, D_MODEL), F32)],
        scratch_shapes=[pltpu.VMEM((tm, D_MODEL), BF16)],
        compiler_params=_params("parallel", "arbitrary"),
        name=name,
    )(x_ctx, x_lat, g.reshape(1, D_MODEL), mod3, mod3, w)


def _qkv_kernel(*refs, n_ctx_blocks, n_unused):
    x_ref, g_ref, sh_ref, sc_ref, w_ref, cos_ref, sin_ref = refs[:7]
    q_ref, k_ref, v_ref, kctx_ref, vctx_ref, h_ref = refs[7 + n_unused:]
    j = pl.program_id(1)
    is_ctx = pl.program_id(0) < n_ctx_blocks

    @pl.when(j == 0)
    def _():
        h_ref[...] = _norm_mod(x_ref[...], g_ref[...], sc_ref[...], sh_ref[...])

    u = _dot(h_ref[...], w_ref[...])

    def rope_into(o_refs, scale):
        cos = cos_ref[...] * scale
        sin = sin_ref[...] * scale
        lane = lax.broadcasted_iota(jnp.int32, cos.shape, 1)
        half = ROPE_AXIS_DIM // 2
        first = (lane % ROPE_AXIS_DIM) < half
        for k in range(u.shape[1] // HEAD_W):
            cols = slice(k * HEAD_W, (k + 1) * HEAD_W)
            uk = u[:, cols]
            partner = jnp.where(first, pltpu.roll(uk, HEAD_W - half, 1), pltpu.roll(uk, half, 1))
            r = uk * cos + partner * sin
            for o_ref in o_refs:
                if len(o_ref.shape) == 3:
                    o_ref[:, :, cols] = r.reshape(o_ref.shape[0], SEQ, HEAD_W)
                else:
                    o_ref[:, cols] = r.astype(o_ref.dtype)

    @pl.when(j == 0)
    def _():
        rope_into([q_ref], Q_PRESCALE)

    @pl.when(jnp.logical_and(j == 1, is_ctx))
    def _():
        rope_into([k_ref, kctx_ref], 1.0)

    @pl.when(jnp.logical_and(j == 1, jnp.logical_not(is_ctx)))
    def _():
        rope_into([k_ref], 1.0)

    @pl.when(j == 2)
    def _():
        v_ref[...] = u.astype(v_ref.dtype)

    @pl.when(jnp.logical_and(j == 2, is_ctx))
    def _():
        vctx_ref[...] = u.reshape(vctx_ref.shape)


def _qkv_proj(x, mod3, layer, g, w, cos_t, sin_t, slot, n_slots, caches):
    tm = 1024
    nctx = N_CTX // tm
    nseq = tm // SEQ
    out = pl.BlockSpec((tm, D_MODEL), lambda i, j: (i, 0))
    ctx_out = pl.BlockSpec((nseq, None, SEQ, D_MODEL), lambda i, j: (jnp.minimum(i, nctx - 1), slot, 0, 0))
    ctx_shape = jax.ShapeDtypeStruct((BATCH, n_slots, SEQ, D_MODEL), F32)
    in_specs = _nmm_in_specs(layer, 0, 1, tm, D_MODEL) + [pl.BlockSpec((tm, HEAD_W), lambda i, j: (i, 0))] * 2
    args = [x, g.reshape(1, D_MODEL), mod3, mod3, w, cos_t, sin_t]
    aliases = {}
    if caches is not None:
        in_specs += [pl.BlockSpec(memory_space=pl.ANY)] * 2
        aliases = {len(args): 3, len(args) + 1: 4}
        args += list(caches)
    return pl.pallas_call(
        functools.partial(_qkv_kernel, n_ctx_blocks=nctx, n_unused=len(aliases)),
        grid=(N_TOK // tm, 3),
        in_specs=in_specs,
        out_specs=[out, out, out, ctx_out, ctx_out],
        out_shape=[jax.ShapeDtypeStruct((N_TOK, D_MODEL), BF16),
                   jax.ShapeDtypeStruct((N_TOK, D_MODEL), BF16),
                   jax.ShapeDtypeStruct((N_TOK, D_MODEL), BF16),
                   ctx_shape, ctx_shape],
        scratch_shapes=[pltpu.VMEM((tm, D_MODEL), BF16)],
        input_output_aliases=aliases,
        compiler_params=pltpu.CompilerParams(dimension_semantics=("arbitrary", "arbitrary"),
                                             vmem_limit_bytes=BIG_VMEM_LIMIT),
        name="att_qkv",
    )(*args)


PAD = SUBLANES


def _block_seqlen(i, tm):
    return jnp.where(i < N_CTX // tm, SEQ, DEC_SEQ)


def _project_with_halo(h, hh, w, u_ref, tm):
    u_ref[PAD:PAD + tm, :] = _dot(h, w)
    uh = _dot(hh, w)
    u_ref[PAD - 1:PAD, :] = uh[SUBLANES - 1:SUBLANES, :]
    u_ref[PAD + tm:PAD + tm + 1, :] = uh[SUBLANES:SUBLANES + 1, :]


def _conv_chunks(u_refs, cws, cbs, row0, seqmask, tm, rows, emit):
    ridx = lax.broadcasted_iota(jnp.int32, (rows, 1), 0)
    for r0 in range(0, tm, rows):
        outs = []
        for u_ref, cw, cb in zip(u_refs, cws, cbs):
            prev = u_ref[pl.ds(PAD - 1 + r0, rows), :]
            cur = u_ref[pl.ds(PAD + r0, rows), :]
            nxt = u_ref[pl.ds(PAD + 1 + r0, rows), :]
            if r0 % SEQ == 0:
                starts = ((row0 + r0) & seqmask) == 0
                prev = jnp.where(jnp.logical_and(ridx == 0, starts), 0.0, prev)
            if (r0 + rows) % SEQ == 0:
                ends = ((row0 + r0 + rows) & seqmask) == 0
                nxt = jnp.where(jnp.logical_and(ridx == rows - 1, ends), 0.0, nxt)
            outs.append(prev * cw[0:1, :] + cur * cw[1:2, :] + nxt * cw[2:3, :] + cb)
        emit(r0, outs)


def _halo_x_specs(tm):
    per = tm // SUBLANES
    last = N_TOK // SUBLANES - 1
    return [
        pl.BlockSpec((tm, D_MODEL), lambda i, j: (i, 0)),
        pl.BlockSpec((SUBLANES, D_MODEL), lambda i, j: (jnp.maximum(i * per - 1, 0), 0)),
        pl.BlockSpec((SUBLANES, D_MODEL), lambda i, j: (jnp.minimum((i + 1) * per, last), 0)),
    ]


def _xbc_kernel(x_ref, xp_ref, xn_ref, g_ref, sh_ref, sc_ref, w_ref, cw_ref, cb_ref, wdt_ref, bdt_ref,
                o_ref, dt_ref, h_ref, hh_ref, u_ref, *, tm, rows):
    i = pl.program_id(0)

    @pl.when(pl.program_id(1) == 0)
    def _():
        g, sc, sh = g_ref[...], sc_ref[...], sh_ref[...]
        h_ref[...] = _norm_mod(x_ref[...], g, sc, sh)
        hh_ref[...] = _norm_mod(jnp.concatenate([xp_ref[...], xn_ref[...]], axis=0), g, sc, sh)
        dt_ref[...] = _softplus(_dot(h_ref[...], wdt_ref[...]) + bdt_ref[...])

    _project_with_halo(h_ref[...], hh_ref[...], w_ref[...], u_ref, tm)

    def emit(r0, outs):
        o_ref[pl.ds(r0, rows), :] = _silu(outs[0])

    _conv_chunks([u_ref], [cw_ref[...]], [cb_ref[...]], i * tm, _block_seqlen(i, tm) - 1, tm, rows, emit)


def _ssd_xbc(x, mod3, layer, g, w_xbc, conv_w, conv_b, w_dt, dt_bias):
    tm, tc, rows = 1024, 512, 32
    ndt = w_dt.shape[1]
    return pl.pallas_call(
        functools.partial(_xbc_kernel, tm=tm, rows=rows),
        grid=(N_TOK // tm, SSD_CONV_CH // tc),
        in_specs=_halo_x_specs(tm) + [
            pl.BlockSpec((1, D_MODEL), lambda i, j: (0, 0)),
            _mod_spec(layer, 0, tm, 2),
            _mod_spec(layer, 1, tm, 2),
            pl.BlockSpec((D_MODEL, tc), lambda i, j: (0, j)),
            pl.BlockSpec((3, tc), lambda i, j: (0, j)),
            pl.BlockSpec((1, tc), lambda i, j: (0, j)),
            pl.BlockSpec((D_MODEL, ndt), lambda i, j: (0, 0)),
            pl.BlockSpec((1, ndt), lambda i, j: (0, 0)),
        ],
        out_specs=[
            pl.BlockSpec((tm, tc), lambda i, j: (i, j)),
            pl.BlockSpec((tm, ndt), lambda i, j: (i, 0)),
        ],
        out_shape=[
            jax.ShapeDtypeStruct((N_TOK, SSD_CONV_CH), F32),
            jax.ShapeDtypeStruct((N_TOK, ndt), F32),
        ],
        scratch_shapes=[pltpu.VMEM((tm, D_MODEL), BF16), pltpu.VMEM((2 * SUBLANES, D_MODEL), BF16),
                        pltpu.VMEM((tm + 2 * SUBLANES, tc), F32)],
        compiler_params=_params("parallel", "arbitrary"),
        name="ssd_xbc",
    )(x, x, x, g.reshape(1, D_MODEL), mod3, mod3, w_xbc, conv_w, conv_b.reshape(1, SSD_CONV_CH), w_dt, dt_bias)


def _ffn_kernel(x_ref, xp_ref, xn_ref, g_ref, sh_ref, sc_ref, gate_ref, wa_ref, wv_ref,
                cwa_ref, cba_ref, cwv_ref, cbv_ref, wd_ref, o_ref,
                h_ref, hh_ref, ua_ref, uv_ref, act_ref, was_ref, wvs_ref, wds_ref, *, tm, th, rows):
    i = pl.program_id(0)
    j = pl.program_id(1)
    col0 = pl.multiple_of(j * th, th)

    @pl.when(i == 0)
    def _():
        was_ref[j] = wa_ref[...].astype(BF16)
        wvs_ref[j] = wv_ref[...].astype(BF16)
        wds_ref[pl.ds(col0, th), :] = wd_ref[...].astype(BF16)

    @pl.when(j == 0)
    def _():
        g, sc, sh = g_ref[...], sc_ref[...], sh_ref[...]
        h_ref[...] = _norm_mod(x_ref[...], g, sc, sh)
        hh_ref[...] = _norm_mod(jnp.concatenate([xp_ref[...], xn_ref[...]], axis=0), g, sc, sh)

    h = h_ref[...]
    hh = hh_ref[...]
    _project_with_halo(h, hh, was_ref[j], ua_ref, tm)
    _project_with_halo(h, hh, wvs_ref[j], uv_ref, tm)

    def emit(r0, outs):
        act_ref[pl.ds(r0, rows), pl.ds(col0, th)] = (_silu(outs[0]) * outs[1]).astype(BF16)

    _conv_chunks([ua_ref, uv_ref], [cwa_ref[...], cwv_ref[...]], [cba_ref[...], cbv_ref[...]],
                 i * tm, _block_seqlen(i, tm) - 1, tm, rows, emit)

    @pl.when(j == pl.num_programs(1) - 1)
    def _():
        o_ref[...] = x_ref[...] + gate_ref[...] * _dot(act_ref[...], wds_ref[...])


def _ffn(x, mod3, layer, g, up_w, conv_w, conv_b, down_w):
    tm, th, rows = 1024, 256, 64
    nb = FFN_HIDDEN // th
    cb = conv_b.reshape(1, 2 * FFN_HIDDEN)
    first = lambda i, j: jnp.where(i == 0, j, nb - 1)
    return pl.pallas_call(
        functools.partial(_ffn_kernel, tm=tm, th=th, rows=rows),
        grid=(N_TOK // tm, nb),
        in_specs=_halo_x_specs(tm) + [
            pl.BlockSpec((1, D_MODEL), lambda i, j: (0, 0)),
            _mod_spec(layer, 3, tm, 2),
            _mod_spec(layer, 4, tm, 2),
            _mod_spec(layer, 5, tm, 2),
            pl.BlockSpec((D_MODEL, th), lambda i, j: (0, first(i, j))),
            pl.BlockSpec((D_MODEL, th), lambda i, j: (0, first(i, j) + nb)),
            pl.BlockSpec((3, th), lambda i, j: (0, j)),
            pl.BlockSpec((1, th), lambda i, j: (0, j)),
            pl.BlockSpec((3, th), lambda i, j: (0, j + nb)),
            pl.BlockSpec((1, th), lambda i, j: (0, j + nb)),
            pl.BlockSpec((th, D_MODEL), lambda i, j: (first(i, j), 0)),
        ],
        out_specs=pl.BlockSpec((tm, D_MODEL), lambda i, j: (i, 0)),
        out_shape=jax.ShapeDtypeStruct((N_TOK, D_MODEL), F32),
        scratch_shapes=[pltpu.VMEM((tm, D_MODEL), BF16), pltpu.VMEM((2 * SUBLANES, D_MODEL), BF16),
                        pltpu.VMEM((tm + 2 * SUBLANES, th), F32), pltpu.VMEM((tm + 2 * SUBLANES, th), F32),
                        pltpu.VMEM((tm, FFN_HIDDEN), BF16),
                        pltpu.VMEM((nb, D_MODEL, th), BF16), pltpu.VMEM((nb, D_MODEL, th), BF16),
                        pltpu.VMEM((FFN_HIDDEN, D_MODEL), BF16)],
        compiler_params=pltpu.CompilerParams(dimension_semantics=("arbitrary", "arbitrary"),
                                             vmem_limit_bytes=BIG_VMEM_LIMIT),
        name="ffn",
    )(x, x, x, g.reshape(1, D_MODEL), mod3, mod3, mod3, up_w, up_w, conv_w, cb, conv_w, cb, down_w)


def _mm_res_kernel(ac_ref, al_ref, w_ref, x_ref, gate_ref, o_ref, *, n_ctx_blocks):
    a = jnp.where(pl.program_id(0) < n_ctx_blocks, ac_ref[...], al_ref[...])
    o_ref[...] = x_ref[...] + gate_ref[...] * _dot(a, w_ref[...])


def _two_source_specs(block, tm):
    nctx = N_CTX // tm
    return [pl.BlockSpec(block, lambda i: (jnp.minimum(i, nctx - 1), 0)),
            pl.BlockSpec(block, lambda i: (jnp.maximum(i - nctx, 0), 0))]


def _matmul_residual(a_ctx, a_lat, w, x, mod3, layer, kgate, *, name):
    tm = 512
    k = w.shape[0]
    return pl.pallas_call(
        functools.partial(_mm_res_kernel, n_ctx_blocks=N_CTX // tm),
        grid=(N_TOK // tm,),
        in_specs=_two_source_specs((tm, k), tm) + [
            pl.BlockSpec((k, D_MODEL), lambda i: (0, 0)),
            pl.BlockSpec((tm, D_MODEL), lambda i: (i, 0)),
            _mod_spec(layer, kgate, tm, 1),
        ],
        out_specs=pl.BlockSpec((tm, D_MODEL), lambda i: (i, 0)),
        out_shape=jax.ShapeDtypeStruct((N_TOK, D_MODEL), F32),
        compiler_params=_params("parallel"),
        name=name,
    )(a_ctx, a_lat, w, x, mod3)


def _scan_kernel(*refs, nc, zero_init, n_unused):
    x_ref, b_ref, c_ref, dt_ref, alog_ref, e_ref, dx_ref = refs[:7]
    h0_ref = None if zero_init else refs[7]
    y_ref, hl_ref, st_ref, yt_ref = refs[7 + (not zero_init) + n_unused:]
    d = pl.program_id(1)
    c = pl.program_id(2)
    q = SSD_CHUNK

    @pl.when(c == 0)
    def _():
        if zero_init:
            st_ref[...] = jnp.zeros_like(st_ref)
        else:
            st_ref[...] = h0_ref[...].T

    half = SSD_HEAD_DIM
    sgn = 1 - 2 * d
    ii = lax.broadcasted_iota(jnp.int32, (q, q), 0)
    jj = lax.broadcasted_iota(jnp.int32, (q, q), 1)
    tri_b = jnp.where(((ii - jj) * sgn) >= 0, 1.0, 0.0).astype(BF16)
    jl = jj & (half - 1)
    tri_mix = [((ii - (half * jh + jl)) * sgn) >= 0 for jh in range(2)]
    lane = lax.broadcasted_iota(jnp.int32, (half, LANES), 1)
    low = lane < half

    def split_cat(v):
        return jnp.concatenate(_split3(v), axis=1)

    dtb = dt_ref[...]
    da = dtb * (-jnp.exp(alog_ref[...]) * LOG2E)
    cs3 = _dot(tri_b, split_cat(da))
    cs = cs3[:, 0:LANES] + cs3[:, LANES:2 * LANES] + cs3[:, 2 * LANES:]
    e3 = e_ref[...]
    dt_x = _dot(split_cat(dtb)[:, 0:2 * LANES], e3[0:2 * LANES, :])
    cs_x = _dot(split_cat(cs), e3)
    tot_x = jnp.where(d == 0, cs_x[q - 1:q, :], cs_x[0:1, :])
    cs_rows = [jnp.concatenate([cs[half * jh:half * (jh + 1), :]] * 2, axis=0).T for jh in range(2)]

    xs = x_ref[...]
    xd = xs * dt_x
    xw = (xd * jnp.exp2(tot_x - cs_x)).astype(BF16)
    dec_out = jnp.exp2(cs_x)
    dec_chunk = jnp.exp2(tot_x)

    for g in range(SSD_GROUPS):
        gs = slice(g * GROUP_W, (g + 1) * GROUP_W)
        bg = b_ref[:, g * SSD_STATE:(g + 1) * SSD_STATE]
        cg = c_ref[:, g * SSD_STATE:(g + 1) * SSD_STATE].astype(BF16)
        bgb = bg.astype(BF16)
        cb_mix = [_dot_nt(cg, jnp.concatenate([bgb[half * jh:half * (jh + 1), :]] * 2, axis=0)) for jh in range(2)]
        st_g = st_ref[:, gs]
        y_off = _dot(cg, st_g.astype(BF16))
        new_g = _dot(bg.T.astype(BF16), xw[:, gs])
        st_ref[:, gs] = st_g * dec_chunk[:, gs] + new_g
        for p in range(HEADS_PER_GROUP // 2):
            h = g * HEADS_PER_GROUP + 2 * p
            cols = slice(h * SSD_HEAD_DIM, (h + 2) * SSD_HEAD_DIM)
            cs_col = cs_x[:, cols]
            xd2 = xd[:, cols]
            ms, rs = [], []
            for jh in range(2):
                row = jnp.where(low[0:1, :], cs_rows[jh][h:h + 1, :], cs_rows[jh][h + 1:h + 2, :])
                decay = jnp.where(tri_mix[jh], jnp.exp2(cs_col - row), 0.0)
                ms.append((cb_mix[jh] * decay).astype(BF16))
                xj = xd2[half * jh:half * (jh + 1), :]
                rs += [jnp.where(low, xj, 0.0), jnp.where(low, 0.0, xj)]
            lhs = jnp.concatenate(ms, axis=1)
            rhs = jnp.concatenate(rs, axis=0).astype(BF16)
            loc = slice(2 * p * SSD_HEAD_DIM, (2 * p + 2) * SSD_HEAD_DIM)
            yt_ref[:, cols] = _dot(lhs, rhs) + y_off[:, loc] * dec_out[:, cols]

    r0 = pl.multiple_of((c + d * (nc - 1 - 2 * c)) * q, q)

    @pl.when(d == 0)
    def _():
        y_ref[pl.ds(r0, q), :] = yt_ref[...] + dx_ref[...] * xs

    @pl.when(d == 1)
    def _():
        y_ref[pl.ds(r0, q), :] += yt_ref[...]

    @pl.when(c == nc - 1)
    def _():
        hl_ref[...] = st_ref[...].T


def _ssd_scan(xbc, dt, a_log_pad, expand, d_x, h0, slot, *, nseq, nc, chunk0, name, n_slots=1, states=None):
    q = SSD_CHUNK
    zero_init = h0 is None
    out_slot = slot if n_slots > 1 else 0

    def chunk(b, d, c):
        return chunk0 + b * nc + c + d * (nc - 1 - 2 * c)

    in_specs = [
        pl.BlockSpec((q, SSD_INNER), lambda b, d, c: (chunk(b, d, c), 0)),
        pl.BlockSpec((q, SSD_BC), lambda b, d, c: (chunk(b, d, c), SSD_INNER // SSD_BC)),
        pl.BlockSpec((q, SSD_BC), lambda b, d, c: (chunk(b, d, c), SSD_INNER // SSD_BC + 1)),
        pl.BlockSpec((q, LANES), lambda b, d, c: (chunk(b, d, c), d)),
        pl.BlockSpec((None, 1, LANES), lambda b, d, c: (d, 0, 0)),
        pl.BlockSpec((3 * LANES, SSD_INNER), lambda b, d, c: (0, 0)),
        pl.BlockSpec((1, SSD_INNER), lambda b, d, c: (0, 0)),
    ]
    args = [xbc, xbc, xbc, dt, a_log_pad, expand, d_x]
    if not zero_init:
        in_specs.append(pl.BlockSpec((None, None, None, SSD_INNER, SSD_STATE),
                                     lambda b, d, c: (b, slot, d, 0, 0)))
        args.append(h0)
    aliases = {}
    if states is not None:
        in_specs.append(pl.BlockSpec(memory_space=pl.ANY))
        args.append(states)
        aliases = {len(args) - 1: 1}
    return pl.pallas_call(
        functools.partial(_scan_kernel, nc=nc, zero_init=zero_init, n_unused=len(aliases)),
        grid=(nseq, 2, nc),
        in_specs=in_specs,
        out_specs=[
            pl.BlockSpec((nc * q, SSD_INNER), lambda b, d, c: (b, 0)),
            pl.BlockSpec((None, None, None, SSD_INNER, SSD_STATE), lambda b, d, c: (b, out_slot, d, 0, 0)),
        ],
        out_shape=[
            jax.ShapeDtypeStruct((nseq * nc * q, SSD_INNER), F32),
            jax.ShapeDtypeStruct((nseq, n_slots, 2, SSD_INNER, SSD_STATE), F32),
        ],
        scratch_shapes=[pltpu.VMEM((SSD_STATE, SSD_INNER), F32), pltpu.VMEM((q, SSD_INNER), F32)],
        input_output_aliases=aliases,
        compiler_params=pltpu.CompilerParams(dimension_semantics=("parallel", "arbitrary", "arbitrary"),
                                             vmem_limit_bytes=BIG_VMEM_LIMIT),
        name=name,
    )(*args)


def _ssd_out_kernel(yc_ref, yl_ref, z_ref, ng_ref, w_ref, x_ref, gate_ref, o_ref, *, n_ctx_blocks):
    y = jnp.where(pl.program_id(0) < n_ctx_blocks, yc_ref[...], yl_ref[...])
    y = y * _silu(z_ref[...].astype(F32))
    gw = SSD_INNER // SSD_GROUPS
    parts = []
    for g in range(SSD_GROUPS):
        yg = y[:, g * gw:(g + 1) * gw]
        parts.append(yg * lax.rsqrt(jnp.mean(yg * yg, axis=-1, keepdims=True) + EPS))
    yn = (jnp.concatenate(parts, axis=1) * ng_ref[...]).astype(BF16)
    o_ref[...] = x_ref[...] + gate_ref[...] * _dot(yn, w_ref[...])


def _ssd_out(y_ctx, y_lat, z, norm_g, out_w, x, mod3, layer):
    tm = 512
    return pl.pallas_call(
        functools.partial(_ssd_out_kernel, n_ctx_blocks=N_CTX // tm),
        grid=(N_TOK // tm,),
        in_specs=_two_source_specs((tm, SSD_INNER), tm) + [
            pl.BlockSpec((tm, SSD_INNER), lambda i: (i, 0)),
            pl.BlockSpec((1, SSD_INNER), lambda i: (0, 0)),
            pl.BlockSpec((SSD_INNER, D_MODEL), lambda i: (0, 0)),
            pl.BlockSpec((tm, D_MODEL), lambda i: (i, 0)),
            _mod_spec(layer, 2, tm, 1),
        ],
        out_specs=pl.BlockSpec((tm, D_MODEL), lambda i: (i, 0)),
        out_shape=jax.ShapeDtypeStruct((N_TOK, D_MODEL), F32),
        compiler_params=_params("parallel"),
        name="ssd_out",
    )(y_ctx, y_lat, z, norm_g.reshape(1, SSD_INNER), out_w, x, mod3)


def _diff_lambda(lam_ref, lam_init):
    lv = lam_ref[...]
    return (jnp.exp(jnp.sum(lv[0:1] * lv[1:2], axis=-1, keepdims=True))
            - jnp.exp(jnp.sum(lv[2:3] * lv[3:4], axis=-1, keepdims=True)) + lam_init)


def _attend_head(q, kb, vb, lam, g, lam_init):
    lane = lax.broadcasted_iota(jnp.int32, q.shape, 1)
    zero = jnp.zeros_like(q)
    probs = []
    for qm in (jnp.where(lane < DA_HEAD_DIM, q, zero), jnp.where(lane >= DA_HEAD_DIM, q, zero)):
        s = _dot_nt(qm, kb)
        p = jnp.exp2(s - jnp.max(s, axis=-1, keepdims=True))
        probs.append((p, jnp.sum(p, axis=-1, keepdims=True)))
    (p1, l1), (p2, l2) = probs
    w = (p1 - p2 * (lam * l1 / l2)).astype(BF16)
    o = _dot(w, vb) * (1.0 / l1)
    o = o * lax.rsqrt(jnp.mean(o * o, axis=-1, keepdims=True) + EPS)
    return o * g * (1.0 - lam_init)


def _attn_ctx_kernel(q_ref, k_ref, v_ref, lam_ref, g_ref, o_ref, *, lam_init):
    lam = _diff_lambda(lam_ref, lam_init)
    g = g_ref[...]
    for h in range(DA_HEADS):
        cols = slice(h * HEAD_W, (h + 1) * HEAD_W)
        o = _attend_head(q_ref[:, cols], k_ref[:, cols], v_ref[:, cols], lam, g, lam_init)
        o_ref[:, cols] = o.astype(o_ref.dtype)


def _attn_lat_kernel(q_ref, k_ref, v_ref, kc_ref, vc_ref, lam_ref, g_ref, o_ref, kb_ref, vb_ref, *, lam_init):
    @pl.when(pl.program_id(2) == 0)
    def _():
        kb_ref[0:DEC_SEQ, :] = k_ref[...]
        kb_ref[DEC_SEQ:, :] = kc_ref[...].astype(BF16)
        vb_ref[0:DEC_SEQ, :] = v_ref[...]
        vb_ref[DEC_SEQ:, :] = vc_ref[...].astype(BF16)

    o = _attend_head(q_ref[...], kb_ref[...], vb_ref[...], _diff_lambda(lam_ref, lam_init), g_ref[...], lam_init)
    o_ref[...] = o.astype(o_ref.dtype)


def _attention(q, k, v, cache_k4, cache_v4, slot, lam_vecs, subln_g, lam_init):
    common = [pl.BlockSpec((4, DA_HEAD_DIM), lambda *_: (0, 0)),
              pl.BlockSpec((1, HEAD_W), lambda *_: (0, 0))]
    g2 = subln_g.reshape(1, HEAD_W)

    seq_block = pl.BlockSpec((SEQ, D_MODEL), lambda b: (b, 0))
    ctx = pl.pallas_call(
        functools.partial(_attn_ctx_kernel, lam_init=lam_init),
        grid=(BATCH,),
        in_specs=[seq_block, seq_block, seq_block] + common,
        out_specs=seq_block,
        out_shape=jax.ShapeDtypeStruct((N_CTX, D_MODEL), BF16),
        compiler_params=_params("parallel"),
        name="attn_ctx",
    )(q, k, v, lam_vecs, g2)

    tq = 256
    nq = DEC_SEQ // tq
    q0 = N_CTX // tq
    s0 = N_CTX // DEC_SEQ
    keys = DEC_SEQ + PAST_LEN
    lat = pl.pallas_call(
        functools.partial(_attn_lat_kernel, lam_init=lam_init),
        grid=(DEC_BATCH, DA_HEADS, nq),
        in_specs=[
            pl.BlockSpec((tq, HEAD_W), lambda b, h, i: (q0 + b * nq + i, h)),
            pl.BlockSpec((DEC_SEQ, HEAD_W), lambda b, h, i: (s0 + b, h)),
            pl.BlockSpec((DEC_SEQ, HEAD_W), lambda b, h, i: (s0 + b, h)),
            pl.BlockSpec((None, None, PAST_LEN, HEAD_W), lambda b, h, i: (b, slot, 0, h)),
            pl.BlockSpec((None, None, PAST_LEN, HEAD_W), lambda b, h, i: (b, slot, 0, h)),
        ] + common,
        out_specs=pl.BlockSpec((tq, HEAD_W), lambda b, h, i: (b * nq + i, h)),
        out_shape=jax.ShapeDtypeStruct((N_LAT, D_MODEL), BF16),
        scratch_shapes=[pltpu.VMEM((keys, HEAD_W), BF16), pltpu.VMEM((keys, HEAD_W), BF16)],
        compiler_params=_params("parallel", "parallel", "arbitrary"),
        name="attn_lat",
    )(q, k, v, cache_k4, cache_v4, lam_vecs, g2)
    return ctx, lat


def _final_kernel(x_ref, g_ref, o_ref):
    x = x_ref[...]
    o_ref[...] = x * lax.rsqrt(jnp.mean(x * x, axis=-1, keepdims=True) + EPS) * g_ref[...]


def _final_norm(x, g, row0, nrows):
    tm = 1024
    b0 = row0 // tm
    return pl.pallas_call(
        _final_kernel,
        grid=(nrows // tm,),
        in_specs=[pl.BlockSpec((tm, D_MODEL), lambda i: (b0 + i, 0)),
                  pl.BlockSpec((1, D_MODEL), lambda i: (0, 0))],
        out_specs=pl.BlockSpec((tm, D_MODEL), lambda i: (i, 0)),
        out_shape=jax.ShapeDtypeStruct((nrows, D_MODEL), F32),
        compiler_params=_params("parallel"),
        name="final_norm",
    )(x, g.reshape(1, D_MODEL))


def _rope_token_tables():
    rows = DEC_SEQ // GRID_W
    row = jnp.repeat(jnp.arange(rows, dtype=F32), GRID_W)
    col = jnp.tile(jnp.arange(GRID_W, dtype=F32), rows)
    inv = 1.0 / (ROPE_BASE ** (jnp.arange(0, ROPE_AXIS_DIM, 2, dtype=F32) / ROPE_AXIS_DIM))
    ang_r = row[:, None] * inv
    ang_c = col[:, None] * inv
    cr, sr, cc, sc = jnp.cos(ang_r), jnp.sin(ang_r), jnp.cos(ang_c), jnp.sin(ang_c)
    cos64 = jnp.concatenate([cr, cr, cc, cc], axis=1)
    sin64 = jnp.concatenate([-sr, sr, -sc, sc], axis=1)
    cos_lat = jnp.tile(cos64, (DEC_BATCH, 2))
    sin_lat = jnp.tile(sin64, (DEC_BATCH, 2))
    cos = jnp.concatenate([jnp.ones((N_CTX, HEAD_W), F32), cos_lat], axis=0)
    sin = jnp.concatenate([jnp.zeros((N_CTX, HEAD_W), F32), sin_lat], axis=0)
    return cos, sin


def _lambda_init(layer):
    return 0.8 - 0.6 * math.exp(-0.3 * layer)


def kernel(x_prompt, x_sample, state_ssd, cache_k, cache_v, c, c_ctx, mod_w, mod_b, norm_mix_g, norm_ffn_g, ssd_in_w, ssd_conv_w, ssd_conv_b, ssd_dt_bias, ssd_a_log, ssd_d, ssd_norm_g, ssd_out_w, att_qkv_w, att_lambda, att_subln_g, att_out_w, ffn_up_w, ffn_conv_w, ffn_conv_b, ffn_down_w, final_norm_g):
    n_ssd = ssd_in_w.shape[0]
    n_att = att_qkv_w.shape[0]
    x = None

    cpad = jnp.concatenate([c_ctx[None], c, jnp.zeros((MOD_ROWS - 1 - DEC_BATCH, D_MODEL), F32)], axis=0)
    mod = _modulation(cpad, mod_w, mod_b)
    mod3 = mod.reshape(DEPTH * MOD_ROWS * 6, 1, D_MODEL)

    cos_t, sin_t = _rope_token_tables()
    expand = (jnp.arange(LANES)[:, None] == (jnp.arange(SSD_INNER)[None, :] // SSD_HEAD_DIM)).astype(BF16)
    expand = jnp.concatenate([expand] * 3, axis=0)
    h0_lat = state_ssd.reshape(DEC_BATCH, n_ssd, 2, SSD_INNER, SSD_STATE)
    cache_k4 = cache_k.reshape(DEC_BATCH, n_att, PAST_LEN, D_MODEL)
    cache_v4 = cache_v.reshape(DEC_BATCH, n_att, PAST_LEN, D_MODEL)

    new_ssd, new_kv = None, None
    for i in range(DEPTH):
        slot = i // 2
        if i % 2 == 0:
            in_w = ssd_in_w[slot]
            w_z = in_w[:, :SSD_INNER].astype(BF16)
            w_xbc = in_w[:, SSD_INNER:SSD_MAIN].astype(BF16)
            pad = jnp.zeros((D_MODEL, LANES - SSD_HEADS), F32)
            w_dt = jnp.concatenate([in_w[:, SSD_MAIN:SSD_MAIN + SSD_HEADS], pad,
                                    in_w[:, SSD_MAIN + SSD_HEADS:], pad], axis=1).astype(BF16)
            zpad = jnp.zeros((LANES - SSD_HEADS,), F32)
            dt_bias = jnp.concatenate([ssd_dt_bias[slot, 0], zpad, ssd_dt_bias[slot, 1], zpad]).reshape(1, 2 * LANES)
            a_log_pad = jnp.concatenate([ssd_a_log[slot], jnp.zeros((2, LANES - SSD_HEADS), F32)],
                                        axis=1).reshape(2, 1, LANES)
            d_x = jnp.repeat(ssd_d[slot], SSD_HEAD_DIM).reshape(1, SSD_INNER)
            if x is None:
                z, x = _norm_mod_matmul_join(x_prompt.reshape(N_CTX, D_MODEL), x_sample.reshape(N_LAT, D_MODEL),
                                             mod3, i, 0, 1, norm_mix_g[i], w_z, tn=1024, out_dtype=BF16,
                                             name="ssd_z_join")
            else:
                z = _norm_mod_matmul(x, mod3, i, 0, 1, norm_mix_g[i], w_z, tn=1024, out_dtype=BF16, name="ssd_z")
            xbc, dt = _ssd_xbc(x, mod3, i, norm_mix_g[i], w_xbc, ssd_conv_w[slot], ssd_conv_b[slot], w_dt, dt_bias)
            y_ctx, new_ssd = _ssd_scan(xbc, dt, a_log_pad, expand, d_x, None, slot,
                                       nseq=BATCH, nc=SEQ // SSD_CHUNK, chunk0=0, name="ssd_scan_ctx",
                                       n_slots=n_ssd, states=new_ssd)
            y_lat, _ = _ssd_scan(xbc, dt, a_log_pad, expand, d_x, h0_lat, slot,
                                 nseq=DEC_BATCH, nc=DEC_SEQ // SSD_CHUNK, chunk0=N_CTX // SSD_CHUNK,
                                 name="ssd_scan_lat")
            x = _ssd_out(y_ctx, y_lat, z, ssd_norm_g[slot], ssd_out_w[slot].astype(BF16), x, mod3, i)
        else:
            lam_init = _lambda_init(i)
            q, k, v, *new_kv = _qkv_proj(x, mod3, i, norm_mix_g[i], att_qkv_w[slot].astype(BF16),
                                         cos_t, sin_t, slot, n_att, new_kv)
            o_ctx, o_lat = _attention(q, k, v, cache_k4, cache_v4, slot, att_lambda[slot], att_subln_g[slot],
                                      lam_init)
            x = _matmul_residual(o_ctx, o_lat, att_out_w[slot].astype(BF16), x, mod3, i, 2, name="att_out")
        x = _ffn(x, mod3, i, norm_ffn_g[i], ffn_up_w[i], ffn_conv_w[i], ffn_conv_b[i], ffn_down_w[i])

    y_prompt = _final_norm(x, final_norm_g, 0, N_CTX).reshape(BATCH, SEQ, D_MODEL)
    y_sample = _final_norm(x, final_norm_g, N_CTX, N_LAT).reshape(DEC_BATCH, DEC_SEQ, D_MODEL)
    return (y_prompt, y_sample,
            new_ssd.reshape(BATCH, n_ssd, 2, SSD_HEADS, SSD_HEAD_DIM, SSD_STATE),
            new_kv[0].reshape(BATCH, n_att, SEQ, DA_HEADS, 2, DA_HEAD_DIM),
            new_kv[1].reshape(BATCH, n_att, SEQ, DA_HEADS, HEAD_W))
```

```python
import functools
import math

import jax
import jax.numpy as jnp
from jax import lax
from jax.experimental import pallas as pl
from jax.experimental.pallas import tpu as pltpu

F32 = jnp.float32
BF16 = jnp.bfloat16

D_MODEL = 1024
BATCH = 16
SEQ = 256
DEPTH = 4
DEC_BATCH = 4
DEC_SEQ = 2048
PAST_LEN = 256
GRID_W = 64
EPS = 1e-6

SSD_INNER = 2048
SSD_HEAD_DIM = 64
SSD_HEADS = 32
SSD_GROUPS = 4
SSD_STATE = 128
SSD_CHUNK = 128
SSD_BC = SSD_GROUPS * SSD_STATE
SSD_CONV_CH = SSD_INNER + 2 * SSD_BC
SSD_MAIN = SSD_INNER + SSD_CONV_CH
HEADS_PER_GROUP = SSD_HEADS // SSD_GROUPS
GROUP_W = HEADS_PER_GROUP * SSD_HEAD_DIM

DA_HEAD_DIM = 64
DA_HEADS = 8
DA_SCALE = DA_HEAD_DIM ** -0.5
LOG2E = math.log2(math.e)
Q_PRESCALE = DA_SCALE * LOG2E
HEAD_W = 2 * DA_HEAD_DIM
ROPE_BASE = 10000.0
ROPE_AXIS_DIM = DA_HEAD_DIM // 2

FFN_HIDDEN = 2816

N_CTX = BATCH * SEQ
N_LAT = DEC_BATCH * DEC_SEQ
N_TOK = N_CTX + N_LAT
MOD_ROWS = 8
LANES = 128
SUBLANES = 8
VMEM_LIMIT = 48 * 1024 * 1024
BIG_VMEM_LIMIT = 56 * 1024 * 1024


def _params(*sem):
    return pltpu.CompilerParams(dimension_semantics=sem, vmem_limit_bytes=VMEM_LIMIT)


def _mod_row(i, tm):
    nctx = N_CTX // tm
    return jnp.where(i < nctx, 0, 1 + (i - nctx) // (DEC_SEQ // tm))


def _silu(x):
    return x * jax.nn.sigmoid(x)


def _softplus(x):
    return jnp.maximum(x, 0.0) + jnp.log1p(jnp.exp(-jnp.abs(x)))


def _split3(x):
    hi = x.astype(BF16)
    r1 = x - hi.astype(F32)
    mid = r1.astype(BF16)
    lo = (r1 - mid.astype(F32)).astype(BF16)
    return hi, mid, lo


def _dot(a, b):
    return jnp.dot(a, b, preferred_element_type=F32)


def _dot_nt(a, b):
    return lax.dot_general(a, b, (((1,), (1,)), ((), ())), preferred_element_type=F32)


def _norm_mod(x, g, sc, sh):
    r = x * lax.rsqrt(jnp.mean(x * x, axis=-1, keepdims=True) + EPS)
    return ((r * g) * (1.0 + sc) + sh).astype(BF16)


def _mod_kernel(c_ref, w_ref, b_ref, o_ref):
    s = _silu(c_ref[...]).astype(BF16)
    o_ref[...] = _dot(s, w_ref[...].astype(BF16)) + b_ref[...]


def _modulation(cpad, mod_w, mod_b):
    tn = 1024
    n = 6 * D_MODEL
    return pl.pallas_call(
        _mod_kernel,
        grid=(DEPTH, n // tn),
        in_specs=[
            pl.BlockSpec((MOD_ROWS, D_MODEL), lambda l, j: (0, 0)),
            pl.BlockSpec((None, D_MODEL, tn), lambda l, j: (l, 0, j)),
            pl.BlockSpec((None, 1, tn), lambda l, j: (l, 0, j)),
        ],
        out_specs=pl.BlockSpec((None, MOD_ROWS, tn), lambda l, j: (l, 0, j)),
        out_shape=jax.ShapeDtypeStruct((DEPTH, MOD_ROWS, n), F32),
        compiler_params=_params("parallel", "parallel"),
        name="modulation",
    )(cpad, mod_w, mod_b.reshape(DEPTH, 1, n))


def _mod_spec(layer, k, tm, ngrid):
    base = layer * MOD_ROWS * 6 + k
    if ngrid == 1:
        return pl.BlockSpec((None, 1, D_MODEL), lambda i: (base + 6 * _mod_row(i, tm), 0, 0))
    return pl.BlockSpec((None, 1, D_MODEL), lambda i, j: (base + 6 * _mod_row(i, tm), 0, 0))


def _nmm_kernel(x_ref, g_ref, sh_ref, sc_ref, w_ref, o_ref, h_ref):
    @pl.when(pl.program_id(1) == 0)
    def _():
        h_ref[...] = _norm_mod(x_ref[...], g_ref[...], sc_ref[...], sh_ref[...])

    o_ref[...] = _dot(h_ref[...], w_ref[...]).astype(o_ref.dtype)


def _nmm_in_specs(layer, kshift, kscale, tm, tn):
    return [
        pl.BlockSpec((tm, D_MODEL), lambda i, j: (i, 0)),
        pl.BlockSpec((1, D_MODEL), lambda i, j: (0, 0)),
        _mod_spec(layer, kshift, tm, 2),
        _mod_spec(layer, kscale, tm, 2),
        pl.BlockSpec((D_MODEL, tn), lambda i, j: (0, j)),
    ]


def _norm_mod_matmul(x, mod3, layer, kshift, kscale, g, w, *, tn, out_dtype, name):
    tm = 1024
    n = w.shape[1]
    return pl.pallas_call(
        _nmm_kernel,
        grid=(N_TOK // tm, n // tn),
        in_specs=_nmm_in_specs(layer, kshift, kscale, tm, tn),
        out_specs=pl.BlockSpec((tm, tn), lambda i, j: (i, j)),
        out_shape=jax.ShapeDtypeStruct((N_TOK, n), out_dtype),
        scratch_shapes=[pltpu.VMEM((tm, D_MODEL), BF16)],
        compiler_params=_params("parallel", "arbitrary"),
        name=name,
    )(x, g.reshape(1, D_MODEL), mod3, mod3, w)


def _nmm_join_kernel(xc_ref, xl_ref, g_ref, sh_ref, sc_ref, w_ref, o_ref, xo_ref, h_ref, *, n_ctx_blocks):
    @pl.when(pl.program_id(1) == 0)
    def _():
        x = jnp.where(pl.program_id(0) < n_ctx_blocks, xc_ref[...], xl_ref[...])
        xo_ref[...] = x
        h_ref[...] = _norm_mod(x, g_ref[...], sc_ref[...], sh_ref[...])

    o_ref[...] = _dot(h_ref[...], w_ref[...]).astype(o_ref.dtype)


def _norm_mod_matmul_join(x_ctx, x_lat, mod3, layer, kshift, kscale, g, w, *, tn, out_dtype, name):
    tm = 1024
    n = w.shape[1]
    nctx = N_CTX // tm
    specs = _nmm_in_specs(layer, kshift, kscale, tm, tn)
    return pl.pallas_call(
        functools.partial(_nmm_join_kernel, n_ctx_blocks=nctx),
        grid=(N_TOK // tm, n // tn),
        in_specs=[pl.BlockSpec((tm, D_MODEL), lambda i, j: (jnp.minimum(i, nctx - 1), 0)),
                  pl.BlockSpec((tm, D_MODEL), lambda i, j: (jnp.maximum(i - nctx, 0), 0))] + specs[1:],
        out_specs=[pl.BlockSpec((tm, tn), lambda i, j: (i, j)),
                   pl.BlockSpec((tm, D_MODEL), lambda i, j: (i, 0))],
        out_shape=[jax.ShapeDtypeStruct((N_TOK, n), out_dtype),
                   jax.ShapeDtypeStruct((N_TOK, D_MODEL), F32)],
        scratch_shapes=[pltpu.VMEM((tm, D_MODEL), BF16)],
        compiler_params=_params("parallel", "arbitrary"),
        name=name,
    )(x_ctx, x_lat, g.reshape(1, D_MODEL), mod3, mod3, w)


def _qkv_kernel(*refs, n_ctx_blocks, n_unused):
    x_ref, g_ref, sh_ref, sc_ref, w_ref, cos_ref, sin_ref = refs[:7]
    q_ref, k_ref, v_ref, kctx_ref, vctx_ref, h_ref = refs[7 + n_unused:]
    j = pl.program_id(1)
    is_ctx = pl.program_id(0) < n_ctx_blocks

    @pl.when(j == 0)
    def _():
        h_ref[...] = _norm_mod(x_ref[...], g_ref[...], sc_ref[...], sh_ref[...])

    u = _dot(h_ref[...], w_ref[...])

    def rope_into(o_refs, scale):
        cos = cos_ref[...] * scale
        sin = sin_ref[...] * scale
        lane = lax.broadcasted_iota(jnp.int32, cos.shape, 1)
        half = ROPE_AXIS_DIM // 2
        first = (lane % ROPE_AXIS_DIM) < half
        for k in range(u.shape[1] // HEAD_W):
            cols = slice(k * HEAD_W, (k + 1) * HEAD_W)
            uk = u[:, cols]
            partner = jnp.where(first, pltpu.roll(uk, HEAD_W - half, 1), pltpu.roll(uk, half, 1))
            r = uk * cos + partner * sin
            for o_ref in o_refs:
                if len(o_ref.shape) == 3:
                    o_ref[:, :, cols] = r.reshape(o_ref.shape[0], SEQ, HEAD_W)
                else:
                    o_ref[:, cols] = r.astype(o_ref.dtype)

    @pl.when(j == 0)
    def _():
        rope_into([q_ref], Q_PRESCALE)

    @pl.when(jnp.logical_and(j == 1, is_ctx))
    def _():
        rope_into([k_ref, kctx_ref], 1.0)

    @pl.when(jnp.logical_and(j == 1, jnp.logical_not(is_ctx)))
    def _():
        rope_into([k_ref], 1.0)

    @pl.when(j == 2)
    def _():
        v_ref[...] = u.astype(v_ref.dtype)

    @pl.when(jnp.logical_and(j == 2, is_ctx))
    def _():
        vctx_ref[...] = u.reshape(vctx_ref.shape)


def _qkv_proj(x, mod3, layer, g, w, cos_t, sin_t, slot, n_slots, caches):
    tm = 1024
    nctx = N_CTX // tm
    nseq = tm // SEQ
    out = pl.BlockSpec((tm, D_MODEL), lambda i, j: (i, 0))
    ctx_out = pl.BlockSpec((nseq, None, SEQ, D_MODEL), lambda i, j: (jnp.minimum(i, nctx - 1), slot, 0, 0))
    ctx_shape = jax.ShapeDtypeStruct((BATCH, n_slots, SEQ, D_MODEL), F32)
    in_specs = _nmm_in_specs(layer, 0, 1, tm, D_MODEL) + [pl.BlockSpec((tm, HEAD_W), lambda i, j: (i, 0))] * 2
    args = [x, g.reshape(1, D_MODEL), mod3, mod3, w, cos_t, sin_t]
    aliases = {}
    if caches is not None:
        in_specs += [pl.BlockSpec(memory_space=pl.ANY)] * 2
        aliases = {len(args): 3, len(args) + 1: 4}
        args += list(caches)
    return pl.pallas_call(
        functools.partial(_qkv_kernel, n_ctx_blocks=nctx, n_unused=len(aliases)),
        grid=(N_TOK // tm, 3),
        in_specs=in_specs,
        out_specs=[out, out, out, ctx_out, ctx_out],
        out_shape=[jax.ShapeDtypeStruct((N_TOK, D_MODEL), BF16),
                   jax.ShapeDtypeStruct((N_TOK, D_MODEL), BF16),
                   jax.ShapeDtypeStruct((N_TOK, D_MODEL), BF16),
                   ctx_shape, ctx_shape],
        scratch_shapes=[pltpu.VMEM((tm, D_MODEL), BF16)],
        input_output_aliases=aliases,
        compiler_params=pltpu.CompilerParams(dimension_semantics=("arbitrary", "arbitrary"),
                                             vmem_limit_bytes=BIG_VMEM_LIMIT),
        name="att_qkv",
    )(*args)


PAD = SUBLANES


def _block_seqlen(i, tm):
    return jnp.where(i < N_CTX // tm, SEQ, DEC_SEQ)


def _project_with_halo(h, hh, w, u_ref, tm):
    u_ref[PAD:PAD + tm, :] = _dot(h, w)
    uh = _dot(hh, w)
    u_ref[PAD - 1:PAD, :] = uh[SUBLANES - 1:SUBLANES, :]
    u_ref[PAD + tm:PAD + tm + 1, :] = uh[SUBLANES:SUBLANES + 1, :]


def _conv_chunks(u_refs, cws, cbs, row0, seqmask, tm, rows, emit):
    ridx = lax.broadcasted_iota(jnp.int32, (rows, 1), 0)
    for r0 in range(0, tm, rows):
        outs = []
        for u_ref, cw, cb in zip(u_refs, cws, cbs):
            prev = u_ref[pl.ds(PAD - 1 + r0, rows), :]
            cur = u_ref[pl.ds(PAD + r0, rows), :]
            nxt = u_ref[pl.ds(PAD + 1 + r0, rows), :]
            if r0 % SEQ == 0:
                starts = ((row0 + r0) & seqmask) == 0
                prev = jnp.where(jnp.logical_and(ridx == 0, starts), 0.0, prev)
            if (r0 + rows) % SEQ == 0:
                ends = ((row0 + r0 + rows) & seqmask) == 0
                nxt = jnp.where(jnp.logical_and(ridx == rows - 1, ends), 0.0, nxt)
            outs.append(prev * cw[0:1, :] + cur * cw[1:2, :] + nxt * cw[2:3, :] + cb)
        emit(r0, outs)


def _halo_x_specs(tm):
    per = tm // SUBLANES
    last = N_TOK // SUBLANES - 1
    return [
        pl.BlockSpec((tm, D_MODEL), lambda i, j: (i, 0)),
        pl.BlockSpec((SUBLANES, D_MODEL), lambda i, j: (jnp.maximum(i * per - 1, 0), 0)),
        pl.BlockSpec((SUBLANES, D_MODEL), lambda i, j: (jnp.minimum((i + 1) * per, last), 0)),
    ]


def _xbc_kernel(x_ref, xp_ref, xn_ref, g_ref, sh_ref, sc_ref, w_ref, cw_ref, cb_ref, wdt_ref, bdt_ref,
                o_ref, dt_ref, h_ref, hh_ref, u_ref, *, tm, rows):
    i = pl.program_id(0)

    @pl.when(pl.program_id(1) == 0)
    def _():
        g, sc, sh = g_ref[...], sc_ref[...], sh_ref[...]
        h_ref[...] = _norm_mod(x_ref[...], g, sc, sh)
        hh_ref[...] = _norm_mod(jnp.concatenate([xp_ref[...], xn_ref[...]], axis=0), g, sc, sh)
        dt_ref[...] = _softplus(_dot(h_ref[...], wdt_ref[...]) + bdt_ref[...])

    _project_with_halo(h_ref[...], hh_ref[...], w_ref[...], u_ref, tm)

    def emit(r0, outs):
        o_ref[pl.ds(r0, rows), :] = _silu(outs[0])

    _conv_chunks([u_ref], [cw_ref[...]], [cb_ref[...]], i * tm, _block_seqlen(i, tm) - 1, tm, rows, emit)


def _ssd_xbc(x, mod3, layer, g, w_xbc, conv_w, conv_b, w_dt, dt_bias):
    tm, tc, rows = 1024, 512, 32
    ndt = w_dt.shape[1]
    return pl.pallas_call(
        functools.partial(_xbc_kernel, tm=tm, rows=rows),
        grid=(N_TOK // tm, SSD_CONV_CH // tc),
        in_specs=_halo_x_specs(tm) + [
            pl.BlockSpec((1, D_MODEL), lambda i, j: (0, 0)),
            _mod_spec(layer, 0, tm, 2),
            _mod_spec(layer, 1, tm, 2),
            pl.BlockSpec((D_MODEL, tc), lambda i, j: (0, j)),
            pl.BlockSpec((3, tc), lambda i, j: (0, j)),
            pl.BlockSpec((1, tc), lambda i, j: (0, j)),
            pl.BlockSpec((D_MODEL, ndt), lambda i, j: (0, 0)),
            pl.BlockSpec((1, ndt), lambda i, j: (0, 0)),
        ],
        out_specs=[
            pl.BlockSpec((tm, tc), lambda i, j: (i, j)),
            pl.BlockSpec((tm, ndt), lambda i, j: (i, 0)),
        ],
        out_shape=[
            jax.ShapeDtypeStruct((N_TOK, SSD_CONV_CH), F32),
            jax.ShapeDtypeStruct((N_TOK, ndt), F32),
        ],
        scratch_shapes=[pltpu.VMEM((tm, D_MODEL), BF16), pltpu.VMEM((2 * SUBLANES, D_MODEL), BF16),
                        pltpu.VMEM((tm + 2 * SUBLANES, tc), F32)],
        compiler_params=_params("parallel", "arbitrary"),
        name="ssd_xbc",
    )(x, x, x, g.reshape(1, D_MODEL), mod3, mod3, w_xbc, conv_w, conv_b.reshape(1, SSD_CONV_CH), w_dt, dt_bias)


def _ffn_kernel(x_ref, xp_ref, xn_ref, g_ref, sh_ref, sc_ref, gate_ref, wa_ref, wv_ref,
                cwa_ref, cba_ref, cwv_ref, cbv_ref, wd_ref, o_ref,
                h_ref, hh_ref, ua_ref, uv_ref, act_ref, *, tm, th, rows):
    i = pl.program_id(0)
    j = pl.program_id(1)

    @pl.when(j == 0)
    def _():
        g, sc, sh = g_ref[...], sc_ref[...], sh_ref[...]
        h_ref[...] = _norm_mod(x_ref[...], g, sc, sh)
        hh_ref[...] = _norm_mod(jnp.concatenate([xp_ref[...], xn_ref[...]], axis=0), g, sc, sh)

    h = h_ref[...]
    hh = hh_ref[...]
    _project_with_halo(h, hh, wa_ref[...], ua_ref, tm)
    _project_with_halo(h, hh, wv_ref[...], uv_ref, tm)

    col0 = pl.multiple_of(j * th, th)

    def emit(r0, outs):
        act_ref[pl.ds(r0, rows), pl.ds(col0, th)] = (_silu(outs[0]) * outs[1]).astype(BF16)

    _conv_chunks([ua_ref, uv_ref], [cwa_ref[...], cwv_ref[...]], [cba_ref[...], cbv_ref[...]],
                 i * tm, _block_seqlen(i, tm) - 1, tm, rows, emit)

    @pl.when(j == pl.num_programs(1) - 1)
    def _():
        o_ref[...] = x_ref[...] + gate_ref[...] * _dot(act_ref[...], wd_ref[...])


def _ffn(x, mod3, layer, g, up_w, conv_w, conv_b, down_w):
    tm, th, rows = 1024, 256, 64
    nb = FFN_HIDDEN // th
    cb = conv_b.reshape(1, 2 * FFN_HIDDEN)
    return pl.pallas_call(
        functools.partial(_ffn_kernel, tm=tm, th=th, rows=rows),
        grid=(N_TOK // tm, nb),
        in_specs=_halo_x_specs(tm) + [
            pl.BlockSpec((1, D_MODEL), lambda i, j: (0, 0)),
            _mod_spec(layer, 3, tm, 2),
            _mod_spec(layer, 4, tm, 2),
            _mod_spec(layer, 5, tm, 2),
            pl.BlockSpec((D_MODEL, th), lambda i, j: (0, j)),
            pl.BlockSpec((D_MODEL, th), lambda i, j: (0, j + nb)),
            pl.BlockSpec((3, th), lambda i, j: (0, j)),
            pl.BlockSpec((1, th), lambda i, j: (0, j)),
            pl.BlockSpec((3, th), lambda i, j: (0, j + nb)),
            pl.BlockSpec((1, th), lambda i, j: (0, j + nb)),
            pl.BlockSpec((FFN_HIDDEN, D_MODEL), lambda i, j: (0, 0)),
        ],
        out_specs=pl.BlockSpec((tm, D_MODEL), lambda i, j: (i, 0)),
        out_shape=jax.ShapeDtypeStruct((N_TOK, D_MODEL), F32),
        scratch_shapes=[pltpu.VMEM((tm, D_MODEL), BF16), pltpu.VMEM((2 * SUBLANES, D_MODEL), BF16),
                        pltpu.VMEM((tm + 2 * SUBLANES, th), F32), pltpu.VMEM((tm + 2 * SUBLANES, th), F32),
                        pltpu.VMEM((tm, FFN_HIDDEN), BF16)],
        compiler_params=_params("parallel", "arbitrary"),
        name="ffn",
    )(x, x, x, g.reshape(1, D_MODEL), mod3, mod3, mod3, up_w, up_w, conv_w, cb, conv_w, cb, down_w)


def _mm_res_kernel(ac_ref, al_ref, w_ref, x_ref, gate_ref, o_ref, *, n_ctx_blocks):
    a = jnp.where(pl.program_id(0) < n_ctx_blocks, ac_ref[...], al_ref[...])
    o_ref[...] = x_ref[...] + gate_ref[...] * _dot(a, w_ref[...])


def _two_source_specs(block, tm):
    nctx = N_CTX // tm
    return [pl.BlockSpec(block, lambda i: (jnp.minimum(i, nctx - 1), 0)),
            pl.BlockSpec(block, lambda i: (jnp.maximum(i - nctx, 0), 0))]


def _matmul_residual(a_ctx, a_lat, w, x, mod3, layer, kgate, *, name):
    tm = 512
    k = w.shape[0]
    return pl.pallas_call(
        functools.partial(_mm_res_kernel, n_ctx_blocks=N_CTX // tm),
        grid=(N_TOK // tm,),
        in_specs=_two_source_specs((tm, k), tm) + [
            pl.BlockSpec((k, D_MODEL), lambda i: (0, 0)),
            pl.BlockSpec((tm, D_MODEL), lambda i: (i, 0)),
            _mod_spec(layer, kgate, tm, 1),
        ],
        out_specs=pl.BlockSpec((tm, D_MODEL), lambda i: (i, 0)),
        out_shape=jax.ShapeDtypeStruct((N_TOK, D_MODEL), F32),
        compiler_params=_params("parallel"),
        name=name,
    )(a_ctx, a_lat, w, x, mod3)


def _scan_kernel(*refs, nc, zero_init, n_unused):
    x_ref, b_ref, c_ref, dt_ref, alog_ref, e_ref, dx_ref = refs[:7]
    h0_ref = None if zero_init else refs[7]
    y_ref, hl_ref, st_ref, yt_ref = refs[7 + (not zero_init) + n_unused:]
    d = pl.program_id(1)
    c = pl.program_id(2)
    q = SSD_CHUNK

    @pl.when(c == 0)
    def _():
        if zero_init:
            st_ref[...] = jnp.zeros_like(st_ref)
        else:
            st_ref[...] = h0_ref[...].T

    half = SSD_HEAD_DIM
    sgn = 1 - 2 * d
    ii = lax.broadcasted_iota(jnp.int32, (q, q), 0)
    jj = lax.broadcasted_iota(jnp.int32, (q, q), 1)
    tri_b = jnp.where(((ii - jj) * sgn) >= 0, 1.0, 0.0).astype(BF16)
    jl = jj & (half - 1)
    tri_mix = [((ii - (half * jh + jl)) * sgn) >= 0 for jh in range(2)]
    lane = lax.broadcasted_iota(jnp.int32, (half, LANES), 1)
    low = lane < half

    def split_cat(v):
        return jnp.concatenate(_split3(v), axis=1)

    dtb = dt_ref[...]
    da = dtb * (-jnp.exp(alog_ref[...]) * LOG2E)
    cs3 = _dot(tri_b, split_cat(da))
    cs = cs3[:, 0:LANES] + cs3[:, LANES:2 * LANES] + cs3[:, 2 * LANES:]
    e3 = e_ref[...]
    dt_x = _dot(split_cat(dtb)[:, 0:2 * LANES], e3[0:2 * LANES, :])
    cs_x = _dot(split_cat(cs), e3)
    tot_x = jnp.where(d == 0, cs_x[q - 1:q, :], cs_x[0:1, :])
    cs_rows = [jnp.concatenate([cs[half * jh:half * (jh + 1), :]] * 2, axis=0).T for jh in range(2)]

    xs = x_ref[...]
    xd = xs * dt_x
    xw = (xd * jnp.exp2(tot_x - cs_x)).astype(BF16)
    dec_out = jnp.exp2(cs_x)
    dec_chunk = jnp.exp2(tot_x)

    for g in range(SSD_GROUPS):
        gs = slice(g * GROUP_W, (g + 1) * GROUP_W)
        bg = b_ref[:, g * SSD_STATE:(g + 1) * SSD_STATE]
        cg = c_ref[:, g * SSD_STATE:(g + 1) * SSD_STATE].astype(BF16)
        bgb = bg.astype(BF16)
        cb_mix = [_dot_nt(cg, jnp.concatenate([bgb[half * jh:half * (jh + 1), :]] * 2, axis=0)) for jh in range(2)]
        st_g = st_ref[:, gs]
        y_off = _dot(cg, st_g.astype(BF16))
        new_g = _dot(bg.T.astype(BF16), xw[:, gs])
        st_ref[:, gs] = st_g * dec_chunk[:, gs] + new_g
        for p in range(HEADS_PER_GROUP // 2):
            h = g * HEADS_PER_GROUP + 2 * p
            cols = slice(h * SSD_HEAD_DIM, (h + 2) * SSD_HEAD_DIM)
            cs_col = cs_x[:, cols]
            xd2 = xd[:, cols]
            ms, rs = [], []
            for jh in range(2):
                row = jnp.where(low[0:1, :], cs_rows[jh][h:h + 1, :], cs_rows[jh][h + 1:h + 2, :])
                decay = jnp.where(tri_mix[jh], jnp.exp2(cs_col - row), 0.0)
                ms.append((cb_mix[jh] * decay).astype(BF16))
                xj = xd2[half * jh:half * (jh + 1), :]
                rs += [jnp.where(low, xj, 0.0), jnp.where(low, 0.0, xj)]
            lhs = jnp.concatenate(ms, axis=1)
            rhs = jnp.concatenate(rs, axis=0).astype(BF16)
            loc = slice(2 * p * SSD_HEAD_DIM, (2 * p + 2) * SSD_HEAD_DIM)
            yt_ref[:, cols] = _dot(lhs, rhs) + y_off[:, loc] * dec_out[:, cols]

    r0 = pl.multiple_of((c + d * (nc - 1 - 2 * c)) * q, q)

    @pl.when(d == 0)
    def _():
        y_ref[pl.ds(r0, q), :] = yt_ref[...] + dx_ref[...] * xs

    @pl.when(d == 1)
    def _():
        y_ref[pl.ds(r0, q), :] += yt_ref[...]

    @pl.when(c == nc - 1)
    def _():
        hl_ref[...] = st_ref[...].T


def _ssd_scan(xbc, dt, a_log_pad, expand, d_x, h0, slot, *, nseq, nc, chunk0, name, n_slots=1, states=None):
    q = SSD_CHUNK
    zero_init = h0 is None
    out_slot = slot if n_slots > 1 else 0

    def chunk(b, d, c):
        return chunk0 + b * nc + c + d * (nc - 1 - 2 * c)

    in_specs = [
        pl.BlockSpec((q, SSD_INNER), lambda b, d, c: (chunk(b, d, c), 0)),
        pl.BlockSpec((q, SSD_BC), lambda b, d, c: (chunk(b, d, c), SSD_INNER // SSD_BC)),
        pl.BlockSpec((q, SSD_BC), lambda b, d, c: (chunk(b, d, c), SSD_INNER // SSD_BC + 1)),
        pl.BlockSpec((q, LANES), lambda b, d, c: (chunk(b, d, c), d)),
        pl.BlockSpec((None, 1, LANES), lambda b, d, c: (d, 0, 0)),
        pl.BlockSpec((3 * LANES, SSD_INNER), lambda b, d, c: (0, 0)),
        pl.BlockSpec((1, SSD_INNER), lambda b, d, c: (0, 0)),
    ]
    args = [xbc, xbc, xbc, dt, a_log_pad, expand, d_x]
    if not zero_init:
        in_specs.append(pl.BlockSpec((None, None, None, SSD_INNER, SSD_STATE),
                                     lambda b, d, c: (b, slot, d, 0, 0)))
        args.append(h0)
    aliases = {}
    if states is not None:
        in_specs.append(pl.BlockSpec(memory_space=pl.ANY))
        args.append(states)
        aliases = {len(args) - 1: 1}
    return pl.pallas_call(
        functools.partial(_scan_kernel, nc=nc, zero_init=zero_init, n_unused=len(aliases)),
        grid=(nseq, 2, nc),
        in_specs=in_specs,
        out_specs=[
            pl.BlockSpec((nc * q, SSD_INNER), lambda b, d, c: (b, 0)),
            pl.BlockSpec((None, None, None, SSD_INNER, SSD_STATE), lambda b, d, c: (b, out_slot, d, 0, 0)),
        ],
        out_shape=[
            jax.ShapeDtypeStruct((nseq * nc * q, SSD_INNER), F32),
            jax.ShapeDtypeStruct((nseq, n_slots, 2, SSD_INNER, SSD_STATE), F32),
        ],
        scratch_shapes=[pltpu.VMEM((SSD_STATE, SSD_INNER), F32), pltpu.VMEM((q, SSD_INNER), F32)],
        input_output_aliases=aliases,
        compiler_params=pltpu.CompilerParams(dimension_semantics=("parallel", "arbitrary", "arbitrary"),
                                             vmem_limit_bytes=BIG_VMEM_LIMIT),
        name=name,
    )(*args)


def _ssd_out_kernel(yc_ref, yl_ref, z_ref, ng_ref, w_ref, x_ref, gate_ref, o_ref, *, n_ctx_blocks):
    y = jnp.where(pl.program_id(0) < n_ctx_blocks, yc_ref[...], yl_ref[...])
    y = y * _silu(z_ref[...].astype(F32))
    gw = SSD_INNER // SSD_GROUPS
    parts = []
    for g in range(SSD_GROUPS):
        yg = y[:, g * gw:(g + 1) * gw]
        parts.append(yg * lax.rsqrt(jnp.mean(yg * yg, axis=-1, keepdims=True) + EPS))
    yn = (jnp.concatenate(parts, axis=1) * ng_ref[...]).astype(BF16)
    o_ref[...] = x_ref[...] + gate_ref[...] * _dot(yn, w_ref[...])


def _ssd_out(y_ctx, y_lat, z, norm_g, out_w, x, mod3, layer):
    tm = 512
    return pl.pallas_call(
        functools.partial(_ssd_out_kernel, n_ctx_blocks=N_CTX // tm),
        grid=(N_TOK // tm,),
        in_specs=_two_source_specs((tm, SSD_INNER), tm) + [
            pl.BlockSpec((tm, SSD_INNER), lambda i: (i, 0)),
            pl.BlockSpec((1, SSD_INNER), lambda i: (0, 0)),
            pl.BlockSpec((SSD_INNER, D_MODEL), lambda i: (0, 0)),
            pl.BlockSpec((tm, D_MODEL), lambda i: (i, 0)),
            _mod_spec(layer, 2, tm, 1),
        ],
        out_specs=pl.BlockSpec((tm, D_MODEL), lambda i: (i, 0)),
        out_shape=jax.ShapeDtypeStruct((N_TOK, D_MODEL), F32),
        compiler_params=_params("parallel"),
        name="ssd_out",
    )(y_ctx, y_lat, z, norm_g.reshape(1, SSD_INNER), out_w, x, mod3)


def _diff_lambda(lam_ref, lam_init):
    lv = lam_ref[...]
    return (jnp.exp(jnp.sum(lv[0:1] * lv[1:2], axis=-1, keepdims=True))
            - jnp.exp(jnp.sum(lv[2:3] * lv[3:4], axis=-1, keepdims=True)) + lam_init)


def _attend_head(q, kb, vb, lam, g, lam_init):
    lane = lax.broadcasted_iota(jnp.int32, q.shape, 1)
    zero = jnp.zeros_like(q)
    probs = []
    for qm in (jnp.where(lane < DA_HEAD_DIM, q, zero), jnp.where(lane >= DA_HEAD_DIM, q, zero)):
        s = _dot_nt(qm, kb)
        p = jnp.exp2(s - jnp.max(s, axis=-1, keepdims=True))
        probs.append((p, jnp.sum(p, axis=-1, keepdims=True)))
    (p1, l1), (p2, l2) = probs
    w = (p1 - p2 * (lam * l1 / l2)).astype(BF16)
    o = _dot(w, vb) * (1.0 / l1)
    o = o * lax.rsqrt(jnp.mean(o * o, axis=-1, keepdims=True) + EPS)
    return o * g * (1.0 - lam_init)


def _attn_ctx_kernel(q_ref, k_ref, v_ref, lam_ref, g_ref, o_ref, *, lam_init):
    lam = _diff_lambda(lam_ref, lam_init)
    g = g_ref[...]
    for h in range(DA_HEADS):
        cols = slice(h * HEAD_W, (h + 1) * HEAD_W)
        o = _attend_head(q_ref[:, cols], k_ref[:, cols], v_ref[:, cols], lam, g, lam_init)
        o_ref[:, cols] = o.astype(o_ref.dtype)


def _attn_lat_kernel(q_ref, k_ref, v_ref, kc_ref, vc_ref, lam_ref, g_ref, o_ref, kb_ref, vb_ref, *, lam_init):
    @pl.when(pl.program_id(2) == 0)
    def _():
        kb_ref[0:DEC_SEQ, :] = k_ref[...]
        kb_ref[DEC_SEQ:, :] = kc_ref[...].astype(BF16)
        vb_ref[0:DEC_SEQ, :] = v_ref[...]
        vb_ref[DEC_SEQ:, :] = vc_ref[...].astype(BF16)

    o = _attend_head(q_ref[...], kb_ref[...], vb_ref[...], _diff_lambda(lam_ref, lam_init), g_ref[...], lam_init)
    o_ref[...] = o.astype(o_ref.dtype)


def _attention(q, k, v, cache_k4, cache_v4, slot, lam_vecs, subln_g, lam_init):
    common = [pl.BlockSpec((4, DA_HEAD_DIM), lambda *_: (0, 0)),
              pl.BlockSpec((1, HEAD_W), lambda *_: (0, 0))]
    g2 = subln_g.reshape(1, HEAD_W)

    seq_block = pl.BlockSpec((SEQ, D_MODEL), lambda b: (b, 0))
    ctx = pl.pallas_call(
        functools.partial(_attn_ctx_kernel, lam_init=lam_init),
        grid=(BATCH,),
        in_specs=[seq_block, seq_block, seq_block] + common,
        out_specs=seq_block,
        out_shape=jax.ShapeDtypeStruct((N_CTX, D_MODEL), BF16),
        compiler_params=_params("parallel"),
        name="attn_ctx",
    )(q, k, v, lam_vecs, g2)

    tq = 256
    nq = DEC_SEQ // tq
    q0 = N_CTX // tq
    s0 = N_CTX // DEC_SEQ
    keys = DEC_SEQ + PAST_LEN
    lat = pl.pallas_call(
        functools.partial(_attn_lat_kernel, lam_init=lam_init),
        grid=(DEC_BATCH, DA_HEADS, nq),
        in_specs=[
            pl.BlockSpec((tq, HEAD_W), lambda b, h, i: (q0 + b * nq + i, h)),
            pl.BlockSpec((DEC_SEQ, HEAD_W), lambda b, h, i: (s0 + b, h)),
            pl.BlockSpec((DEC_SEQ, HEAD_W), lambda b, h, i: (s0 + b, h)),
            pl.BlockSpec((None, None, PAST_LEN, HEAD_W), lambda b, h, i: (b, slot, 0, h)),
            pl.BlockSpec((None, None, PAST_LEN, HEAD_W), lambda b, h, i: (b, slot, 0, h)),
        ] + common,
        out_specs=pl.BlockSpec((tq, HEAD_W), lambda b, h, i: (b * nq + i, h)),
        out_shape=jax.ShapeDtypeStruct((N_LAT, D_MODEL), BF16),
        scratch_shapes=[pltpu.VMEM((keys, HEAD_W), BF16), pltpu.VMEM((keys, HEAD_W), BF16)],
        compiler_params=_params("parallel", "parallel", "arbitrary"),
        name="attn_lat",
    )(q, k, v, cache_k4, cache_v4, lam_vecs, g2)
    return ctx, lat


def _final_kernel(x_ref, g_ref, o_ref):
    x = x_ref[...]
    o_ref[...] = x * lax.rsqrt(jnp.mean(x * x, axis=-1, keepdims=True) + EPS) * g_ref[...]


def _final_norm(x, g, row0, nrows):
    tm = 1024
    b0 = row0 // tm
    return pl.pallas_call(
        _final_kernel,
        grid=(nrows // tm,),
        in_specs=[pl.BlockSpec((tm, D_MODEL), lambda i: (b0 + i, 0)),
                  pl.BlockSpec((1, D_MODEL), lambda i: (0, 0))],
        out_specs=pl.BlockSpec((tm, D_MODEL), lambda i: (i, 0)),
        out_shape=jax.ShapeDtypeStruct((nrows, D_MODEL), F32),
        compiler_params=_params("parallel"),
        name="final_norm",
    )(x, g.reshape(1, D_MODEL))


def _rope_token_tables():
    rows = DEC_SEQ // GRID_W
    row = jnp.repeat(jnp.arange(rows, dtype=F32), GRID_W)
    col = jnp.tile(jnp.arange(GRID_W, dtype=F32), rows)
    inv = 1.0 / (ROPE_BASE ** (jnp.arange(0, ROPE_AXIS_DIM, 2, dtype=F32) / ROPE_AXIS_DIM))
    ang_r = row[:, None] * inv
    ang_c = col[:, None] * inv
    cr, sr, cc, sc = jnp.cos(ang_r), jnp.sin(ang_r), jnp.cos(ang_c), jnp.sin(ang_c)
    cos64 = jnp.concatenate([cr, cr, cc, cc], axis=1)
    sin64 = jnp.concatenate([-sr, sr, -sc, sc], axis=1)
    cos_lat = jnp.tile(cos64, (DEC_BATCH, 2))
    sin_lat = jnp.tile(sin64, (DEC_BATCH, 2))
    cos = jnp.concatenate([jnp.ones((N_CTX, HEAD_W), F32), cos_lat], axis=0)
    sin = jnp.concatenate([jnp.zeros((N_CTX, HEAD_W), F32), sin_lat], axis=0)
    return cos, sin


def _lambda_init(layer):
    return 0.8 - 0.6 * math.exp(-0.3 * layer)


def kernel(x_prompt, x_sample, state_ssd, cache_k, cache_v, c, c_ctx, mod_w, mod_b, norm_mix_g, norm_ffn_g, ssd_in_w, ssd_conv_w, ssd_conv_b, ssd_dt_bias, ssd_a_log, ssd_d, ssd_norm_g, ssd_out_w, att_qkv_w, att_lambda, att_subln_g, att_out_w, ffn_up_w, ffn_conv_w, ffn_conv_b, ffn_down_w, final_norm_g):
    n_ssd = ssd_in_w.shape[0]
    n_att = att_qkv_w.shape[0]
    x = None

    cpad = jnp.concatenate([c_ctx[None], c, jnp.zeros((MOD_ROWS - 1 - DEC_BATCH, D_MODEL), F32)], axis=0)
    mod = _modulation(cpad, mod_w, mod_b)
    mod3 = mod.reshape(DEPTH * MOD_ROWS * 6, 1, D_MODEL)

    cos_t, sin_t = _rope_token_tables()
    expand = (jnp.arange(LANES)[:, None] == (jnp.arange(SSD_INNER)[None, :] // SSD_HEAD_DIM)).astype(BF16)
    expand = jnp.concatenate([expand] * 3, axis=0)
    h0_lat = state_ssd.reshape(DEC_BATCH, n_ssd, 2, SSD_INNER, SSD_STATE)
    cache_k4 = cache_k.reshape(DEC_BATCH, n_att, PAST_LEN, D_MODEL)
    cache_v4 = cache_v.reshape(DEC_BATCH, n_att, PAST_LEN, D_MODEL)

    new_ssd, new_kv = None, None
    for i in range(DEPTH):
        slot = i // 2
        if i % 2 == 0:
            in_w = ssd_in_w[slot]
            w_z = in_w[:, :SSD_INNER].astype(BF16)
            w_xbc = in_w[:, SSD_INNER:SSD_MAIN].astype(BF16)
            pad = jnp.zeros((D_MODEL, LANES - SSD_HEADS), F32)
            w_dt = jnp.concatenate([in_w[:, SSD_MAIN:SSD_MAIN + SSD_HEADS], pad,
                                    in_w[:, SSD_MAIN + SSD_HEADS:], pad], axis=1).astype(BF16)
            zpad = jnp.zeros((LANES - SSD_HEADS,), F32)
            dt_bias = jnp.concatenate([ssd_dt_bias[slot, 0], zpad, ssd_dt_bias[slot, 1], zpad]).reshape(1, 2 * LANES)
            a_log_pad = jnp.concatenate([ssd_a_log[slot], jnp.zeros((2, LANES - SSD_HEADS), F32)],
                                        axis=1).reshape(2, 1, LANES)
            d_x = jnp.repeat(ssd_d[slot], SSD_HEAD_DIM).reshape(1, SSD_INNER)
            if x is None:
                z, x = _norm_mod_matmul_join(x_prompt.reshape(N_CTX, D_MODEL), x_sample.reshape(N_LAT, D_MODEL),
                                             mod3, i, 0, 1, norm_mix_g[i], w_z, tn=1024, out_dtype=BF16,
                                             name="ssd_z_join")
            else:
                z = _norm_mod_matmul(x, mod3, i, 0, 1, norm_mix_g[i], w_z, tn=1024, out_dtype=BF16, name="ssd_z")
            xbc, dt = _ssd_xbc(x, mod3, i, norm_mix_g[i], w_xbc, ssd_conv_w[slot], ssd_conv_b[slot], w_dt, dt_bias)
            y_ctx, new_ssd = _ssd_scan(xbc, dt, a_log_pad, expand, d_x, None, slot,
                                       nseq=BATCH, nc=SEQ // SSD_CHUNK, chunk0=0, name="ssd_scan_ctx",
                                       n_slots=n_ssd, states=new_ssd)
            y_lat, _ = _ssd_scan(xbc, dt, a_log_pad, expand, d_x, h0_lat, slot,
                                 nseq=DEC_BATCH, nc=DEC_SEQ // SSD_CHUNK, chunk0=N_CTX // SSD_CHUNK,
                                 name="ssd_scan_lat")
            x = _ssd_out(y_ctx, y_lat, z, ssd_norm_g[slot], ssd_out_w[slot].astype(BF16), x, mod3, i)
        else:
            lam_init = _lambda_init(i)
            q, k, v, *new_kv = _qkv_proj(x, mod3, i, norm_mix_g[i], att_qkv_w[slot].astype(BF16),
                                         cos_t, sin_t, slot, n_att, new_kv)
            o_ctx, o_lat = _attention(q, k, v, cache_k4, cache_v4, slot, att_lambda[slot], att_subln_g[slot],
                                      lam_init)
            x = _matmul_residual(o_ctx, o_lat, att_out_w[slot].astype(BF16), x, mod3, i, 2, name="att_out")
        x = _ffn(x, mod3, i, norm_ffn_g[i], ffn_up_w[i].astype(BF16), ffn_conv_w[i], ffn_conv_b[i],
                 ffn_down_w[i].astype(BF16))

    y_prompt = _final_norm(x, final_norm_g, 0, N_CTX).reshape(BATCH, SEQ, D_MODEL)
    y_sample = _final_norm(x, final_norm_g, N_CTX, N_LAT).reshape(DEC_BATCH, DEC_SEQ, D_MODEL)
    return (y_prompt, y_sample,
            new_ssd.reshape(BATCH, n_ssd, 2, SSD_HEADS, SSD_HEAD_DIM, SSD_STATE),
            new_kv[0].reshape(BATCH, n_att, SEQ, DA_HEADS, 2, DA_HEAD_DIM),
            new_kv[1].reshape(BATCH, n_att, SEQ, DA_HEADS, HEAD_W))
```

```python
import functools
import math

import jax
import jax.numpy as jnp
from jax import lax
from jax.experimental import pallas as pl
from jax.experimental.pallas import tpu as pltpu

F32 = jnp.float32
BF16 = jnp.bfloat16

D_MODEL = 1024
BATCH = 16
SEQ = 256
DEPTH = 4
DEC_BATCH = 4
DEC_SEQ = 2048
PAST_LEN = 256
GRID_W = 64
EPS = 1e-6

SSD_INNER = 2048
SSD_HEAD_DIM = 64
SSD_HEADS = 32
SSD_GROUPS = 4
SSD_STATE = 128
SSD_CHUNK = 128
SSD_BC = SSD_GROUPS * SSD_STATE
SSD_CONV_CH = SSD_INNER + 2 * SSD_BC
SSD_MAIN = SSD_INNER + SSD_CONV_CH
HEADS_PER_GROUP = SSD_HEADS // SSD_GROUPS
GROUP_W = HEADS_PER_GROUP * SSD_HEAD_DIM

DA_HEAD_DIM = 64
DA_HEADS = 8
DA_SCALE = DA_HEAD_DIM ** -0.5
LOG2E = math.log2(math.e)
Q_PRESCALE = DA_SCALE * LOG2E
HEAD_W = 2 * DA_HEAD_DIM
ROPE_BASE = 10000.0
ROPE_AXIS_DIM = DA_HEAD_DIM // 2

FFN_HIDDEN = 2816

N_CTX = BATCH * SEQ
N_LAT = DEC_BATCH * DEC_SEQ
N_TOK = N_CTX + N_LAT
MOD_ROWS = 8
LANES = 128
SUBLANES = 8
VMEM_LIMIT = 48 * 1024 * 1024
BIG_VMEM_LIMIT = 56 * 1024 * 1024


def _params(*sem):
    return pltpu.CompilerParams(dimension_semantics=sem, vmem_limit_bytes=VMEM_LIMIT)


def _mod_row(i, tm):
    nctx = N_CTX // tm
    return jnp.where(i < nctx, 0, 1 + (i - nctx) // (DEC_SEQ // tm))


def _silu(x):
    return x * jax.nn.sigmoid(x)


def _softplus(x):
    return jnp.maximum(x, 0.0) + jnp.log1p(jnp.exp(-jnp.abs(x)))


def _split3(x):
    hi = x.astype(BF16)
    r1 = x - hi.astype(F32)
    mid = r1.astype(BF16)
    lo = (r1 - mid.astype(F32)).astype(BF16)
    return hi, mid, lo


def _dot(a, b):
    return jnp.dot(a, b, preferred_element_type=F32)


def _dot_nt(a, b):
    return lax.dot_general(a, b, (((1,), (1,)), ((), ())), preferred_element_type=F32)


def _norm_mod(x, g, sc, sh):
    r = x * lax.rsqrt(jnp.mean(x * x, axis=-1, keepdims=True) + EPS)
    return ((r * g) * (1.0 + sc) + sh).astype(BF16)


def _mod_kernel(c_ref, w_ref, b_ref, o_ref):
    s = _silu(c_ref[...]).astype(BF16)
    o_ref[...] = _dot(s, w_ref[...].astype(BF16)) + b_ref[...]


def _modulation(cpad, mod_w, mod_b):
    tn = 1024
    n = 6 * D_MODEL
    return pl.pallas_call(
        _mod_kernel,
        grid=(DEPTH, n // tn),
        in_specs=[
            pl.BlockSpec((MOD_ROWS, D_MODEL), lambda l, j: (0, 0)),
            pl.BlockSpec((None, D_MODEL, tn), lambda l, j: (l, 0, j)),
            pl.BlockSpec((None, 1, tn), lambda l, j: (l, 0, j)),
        ],
        out_specs=pl.BlockSpec((None, MOD_ROWS, tn), lambda l, j: (l, 0, j)),
        out_shape=jax.ShapeDtypeStruct((DEPTH, MOD_ROWS, n), F32),
        compiler_params=_params("parallel", "parallel"),
        name="modulation",
    )(cpad, mod_w, mod_b.reshape(DEPTH, 1, n))


def _mod_spec(layer, k, tm, ngrid):
    base = layer * MOD_ROWS * 6 + k
    if ngrid == 1:
        return pl.BlockSpec((None, 1, D_MODEL), lambda i: (base + 6 * _mod_row(i, tm), 0, 0))
    return pl.BlockSpec((None, 1, D_MODEL), lambda i, j: (base + 6 * _mod_row(i, tm), 0, 0))


def _nmm_kernel(x_ref, g_ref, sh_ref, sc_ref, w_ref, o_ref, h_ref):
    @pl.when(pl.program_id(1) == 0)
    def _():
        h_ref[...] = _norm_mod(x_ref[...], g_ref[...], sc_ref[...], sh_ref[...])

    o_ref[...] = _dot(h_ref[...], w_ref[...]).astype(o_ref.dtype)


def _nmm_in_specs(layer, kshift, kscale, tm, tn):
    return [
        pl.BlockSpec((tm, D_MODEL), lambda i, j: (i, 0)),
        pl.BlockSpec((1, D_MODEL), lambda i, j: (0, 0)),
        _mod_spec(layer, kshift, tm, 2),
        _mod_spec(layer, kscale, tm, 2),
        pl.BlockSpec((D_MODEL, tn), lambda i, j: (0, j)),
    ]


def _norm_mod_matmul(x, mod3, layer, kshift, kscale, g, w, *, tn, out_dtype, name):
    tm = 1024
    n = w.shape[1]
    return pl.pallas_call(
        _nmm_kernel,
        grid=(N_TOK // tm, n // tn),
        in_specs=_nmm_in_specs(layer, kshift, kscale, tm, tn),
        out_specs=pl.BlockSpec((tm, tn), lambda i, j: (i, j)),
        out_shape=jax.ShapeDtypeStruct((N_TOK, n), out_dtype),
        scratch_shapes=[pltpu.VMEM((tm, D_MODEL), BF16)],
        compiler_params=_params("parallel", "arbitrary"),
        name=name,
    )(x, g.reshape(1, D_MODEL), mod3, mod3, w)


def _nmm_join_kernel(xc_ref, xl_ref, g_ref, sh_ref, sc_ref, w_ref, o_ref, xo_ref, h_ref, *, n_ctx_blocks):
    @pl.when(pl.program_id(1) == 0)
    def _():
        x = jnp.where(pl.program_id(0) < n_ctx_blocks, xc_ref[...], xl_ref[...])
        xo_ref[...] = x
        h_ref[...] = _norm_mod(x, g_ref[...], sc_ref[...], sh_ref[...])

    o_ref[...] = _dot(h_ref[...], w_ref[...]).astype(o_ref.dtype)


def _norm_mod_matmul_join(x_ctx, x_lat, mod3, layer, kshift, kscale, g, w, *, tn, out_dtype, name):
    tm = 1024
    n = w.shape[1]
    nctx = N_CTX // tm
    specs = _nmm_in_specs(layer, kshift, kscale, tm, tn)
    return pl.pallas_call(
        functools.partial(_nmm_join_kernel, n_ctx_blocks=nctx),
        grid=(N_TOK // tm, n // tn),
        in_specs=[pl.BlockSpec((tm, D_MODEL), lambda i, j: (jnp.minimum(i, nctx - 1), 0)),
                  pl.BlockSpec((tm, D_MODEL), lambda i, j: (jnp.maximum(i - nctx, 0), 0))] + specs[1:],
        out_specs=[pl.BlockSpec((tm, tn), lambda i, j: (i, j)),
                   pl.BlockSpec((tm, D_MODEL), lambda i, j: (i, 0))],
        out_shape=[jax.ShapeDtypeStruct((N_TOK, n), out_dtype),
                   jax.ShapeDtypeStruct((N_TOK, D_MODEL), F32)],
        scratch_shapes=[pltpu.VMEM((tm, D_MODEL), BF16)],
        compiler_params=_params("parallel", "arbitrary"),
        name=name,
    )(x_ctx, x_lat, g.reshape(1, D_MODEL), mod3, mod3, w)


def _qkv_kernel(*refs, n_ctx_blocks, n_unused):
    x_ref, g_ref, sh_ref, sc_ref, w_ref, cos_ref, sin_ref = refs[:7]
    q_ref, k_ref, v_ref, kctx_ref, vctx_ref, h_ref = refs[7 + n_unused:]
    j = pl.program_id(1)
    is_ctx = pl.program_id(0) < n_ctx_blocks

    @pl.when(j == 0)
    def _():
        h_ref[...] = _norm_mod(x_ref[...], g_ref[...], sc_ref[...], sh_ref[...])

    u = _dot(h_ref[...], w_ref[...])

    def rope_into(o_refs, scale):
        cos = cos_ref[...] * scale
        sin = sin_ref[...] * scale
        lane = lax.broadcasted_iota(jnp.int32, cos.shape, 1)
        half = ROPE_AXIS_DIM // 2
        first = (lane % ROPE_AXIS_DIM) < half
        for k in range(u.shape[1] // HEAD_W):
            cols = slice(k * HEAD_W, (k + 1) * HEAD_W)
            uk = u[:, cols]
            partner = jnp.where(first, pltpu.roll(uk, HEAD_W - half, 1), pltpu.roll(uk, half, 1))
            r = uk * cos + partner * sin
            for o_ref in o_refs:
                if len(o_ref.shape) == 3:
                    o_ref[:, :, cols] = r.reshape(o_ref.shape[0], SEQ, HEAD_W)
                else:
                    o_ref[:, cols] = r.astype(o_ref.dtype)

    @pl.when(j == 0)
    def _():
        rope_into([q_ref], Q_PRESCALE)

    @pl.when(jnp.logical_and(j == 1, is_ctx))
    def _():
        rope_into([k_ref, kctx_ref], 1.0)

    @pl.when(jnp.logical_and(j == 1, jnp.logical_not(is_ctx)))
    def _():
        rope_into([k_ref], 1.0)

    @pl.when(j == 2)
    def _():
        v_ref[...] = u.astype(v_ref.dtype)

    @pl.when(jnp.logical_and(j == 2, is_ctx))
    def _():
        vctx_ref[...] = u.reshape(vctx_ref.shape)


def _qkv_proj(x, mod3, layer, g, w, cos_t, sin_t, slot, n_slots, caches):
    tm = 1024
    nctx = N_CTX // tm
    nseq = tm // SEQ
    out = pl.BlockSpec((tm, D_MODEL), lambda i, j: (i, 0))
    ctx_out = pl.BlockSpec((nseq, None, SEQ, D_MODEL), lambda i, j: (jnp.minimum(i, nctx - 1), slot, 0, 0))
    ctx_shape = jax.ShapeDtypeStruct((BATCH, n_slots, SEQ, D_MODEL), F32)
    in_specs = _nmm_in_specs(layer, 0, 1, tm, D_MODEL) + [pl.BlockSpec((tm, HEAD_W), lambda i, j: (i, 0))] * 2
    args = [x, g.reshape(1, D_MODEL), mod3, mod3, w, cos_t, sin_t]
    aliases = {}
    if caches is not None:
        in_specs += [pl.BlockSpec(memory_space=pl.ANY)] * 2
        aliases = {len(args): 3, len(args) + 1: 4}
        args += list(caches)
    return pl.pallas_call(
        functools.partial(_qkv_kernel, n_ctx_blocks=nctx, n_unused=len(aliases)),
        grid=(N_TOK // tm, 3),
        in_specs=in_specs,
        out_specs=[out, out, out, ctx_out, ctx_out],
        out_shape=[jax.ShapeDtypeStruct((N_TOK, D_MODEL), BF16),
                   jax.ShapeDtypeStruct((N_TOK, D_MODEL), BF16),
                   jax.ShapeDtypeStruct((N_TOK, D_MODEL), BF16),
                   ctx_shape, ctx_shape],
        scratch_shapes=[pltpu.VMEM((tm, D_MODEL), BF16)],
        input_output_aliases=aliases,
        compiler_params=pltpu.CompilerParams(dimension_semantics=("arbitrary", "arbitrary"),
                                             vmem_limit_bytes=BIG_VMEM_LIMIT),
        name="att_qkv",
    )(*args)


PAD = SUBLANES


def _block_seqlen(i, tm):
    return jnp.where(i < N_CTX // tm, SEQ, DEC_SEQ)


def _project_with_halo(h, hh, w, u_ref, tm):
    u_ref[PAD:PAD + tm, :] = _dot(h, w)
    uh = _dot(hh, w)
    u_ref[PAD - 1:PAD, :] = uh[SUBLANES - 1:SUBLANES, :]
    u_ref[PAD + tm:PAD + tm + 1, :] = uh[SUBLANES:SUBLANES + 1, :]


def _conv_chunks(u_refs, cws, cbs, row0, seqmask, tm, rows, emit):
    ridx = lax.broadcasted_iota(jnp.int32, (rows, 1), 0)
    for r0 in range(0, tm, rows):
        outs = []
        for u_ref, cw, cb in zip(u_refs, cws, cbs):
            prev = u_ref[pl.ds(PAD - 1 + r0, rows), :]
            cur = u_ref[pl.ds(PAD + r0, rows), :]
            nxt = u_ref[pl.ds(PAD + 1 + r0, rows), :]
            if r0 % SEQ == 0:
                starts = ((row0 + r0) & seqmask) == 0
                prev = jnp.where(jnp.logical_and(ridx == 0, starts), 0.0, prev)
            if (r0 + rows) % SEQ == 0:
                ends = ((row0 + r0 + rows) & seqmask) == 0
                nxt = jnp.where(jnp.logical_and(ridx == rows - 1, ends), 0.0, nxt)
            outs.append(prev * cw[0:1, :] + cur * cw[1:2, :] + nxt * cw[2:3, :] + cb)
        emit(r0, outs)


def _halo_x_specs(tm):
    per = tm // SUBLANES
    last = N_TOK // SUBLANES - 1
    return [
        pl.BlockSpec((tm, D_MODEL), lambda i, j: (i, 0)),
        pl.BlockSpec((SUBLANES, D_MODEL), lambda i, j: (jnp.maximum(i * per - 1, 0), 0)),
        pl.BlockSpec((SUBLANES, D_MODEL), lambda i, j: (jnp.minimum((i + 1) * per, last), 0)),
    ]


def _xbc_kernel(x_ref, xp_ref, xn_ref, g_ref, sh_ref, sc_ref, w_ref, cw_ref, cb_ref, wdt_ref, bdt_ref,
                o_ref, dt_ref, h_ref, hh_ref, u_ref, *, tm, rows):
    i = pl.program_id(0)

    @pl.when(pl.program_id(1) == 0)
    def _():
        g, sc, sh = g_ref[...], sc_ref[...], sh_ref[...]
        h_ref[...] = _norm_mod(x_ref[...], g, sc, sh)
        hh_ref[...] = _norm_mod(jnp.concatenate([xp_ref[...], xn_ref[...]], axis=0), g, sc, sh)
        dt_ref[...] = _softplus(_dot(h_ref[...], wdt_ref[...]) + bdt_ref[...])

    _project_with_halo(h_ref[...], hh_ref[...], w_ref[...], u_ref, tm)

    def emit(r0, outs):
        o_ref[pl.ds(r0, rows), :] = _silu(outs[0])

    _conv_chunks([u_ref], [cw_ref[...]], [cb_ref[...]], i * tm, _block_seqlen(i, tm) - 1, tm, rows, emit)


def _ssd_xbc(x, mod3, layer, g, w_xbc, conv_w, conv_b, w_dt, dt_bias):
    tm, tc, rows = 1024, 512, 32
    ndt = w_dt.shape[1]
    return pl.pallas_call(
        functools.partial(_xbc_kernel, tm=tm, rows=rows),
        grid=(N_TOK // tm, SSD_CONV_CH // tc),
        in_specs=_halo_x_specs(tm) + [
            pl.BlockSpec((1, D_MODEL), lambda i, j: (0, 0)),
            _mod_spec(layer, 0, tm, 2),
            _mod_spec(layer, 1, tm, 2),
            pl.BlockSpec((D_MODEL, tc), lambda i, j: (0, j)),
            pl.BlockSpec((3, tc), lambda i, j: (0, j)),
            pl.BlockSpec((1, tc), lambda i, j: (0, j)),
            pl.BlockSpec((D_MODEL, ndt), lambda i, j: (0, 0)),
            pl.BlockSpec((1, ndt), lambda i, j: (0, 0)),
        ],
        out_specs=[
            pl.BlockSpec((tm, tc), lambda i, j: (i, j)),
            pl.BlockSpec((tm, ndt), lambda i, j: (i, 0)),
        ],
        out_shape=[
            jax.ShapeDtypeStruct((N_TOK, SSD_CONV_CH), F32),
            jax.ShapeDtypeStruct((N_TOK, ndt), F32),
        ],
        scratch_shapes=[pltpu.VMEM((tm, D_MODEL), BF16), pltpu.VMEM((2 * SUBLANES, D_MODEL), BF16),
                        pltpu.VMEM((tm + 2 * SUBLANES, tc), F32)],
        compiler_params=_params("parallel", "arbitrary"),
        name="ssd_xbc",
    )(x, x, x, g.reshape(1, D_MODEL), mod3, mod3, w_xbc, conv_w, conv_b.reshape(1, SSD_CONV_CH), w_dt, dt_bias)


def _ffn_kernel(x_ref, xp_ref, xn_ref, g_ref, sh_ref, sc_ref, gate_ref, wa_ref, wv_ref,
                cwa_ref, cba_ref, cwv_ref, cbv_ref, wd_ref, o_ref,
                h_ref, hh_ref, ua_ref, uv_ref, act_ref, *, tm, th, rows):
    i = pl.program_id(0)
    j = pl.program_id(1)

    @pl.when(j == 0)
    def _():
        g, sc, sh = g_ref[...], sc_ref[...], sh_ref[...]
        h_ref[...] = _norm_mod(x_ref[...], g, sc, sh)
        hh_ref[...] = _norm_mod(jnp.concatenate([xp_ref[...], xn_ref[...]], axis=0), g, sc, sh)

    h = h_ref[...]
    hh = hh_ref[...]
    _project_with_halo(h, hh, wa_ref[...], ua_ref, tm)
    _project_with_halo(h, hh, wv_ref[...], uv_ref, tm)

    col0 = pl.multiple_of(j * th, th)

    def emit(r0, outs):
        act_ref[pl.ds(r0, rows), pl.ds(col0, th)] = (_silu(outs[0]) * outs[1]).astype(BF16)

    _conv_chunks([ua_ref, uv_ref], [cwa_ref[...], cwv_ref[...]], [cba_ref[...], cbv_ref[...]],
                 i * tm, _block_seqlen(i, tm) - 1, tm, rows, emit)

    @pl.when(j == pl.num_programs(1) - 1)
    def _():
        o_ref[...] = x_ref[...] + gate_ref[...] * _dot(act_ref[...], wd_ref[...])


def _ffn(x, mod3, layer, g, up_w, conv_w, conv_b, down_w):
    tm, th, rows = 1024, 256, 64
    nb = FFN_HIDDEN // th
    cb = conv_b.reshape(1, 2 * FFN_HIDDEN)
    return pl.pallas_call(
        functools.partial(_ffn_kernel, tm=tm, th=th, rows=rows),
        grid=(N_TOK // tm, nb),
        in_specs=_halo_x_specs(tm) + [
            pl.BlockSpec((1, D_MODEL), lambda i, j: (0, 0)),
            _mod_spec(layer, 3, tm, 2),
            _mod_spec(layer, 4, tm, 2),
            _mod_spec(layer, 5, tm, 2),
            pl.BlockSpec((D_MODEL, th), lambda i, j: (0, j)),
            pl.BlockSpec((D_MODEL, th), lambda i, j: (0, j + nb)),
            pl.BlockSpec((3, th), lambda i, j: (0, j)),
            pl.BlockSpec((1, th), lambda i, j: (0, j)),
            pl.BlockSpec((3, th), lambda i, j: (0, j + nb)),
            pl.BlockSpec((1, th), lambda i, j: (0, j + nb)),
            pl.BlockSpec((FFN_HIDDEN, D_MODEL), lambda i, j: (0, 0)),
        ],
        out_specs=pl.BlockSpec((tm, D_MODEL), lambda i, j: (i, 0)),
        out_shape=jax.ShapeDtypeStruct((N_TOK, D_MODEL), F32),
        scratch_shapes=[pltpu.VMEM((tm, D_MODEL), BF16), pltpu.VMEM((2 * SUBLANES, D_MODEL), BF16),
                        pltpu.VMEM((tm + 2 * SUBLANES, th), F32), pltpu.VMEM((tm + 2 * SUBLANES, th), F32),
                        pltpu.VMEM((tm, FFN_HIDDEN), BF16)],
        compiler_params=_params("parallel", "arbitrary"),
        name="ffn",
    )(x, x, x, g.reshape(1, D_MODEL), mod3, mod3, mod3, up_w, up_w, conv_w, cb, conv_w, cb, down_w)


def _mm_res_kernel(ac_ref, al_ref, w_ref, x_ref, gate_ref, o_ref, *, n_ctx_blocks):
    a = jnp.where(pl.program_id(0) < n_ctx_blocks, ac_ref[...], al_ref[...])
    o_ref[...] = x_ref[...] + gate_ref[...] * _dot(a, w_ref[...])


def _two_source_specs(block, tm):
    nctx = N_CTX // tm
    return [pl.BlockSpec(block, lambda i: (jnp.minimum(i, nctx - 1), 0)),
            pl.BlockSpec(block, lambda i: (jnp.maximum(i - nctx, 0), 0))]


def _matmul_residual(a_ctx, a_lat, w, x, mod3, layer, kgate, *, name):
    tm = 512
    k = w.shape[0]
    return pl.pallas_call(
        functools.partial(_mm_res_kernel, n_ctx_blocks=N_CTX // tm),
        grid=(N_TOK // tm,),
        in_specs=_two_source_specs((tm, k), tm) + [
            pl.BlockSpec((k, D_MODEL), lambda i: (0, 0)),
            pl.BlockSpec((tm, D_MODEL), lambda i: (i, 0)),
            _mod_spec(layer, kgate, tm, 1),
        ],
        out_specs=pl.BlockSpec((tm, D_MODEL), lambda i: (i, 0)),
        out_shape=jax.ShapeDtypeStruct((N_TOK, D_MODEL), F32),
        compiler_params=_params("parallel"),
        name=name,
    )(a_ctx, a_lat, w, x, mod3)


def _scan_kernel(*refs, nc, zero_init, n_unused, fill_slot):
    x_ref, b_ref, c_ref, dt_ref, alog_ref, e_ref, dx_ref = refs[:7]
    h0_ref = None if zero_init else refs[7]
    y_ref, hl_ref, st_ref, yt_ref = refs[7 + (not zero_init) + n_unused:]
    d = pl.program_id(1)
    c = pl.program_id(2)
    q = SSD_CHUNK

    @pl.when(c == 0)
    def _():
        if zero_init:
            st_ref[...] = jnp.zeros_like(st_ref)
        else:
            st_ref[...] = h0_ref[...].T

    half = SSD_HEAD_DIM
    sgn = 1 - 2 * d
    ii = lax.broadcasted_iota(jnp.int32, (q, q), 0)
    jj = lax.broadcasted_iota(jnp.int32, (q, q), 1)
    tri_b = jnp.where(((ii - jj) * sgn) >= 0, 1.0, 0.0).astype(BF16)
    jl = jj & (half - 1)
    tri_mix = [((ii - (half * jh + jl)) * sgn) >= 0 for jh in range(2)]
    lane = lax.broadcasted_iota(jnp.int32, (half, LANES), 1)
    low = lane < half

    def split_cat(v):
        return jnp.concatenate(_split3(v), axis=1)

    dtb = dt_ref[...]
    da = dtb * (-jnp.exp(alog_ref[...]) * LOG2E)
    cs3 = _dot(tri_b, split_cat(da))
    cs = cs3[:, 0:LANES] + cs3[:, LANES:2 * LANES] + cs3[:, 2 * LANES:]
    e3 = e_ref[...]
    dt_x = _dot(split_cat(dtb)[:, 0:2 * LANES], e3[0:2 * LANES, :])
    cs_x = _dot(split_cat(cs), e3)
    tot_x = jnp.where(d == 0, cs_x[q - 1:q, :], cs_x[0:1, :])
    cs_rows = [jnp.concatenate([cs[half * jh:half * (jh + 1), :]] * 2, axis=0).T for jh in range(2)]

    xs = x_ref[...]
    xd = xs * dt_x
    xw = (xd * jnp.exp2(tot_x - cs_x)).astype(BF16)
    dec_out = jnp.exp2(cs_x)
    dec_chunk = jnp.exp2(tot_x)

    for g in range(SSD_GROUPS):
        gs = slice(g * GROUP_W, (g + 1) * GROUP_W)
        bg = b_ref[:, g * SSD_STATE:(g + 1) * SSD_STATE]
        cg = c_ref[:, g * SSD_STATE:(g + 1) * SSD_STATE].astype(BF16)
        bgb = bg.astype(BF16)
        cb_mix = [_dot_nt(cg, jnp.concatenate([bgb[half * jh:half * (jh + 1), :]] * 2, axis=0)) for jh in range(2)]
        st_g = st_ref[:, gs]
        y_off = _dot(cg, st_g.astype(BF16))
        new_g = _dot(bg.T.astype(BF16), xw[:, gs])
        st_ref[:, gs] = st_g * dec_chunk[:, gs] + new_g
        for p in range(HEADS_PER_GROUP // 2):
            h = g * HEADS_PER_GROUP + 2 * p
            cols = slice(h * SSD_HEAD_DIM, (h + 2) * SSD_HEAD_DIM)
            cs_col = cs_x[:, cols]
            xd2 = xd[:, cols]
            ms, rs = [], []
            for jh in range(2):
                row = jnp.where(low[0:1, :], cs_rows[jh][h:h + 1, :], cs_rows[jh][h + 1:h + 2, :])
                decay = jnp.where(tri_mix[jh], jnp.exp2(cs_col - row), 0.0)
                ms.append((cb_mix[jh] * decay).astype(BF16))
                xj = xd2[half * jh:half * (jh + 1), :]
                rs += [jnp.where(low, xj, 0.0), jnp.where(low, 0.0, xj)]
            lhs = jnp.concatenate(ms, axis=1)
            rhs = jnp.concatenate(rs, axis=0).astype(BF16)
            loc = slice(2 * p * SSD_HEAD_DIM, (2 * p + 2) * SSD_HEAD_DIM)
            yt_ref[:, cols] = _dot(lhs, rhs) + y_off[:, loc] * dec_out[:, cols]

    r0 = pl.multiple_of((c + d * (nc - 1 - 2 * c)) * q, q)

    @pl.when(d == 0)
    def _():
        y_ref[pl.ds(r0, q), :] = yt_ref[...] + dx_ref[...] * xs

    @pl.when(d == 1)
    def _():
        y_ref[pl.ds(r0, q), :] += yt_ref[...]

    @pl.when(c == nc - 1)
    def _():
        if fill_slot is None:
            hl_ref[...] = st_ref[...].T
        else:
            for s in range(hl_ref.shape[0]):
                hl_ref[s] = st_ref[...].T if s == fill_slot else jnp.zeros(hl_ref.shape[1:], F32)


def _ssd_scan(xbc, dt, a_log_pad, expand, d_x, h0, slot, *, nseq, nc, chunk0, name, n_slots=1, states=None):
    q = SSD_CHUNK
    zero_init = h0 is None
    out_slot = slot if n_slots > 1 else 0

    def chunk(b, d, c):
        return chunk0 + b * nc + c + d * (nc - 1 - 2 * c)

    in_specs = [
        pl.BlockSpec((q, SSD_INNER), lambda b, d, c: (chunk(b, d, c), 0)),
        pl.BlockSpec((q, SSD_BC), lambda b, d, c: (chunk(b, d, c), SSD_INNER // SSD_BC)),
        pl.BlockSpec((q, SSD_BC), lambda b, d, c: (chunk(b, d, c), SSD_INNER // SSD_BC + 1)),
        pl.BlockSpec((q, LANES), lambda b, d, c: (chunk(b, d, c), d)),
        pl.BlockSpec((None, 1, LANES), lambda b, d, c: (d, 0, 0)),
        pl.BlockSpec((3 * LANES, SSD_INNER), lambda b, d, c: (0, 0)),
        pl.BlockSpec((1, SSD_INNER), lambda b, d, c: (0, 0)),
    ]
    args = [xbc, xbc, xbc, dt, a_log_pad, expand, d_x]
    if not zero_init:
        in_specs.append(pl.BlockSpec((None, None, None, SSD_INNER, SSD_STATE),
                                     lambda b, d, c: (b, slot, d, 0, 0)))
        args.append(h0)
    aliases = {}
    if states is not None:
        in_specs.append(pl.BlockSpec(memory_space=pl.ANY))
        args.append(states)
        aliases = {len(args) - 1: 1}
    fill_all = states is None and n_slots > 1
    if fill_all:
        state_spec = pl.BlockSpec((None, n_slots, None, SSD_INNER, SSD_STATE), lambda b, d, c: (b, 0, d, 0, 0))
    else:
        state_spec = pl.BlockSpec((None, None, None, SSD_INNER, SSD_STATE),
                                  lambda b, d, c: (b, out_slot, d, 0, 0))
    return pl.pallas_call(
        functools.partial(_scan_kernel, nc=nc, zero_init=zero_init, n_unused=len(aliases),
                          fill_slot=out_slot if fill_all else None),
        grid=(nseq, 2, nc),
        in_specs=in_specs,
        out_specs=[
            pl.BlockSpec((nc * q, SSD_INNER), lambda b, d, c: (b, 0)),
            state_spec,
        ],
        out_shape=[
            jax.ShapeDtypeStruct((nseq * nc * q, SSD_INNER), F32),
            jax.ShapeDtypeStruct((nseq, n_slots, 2, SSD_INNER, SSD_STATE), F32),
        ],
        scratch_shapes=[pltpu.VMEM((SSD_STATE, SSD_INNER), F32), pltpu.VMEM((q, SSD_INNER), F32)],
        input_output_aliases=aliases,
        compiler_params=pltpu.CompilerParams(dimension_semantics=("parallel", "arbitrary", "arbitrary"),
                                             vmem_limit_bytes=BIG_VMEM_LIMIT),
        name=name,
    )(*args)


def _ssd_out_kernel(yc_ref, yl_ref, z_ref, ng_ref, w_ref, x_ref, gate_ref, o_ref, *, n_ctx_blocks):
    y = jnp.where(pl.program_id(0) < n_ctx_blocks, yc_ref[...], yl_ref[...])
    y = y * _silu(z_ref[...].astype(F32))
    gw = SSD_INNER // SSD_GROUPS
    parts = []
    for g in range(SSD_GROUPS):
        yg = y[:, g * gw:(g + 1) * gw]
        parts.append(yg * lax.rsqrt(jnp.mean(yg * yg, axis=-1, keepdims=True) + EPS))
    yn = (jnp.concatenate(parts, axis=1) * ng_ref[...]).astype(BF16)
    o_ref[...] = x_ref[...] + gate_ref[...] * _dot(yn, w_ref[...])


def _ssd_out(y_ctx, y_lat, z, norm_g, out_w, x, mod3, layer):
    tm = 512
    return pl.pallas_call(
        functools.partial(_ssd_out_kernel, n_ctx_blocks=N_CTX // tm),
        grid=(N_TOK // tm,),
        in_specs=_two_source_specs((tm, SSD_INNER), tm) + [
            pl.BlockSpec((tm, SSD_INNER), lambda i: (i, 0)),
            pl.BlockSpec((1, SSD_INNER), lambda i: (0, 0)),
            pl.BlockSpec((SSD_INNER, D_MODEL), lambda i: (0, 0)),
            pl.BlockSpec((tm, D_MODEL), lambda i: (i, 0)),
            _mod_spec(layer, 2, tm, 1),
        ],
        out_specs=pl.BlockSpec((tm, D_MODEL), lambda i: (i, 0)),
        out_shape=jax.ShapeDtypeStruct((N_TOK, D_MODEL), F32),
        compiler_params=_params("parallel"),
        name="ssd_out",
    )(y_ctx, y_lat, z, norm_g.reshape(1, SSD_INNER), out_w, x, mod3)


def _diff_lambda(lam_ref, lam_init):
    lv = lam_ref[...]
    return (jnp.exp(jnp.sum(lv[0:1] * lv[1:2], axis=-1, keepdims=True))
            - jnp.exp(jnp.sum(lv[2:3] * lv[3:4], axis=-1, keepdims=True)) + lam_init)


def _attend_head(q, kb, vb, lam, g, lam_init):
    lane = lax.broadcasted_iota(jnp.int32, q.shape, 1)
    zero = jnp.zeros_like(q)
    probs = []
    for qm in (jnp.where(lane < DA_HEAD_DIM, q, zero), jnp.where(lane >= DA_HEAD_DIM, q, zero)):
        s = _dot_nt(qm, kb)
        p = jnp.exp2(s - jnp.max(s, axis=-1, keepdims=True))
        probs.append((p, jnp.sum(p, axis=-1, keepdims=True)))
    (p1, l1), (p2, l2) = probs
    w = (p1 - p2 * (lam * l1 / l2)).astype(BF16)
    o = _dot(w, vb) * (1.0 / l1)
    o = o * lax.rsqrt(jnp.mean(o * o, axis=-1, keepdims=True) + EPS)
    return o * g * (1.0 - lam_init)


def _attn_ctx_kernel(q_ref, k_ref, v_ref, lam_ref, g_ref, o_ref, *, lam_init):
    lam = _diff_lambda(lam_ref, lam_init)
    g = g_ref[...]
    for h in range(DA_HEADS):
        cols = slice(h * HEAD_W, (h + 1) * HEAD_W)
        o = _attend_head(q_ref[:, cols], k_ref[:, cols], v_ref[:, cols], lam, g, lam_init)
        o_ref[:, cols] = o.astype(o_ref.dtype)


def _attn_lat_kernel(q_ref, k_ref, v_ref, kc_ref, vc_ref, lam_ref, g_ref, o_ref, kb_ref, vb_ref, *, lam_init):
    @pl.when(pl.program_id(2) == 0)
    def _():
        kb_ref[0:DEC_SEQ, :] = k_ref[...]
        kb_ref[DEC_SEQ:, :] = kc_ref[...].astype(BF16)
        vb_ref[0:DEC_SEQ, :] = v_ref[...]
        vb_ref[DEC_SEQ:, :] = vc_ref[...].astype(BF16)

    o = _attend_head(q_ref[...], kb_ref[...], vb_ref[...], _diff_lambda(lam_ref, lam_init), g_ref[...], lam_init)
    o_ref[...] = o.astype(o_ref.dtype)


def _attention(q, k, v, cache_k4, cache_v4, slot, lam_vecs, subln_g, lam_init):
    common = [pl.BlockSpec((4, DA_HEAD_DIM), lambda *_: (0, 0)),
              pl.BlockSpec((1, HEAD_W), lambda *_: (0, 0))]
    g2 = subln_g.reshape(1, HEAD_W)

    seq_block = pl.BlockSpec((SEQ, D_MODEL), lambda b: (b, 0))
    ctx = pl.pallas_call(
        functools.partial(_attn_ctx_kernel, lam_init=lam_init),
        grid=(BATCH,),
        in_specs=[seq_block, seq_block, seq_block] + common,
        out_specs=seq_block,
        out_shape=jax.ShapeDtypeStruct((N_CTX, D_MODEL), BF16),
        compiler_params=_params("parallel"),
        name="attn_ctx",
    )(q, k, v, lam_vecs, g2)

    tq = 256
    nq = DEC_SEQ // tq
    q0 = N_CTX // tq
    s0 = N_CTX // DEC_SEQ
    keys = DEC_SEQ + PAST_LEN
    lat = pl.pallas_call(
        functools.partial(_attn_lat_kernel, lam_init=lam_init),
        grid=(DEC_BATCH, DA_HEADS, nq),
        in_specs=[
            pl.BlockSpec((tq, HEAD_W), lambda b, h, i: (q0 + b * nq + i, h)),
            pl.BlockSpec((DEC_SEQ, HEAD_W), lambda b, h, i: (s0 + b, h)),
            pl.BlockSpec((DEC_SEQ, HEAD_W), lambda b, h, i: (s0 + b, h)),
            pl.BlockSpec((None, None, PAST_LEN, HEAD_W), lambda b, h, i: (b, slot, 0, h)),
            pl.BlockSpec((None, None, PAST_LEN, HEAD_W), lambda b, h, i: (b, slot, 0, h)),
        ] + common,
        out_specs=pl.BlockSpec((tq, HEAD_W), lambda b, h, i: (b * nq + i, h)),
        out_shape=jax.ShapeDtypeStruct((N_LAT, D_MODEL), BF16),
        scratch_shapes=[pltpu.VMEM((keys, HEAD_W), BF16), pltpu.VMEM((keys, HEAD_W), BF16)],
        compiler_params=_params("parallel", "parallel", "arbitrary"),
        name="attn_lat",
    )(q, k, v, cache_k4, cache_v4, lam_vecs, g2)
    return ctx, lat


def _final_kernel(x_ref, g_ref, o_ref):
    x = x_ref[...]
    o_ref[...] = x * lax.rsqrt(jnp.mean(x * x, axis=-1, keepdims=True) + EPS) * g_ref[...]


def _final_norm(x, g, row0, nrows):
    tm = 1024
    b0 = row0 // tm
    return pl.pallas_call(
        _final_kernel,
        grid=(nrows // tm,),
        in_specs=[pl.BlockSpec((tm, D_MODEL), lambda i: (b0 + i, 0)),
                  pl.BlockSpec((1, D_MODEL), lambda i: (0, 0))],
        out_specs=pl.BlockSpec((tm, D_MODEL), lambda i: (i, 0)),
        out_shape=jax.ShapeDtypeStruct((nrows, D_MODEL), F32),
        compiler_params=_params("parallel"),
        name="final_norm",
    )(x, g.reshape(1, D_MODEL))


def _rope_token_tables():
    rows = DEC_SEQ // GRID_W
    row = jnp.repeat(jnp.arange(rows, dtype=F32), GRID_W)
    col = jnp.tile(jnp.arange(GRID_W, dtype=F32), rows)
    inv = 1.0 / (ROPE_BASE ** (jnp.arange(0, ROPE_AXIS_DIM, 2, dtype=F32) / ROPE_AXIS_DIM))
    ang_r = row[:, None] * inv
    ang_c = col[:, None] * inv
    cr, sr, cc, sc = jnp.cos(ang_r), jnp.sin(ang_r), jnp.cos(ang_c), jnp.sin(ang_c)
    cos64 = jnp.concatenate([cr, cr, cc, cc], axis=1)
    sin64 = jnp.concatenate([-sr, sr, -sc, sc], axis=1)
    cos_lat = jnp.tile(cos64, (DEC_BATCH, 2))
    sin_lat = jnp.tile(sin64, (DEC_BATCH, 2))
    cos = jnp.concatenate([jnp.ones((N_CTX, HEAD_W), F32), cos_lat], axis=0)
    sin = jnp.concatenate([jnp.zeros((N_CTX, HEAD_W), F32), sin_lat], axis=0)
    return cos, sin


def _lambda_init(layer):
    return 0.8 - 0.6 * math.exp(-0.3 * layer)


def kernel(x_prompt, x_sample, state_ssd, cache_k, cache_v, c, c_ctx, mod_w, mod_b, norm_mix_g, norm_ffn_g, ssd_in_w, ssd_conv_w, ssd_conv_b, ssd_dt_bias, ssd_a_log, ssd_d, ssd_norm_g, ssd_out_w, att_qkv_w, att_lambda, att_subln_g, att_out_w, ffn_up_w, ffn_conv_w, ffn_conv_b, ffn_down_w, final_norm_g):
    n_ssd = ssd_in_w.shape[0]
    n_att = att_qkv_w.shape[0]
    x = None

    cpad = jnp.concatenate([c_ctx[None], c, jnp.zeros((MOD_ROWS - 1 - DEC_BATCH, D_MODEL), F32)], axis=0)
    mod = _modulation(cpad, mod_w, mod_b)
    mod3 = mod.reshape(DEPTH * MOD_ROWS * 6, 1, D_MODEL)

    cos_t, sin_t = _rope_token_tables()
    expand = (jnp.arange(LANES)[:, None] == (jnp.arange(SSD_INNER)[None, :] // SSD_HEAD_DIM)).astype(BF16)
    expand = jnp.concatenate([expand] * 3, axis=0)
    h0_lat = state_ssd.reshape(DEC_BATCH, n_ssd, 2, SSD_INNER, SSD_STATE)
    cache_k4 = cache_k.reshape(DEC_BATCH, n_att, PAST_LEN, D_MODEL)
    cache_v4 = cache_v.reshape(DEC_BATCH, n_att, PAST_LEN, D_MODEL)

    new_ssd = None
    new_kv = [jnp.zeros((BATCH, n_att, SEQ, D_MODEL), F32)] * 2
    for i in range(DEPTH):
        slot = i // 2
        if i % 2 == 0:
            in_w = ssd_in_w[slot]
            w_z = in_w[:, :SSD_INNER].astype(BF16)
            w_xbc = in_w[:, SSD_INNER:SSD_MAIN].astype(BF16)
            pad = jnp.zeros((D_MODEL, LANES - SSD_HEADS), F32)
            w_dt = jnp.concatenate([in_w[:, SSD_MAIN:SSD_MAIN + SSD_HEADS], pad,
                                    in_w[:, SSD_MAIN + SSD_HEADS:], pad], axis=1).astype(BF16)
            zpad = jnp.zeros((LANES - SSD_HEADS,), F32)
            dt_bias = jnp.concatenate([ssd_dt_bias[slot, 0], zpad, ssd_dt_bias[slot, 1], zpad]).reshape(1, 2 * LANES)
            a_log_pad = jnp.concatenate([ssd_a_log[slot], jnp.zeros((2, LANES - SSD_HEADS), F32)],
                                        axis=1).reshape(2, 1, LANES)
            d_x = jnp.repeat(ssd_d[slot], SSD_HEAD_DIM).reshape(1, SSD_INNER)
            if x is None:
                z, x = _norm_mod_matmul_join(x_prompt.reshape(N_CTX, D_MODEL), x_sample.reshape(N_LAT, D_MODEL),
                                             mod3, i, 0, 1, norm_mix_g[i], w_z, tn=1024, out_dtype=BF16,
                                             name="ssd_z_join")
            else:
                z = _norm_mod_matmul(x, mod3, i, 0, 1, norm_mix_g[i], w_z, tn=1024, out_dtype=BF16, name="ssd_z")
            xbc, dt = _ssd_xbc(x, mod3, i, norm_mix_g[i], w_xbc, ssd_conv_w[slot], ssd_conv_b[slot], w_dt, dt_bias)
            y_ctx, new_ssd = _ssd_scan(xbc, dt, a_log_pad, expand, d_x, None, slot,
                                       nseq=BATCH, nc=SEQ // SSD_CHUNK, chunk0=0, name="ssd_scan_ctx",
                                       n_slots=n_ssd, states=new_ssd)
            y_lat, _ = _ssd_scan(xbc, dt, a_log_pad, expand, d_x, h0_lat, slot,
                                 nseq=DEC_BATCH, nc=DEC_SEQ // SSD_CHUNK, chunk0=N_CTX // SSD_CHUNK,
                                 name="ssd_scan_lat")
            x = _ssd_out(y_ctx, y_lat, z, ssd_norm_g[slot], ssd_out_w[slot].astype(BF16), x, mod3, i)
        else:
            lam_init = _lambda_init(i)
            q, k, v, *new_kv = _qkv_proj(x, mod3, i, norm_mix_g[i], att_qkv_w[slot].astype(BF16),
                                         cos_t, sin_t, slot, n_att, new_kv)
            o_ctx, o_lat = _attention(q, k, v, cache_k4, cache_v4, slot, att_lambda[slot], att_subln_g[slot],
                                      lam_init)
            x = _matmul_residual(o_ctx, o_lat, att_out_w[slot].astype(BF16), x, mod3, i, 2, name="att_out")
        x = _ffn(x, mod3, i, norm_ffn_g[i], ffn_up_w[i].astype(BF16), ffn_conv_w[i], ffn_conv_b[i],
                 ffn_down_w[i].astype(BF16))

    y_prompt = _final_norm(x, final_norm_g, 0, N_CTX).reshape(BATCH, SEQ, D_MODEL)
    y_sample = _final_norm(x, final_norm_g, N_CTX, N_LAT).reshape(DEC_BATCH, DEC_SEQ, D_MODEL)
    return (y_prompt, y_sample,
            new_ssd.reshape(BATCH, n_ssd, 2, SSD_HEADS, SSD_HEAD_DIM, SSD_STATE),
            new_kv[0].reshape(BATCH, n_att, SEQ, DA_HEADS, 2, DA_HEAD_DIM),
            new_kv[1].reshape(BATCH, n_att, SEQ, DA_HEADS, HEAD_W))
```

```python
import functools
import math

import jax
import jax.numpy as jnp
from jax import lax
from jax.experimental import pallas as pl
from jax.experimental.pallas import tpu as pltpu

F32 = jnp.float32
BF16 = jnp.bfloat16

D_MODEL = 1024
BATCH = 16
SEQ = 256
DEPTH = 4
DEC_BATCH = 4
DEC_SEQ = 2048
PAST_LEN = 256
GRID_W = 64
EPS = 1e-6

SSD_INNER = 2048
SSD_HEAD_DIM = 64
SSD_HEADS = 32
SSD_GROUPS = 4
SSD_STATE = 128
SSD_CHUNK = 128
SSD_BC = SSD_GROUPS * SSD_STATE
SSD_CONV_CH = SSD_INNER + 2 * SSD_BC
SSD_MAIN = SSD_INNER + SSD_CONV_CH
HEADS_PER_GROUP = SSD_HEADS // SSD_GROUPS
GROUP_W = HEADS_PER_GROUP * SSD_HEAD_DIM

DA_HEAD_DIM = 64
DA_HEADS = 8
DA_SCALE = DA_HEAD_DIM ** -0.5
LOG2E = math.log2(math.e)
Q_PRESCALE = DA_SCALE * LOG2E
HEAD_W = 2 * DA_HEAD_DIM
ROPE_BASE = 10000.0
ROPE_AXIS_DIM = DA_HEAD_DIM // 2

FFN_HIDDEN = 2816

N_CTX = BATCH * SEQ
N_LAT = DEC_BATCH * DEC_SEQ
N_TOK = N_CTX + N_LAT
MOD_ROWS = 8
LANES = 128
SUBLANES = 8
VMEM_LIMIT = 48 * 1024 * 1024
BIG_VMEM_LIMIT = 56 * 1024 * 1024


def _params(*sem):
    return pltpu.CompilerParams(dimension_semantics=sem, vmem_limit_bytes=VMEM_LIMIT)


def _mod_row(i, tm):
    nctx = N_CTX // tm
    return jnp.where(i < nctx, 0, 1 + (i - nctx) // (DEC_SEQ // tm))


def _silu(x):
    return x * jax.nn.sigmoid(x)


def _softplus(x):
    return jnp.maximum(x, 0.0) + jnp.log1p(jnp.exp(-jnp.abs(x)))


def _split3(x):
    hi = x.astype(BF16)
    r1 = x - hi.astype(F32)
    mid = r1.astype(BF16)
    lo = (r1 - mid.astype(F32)).astype(BF16)
    return hi, mid, lo


def _dot(a, b):
    return jnp.dot(a, b, preferred_element_type=F32)


def _dot_nt(a, b):
    return lax.dot_general(a, b, (((1,), (1,)), ((), ())), preferred_element_type=F32)


def _norm_mod(x, g, sc, sh):
    r = x * lax.rsqrt(jnp.mean(x * x, axis=-1, keepdims=True) + EPS)
    return ((r * g) * (1.0 + sc) + sh).astype(BF16)


def _mod_kernel(c_ref, w_ref, b_ref, o_ref):
    s = _silu(c_ref[...]).astype(BF16)
    o_ref[...] = _dot(s, w_ref[...].astype(BF16)) + b_ref[...]


def _modulation(cpad, mod_w, mod_b):
    tn = 1024
    n = 6 * D_MODEL
    return pl.pallas_call(
        _mod_kernel,
        grid=(DEPTH, n // tn),
        in_specs=[
            pl.BlockSpec((MOD_ROWS, D_MODEL), lambda l, j: (0, 0)),
            pl.BlockSpec((None, D_MODEL, tn), lambda l, j: (l, 0, j)),
            pl.BlockSpec((None, 1, tn), lambda l, j: (l, 0, j)),
        ],
        out_specs=pl.BlockSpec((None, MOD_ROWS, tn), lambda l, j: (l, 0, j)),
        out_shape=jax.ShapeDtypeStruct((DEPTH, MOD_ROWS, n), F32),
        compiler_params=_params("parallel", "parallel"),
        name="modulation",
    )(cpad, mod_w, mod_b.reshape(DEPTH, 1, n))


def _mod_spec(layer, k, tm, ngrid):
    base = layer * MOD_ROWS * 6 + k
    if ngrid == 1:
        return pl.BlockSpec((None, 1, D_MODEL), lambda i: (base + 6 * _mod_row(i, tm), 0, 0))
    return pl.BlockSpec((None, 1, D_MODEL), lambda i, j: (base + 6 * _mod_row(i, tm), 0, 0))


def _nmm_kernel(x_ref, g_ref, sh_ref, sc_ref, w_ref, o_ref, h_ref):
    @pl.when(pl.program_id(1) == 0)
    def _():
        h_ref[...] = _norm_mod(x_ref[...], g_ref[...], sc_ref[...], sh_ref[...])

    o_ref[...] = _dot(h_ref[...], w_ref[...]).astype(o_ref.dtype)


def _nmm_in_specs(layer, kshift, kscale, tm, tn):
    return [
        pl.BlockSpec((tm, D_MODEL), lambda i, j: (i, 0)),
        pl.BlockSpec((1, D_MODEL), lambda i, j: (0, 0)),
        _mod_spec(layer, kshift, tm, 2),
        _mod_spec(layer, kscale, tm, 2),
        pl.BlockSpec((D_MODEL, tn), lambda i, j: (0, j)),
    ]


def _norm_mod_matmul(x, mod3, layer, kshift, kscale, g, w, *, tn, out_dtype, name):
    tm = 1024
    n = w.shape[1]
    return pl.pallas_call(
        _nmm_kernel,
        grid=(N_TOK // tm, n // tn),
        in_specs=_nmm_in_specs(layer, kshift, kscale, tm, tn),
        out_specs=pl.BlockSpec((tm, tn), lambda i, j: (i, j)),
        out_shape=jax.ShapeDtypeStruct((N_TOK, n), out_dtype),
        scratch_shapes=[pltpu.VMEM((tm, D_MODEL), BF16)],
        compiler_params=_params("parallel", "arbitrary"),
        name=name,
    )(x, g.reshape(1, D_MODEL), mod3, mod3, w)


def _nmm_join_kernel(xc_ref, xl_ref, g_ref, sh_ref, sc_ref, w_ref, o_ref, xo_ref, h_ref, *, n_ctx_blocks):
    @pl.when(pl.program_id(1) == 0)
    def _():
        x = jnp.where(pl.program_id(0) < n_ctx_blocks, xc_ref[...], xl_ref[...])
        xo_ref[...] = x
        h_ref[...] = _norm_mod(x, g_ref[...], sc_ref[...], sh_ref[...])

    o_ref[...] = _dot(h_ref[...], w_ref[...]).astype(o_ref.dtype)


def _norm_mod_matmul_join(x_ctx, x_lat, mod3, layer, kshift, kscale, g, w, *, tn, out_dtype, name):
    tm = 1024
    n = w.shape[1]
    nctx = N_CTX // tm
    specs = _nmm_in_specs(layer, kshift, kscale, tm, tn)
    return pl.pallas_call(
        functools.partial(_nmm_join_kernel, n_ctx_blocks=nctx),
        grid=(N_TOK // tm, n // tn),
        in_specs=[pl.BlockSpec((tm, D_MODEL), lambda i, j: (jnp.minimum(i, nctx - 1), 0)),
                  pl.BlockSpec((tm, D_MODEL), lambda i, j: (jnp.maximum(i - nctx, 0), 0))] + specs[1:],
        out_specs=[pl.BlockSpec((tm, tn), lambda i, j: (i, j)),
                   pl.BlockSpec((tm, D_MODEL), lambda i, j: (i, 0))],
        out_shape=[jax.ShapeDtypeStruct((N_TOK, n), out_dtype),
                   jax.ShapeDtypeStruct((N_TOK, D_MODEL), F32)],
        scratch_shapes=[pltpu.VMEM((tm, D_MODEL), BF16)],
        compiler_params=_params("parallel", "arbitrary"),
        name=name,
    )(x_ctx, x_lat, g.reshape(1, D_MODEL), mod3, mod3, w)


def _qkv_kernel(*refs, n_ctx_blocks, n_unused):
    x_ref, g_ref, sh_ref, sc_ref, w_ref, cos_ref, sin_ref = refs[:7]
    q_ref, k_ref, v_ref, kctx_ref, vctx_ref, h_ref = refs[7 + n_unused:]
    j = pl.program_id(1)
    is_ctx = pl.program_id(0) < n_ctx_blocks

    @pl.when(j == 0)
    def _():
        h_ref[...] = _norm_mod(x_ref[...], g_ref[...], sc_ref[...], sh_ref[...])

    u = _dot(h_ref[...], w_ref[...])

    def rope_into(o_refs, scale):
        cos = cos_ref[...] * scale
        sin = sin_ref[...] * scale
        lane = lax.broadcasted_iota(jnp.int32, cos.shape, 1)
        half = ROPE_AXIS_DIM // 2
        first = (lane % ROPE_AXIS_DIM) < half
        for k in range(u.shape[1] // HEAD_W):
            cols = slice(k * HEAD_W, (k + 1) * HEAD_W)
            uk = u[:, cols]
            partner = jnp.where(first, pltpu.roll(uk, HEAD_W - half, 1), pltpu.roll(uk, half, 1))
            r = uk * cos + partner * sin
            for o_ref in o_refs:
                if len(o_ref.shape) == 3:
                    o_ref[:, :, cols] = r.reshape(o_ref.shape[0], SEQ, HEAD_W)
                else:
                    o_ref[:, cols] = r.astype(o_ref.dtype)

    @pl.when(j == 0)
    def _():
        rope_into([q_ref], Q_PRESCALE)

    @pl.when(jnp.logical_and(j == 1, is_ctx))
    def _():
        rope_into([k_ref, kctx_ref], 1.0)

    @pl.when(jnp.logical_and(j == 1, jnp.logical_not(is_ctx)))
    def _():
        rope_into([k_ref], 1.0)

    @pl.when(j == 2)
    def _():
        v_ref[...] = u.astype(v_ref.dtype)

    @pl.when(jnp.logical_and(j == 2, is_ctx))
    def _():
        vctx_ref[...] = u.reshape(vctx_ref.shape)


def _qkv_proj(x, mod3, layer, g, w, cos_t, sin_t, slot, n_slots, caches):
    tm = 1024
    nctx = N_CTX // tm
    nseq = tm // SEQ
    out = pl.BlockSpec((tm, D_MODEL), lambda i, j: (i, 0))
    ctx_out = pl.BlockSpec((nseq, None, SEQ, D_MODEL), lambda i, j: (jnp.minimum(i, nctx - 1), slot, 0, 0))
    ctx_shape = jax.ShapeDtypeStruct((BATCH, n_slots, SEQ, D_MODEL), F32)
    in_specs = _nmm_in_specs(layer, 0, 1, tm, D_MODEL) + [pl.BlockSpec((tm, HEAD_W), lambda i, j: (i, 0))] * 2
    args = [x, g.reshape(1, D_MODEL), mod3, mod3, w, cos_t, sin_t]
    aliases = {}
    if caches is not None:
        in_specs += [pl.BlockSpec(memory_space=pl.ANY)] * 2
        aliases = {len(args): 3, len(args) + 1: 4}
        args += list(caches)
    return pl.pallas_call(
        functools.partial(_qkv_kernel, n_ctx_blocks=nctx, n_unused=len(aliases)),
        grid=(N_TOK // tm, 3),
        in_specs=in_specs,
        out_specs=[out, out, out, ctx_out, ctx_out],
        out_shape=[jax.ShapeDtypeStruct((N_TOK, D_MODEL), BF16),
                   jax.ShapeDtypeStruct((N_TOK, D_MODEL), BF16),
                   jax.ShapeDtypeStruct((N_TOK, D_MODEL), BF16),
                   ctx_shape, ctx_shape],
        scratch_shapes=[pltpu.VMEM((tm, D_MODEL), BF16)],
        input_output_aliases=aliases,
        compiler_params=pltpu.CompilerParams(dimension_semantics=("arbitrary", "arbitrary"),
                                             vmem_limit_bytes=BIG_VMEM_LIMIT),
        name="att_qkv",
    )(*args)


PAD = SUBLANES


def _block_seqlen(i, tm):
    return jnp.where(i < N_CTX // tm, SEQ, DEC_SEQ)


def _project_with_halo(h, hh, w, u_ref, tm):
    u_ref[PAD:PAD + tm, :] = _dot(h, w)
    uh = _dot(hh, w)
    u_ref[PAD - 1:PAD, :] = uh[SUBLANES - 1:SUBLANES, :]
    u_ref[PAD + tm:PAD + tm + 1, :] = uh[SUBLANES:SUBLANES + 1, :]


def _conv_chunks(u_refs, cws, cbs, row0, seqmask, tm, rows, emit):
    ridx = lax.broadcasted_iota(jnp.int32, (rows, 1), 0)
    for r0 in range(0, tm, rows):
        outs = []
        for u_ref, cw, cb in zip(u_refs, cws, cbs):
            prev = u_ref[pl.ds(PAD - 1 + r0, rows), :]
            cur = u_ref[pl.ds(PAD + r0, rows), :]
            nxt = u_ref[pl.ds(PAD + 1 + r0, rows), :]
            if r0 % SEQ == 0:
                starts = ((row0 + r0) & seqmask) == 0
                prev = jnp.where(jnp.logical_and(ridx == 0, starts), 0.0, prev)
            if (r0 + rows) % SEQ == 0:
                ends = ((row0 + r0 + rows) & seqmask) == 0
                nxt = jnp.where(jnp.logical_and(ridx == rows - 1, ends), 0.0, nxt)
            outs.append(prev * cw[0:1, :] + cur * cw[1:2, :] + nxt * cw[2:3, :] + cb)
        emit(r0, outs)


def _halo_x_specs(tm):
    per = tm // SUBLANES
    last = N_TOK // SUBLANES - 1
    return [
        pl.BlockSpec((tm, D_MODEL), lambda i, j: (i, 0)),
        pl.BlockSpec((SUBLANES, D_MODEL), lambda i, j: (jnp.maximum(i * per - 1, 0), 0)),
        pl.BlockSpec((SUBLANES, D_MODEL), lambda i, j: (jnp.minimum((i + 1) * per, last), 0)),
    ]


def _xbc_kernel(x_ref, xp_ref, xn_ref, g_ref, sh_ref, sc_ref, w_ref, cw_ref, cb_ref, wdt_ref, bdt_ref,
                o_ref, dt_ref, h_ref, hh_ref, u_ref, *, tm, rows):
    i = pl.program_id(0)

    @pl.when(pl.program_id(1) == 0)
    def _():
        g, sc, sh = g_ref[...], sc_ref[...], sh_ref[...]
        h_ref[...] = _norm_mod(x_ref[...], g, sc, sh)
        hh_ref[...] = _norm_mod(jnp.concatenate([xp_ref[...], xn_ref[...]], axis=0), g, sc, sh)
        dt_ref[...] = _softplus(_dot(h_ref[...], wdt_ref[...]) + bdt_ref[...])

    _project_with_halo(h_ref[...], hh_ref[...], w_ref[...], u_ref, tm)

    def emit(r0, outs):
        o_ref[pl.ds(r0, rows), :] = _silu(outs[0])

    _conv_chunks([u_ref], [cw_ref[...]], [cb_ref[...]], i * tm, _block_seqlen(i, tm) - 1, tm, rows, emit)


def _ssd_xbc(x, mod3, layer, g, w_xbc, conv_w, conv_b, w_dt, dt_bias):
    tm, tc, rows = 1024, 512, 32
    ndt = w_dt.shape[1]
    return pl.pallas_call(
        functools.partial(_xbc_kernel, tm=tm, rows=rows),
        grid=(N_TOK // tm, SSD_CONV_CH // tc),
        in_specs=_halo_x_specs(tm) + [
            pl.BlockSpec((1, D_MODEL), lambda i, j: (0, 0)),
            _mod_spec(layer, 0, tm, 2),
            _mod_spec(layer, 1, tm, 2),
            pl.BlockSpec((D_MODEL, tc), lambda i, j: (0, j)),
            pl.BlockSpec((3, tc), lambda i, j: (0, j)),
            pl.BlockSpec((1, tc), lambda i, j: (0, j)),
            pl.BlockSpec((D_MODEL, ndt), lambda i, j: (0, 0)),
            pl.BlockSpec((1, ndt), lambda i, j: (0, 0)),
        ],
        out_specs=[
            pl.BlockSpec((tm, tc), lambda i, j: (i, j)),
            pl.BlockSpec((tm, ndt), lambda i, j: (i, 0)),
        ],
        out_shape=[
            jax.ShapeDtypeStruct((N_TOK, SSD_CONV_CH), F32),
            jax.ShapeDtypeStruct((N_TOK, ndt), F32),
        ],
        scratch_shapes=[pltpu.VMEM((tm, D_MODEL), BF16), pltpu.VMEM((2 * SUBLANES, D_MODEL), BF16),
                        pltpu.VMEM((tm + 2 * SUBLANES, tc), F32)],
        compiler_params=_params("parallel", "arbitrary"),
        name="ssd_xbc",
    )(x, x, x, g.reshape(1, D_MODEL), mod3, mod3, w_xbc, conv_w, conv_b.reshape(1, SSD_CONV_CH), w_dt, dt_bias)


def _ffn_kernel(*refs, tm, th, rows, n_ctx_blocks):
    (x_ref, xp_ref, xn_ref, g_ref, sh_ref, sc_ref, gate_ref, wa_ref, wv_ref,
     cwa_ref, cba_ref, cwv_ref, cbv_ref, wd_ref) = refs[:14]
    h_ref, hh_ref, ua_ref, uv_ref, act_ref = refs[-5:]
    final = len(refs) == 14 + 3 + 5
    i = pl.program_id(0)
    j = pl.program_id(1)

    @pl.when(j == 0)
    def _():
        g, sc, sh = g_ref[...], sc_ref[...], sh_ref[...]
        h_ref[...] = _norm_mod(x_ref[...], g, sc, sh)
        hh_ref[...] = _norm_mod(jnp.concatenate([xp_ref[...], xn_ref[...]], axis=0), g, sc, sh)

    h = h_ref[...]
    hh = hh_ref[...]
    _project_with_halo(h, hh, wa_ref[...], ua_ref, tm)
    _project_with_halo(h, hh, wv_ref[...], uv_ref, tm)

    col0 = pl.multiple_of(j * th, th)

    def emit(r0, outs):
        act_ref[pl.ds(r0, rows), pl.ds(col0, th)] = (_silu(outs[0]) * outs[1]).astype(BF16)

    _conv_chunks([ua_ref, uv_ref], [cwa_ref[...], cwv_ref[...]], [cba_ref[...], cbv_ref[...]],
                 i * tm, _block_seqlen(i, tm) - 1, tm, rows, emit)

    @pl.when(j == pl.num_programs(1) - 1)
    def _():
        xo = x_ref[...] + gate_ref[...] * _dot(act_ref[...], wd_ref[...])
        if not final:
            refs[14][...] = xo
        else:
            fg_ref, yc_ref, yl_ref = refs[14:17]
            y = xo * lax.rsqrt(jnp.mean(xo * xo, axis=-1, keepdims=True) + EPS) * fg_ref[...]

            @pl.when(i < n_ctx_blocks)
            def _():
                yc_ref[...] = y

            @pl.when(i >= n_ctx_blocks)
            def _():
                yl_ref[...] = y


def _ffn(x, mod3, layer, g, up_w, conv_w, conv_b, down_w, final_g=None):
    tm, th, rows = 1024, 256, 64
    nb = FFN_HIDDEN // th
    nctx = N_CTX // tm
    cb = conv_b.reshape(1, 2 * FFN_HIDDEN)
    in_specs = _halo_x_specs(tm) + [
        pl.BlockSpec((1, D_MODEL), lambda i, j: (0, 0)),
        _mod_spec(layer, 3, tm, 2),
        _mod_spec(layer, 4, tm, 2),
        _mod_spec(layer, 5, tm, 2),
        pl.BlockSpec((D_MODEL, th), lambda i, j: (0, j)),
        pl.BlockSpec((D_MODEL, th), lambda i, j: (0, j + nb)),
        pl.BlockSpec((3, th), lambda i, j: (0, j)),
        pl.BlockSpec((1, th), lambda i, j: (0, j)),
        pl.BlockSpec((3, th), lambda i, j: (0, j + nb)),
        pl.BlockSpec((1, th), lambda i, j: (0, j + nb)),
        pl.BlockSpec((FFN_HIDDEN, D_MODEL), lambda i, j: (0, 0)),
    ]
    args = [x, x, x, g.reshape(1, D_MODEL), mod3, mod3, mod3, up_w, up_w, conv_w, cb, conv_w, cb, down_w]
    if final_g is None:
        out_specs = pl.BlockSpec((tm, D_MODEL), lambda i, j: (i, 0))
        out_shape = jax.ShapeDtypeStruct((N_TOK, D_MODEL), F32)
        params = _params("parallel", "arbitrary")
    else:
        in_specs.append(pl.BlockSpec((1, D_MODEL), lambda i, j: (0, 0)))
        args.append(final_g.reshape(1, D_MODEL))
        out_specs = [pl.BlockSpec((tm, D_MODEL), lambda i, j: (jnp.minimum(i, nctx - 1), 0)),
                     pl.BlockSpec((tm, D_MODEL), lambda i, j: (jnp.maximum(i - nctx, 0), 0))]
        out_shape = [jax.ShapeDtypeStruct((N_CTX, D_MODEL), F32), jax.ShapeDtypeStruct((N_LAT, D_MODEL), F32)]
        params = pltpu.CompilerParams(dimension_semantics=("arbitrary", "arbitrary"),
                                      vmem_limit_bytes=BIG_VMEM_LIMIT)
    return pl.pallas_call(
        functools.partial(_ffn_kernel, tm=tm, th=th, rows=rows, n_ctx_blocks=nctx),
        grid=(N_TOK // tm, nb),
        in_specs=in_specs,
        out_specs=out_specs,
        out_shape=out_shape,
        scratch_shapes=[pltpu.VMEM((tm, D_MODEL), BF16), pltpu.VMEM((2 * SUBLANES, D_MODEL), BF16),
                        pltpu.VMEM((tm + 2 * SUBLANES, th), F32), pltpu.VMEM((tm + 2 * SUBLANES, th), F32),
                        pltpu.VMEM((tm, FFN_HIDDEN), BF16)],
        compiler_params=params,
        name="ffn",
    )(*args)


def _mm_res_kernel(ac_ref, al_ref, w_ref, x_ref, gate_ref, o_ref, *, n_ctx_blocks):
    a = jnp.where(pl.program_id(0) < n_ctx_blocks, ac_ref[...], al_ref[...])
    o_ref[...] = x_ref[...] + gate_ref[...] * _dot(a, w_ref[...])


def _two_source_specs(block, tm):
    nctx = N_CTX // tm
    return [pl.BlockSpec(block, lambda i: (jnp.minimum(i, nctx - 1), 0)),
            pl.BlockSpec(block, lambda i: (jnp.maximum(i - nctx, 0), 0))]


def _matmul_residual(a_ctx, a_lat, w, x, mod3, layer, kgate, *, name):
    tm = 512
    k = w.shape[0]
    return pl.pallas_call(
        functools.partial(_mm_res_kernel, n_ctx_blocks=N_CTX // tm),
        grid=(N_TOK // tm,),
        in_specs=_two_source_specs((tm, k), tm) + [
            pl.BlockSpec((k, D_MODEL), lambda i: (0, 0)),
            pl.BlockSpec((tm, D_MODEL), lambda i: (i, 0)),
            _mod_spec(layer, kgate, tm, 1),
        ],
        out_specs=pl.BlockSpec((tm, D_MODEL), lambda i: (i, 0)),
        out_shape=jax.ShapeDtypeStruct((N_TOK, D_MODEL), F32),
        compiler_params=_params("parallel"),
        name=name,
    )(a_ctx, a_lat, w, x, mod3)


def _scan_kernel(*refs, nc, zero_init, n_unused, fill_slot):
    x_ref, b_ref, c_ref, dt_ref, alog_ref, e_ref, dx_ref = refs[:7]
    h0_ref = None if zero_init else refs[7]
    y_ref, hl_ref, st_ref, yt_ref = refs[7 + (not zero_init) + n_unused:]
    d = pl.program_id(1)
    c = pl.program_id(2)
    q = SSD_CHUNK

    @pl.when(c == 0)
    def _():
        if zero_init:
            st_ref[...] = jnp.zeros_like(st_ref)
        else:
            st_ref[...] = h0_ref[...].T

    half = SSD_HEAD_DIM
    sgn = 1 - 2 * d
    ii = lax.broadcasted_iota(jnp.int32, (q, q), 0)
    jj = lax.broadcasted_iota(jnp.int32, (q, q), 1)
    tri_b = jnp.where(((ii - jj) * sgn) >= 0, 1.0, 0.0).astype(BF16)
    jl = jj & (half - 1)
    tri_mix = [((ii - (half * jh + jl)) * sgn) >= 0 for jh in range(2)]
    lane = lax.broadcasted_iota(jnp.int32, (half, LANES), 1)
    low = lane < half

    def split_cat(v):
        return jnp.concatenate(_split3(v), axis=1)

    dtb = dt_ref[...]
    da = dtb * (-jnp.exp(alog_ref[...]) * LOG2E)
    cs3 = _dot(tri_b, split_cat(da))
    cs = cs3[:, 0:LANES] + cs3[:, LANES:2 * LANES] + cs3[:, 2 * LANES:]
    e3 = e_ref[...]
    dt_x = _dot(split_cat(dtb)[:, 0:2 * LANES], e3[0:2 * LANES, :])
    cs_x = _dot(split_cat(cs), e3)
    tot_x = jnp.where(d == 0, cs_x[q - 1:q, :], cs_x[0:1, :])
    cs_rows = [jnp.concatenate([cs[half * jh:half * (jh + 1), :]] * 2, axis=0).T for jh in range(2)]

    xs = x_ref[...]
    xd = xs * dt_x
    xw = (xd * jnp.exp2(tot_x - cs_x)).astype(BF16)
    dec_out = jnp.exp2(cs_x)
    dec_chunk = jnp.exp2(tot_x)

    for g in range(SSD_GROUPS):
        gs = slice(g * GROUP_W, (g + 1) * GROUP_W)
        bg = b_ref[:, g * SSD_STATE:(g + 1) * SSD_STATE]
        cg = c_ref[:, g * SSD_STATE:(g + 1) * SSD_STATE].astype(BF16)
        bgb = bg.astype(BF16)
        cb_mix = [_dot_nt(cg, jnp.concatenate([bgb[half * jh:half * (jh + 1), :]] * 2, axis=0)) for jh in range(2)]
        st_g = st_ref[:, gs]
        y_off = _dot(cg, st_g.astype(BF16))
        new_g = _dot(bg.T.astype(BF16), xw[:, gs])
        st_ref[:, gs] = st_g * dec_chunk[:, gs] + new_g
        for p in range(HEADS_PER_GROUP // 2):
            h = g * HEADS_PER_GROUP + 2 * p
            cols = slice(h * SSD_HEAD_DIM, (h + 2) * SSD_HEAD_DIM)
            cs_col = cs_x[:, cols]
            xd2 = xd[:, cols]
            ms, rs = [], []
            for jh in range(2):
                row = jnp.where(low[0:1, :], cs_rows[jh][h:h + 1, :], cs_rows[jh][h + 1:h + 2, :])
                decay = jnp.where(tri_mix[jh], jnp.exp2(cs_col - row), 0.0)
                ms.append((cb_mix[jh] * decay).astype(BF16))
                xj = xd2[half * jh:half * (jh + 1), :]
                rs += [jnp.where(low, xj, 0.0), jnp.where(low, 0.0, xj)]
            lhs = jnp.concatenate(ms, axis=1)
            rhs = jnp.concatenate(rs, axis=0).astype(BF16)
            loc = slice(2 * p * SSD_HEAD_DIM, (2 * p + 2) * SSD_HEAD_DIM)
            yt_ref[:, cols] = _dot(lhs, rhs) + y_off[:, loc] * dec_out[:, cols]

    r0 = pl.multiple_of((c + d * (nc - 1 - 2 * c)) * q, q)

    @pl.when(d == 0)
    def _():
        y_ref[pl.ds(r0, q), :] = yt_ref[...] + dx_ref[...] * xs

    @pl.when(d == 1)
    def _():
        y_ref[pl.ds(r0, q), :] += yt_ref[...]

    @pl.when(c == nc - 1)
    def _():
        if fill_slot is None:
            hl_ref[...] = st_ref[...].T
        else:
            for s in range(hl_ref.shape[0]):
                hl_ref[s] = st_ref[...].T if s == fill_slot else jnp.zeros(hl_ref.shape[1:], F32)


def _ssd_scan(xbc, dt, a_log_pad, expand, d_x, h0, slot, *, nseq, nc, chunk0, name, n_slots=1, states=None):
    q = SSD_CHUNK
    zero_init = h0 is None
    out_slot = slot if n_slots > 1 else 0

    def chunk(b, d, c):
        return chunk0 + b * nc + c + d * (nc - 1 - 2 * c)

    in_specs = [
        pl.BlockSpec((q, SSD_INNER), lambda b, d, c: (chunk(b, d, c), 0)),
        pl.BlockSpec((q, SSD_BC), lambda b, d, c: (chunk(b, d, c), SSD_INNER // SSD_BC)),
        pl.BlockSpec((q, SSD_BC), lambda b, d, c: (chunk(b, d, c), SSD_INNER // SSD_BC + 1)),
        pl.BlockSpec((q, LANES), lambda b, d, c: (chunk(b, d, c), d)),
        pl.BlockSpec((None, 1, LANES), lambda b, d, c: (d, 0, 0)),
        pl.BlockSpec((3 * LANES, SSD_INNER), lambda b, d, c: (0, 0)),
        pl.BlockSpec((1, SSD_INNER), lambda b, d, c: (0, 0)),
    ]
    args = [xbc, xbc, xbc, dt, a_log_pad, expand, d_x]
    if not zero_init:
        in_specs.append(pl.BlockSpec((None, None, None, SSD_INNER, SSD_STATE),
                                     lambda b, d, c: (b, slot, d, 0, 0)))
        args.append(h0)
    aliases = {}
    if states is not None:
        in_specs.append(pl.BlockSpec(memory_space=pl.ANY))
        args.append(states)
        aliases = {len(args) - 1: 1}
    fill_all = states is None and n_slots > 1
    if fill_all:
        state_spec = pl.BlockSpec((None, n_slots, None, SSD_INNER, SSD_STATE), lambda b, d, c: (b, 0, d, 0, 0))
    else:
        state_spec = pl.BlockSpec((None, None, None, SSD_INNER, SSD_STATE),
                                  lambda b, d, c: (b, out_slot, d, 0, 0))
    return pl.pallas_call(
        functools.partial(_scan_kernel, nc=nc, zero_init=zero_init, n_unused=len(aliases),
                          fill_slot=out_slot if fill_all else None),
        grid=(nseq, 2, nc),
        in_specs=in_specs,
        out_specs=[
            pl.BlockSpec((nc * q, SSD_INNER), lambda b, d, c: (b, 0)),
            state_spec,
        ],
        out_shape=[
            jax.ShapeDtypeStruct((nseq * nc * q, SSD_INNER), F32),
            jax.ShapeDtypeStruct((nseq, n_slots, 2, SSD_INNER, SSD_STATE), F32),
        ],
        scratch_shapes=[pltpu.VMEM((SSD_STATE, SSD_INNER), F32), pltpu.VMEM((q, SSD_INNER), F32)],
        input_output_aliases=aliases,
        compiler_params=pltpu.CompilerParams(dimension_semantics=("parallel", "arbitrary", "arbitrary"),
                                             vmem_limit_bytes=BIG_VMEM_LIMIT),
        name=name,
    )(*args)


def _ssd_out_kernel(yc_ref, yl_ref, z_ref, ng_ref, w_ref, x_ref, gate_ref, o_ref, *, n_ctx_blocks):
    y = jnp.where(pl.program_id(0) < n_ctx_blocks, yc_ref[...], yl_ref[...])
    y = y * _silu(z_ref[...].astype(F32))
    gw = SSD_INNER // SSD_GROUPS
    parts = []
    for g in range(SSD_GROUPS):
        yg = y[:, g * gw:(g + 1) * gw]
        parts.append(yg * lax.rsqrt(jnp.mean(yg * yg, axis=-1, keepdims=True) + EPS))
    yn = (jnp.concatenate(parts, axis=1) * ng_ref[...]).astype(BF16)
    o_ref[...] = x_ref[...] + gate_ref[...] * _dot(yn, w_ref[...])


def _ssd_out(y_ctx, y_lat, z, norm_g, out_w, x, mod3, layer):
    tm = 512
    return pl.pallas_call(
        functools.partial(_ssd_out_kernel, n_ctx_blocks=N_CTX // tm),
        grid=(N_TOK // tm,),
        in_specs=_two_source_specs((tm, SSD_INNER), tm) + [
            pl.BlockSpec((tm, SSD_INNER), lambda i: (i, 0)),
            pl.BlockSpec((1, SSD_INNER), lambda i: (0, 0)),
            pl.BlockSpec((SSD_INNER, D_MODEL), lambda i: (0, 0)),
            pl.BlockSpec((tm, D_MODEL), lambda i: (i, 0)),
            _mod_spec(layer, 2, tm, 1),
        ],
        out_specs=pl.BlockSpec((tm, D_MODEL), lambda i: (i, 0)),
        out_shape=jax.ShapeDtypeStruct((N_TOK, D_MODEL), F32),
        compiler_params=_params("parallel"),
        name="ssd_out",
    )(y_ctx, y_lat, z, norm_g.reshape(1, SSD_INNER), out_w, x, mod3)


def _diff_lambda(lam_ref, lam_init):
    lv = lam_ref[...]
    return (jnp.exp(jnp.sum(lv[0:1] * lv[1:2], axis=-1, keepdims=True))
            - jnp.exp(jnp.sum(lv[2:3] * lv[3:4], axis=-1, keepdims=True)) + lam_init)


def _attend_head(q, kb, vb, lam, g, lam_init):
    lane = lax.broadcasted_iota(jnp.int32, q.shape, 1)
    zero = jnp.zeros_like(q)
    probs = []
    for qm in (jnp.where(lane < DA_HEAD_DIM, q, zero), jnp.where(lane >= DA_HEAD_DIM, q, zero)):
        s = _dot_nt(qm, kb)
        p = jnp.exp2(s - jnp.max(s, axis=-1, keepdims=True))
        probs.append((p, jnp.sum(p, axis=-1, keepdims=True)))
    (p1, l1), (p2, l2) = probs
    w = (p1 - p2 * (lam * l1 / l2)).astype(BF16)
    o = _dot(w, vb) * (1.0 / l1)
    o = o * lax.rsqrt(jnp.mean(o * o, axis=-1, keepdims=True) + EPS)
    return o * g * (1.0 - lam_init)


def _attn_ctx_kernel(q_ref, k_ref, v_ref, lam_ref, g_ref, o_ref, *, lam_init):
    lam = _diff_lambda(lam_ref, lam_init)
    g = g_ref[...]
    for h in range(DA_HEADS):
        cols = slice(h * HEAD_W, (h + 1) * HEAD_W)
        o = _attend_head(q_ref[:, cols], k_ref[:, cols], v_ref[:, cols], lam, g, lam_init)
        o_ref[:, cols] = o.astype(o_ref.dtype)


def _attn_lat_kernel(q_ref, k_ref, v_ref, kc_ref, vc_ref, lam_ref, g_ref, o_ref, kb_ref, vb_ref, *, lam_init):
    @pl.when(pl.program_id(2) == 0)
    def _():
        kb_ref[0:DEC_SEQ, :] = k_ref[...]
        kb_ref[DEC_SEQ:, :] = kc_ref[...].astype(BF16)
        vb_ref[0:DEC_SEQ, :] = v_ref[...]
        vb_ref[DEC_SEQ:, :] = vc_ref[...].astype(BF16)

    o = _attend_head(q_ref[...], kb_ref[...], vb_ref[...], _diff_lambda(lam_ref, lam_init), g_ref[...], lam_init)
    o_ref[...] = o.astype(o_ref.dtype)


def _attention(q, k, v, cache_k4, cache_v4, slot, lam_vecs, subln_g, lam_init):
    common = [pl.BlockSpec((4, DA_HEAD_DIM), lambda *_: (0, 0)),
              pl.BlockSpec((1, HEAD_W), lambda *_: (0, 0))]
    g2 = subln_g.reshape(1, HEAD_W)

    seq_block = pl.BlockSpec((SEQ, D_MODEL), lambda b: (b, 0))
    ctx = pl.pallas_call(
        functools.partial(_attn_ctx_kernel, lam_init=lam_init),
        grid=(BATCH,),
        in_specs=[seq_block, seq_block, seq_block] + common,
        out_specs=seq_block,
        out_shape=jax.ShapeDtypeStruct((N_CTX, D_MODEL), BF16),
        compiler_params=_params("parallel"),
        name="attn_ctx",
    )(q, k, v, lam_vecs, g2)

    tq = 256
    nq = DEC_SEQ // tq
    q0 = N_CTX // tq
    s0 = N_CTX // DEC_SEQ
    keys = DEC_SEQ + PAST_LEN
    lat = pl.pallas_call(
        functools.partial(_attn_lat_kernel, lam_init=lam_init),
        grid=(DEC_BATCH, DA_HEADS, nq),
        in_specs=[
            pl.BlockSpec((tq, HEAD_W), lambda b, h, i: (q0 + b * nq + i, h)),
            pl.BlockSpec((DEC_SEQ, HEAD_W), lambda b, h, i: (s0 + b, h)),
            pl.BlockSpec((DEC_SEQ, HEAD_W), lambda b, h, i: (s0 + b, h)),
            pl.BlockSpec((None, None, PAST_LEN, HEAD_W), lambda b, h, i: (b, slot, 0, h)),
            pl.BlockSpec((None, None, PAST_LEN, HEAD_W), lambda b, h, i: (b, slot, 0, h)),
        ] + common,
        out_specs=pl.BlockSpec((tq, HEAD_W), lambda b, h, i: (b * nq + i, h)),
        out_shape=jax.ShapeDtypeStruct((N_LAT, D_MODEL), BF16),
        scratch_shapes=[pltpu.VMEM((keys, HEAD_W), BF16), pltpu.VMEM((keys, HEAD_W), BF16)],
        compiler_params=_params("parallel", "parallel", "arbitrary"),
        name="attn_lat",
    )(q, k, v, cache_k4, cache_v4, lam_vecs, g2)
    return ctx, lat


def _rope_token_tables():
    rows = DEC_SEQ // GRID_W
    row = jnp.repeat(jnp.arange(rows, dtype=F32), GRID_W)
    col = jnp.tile(jnp.arange(GRID_W, dtype=F32), rows)
    inv = 1.0 / (ROPE_BASE ** (jnp.arange(0, ROPE_AXIS_DIM, 2, dtype=F32) / ROPE_AXIS_DIM))
    ang_r = row[:, None] * inv
    ang_c = col[:, None] * inv
    cr, sr, cc, sc = jnp.cos(ang_r), jnp.sin(ang_r), jnp.cos(ang_c), jnp.sin(ang_c)
    cos64 = jnp.concatenate([cr, cr, cc, cc], axis=1)
    sin64 = jnp.concatenate([-sr, sr, -sc, sc], axis=1)
    cos_lat = jnp.tile(cos64, (DEC_BATCH, 2))
    sin_lat = jnp.tile(sin64, (DEC_BATCH, 2))
    cos = jnp.concatenate([jnp.ones((N_CTX, HEAD_W), F32), cos_lat], axis=0)
    sin = jnp.concatenate([jnp.zeros((N_CTX, HEAD_W), F32), sin_lat], axis=0)
    return cos, sin


def _lambda_init(layer):
    return 0.8 - 0.6 * math.exp(-0.3 * layer)


def kernel(x_prompt, x_sample, state_ssd, cache_k, cache_v, c, c_ctx, mod_w, mod_b, norm_mix_g, norm_ffn_g, ssd_in_w, ssd_conv_w, ssd_conv_b, ssd_dt_bias, ssd_a_log, ssd_d, ssd_norm_g, ssd_out_w, att_qkv_w, att_lambda, att_subln_g, att_out_w, ffn_up_w, ffn_conv_w, ffn_conv_b, ffn_down_w, final_norm_g):
    n_ssd = ssd_in_w.shape[0]
    n_att = att_qkv_w.shape[0]
    x = None

    cpad = jnp.concatenate([c_ctx[None], c, jnp.zeros((MOD_ROWS - 1 - DEC_BATCH, D_MODEL), F32)], axis=0)
    mod = _modulation(cpad, mod_w, mod_b)
    mod3 = mod.reshape(DEPTH * MOD_ROWS * 6, 1, D_MODEL)

    cos_t, sin_t = _rope_token_tables()
    expand = (jnp.arange(LANES)[:, None] == (jnp.arange(SSD_INNER)[None, :] // SSD_HEAD_DIM)).astype(BF16)
    expand = jnp.concatenate([expand] * 3, axis=0)
    h0_lat = state_ssd.reshape(DEC_BATCH, n_ssd, 2, SSD_INNER, SSD_STATE)
    cache_k4 = cache_k.reshape(DEC_BATCH, n_att, PAST_LEN, D_MODEL)
    cache_v4 = cache_v.reshape(DEC_BATCH, n_att, PAST_LEN, D_MODEL)

    new_ssd = None
    new_kv = [jnp.zeros((BATCH, n_att, SEQ, D_MODEL), F32)] * 2
    for i in range(DEPTH):
        slot = i // 2
        if i % 2 == 0:
            in_w = ssd_in_w[slot]
            w_z = in_w[:, :SSD_INNER].astype(BF16)
            w_xbc = in_w[:, SSD_INNER:SSD_MAIN].astype(BF16)
            pad = jnp.zeros((D_MODEL, LANES - SSD_HEADS), F32)
            w_dt = jnp.concatenate([in_w[:, SSD_MAIN:SSD_MAIN + SSD_HEADS], pad,
                                    in_w[:, SSD_MAIN + SSD_HEADS:], pad], axis=1).astype(BF16)
            zpad = jnp.zeros((LANES - SSD_HEADS,), F32)
            dt_bias = jnp.concatenate([ssd_dt_bias[slot, 0], zpad, ssd_dt_bias[slot, 1], zpad]).reshape(1, 2 * LANES)
            a_log_pad = jnp.concatenate([ssd_a_log[slot], jnp.zeros((2, LANES - SSD_HEADS), F32)],
                                        axis=1).reshape(2, 1, LANES)
            d_x = jnp.repeat(ssd_d[slot], SSD_HEAD_DIM).reshape(1, SSD_INNER)
            if x is None:
                z, x = _norm_mod_matmul_join(x_prompt.reshape(N_CTX, D_MODEL), x_sample.reshape(N_LAT, D_MODEL),
                                             mod3, i, 0, 1, norm_mix_g[i], w_z, tn=1024, out_dtype=BF16,
                                             name="ssd_z_join")
            else:
                z = _norm_mod_matmul(x, mod3, i, 0, 1, norm_mix_g[i], w_z, tn=1024, out_dtype=BF16, name="ssd_z")
            xbc, dt = _ssd_xbc(x, mod3, i, norm_mix_g[i], w_xbc, ssd_conv_w[slot], ssd_conv_b[slot], w_dt, dt_bias)
            y_ctx, new_ssd = _ssd_scan(xbc, dt, a_log_pad, expand, d_x, None, slot,
                                       nseq=BATCH, nc=SEQ // SSD_CHUNK, chunk0=0, name="ssd_scan_ctx",
                                       n_slots=n_ssd, states=new_ssd)
            y_lat, _ = _ssd_scan(xbc, dt, a_log_pad, expand, d_x, h0_lat, slot,
                                 nseq=DEC_BATCH, nc=DEC_SEQ // SSD_CHUNK, chunk0=N_CTX // SSD_CHUNK,
                                 name="ssd_scan_lat")
            x = _ssd_out(y_ctx, y_lat, z, ssd_norm_g[slot], ssd_out_w[slot].astype(BF16), x, mod3, i)
        else:
            lam_init = _lambda_init(i)
            q, k, v, *new_kv = _qkv_proj(x, mod3, i, norm_mix_g[i], att_qkv_w[slot].astype(BF16),
                                         cos_t, sin_t, slot, n_att, new_kv)
            o_ctx, o_lat = _attention(q, k, v, cache_k4, cache_v4, slot, att_lambda[slot], att_subln_g[slot],
                                      lam_init)
            x = _matmul_residual(o_ctx, o_lat, att_out_w[slot].astype(BF16), x, mod3, i, 2, name="att_out")
        x = _ffn(x, mod3, i, norm_ffn_g[i], ffn_up_w[i].astype(BF16), ffn_conv_w[i], ffn_conv_b[i],
                 ffn_down_w[i].astype(BF16), final_g=final_norm_g if i == DEPTH - 1 else None)

    y_prompt = x[0].reshape(BATCH, SEQ, D_MODEL)
    y_sample = x[1].reshape(DEC_BATCH, DEC_SEQ, D_MODEL)
    return (y_prompt, y_sample,
            new_ssd.reshape(BATCH, n_ssd, 2, SSD_HEADS, SSD_HEAD_DIM, SSD_STATE),
            new_kv[0].reshape(BATCH, n_att, SEQ, DA_HEADS, 2, DA_HEAD_DIM),
            new_kv[1].reshape(BATCH, n_att, SEQ, DA_HEADS, HEAD_W))
```

```python
import functools
import math

import jax
import jax.numpy as jnp
from jax import lax
from jax.experimental import pallas as pl
from jax.experimental.pallas import tpu as pltpu

F32 = jnp.float32
BF16 = jnp.bfloat16

D_MODEL = 1024
BATCH = 16
SEQ = 256
DEPTH = 4
DEC_BATCH = 4
DEC_SEQ = 2048
PAST_LEN = 256
GRID_W = 64
EPS = 1e-6

SSD_INNER = 2048
SSD_HEAD_DIM = 64
SSD_HEADS = 32
SSD_GROUPS = 4
SSD_STATE = 128
SSD_CHUNK = 128
SSD_BC = SSD_GROUPS * SSD_STATE
SSD_CONV_CH = SSD_INNER + 2 * SSD_BC
SSD_MAIN = SSD_INNER + SSD_CONV_CH
HEADS_PER_GROUP = SSD_HEADS // SSD_GROUPS
GROUP_W = HEADS_PER_GROUP * SSD_HEAD_DIM

DA_HEAD_DIM = 64
DA_HEADS = 8
DA_SCALE = DA_HEAD_DIM ** -0.5
LOG2E = math.log2(math.e)
Q_PRESCALE = DA_SCALE * LOG2E
HEAD_W = 2 * DA_HEAD_DIM
ROPE_BASE = 10000.0
ROPE_AXIS_DIM = DA_HEAD_DIM // 2

FFN_HIDDEN = 2816

N_CTX = BATCH * SEQ
N_LAT = DEC_BATCH * DEC_SEQ
N_TOK = N_CTX + N_LAT
MOD_ROWS = 8
LANES = 128
SUBLANES = 8
VMEM_LIMIT = 48 * 1024 * 1024
BIG_VMEM_LIMIT = 56 * 1024 * 1024


def _params(*sem):
    return pltpu.CompilerParams(dimension_semantics=sem, vmem_limit_bytes=VMEM_LIMIT)


def _mod_row(i, tm):
    nctx = N_CTX // tm
    return jnp.where(i < nctx, 0, 1 + (i - nctx) // (DEC_SEQ // tm))


def _silu(x):
    return x * jax.nn.sigmoid(x)


def _softplus(x):
    return jnp.maximum(x, 0.0) + jnp.log1p(jnp.exp(-jnp.abs(x)))


def _split3(x):
    hi = x.astype(BF16)
    r1 = x - hi.astype(F32)
    mid = r1.astype(BF16)
    lo = (r1 - mid.astype(F32)).astype(BF16)
    return hi, mid, lo


def _dot(a, b):
    return jnp.dot(a, b, preferred_element_type=F32)


def _dot_nt(a, b):
    return lax.dot_general(a, b, (((1,), (1,)), ((), ())), preferred_element_type=F32)


def _norm_mod(x, g, sc, sh):
    r = x * lax.rsqrt(jnp.mean(x * x, axis=-1, keepdims=True) + EPS)
    return ((r * g) * (1.0 + sc) + sh).astype(BF16)


def _mod_kernel(c_ref, w_ref, b_ref, o_ref):
    s = _silu(c_ref[...]).astype(BF16)
    o_ref[...] = _dot(s, w_ref[...].astype(BF16)) + b_ref[...]


def _modulation(cpad, mod_w, mod_b):
    tn = 1024
    n = 6 * D_MODEL
    return pl.pallas_call(
        _mod_kernel,
        grid=(DEPTH, n // tn),
        in_specs=[
            pl.BlockSpec((MOD_ROWS, D_MODEL), lambda l, j: (0, 0)),
            pl.BlockSpec((None, D_MODEL, tn), lambda l, j: (l, 0, j)),
            pl.BlockSpec((None, 1, tn), lambda l, j: (l, 0, j)),
        ],
        out_specs=pl.BlockSpec((None, MOD_ROWS, tn), lambda l, j: (l, 0, j)),
        out_shape=jax.ShapeDtypeStruct((DEPTH, MOD_ROWS, n), F32),
        compiler_params=_params("parallel", "parallel"),
        name="modulation",
    )(cpad, mod_w, mod_b.reshape(DEPTH, 1, n))


def _mod_spec(layer, k, tm, ngrid):
    base = layer * MOD_ROWS * 6 + k
    if ngrid == 1:
        return pl.BlockSpec((None, 1, D_MODEL), lambda i: (base + 6 * _mod_row(i, tm), 0, 0))
    return pl.BlockSpec((None, 1, D_MODEL), lambda i, j: (base + 6 * _mod_row(i, tm), 0, 0))


def _nmm_kernel(x_ref, g_ref, sh_ref, sc_ref, w_ref, o_ref, h_ref):
    @pl.when(pl.program_id(1) == 0)
    def _():
        h_ref[...] = _norm_mod(x_ref[...], g_ref[...], sc_ref[...], sh_ref[...])

    o_ref[...] = _dot(h_ref[...], w_ref[...]).astype(o_ref.dtype)


def _nmm_in_specs(layer, kshift, kscale, tm, tn):
    return [
        pl.BlockSpec((tm, D_MODEL), lambda i, j: (i, 0)),
        pl.BlockSpec((1, D_MODEL), lambda i, j: (0, 0)),
        _mod_spec(layer, kshift, tm, 2),
        _mod_spec(layer, kscale, tm, 2),
        pl.BlockSpec((D_MODEL, tn), lambda i, j: (0, j)),
    ]


def _norm_mod_matmul(x, mod3, layer, kshift, kscale, g, w, *, tn, out_dtype, name):
    tm = 1024
    n = w.shape[1]
    return pl.pallas_call(
        _nmm_kernel,
        grid=(N_TOK // tm, n // tn),
        in_specs=_nmm_in_specs(layer, kshift, kscale, tm, tn),
        out_specs=pl.BlockSpec((tm, tn), lambda i, j: (i, j)),
        out_shape=jax.ShapeDtypeStruct((N_TOK, n), out_dtype),
        scratch_shapes=[pltpu.VMEM((tm, D_MODEL), BF16)],
        compiler_params=_params("parallel", "arbitrary"),
        name=name,
    )(x, g.reshape(1, D_MODEL), mod3, mod3, w)


def _nmm_join_kernel(xc_ref, xl_ref, g_ref, sh_ref, sc_ref, w_ref, o_ref, xo_ref, h_ref, *, n_ctx_blocks):
    @pl.when(pl.program_id(1) == 0)
    def _():
        x = jnp.where(pl.program_id(0) < n_ctx_blocks, xc_ref[...], xl_ref[...])
        xo_ref[...] = x
        h_ref[...] = _norm_mod(x, g_ref[...], sc_ref[...], sh_ref[...])

    o_ref[...] = _dot(h_ref[...], w_ref[...]).astype(o_ref.dtype)


def _norm_mod_matmul_join(x_ctx, x_lat, mod3, layer, kshift, kscale, g, w, *, tn, out_dtype, name):
    tm = 1024
    n = w.shape[1]
    nctx = N_CTX // tm
    specs = _nmm_in_specs(layer, kshift, kscale, tm, tn)
    return pl.pallas_call(
        functools.partial(_nmm_join_kernel, n_ctx_blocks=nctx),
        grid=(N_TOK // tm, n // tn),
        in_specs=[pl.BlockSpec((tm, D_MODEL), lambda i, j: (jnp.minimum(i, nctx - 1), 0)),
                  pl.BlockSpec((tm, D_MODEL), lambda i, j: (jnp.maximum(i - nctx, 0), 0))] + specs[1:],
        out_specs=[pl.BlockSpec((tm, tn), lambda i, j: (i, j)),
                   pl.BlockSpec((tm, D_MODEL), lambda i, j: (i, 0))],
        out_shape=[jax.ShapeDtypeStruct((N_TOK, n), out_dtype),
                   jax.ShapeDtypeStruct((N_TOK, D_MODEL), F32)],
        scratch_shapes=[pltpu.VMEM((tm, D_MODEL), BF16)],
        compiler_params=_params("parallel", "arbitrary"),
        name=name,
    )(x_ctx, x_lat, g.reshape(1, D_MODEL), mod3, mod3, w)


def _qkv_kernel(*refs, n_ctx_blocks, n_unused):
    x_ref, g_ref, sh_ref, sc_ref, w_ref, cos_ref, sin_ref = refs[:7]
    q_ref, k_ref, v_ref, kctx_ref, vctx_ref, h_ref = refs[7 + n_unused:]
    j = pl.program_id(1)
    is_ctx = pl.program_id(0) < n_ctx_blocks

    @pl.when(j == 0)
    def _():
        h_ref[...] = _norm_mod(x_ref[...], g_ref[...], sc_ref[...], sh_ref[...])

    u = _dot(h_ref[...], w_ref[...])

    def rope_into(o_refs, scale):
        cos = cos_ref[...] * scale
        sin = sin_ref[...] * scale
        lane = lax.broadcasted_iota(jnp.int32, cos.shape, 1)
        half = ROPE_AXIS_DIM // 2
        first = (lane % ROPE_AXIS_DIM) < half
        for k in range(u.shape[1] // HEAD_W):
            cols = slice(k * HEAD_W, (k + 1) * HEAD_W)
            uk = u[:, cols]
            partner = jnp.where(first, pltpu.roll(uk, HEAD_W - half, 1), pltpu.roll(uk, half, 1))
            r = uk * cos + partner * sin
            for o_ref in o_refs:
                if len(o_ref.shape) == 3:
                    o_ref[:, :, cols] = r.reshape(o_ref.shape[0], SEQ, HEAD_W)
                else:
                    o_ref[:, cols] = r.astype(o_ref.dtype)

    @pl.when(j == 0)
    def _():
        rope_into([q_ref], Q_PRESCALE)

    @pl.when(jnp.logical_and(j == 1, is_ctx))
    def _():
        rope_into([k_ref, kctx_ref], 1.0)

    @pl.when(jnp.logical_and(j == 1, jnp.logical_not(is_ctx)))
    def _():
        rope_into([k_ref], 1.0)

    @pl.when(j == 2)
    def _():
        v_ref[...] = u.astype(v_ref.dtype)

    @pl.when(jnp.logical_and(j == 2, is_ctx))
    def _():
        vctx_ref[...] = u.reshape(vctx_ref.shape)


def _qkv_proj(x, mod3, layer, g, w, cos_t, sin_t, slot, n_slots, caches):
    tm = 1024
    nctx = N_CTX // tm
    nseq = tm // SEQ
    out = pl.BlockSpec((tm, D_MODEL), lambda i, j: (i, 0))
    ctx_out = pl.BlockSpec((nseq, None, SEQ, D_MODEL), lambda i, j: (jnp.minimum(i, nctx - 1), slot, 0, 0))
    ctx_shape = jax.ShapeDtypeStruct((BATCH, n_slots, SEQ, D_MODEL), F32)
    in_specs = _nmm_in_specs(layer, 0, 1, tm, D_MODEL) + [pl.BlockSpec((tm, HEAD_W), lambda i, j: (i, 0))] * 2
    args = [x, g.reshape(1, D_MODEL), mod3, mod3, w, cos_t, sin_t]
    aliases = {}
    if caches is not None:
        in_specs += [pl.BlockSpec(memory_space=pl.ANY)] * 2
        aliases = {len(args): 3, len(args) + 1: 4}
        args += list(caches)
    return pl.pallas_call(
        functools.partial(_qkv_kernel, n_ctx_blocks=nctx, n_unused=len(aliases)),
        grid=(N_TOK // tm, 3),
        in_specs=in_specs,
        out_specs=[out, out, out, ctx_out, ctx_out],
        out_shape=[jax.ShapeDtypeStruct((N_TOK, D_MODEL), BF16),
                   jax.ShapeDtypeStruct((N_TOK, D_MODEL), BF16),
                   jax.ShapeDtypeStruct((N_TOK, D_MODEL), BF16),
                   ctx_shape, ctx_shape],
        scratch_shapes=[pltpu.VMEM((tm, D_MODEL), BF16)],
        input_output_aliases=aliases,
        compiler_params=pltpu.CompilerParams(dimension_semantics=("arbitrary", "arbitrary"),
                                             vmem_limit_bytes=BIG_VMEM_LIMIT),
        name="att_qkv",
    )(*args)


PAD = SUBLANES


def _block_seqlen(i, tm):
    return jnp.where(i < N_CTX // tm, SEQ, DEC_SEQ)


def _project_with_halo(h, hh, w, u_refs, tm):
    u_ref, up_ref, un_ref = u_refs
    u = _dot(h, w)
    uh = _dot(hh, w)
    u_ref[PAD:PAD + tm, :] = u
    up_ref[PAD + 1:PAD + 1 + tm, :] = u
    up_ref[PAD:PAD + 1, :] = uh[SUBLANES - 1:SUBLANES, :]
    un_ref[PAD - 1:PAD - 1 + tm, :] = u
    un_ref[PAD + tm - 1:PAD + tm, :] = uh[SUBLANES:SUBLANES + 1, :]


def _conv_chunks(u_refs, cws, cbs, row0, seqmask, tm, rows, emit):
    ridx = lax.broadcasted_iota(jnp.int32, (rows, 1), 0)
    for r0 in range(0, tm, rows):
        outs = []
        for (u_ref, up_ref, un_ref), cw, cb in zip(u_refs, cws, cbs):
            prev = up_ref[pl.ds(PAD + r0, rows), :]
            cur = u_ref[pl.ds(PAD + r0, rows), :]
            nxt = un_ref[pl.ds(PAD + r0, rows), :]
            if r0 % SEQ == 0:
                starts = ((row0 + r0) & seqmask) == 0
                prev = jnp.where(jnp.logical_and(ridx == 0, starts), 0.0, prev)
            if (r0 + rows) % SEQ == 0:
                ends = ((row0 + r0 + rows) & seqmask) == 0
                nxt = jnp.where(jnp.logical_and(ridx == rows - 1, ends), 0.0, nxt)
            outs.append(prev * cw[0:1, :] + cur * cw[1:2, :] + nxt * cw[2:3, :] + cb)
        emit(r0, outs)


def _halo_x_specs(tm):
    per = tm // SUBLANES
    last = N_TOK // SUBLANES - 1
    return [
        pl.BlockSpec((tm, D_MODEL), lambda i, j: (i, 0)),
        pl.BlockSpec((SUBLANES, D_MODEL), lambda i, j: (jnp.maximum(i * per - 1, 0), 0)),
        pl.BlockSpec((SUBLANES, D_MODEL), lambda i, j: (jnp.minimum((i + 1) * per, last), 0)),
    ]


def _xbc_kernel(x_ref, xp_ref, xn_ref, g_ref, sh_ref, sc_ref, w_ref, cw_ref, cb_ref, wdt_ref, bdt_ref,
                o_ref, dt_ref, h_ref, hh_ref, u_ref, up_ref, un_ref, *, tm, rows):
    i = pl.program_id(0)

    @pl.when(pl.program_id(1) == 0)
    def _():
        g, sc, sh = g_ref[...], sc_ref[...], sh_ref[...]
        h_ref[...] = _norm_mod(x_ref[...], g, sc, sh)
        hh_ref[...] = _norm_mod(jnp.concatenate([xp_ref[...], xn_ref[...]], axis=0), g, sc, sh)
        dt_ref[...] = _softplus(_dot(h_ref[...], wdt_ref[...]) + bdt_ref[...])

    u3 = (u_ref, up_ref, un_ref)
    _project_with_halo(h_ref[...], hh_ref[...], w_ref[...], u3, tm)

    def emit(r0, outs):
        o_ref[pl.ds(r0, rows), :] = _silu(outs[0])

    _conv_chunks([u3], [cw_ref[...]], [cb_ref[...]], i * tm, _block_seqlen(i, tm) - 1, tm, rows, emit)


def _ssd_xbc(x, mod3, layer, g, w_xbc, conv_w, conv_b, w_dt, dt_bias):
    tm, tc, rows = 1024, 512, 32
    ndt = w_dt.shape[1]
    return pl.pallas_call(
        functools.partial(_xbc_kernel, tm=tm, rows=rows),
        grid=(N_TOK // tm, SSD_CONV_CH // tc),
        in_specs=_halo_x_specs(tm) + [
            pl.BlockSpec((1, D_MODEL), lambda i, j: (0, 0)),
            _mod_spec(layer, 0, tm, 2),
            _mod_spec(layer, 1, tm, 2),
            pl.BlockSpec((D_MODEL, tc), lambda i, j: (0, j)),
            pl.BlockSpec((3, tc), lambda i, j: (0, j)),
            pl.BlockSpec((1, tc), lambda i, j: (0, j)),
            pl.BlockSpec((D_MODEL, ndt), lambda i, j: (0, 0)),
            pl.BlockSpec((1, ndt), lambda i, j: (0, 0)),
        ],
        out_specs=[
            pl.BlockSpec((tm, tc), lambda i, j: (i, j)),
            pl.BlockSpec((tm, ndt), lambda i, j: (i, 0)),
        ],
        out_shape=[
            jax.ShapeDtypeStruct((N_TOK, SSD_CONV_CH), F32),
            jax.ShapeDtypeStruct((N_TOK, ndt), F32),
        ],
        scratch_shapes=[pltpu.VMEM((tm, D_MODEL), BF16), pltpu.VMEM((2 * SUBLANES, D_MODEL), BF16),
                        *[pltpu.VMEM((tm + 2 * SUBLANES, tc), F32)] * 3],
        compiler_params=_params("parallel", "arbitrary"),
        name="ssd_xbc",
    )(x, x, x, g.reshape(1, D_MODEL), mod3, mod3, w_xbc, conv_w, conv_b.reshape(1, SSD_CONV_CH), w_dt, dt_bias)


def _ffn_kernel(*refs, tm, th, rows, n_ctx_blocks):
    (x_ref, xp_ref, xn_ref, g_ref, sh_ref, sc_ref, gate_ref, wa_ref, wv_ref,
     cwa_ref, cba_ref, cwv_ref, cbv_ref, wd_ref) = refs[:14]
    h_ref, hh_ref, act_ref = refs[-9:-6]
    ua3, uv3 = refs[-6:-3], refs[-3:]
    final = len(refs) == 14 + 3 + 9
    i = pl.program_id(0)
    j = pl.program_id(1)

    @pl.when(j == 0)
    def _():
        g, sc, sh = g_ref[...], sc_ref[...], sh_ref[...]
        h_ref[...] = _norm_mod(x_ref[...], g, sc, sh)
        hh_ref[...] = _norm_mod(jnp.concatenate([xp_ref[...], xn_ref[...]], axis=0), g, sc, sh)

    h = h_ref[...]
    hh = hh_ref[...]
    _project_with_halo(h, hh, wa_ref[...], ua3, tm)
    _project_with_halo(h, hh, wv_ref[...], uv3, tm)

    col0 = pl.multiple_of(j * th, th)

    def emit(r0, outs):
        act_ref[pl.ds(r0, rows), pl.ds(col0, th)] = (_silu(outs[0]) * outs[1]).astype(BF16)

    _conv_chunks([ua3, uv3], [cwa_ref[...], cwv_ref[...]], [cba_ref[...], cbv_ref[...]],
                 i * tm, _block_seqlen(i, tm) - 1, tm, rows, emit)

    @pl.when(j == pl.num_programs(1) - 1)
    def _():
        xo = x_ref[...] + gate_ref[...] * _dot(act_ref[...], wd_ref[...])
        if not final:
            refs[14][...] = xo
        else:
            fg_ref, yc_ref, yl_ref = refs[14:17]
            y = xo * lax.rsqrt(jnp.mean(xo * xo, axis=-1, keepdims=True) + EPS) * fg_ref[...]

            @pl.when(i < n_ctx_blocks)
            def _():
                yc_ref[...] = y

            @pl.when(i >= n_ctx_blocks)
            def _():
                yl_ref[...] = y


def _ffn(x, mod3, layer, g, up_w, conv_w, conv_b, down_w, final_g=None):
    tm, th, rows = 1024, 256, 64
    nb = FFN_HIDDEN // th
    nctx = N_CTX // tm
    cb = conv_b.reshape(1, 2 * FFN_HIDDEN)
    in_specs = _halo_x_specs(tm) + [
        pl.BlockSpec((1, D_MODEL), lambda i, j: (0, 0)),
        _mod_spec(layer, 3, tm, 2),
        _mod_spec(layer, 4, tm, 2),
        _mod_spec(layer, 5, tm, 2),
        pl.BlockSpec((D_MODEL, th), lambda i, j: (0, j)),
        pl.BlockSpec((D_MODEL, th), lambda i, j: (0, j + nb)),
        pl.BlockSpec((3, th), lambda i, j: (0, j)),
        pl.BlockSpec((1, th), lambda i, j: (0, j)),
        pl.BlockSpec((3, th), lambda i, j: (0, j + nb)),
        pl.BlockSpec((1, th), lambda i, j: (0, j + nb)),
        pl.BlockSpec((FFN_HIDDEN, D_MODEL), lambda i, j: (0, 0)),
    ]
    args = [x, x, x, g.reshape(1, D_MODEL), mod3, mod3, mod3, up_w, up_w, conv_w, cb, conv_w, cb, down_w]
    if final_g is None:
        out_specs = pl.BlockSpec((tm, D_MODEL), lambda i, j: (i, 0))
        out_shape = jax.ShapeDtypeStruct((N_TOK, D_MODEL), F32)
        params = _params("parallel", "arbitrary")
    else:
        in_specs.append(pl.BlockSpec((1, D_MODEL), lambda i, j: (0, 0)))
        args.append(final_g.reshape(1, D_MODEL))
        out_specs = [pl.BlockSpec((tm, D_MODEL), lambda i, j: (jnp.minimum(i, nctx - 1), 0)),
                     pl.BlockSpec((tm, D_MODEL), lambda i, j: (jnp.maximum(i - nctx, 0), 0))]
        out_shape = [jax.ShapeDtypeStruct((N_CTX, D_MODEL), F32), jax.ShapeDtypeStruct((N_LAT, D_MODEL), F32)]
        params = pltpu.CompilerParams(dimension_semantics=("arbitrary", "arbitrary"),
                                      vmem_limit_bytes=BIG_VMEM_LIMIT)
    return pl.pallas_call(
        functools.partial(_ffn_kernel, tm=tm, th=th, rows=rows, n_ctx_blocks=nctx),
        grid=(N_TOK // tm, nb),
        in_specs=in_specs,
        out_specs=out_specs,
        out_shape=out_shape,
        scratch_shapes=[pltpu.VMEM((tm, D_MODEL), BF16), pltpu.VMEM((2 * SUBLANES, D_MODEL), BF16),
                        pltpu.VMEM((tm, FFN_HIDDEN), BF16),
                        *[pltpu.VMEM((tm + 2 * SUBLANES, th), F32)] * 6],
        compiler_params=params,
        name="ffn",
    )(*args)


def _mm_res_kernel(ac_ref, al_ref, w_ref, x_ref, gate_ref, o_ref, *, n_ctx_blocks):
    a = jnp.where(pl.program_id(0) < n_ctx_blocks, ac_ref[...], al_ref[...])
    o_ref[...] = x_ref[...] + gate_ref[...] * _dot(a, w_ref[...])


def _two_source_specs(block, tm):
    nctx = N_CTX // tm
    return [pl.BlockSpec(block, lambda i: (jnp.minimum(i, nctx - 1), 0)),
            pl.BlockSpec(block, lambda i: (jnp.maximum(i - nctx, 0), 0))]


def _matmul_residual(a_ctx, a_lat, w, x, mod3, layer, kgate, *, name):
    tm = 512
    k = w.shape[0]
    return pl.pallas_call(
        functools.partial(_mm_res_kernel, n_ctx_blocks=N_CTX // tm),
        grid=(N_TOK // tm,),
        in_specs=_two_source_specs((tm, k), tm) + [
            pl.BlockSpec((k, D_MODEL), lambda i: (0, 0)),
            pl.BlockSpec((tm, D_MODEL), lambda i: (i, 0)),
            _mod_spec(layer, kgate, tm, 1),
        ],
        out_specs=pl.BlockSpec((tm, D_MODEL), lambda i: (i, 0)),
        out_shape=jax.ShapeDtypeStruct((N_TOK, D_MODEL), F32),
        compiler_params=_params("parallel"),
        name=name,
    )(a_ctx, a_lat, w, x, mod3)


def _scan_kernel(*refs, nc, zero_init, n_unused, fill_slot):
    x_ref, b_ref, c_ref, dt_ref, alog_ref, e_ref, dx_ref = refs[:7]
    h0_ref = None if zero_init else refs[7]
    y_ref, hl_ref, st_ref, yt_ref = refs[7 + (not zero_init) + n_unused:]
    d = pl.program_id(1)
    c = pl.program_id(2)
    q = SSD_CHUNK

    @pl.when(c == 0)
    def _():
        if zero_init:
            st_ref[...] = jnp.zeros_like(st_ref)
        else:
            st_ref[...] = h0_ref[...].T

    half = SSD_HEAD_DIM
    sgn = 1 - 2 * d
    ii = lax.broadcasted_iota(jnp.int32, (q, q), 0)
    jj = lax.broadcasted_iota(jnp.int32, (q, q), 1)
    tri_b = jnp.where(((ii - jj) * sgn) >= 0, 1.0, 0.0).astype(BF16)
    jl = jj & (half - 1)
    tri_mix = [((ii - (half * jh + jl)) * sgn) >= 0 for jh in range(2)]
    lane = lax.broadcasted_iota(jnp.int32, (half, LANES), 1)
    low = lane < half

    def split_cat(v):
        return jnp.concatenate(_split3(v), axis=1)

    dtb = dt_ref[...]
    da = dtb * (-jnp.exp(alog_ref[...]) * LOG2E)
    cs3 = _dot(tri_b, split_cat(da))
    cs = cs3[:, 0:LANES] + cs3[:, LANES:2 * LANES] + cs3[:, 2 * LANES:]
    e3 = e_ref[...]
    dt_x = _dot(split_cat(dtb)[:, 0:2 * LANES], e3[0:2 * LANES, :])
    cs_x = _dot(split_cat(cs), e3)
    tot_x = jnp.where(d == 0, cs_x[q - 1:q, :], cs_x[0:1, :])
    cs_rows = [jnp.concatenate([cs[half * jh:half * (jh + 1), :]] * 2, axis=0).T for jh in range(2)]

    xs = x_ref[...]
    xd = xs * dt_x
    xw = (xd * jnp.exp2(tot_x - cs_x)).astype(BF16)
    dec_out = jnp.exp2(cs_x)
    dec_chunk = jnp.exp2(tot_x)

    for g in range(SSD_GROUPS):
        gs = slice(g * GROUP_W, (g + 1) * GROUP_W)
        bg = b_ref[:, g * SSD_STATE:(g + 1) * SSD_STATE]
        cg = c_ref[:, g * SSD_STATE:(g + 1) * SSD_STATE].astype(BF16)
        bgb = bg.astype(BF16)
        cb_mix = [_dot_nt(cg, jnp.concatenate([bgb[half * jh:half * (jh + 1), :]] * 2, axis=0)) for jh in range(2)]
        st_g = st_ref[:, gs]
        y_off = _dot(cg, st_g.astype(BF16))
        new_g = _dot(bg.T.astype(BF16), xw[:, gs])
        st_ref[:, gs] = st_g * dec_chunk[:, gs] + new_g
        for p in range(HEADS_PER_GROUP // 2):
            h = g * HEADS_PER_GROUP + 2 * p
            cols = slice(h * SSD_HEAD_DIM, (h + 2) * SSD_HEAD_DIM)
            cs_col = cs_x[:, cols]
            xd2 = xd[:, cols]
            ms, rs = [], []
            for jh in range(2):
                row = jnp.where(low[0:1, :], cs_rows[jh][h:h + 1, :], cs_rows[jh][h + 1:h + 2, :])
                decay = jnp.where(tri_mix[jh], jnp.exp2(cs_col - row), 0.0)
                ms.append((cb_mix[jh] * decay).astype(BF16))
                xj = xd2[half * jh:half * (jh + 1), :]
                rs += [jnp.where(low, xj, 0.0), jnp.where(low, 0.0, xj)]
            lhs = jnp.concatenate(ms, axis=1)
            rhs = jnp.concatenate(rs, axis=0).astype(BF16)
            loc = slice(2 * p * SSD_HEAD_DIM, (2 * p + 2) * SSD_HEAD_DIM)
            yt_ref[:, cols] = _dot(lhs, rhs) + y_off[:, loc] * dec_out[:, cols]

    r0 = pl.multiple_of((c + d * (nc - 1 - 2 * c)) * q, q)

    @pl.when(d == 0)
    def _():
        y_ref[pl.ds(r0, q), :] = yt_ref[...] + dx_ref[...] * xs

    @pl.when(d == 1)
    def _():
        y_ref[pl.ds(r0, q), :] += yt_ref[...]

    @pl.when(c == nc - 1)
    def _():
        if fill_slot is None:
            hl_ref[...] = st_ref[...].T
        else:
            for s in range(hl_ref.shape[0]):
                hl_ref[s] = st_ref[...].T if s == fill_slot else jnp.zeros(hl_ref.shape[1:], F32)


def _ssd_scan(xbc, dt, a_log_pad, expand, d_x, h0, slot, *, nseq, nc, chunk0, name, n_slots=1, states=None):
    q = SSD_CHUNK
    zero_init = h0 is None
    out_slot = slot if n_slots > 1 else 0

    def chunk(b, d, c):
        return chunk0 + b * nc + c + d * (nc - 1 - 2 * c)

    in_specs = [
        pl.BlockSpec((q, SSD_INNER), lambda b, d, c: (chunk(b, d, c), 0)),
        pl.BlockSpec((q, SSD_BC), lambda b, d, c: (chunk(b, d, c), SSD_INNER // SSD_BC)),
        pl.BlockSpec((q, SSD_BC), lambda b, d, c: (chunk(b, d, c), SSD_INNER // SSD_BC + 1)),
        pl.BlockSpec((q, LANES), lambda b, d, c: (chunk(b, d, c), d)),
        pl.BlockSpec((None, 1, LANES), lambda b, d, c: (d, 0, 0)),
        pl.BlockSpec((3 * LANES, SSD_INNER), lambda b, d, c: (0, 0)),
        pl.BlockSpec((1, SSD_INNER), lambda b, d, c: (0, 0)),
    ]
    args = [xbc, xbc, xbc, dt, a_log_pad, expand, d_x]
    if not zero_init:
        in_specs.append(pl.BlockSpec((None, None, None, SSD_INNER, SSD_STATE),
                                     lambda b, d, c: (b, slot, d, 0, 0)))
        args.append(h0)
    aliases = {}
    if states is not None:
        in_specs.append(pl.BlockSpec(memory_space=pl.ANY))
        args.append(states)
        aliases = {len(args) - 1: 1}
    fill_all = states is None and n_slots > 1
    if fill_all:
        state_spec = pl.BlockSpec((None, n_slots, None, SSD_INNER, SSD_STATE), lambda b, d, c: (b, 0, d, 0, 0))
    else:
        state_spec = pl.BlockSpec((None, None, None, SSD_INNER, SSD_STATE),
                                  lambda b, d, c: (b, out_slot, d, 0, 0))
    return pl.pallas_call(
        functools.partial(_scan_kernel, nc=nc, zero_init=zero_init, n_unused=len(aliases),
                          fill_slot=out_slot if fill_all else None),
        grid=(nseq, 2, nc),
        in_specs=in_specs,
        out_specs=[
            pl.BlockSpec((nc * q, SSD_INNER), lambda b, d, c: (b, 0)),
            state_spec,
        ],
        out_shape=[
            jax.ShapeDtypeStruct((nseq * nc * q, SSD_INNER), F32),
            jax.ShapeDtypeStruct((nseq, n_slots, 2, SSD_INNER, SSD_STATE), F32),
        ],
        scratch_shapes=[pltpu.VMEM((SSD_STATE, SSD_INNER), F32), pltpu.VMEM((q, SSD_INNER), F32)],
        input_output_aliases=aliases,
        compiler_params=pltpu.CompilerParams(dimension_semantics=("parallel", "arbitrary", "arbitrary"),
                                             vmem_limit_bytes=BIG_VMEM_LIMIT),
        name=name,
    )(*args)


def _ssd_out_kernel(yc_ref, yl_ref, z_ref, ng_ref, w_ref, x_ref, gate_ref, o_ref, *, n_ctx_blocks):
    y = jnp.where(pl.program_id(0) < n_ctx_blocks, yc_ref[...], yl_ref[...])
    y = y * _silu(z_ref[...].astype(F32))
    gw = SSD_INNER // SSD_GROUPS
    parts = []
    for g in range(SSD_GROUPS):
        yg = y[:, g * gw:(g + 1) * gw]
        parts.append(yg * lax.rsqrt(jnp.mean(yg * yg, axis=-1, keepdims=True) + EPS))
    yn = (jnp.concatenate(parts, axis=1) * ng_ref[...]).astype(BF16)
    o_ref[...] = x_ref[...] + gate_ref[...] * _dot(yn, w_ref[...])


def _ssd_out(y_ctx, y_lat, z, norm_g, out_w, x, mod3, layer):
    tm = 512
    return pl.pallas_call(
        functools.partial(_ssd_out_kernel, n_ctx_blocks=N_CTX // tm),
        grid=(N_TOK // tm,),
        in_specs=_two_source_specs((tm, SSD_INNER), tm) + [
            pl.BlockSpec((tm, SSD_INNER), lambda i: (i, 0)),
            pl.BlockSpec((1, SSD_INNER), lambda i: (0, 0)),
            pl.BlockSpec((SSD_INNER, D_MODEL), lambda i: (0, 0)),
            pl.BlockSpec((tm, D_MODEL), lambda i: (i, 0)),
            _mod_spec(layer, 2, tm, 1),
        ],
        out_specs=pl.BlockSpec((tm, D_MODEL), lambda i: (i, 0)),
        out_shape=jax.ShapeDtypeStruct((N_TOK, D_MODEL), F32),
        compiler_params=_params("parallel"),
        name="ssd_out",
    )(y_ctx, y_lat, z, norm_g.reshape(1, SSD_INNER), out_w, x, mod3)


def _diff_lambda(lam_ref, lam_init):
    lv = lam_ref[...]
    return (jnp.exp(jnp.sum(lv[0:1] * lv[1:2], axis=-1, keepdims=True))
            - jnp.exp(jnp.sum(lv[2:3] * lv[3:4], axis=-1, keepdims=True)) + lam_init)


def _attend_head(q, kb, vb, lam, g, lam_init):
    lane = lax.broadcasted_iota(jnp.int32, q.shape, 1)
    zero = jnp.zeros_like(q)
    probs = []
    for qm in (jnp.where(lane < DA_HEAD_DIM, q, zero), jnp.where(lane >= DA_HEAD_DIM, q, zero)):
        s = _dot_nt(qm, kb)
        p = jnp.exp2(s - jnp.max(s, axis=-1, keepdims=True))
        probs.append((p, jnp.sum(p, axis=-1, keepdims=True)))
    (p1, l1), (p2, l2) = probs
    w = (p1 - p2 * (lam * l1 / l2)).astype(BF16)
    o = _dot(w, vb) * (1.0 / l1)
    o = o * lax.rsqrt(jnp.mean(o * o, axis=-1, keepdims=True) + EPS)
    return o * g * (1.0 - lam_init)


def _attn_ctx_kernel(q_ref, k_ref, v_ref, lam_ref, g_ref, o_ref, *, lam_init):
    lam = _diff_lambda(lam_ref, lam_init)
    g = g_ref[...]
    for h in range(DA_HEADS):
        cols = slice(h * HEAD_W, (h + 1) * HEAD_W)
        o = _attend_head(q_ref[:, cols], k_ref[:, cols], v_ref[:, cols], lam, g, lam_init)
        o_ref[:, cols] = o.astype(o_ref.dtype)


def _attn_lat_kernel(q_ref, k_ref, v_ref, kc_ref, vc_ref, lam_ref, g_ref, o_ref, kb_ref, vb_ref, *, lam_init):
    @pl.when(pl.program_id(2) == 0)
    def _():
        kb_ref[0:DEC_SEQ, :] = k_ref[...]
        kb_ref[DEC_SEQ:, :] = kc_ref[...].astype(BF16)
        vb_ref[0:DEC_SEQ, :] = v_ref[...]
        vb_ref[DEC_SEQ:, :] = vc_ref[...].astype(BF16)

    o = _attend_head(q_ref[...], kb_ref[...], vb_ref[...], _diff_lambda(lam_ref, lam_init), g_ref[...], lam_init)
    o_ref[...] = o.astype(o_ref.dtype)


def _attention(q, k, v, cache_k4, cache_v4, slot, lam_vecs, subln_g, lam_init):
    common = [pl.BlockSpec((4, DA_HEAD_DIM), lambda *_: (0, 0)),
              pl.BlockSpec((1, HEAD_W), lambda *_: (0, 0))]
    g2 = subln_g.reshape(1, HEAD_W)

    seq_block = pl.BlockSpec((SEQ, D_MODEL), lambda b: (b, 0))
    ctx = pl.pallas_call(
        functools.partial(_attn_ctx_kernel, lam_init=lam_init),
        grid=(BATCH,),
        in_specs=[seq_block, seq_block, seq_block] + common,
        out_specs=seq_block,
        out_shape=jax.ShapeDtypeStruct((N_CTX, D_MODEL), BF16),
        compiler_params=_params("parallel"),
        name="attn_ctx",
    )(q, k, v, lam_vecs, g2)

    tq = 256
    nq = DEC_SEQ // tq
    q0 = N_CTX // tq
    s0 = N_CTX // DEC_SEQ
    keys = DEC_SEQ + PAST_LEN
    lat = pl.pallas_call(
        functools.partial(_attn_lat_kernel, lam_init=lam_init),
        grid=(DEC_BATCH, DA_HEADS, nq),
        in_specs=[
            pl.BlockSpec((tq, HEAD_W), lambda b, h, i: (q0 + b * nq + i, h)),
            pl.BlockSpec((DEC_SEQ, HEAD_W), lambda b, h, i: (s0 + b, h)),
            pl.BlockSpec((DEC_SEQ, HEAD_W), lambda b, h, i: (s0 + b, h)),
            pl.BlockSpec((None, None, PAST_LEN, HEAD_W), lambda b, h, i: (b, slot, 0, h)),
            pl.BlockSpec((None, None, PAST_LEN, HEAD_W), lambda b, h, i: (b, slot, 0, h)),
        ] + common,
        out_specs=pl.BlockSpec((tq, HEAD_W), lambda b, h, i: (b * nq + i, h)),
        out_shape=jax.ShapeDtypeStruct((N_LAT, D_MODEL), BF16),
        scratch_shapes=[pltpu.VMEM((keys, HEAD_W), BF16), pltpu.VMEM((keys, HEAD_W), BF16)],
        compiler_params=_params("parallel", "parallel", "arbitrary"),
        name="attn_lat",
    )(q, k, v, cache_k4, cache_v4, lam_vecs, g2)
    return ctx, lat


def _rope_token_tables():
    rows = DEC_SEQ // GRID_W
    row = jnp.repeat(jnp.arange(rows, dtype=F32), GRID_W)
    col = jnp.tile(jnp.arange(GRID_W, dtype=F32), rows)
    inv = 1.0 / (ROPE_BASE ** (jnp.arange(0, ROPE_AXIS_DIM, 2, dtype=F32) / ROPE_AXIS_DIM))
    ang_r = row[:, None] * inv
    ang_c = col[:, None] * inv
    cr, sr, cc, sc = jnp.cos(ang_r), jnp.sin(ang_r), jnp.cos(ang_c), jnp.sin(ang_c)
    cos64 = jnp.concatenate([cr, cr, cc, cc], axis=1)
    sin64 = jnp.concatenate([-sr, sr, -sc, sc], axis=1)
    cos_lat = jnp.tile(cos64, (DEC_BATCH, 2))
    sin_lat = jnp.tile(sin64, (DEC_BATCH, 2))
    cos = jnp.concatenate([jnp.ones((N_CTX, HEAD_W), F32), cos_lat], axis=0)
    sin = jnp.concatenate([jnp.zeros((N_CTX, HEAD_W), F32), sin_lat], axis=0)
    return cos, sin


def _lambda_init(layer):
    return 0.8 - 0.6 * math.exp(-0.3 * layer)


def kernel(x_prompt, x_sample, state_ssd, cache_k, cache_v, c, c_ctx, mod_w, mod_b, norm_mix_g, norm_ffn_g, ssd_in_w, ssd_conv_w, ssd_conv_b, ssd_dt_bias, ssd_a_log, ssd_d, ssd_norm_g, ssd_out_w, att_qkv_w, att_lambda, att_subln_g, att_out_w, ffn_up_w, ffn_conv_w, ffn_conv_b, ffn_down_w, final_norm_g):
    n_ssd = ssd_in_w.shape[0]
    n_att = att_qkv_w.shape[0]
    x = None

    cpad = jnp.concatenate([c_ctx[None], c, jnp.zeros((MOD_ROWS - 1 - DEC_BATCH, D_MODEL), F32)], axis=0)
    mod = _modulation(cpad, mod_w, mod_b)
    mod3 = mod.reshape(DEPTH * MOD_ROWS * 6, 1, D_MODEL)

    cos_t, sin_t = _rope_token_tables()
    expand = (jnp.arange(LANES)[:, None] == (jnp.arange(SSD_INNER)[None, :] // SSD_HEAD_DIM)).astype(BF16)
    expand = jnp.concatenate([expand] * 3, axis=0)
    h0_lat = state_ssd.reshape(DEC_BATCH, n_ssd, 2, SSD_INNER, SSD_STATE)
    cache_k4 = cache_k.reshape(DEC_BATCH, n_att, PAST_LEN, D_MODEL)
    cache_v4 = cache_v.reshape(DEC_BATCH, n_att, PAST_LEN, D_MODEL)

    new_ssd = None
    new_kv = [jnp.zeros((BATCH, n_att, SEQ, D_MODEL), F32)] * 2
    for i in range(DEPTH):
        slot = i // 2
        if i % 2 == 0:
            in_w = ssd_in_w[slot]
            w_z = in_w[:, :SSD_INNER].astype(BF16)
            w_xbc = in_w[:, SSD_INNER:SSD_MAIN].astype(BF16)
            pad = jnp.zeros((D_MODEL, LANES - SSD_HEADS), F32)
            w_dt = jnp.concatenate([in_w[:, SSD_MAIN:SSD_MAIN + SSD_HEADS], pad,
                                    in_w[:, SSD_MAIN + SSD_HEADS:], pad], axis=1).astype(BF16)
            zpad = jnp.zeros((LANES - SSD_HEADS,), F32)
            dt_bias = jnp.concatenate([ssd_dt_bias[slot, 0], zpad, ssd_dt_bias[slot, 1], zpad]).reshape(1, 2 * LANES)
            a_log_pad = jnp.concatenate([ssd_a_log[slot], jnp.zeros((2, LANES - SSD_HEADS), F32)],
                                        axis=1).reshape(2, 1, LANES)
            d_x = jnp.repeat(ssd_d[slot], SSD_HEAD_DIM).reshape(1, SSD_INNER)
            if x is None:
                z, x = _norm_mod_matmul_join(x_prompt.reshape(N_CTX, D_MODEL), x_sample.reshape(N_LAT, D_MODEL),
                                             mod3, i, 0, 1, norm_mix_g[i], w_z, tn=1024, out_dtype=BF16,
                                             name="ssd_z_join")
            else:
                z = _norm_mod_matmul(x, mod3, i, 0, 1, norm_mix_g[i], w_z, tn=1024, out_dtype=BF16, name="ssd_z")
            xbc, dt = _ssd_xbc(x, mod3, i, norm_mix_g[i], w_xbc, ssd_conv_w[slot], ssd_conv_b[slot], w_dt, dt_bias)
            y_ctx, new_ssd = _ssd_scan(xbc, dt, a_log_pad, expand, d_x, None, slot,
                                       nseq=BATCH, nc=SEQ // SSD_CHUNK, chunk0=0, name="ssd_scan_ctx",
                                       n_slots=n_ssd, states=new_ssd)
            y_lat, _ = _ssd_scan(xbc, dt, a_log_pad, expand, d_x, h0_lat, slot,
                                 nseq=DEC_BATCH, nc=DEC_SEQ // SSD_CHUNK, chunk0=N_CTX // SSD_CHUNK,
                                 name="ssd_scan_lat")
            x = _ssd_out(y_ctx, y_lat, z, ssd_norm_g[slot], ssd_out_w[slot].astype(BF16), x, mod3, i)
        else:
            lam_init = _lambda_init(i)
            q, k, v, *new_kv = _qkv_proj(x, mod3, i, norm_mix_g[i], att_qkv_w[slot].astype(BF16),
                                         cos_t, sin_t, slot, n_att, new_kv)
            o_ctx, o_lat = _attention(q, k, v, cache_k4, cache_v4, slot, att_lambda[slot], att_subln_g[slot],
                                      lam_init)
            x = _matmul_residual(o_ctx, o_lat, att_out_w[slot].astype(BF16), x, mod3, i, 2, name="att_out")
        x = _ffn(x, mod3, i, norm_ffn_g[i], ffn_up_w[i].astype(BF16), ffn_conv_w[i], ffn_conv_b[i],
                 ffn_down_w[i].astype(BF16), final_g=final_norm_g if i == DEPTH - 1 else None)

    y_prompt = x[0].reshape(BATCH, SEQ, D_MODEL)
    y_sample = x[1].reshape(DEC_BATCH, DEC_SEQ, D_MODEL)
    return (y_prompt, y_sample,
            new_ssd.reshape(BATCH, n_ssd, 2, SSD_HEADS, SSD_HEAD_DIM, SSD_STATE),
            new_kv[0].reshape(BATCH, n_att, SEQ, DA_HEADS, 2, DA_HEAD_DIM),
            new_kv[1].reshape(BATCH, n_att, SEQ, DA_HEADS, HEAD_W))
```

```python
import functools
import math

import jax
import jax.numpy as jnp
from jax import lax
from jax.experimental import pallas as pl
from jax.experimental.pallas import tpu as pltpu

F32 = jnp.float32
BF16 = jnp.bfloat16

D_MODEL = 1024
BATCH = 16
SEQ = 256
DEPTH = 4
DEC_BATCH = 4
DEC_SEQ = 2048
PAST_LEN = 256
GRID_W = 64
EPS = 1e-6

SSD_INNER = 2048
SSD_HEAD_DIM = 64
SSD_HEADS = 32
SSD_GROUPS = 4
SSD_STATE = 128
SSD_CHUNK = 128
SSD_BC = SSD_GROUPS * SSD_STATE
SSD_CONV_CH = SSD_INNER + 2 * SSD_BC
SSD_MAIN = SSD_INNER + SSD_CONV_CH
HEADS_PER_GROUP = SSD_HEADS // SSD_GROUPS
GROUP_W = HEADS_PER_GROUP * SSD_HEAD_DIM

DA_HEAD_DIM = 64
DA_HEADS = 8
DA_SCALE = DA_HEAD_DIM ** -0.5
LOG2E = math.log2(math.e)
Q_PRESCALE = DA_SCALE * LOG2E
HEAD_W = 2 * DA_HEAD_DIM
ROPE_BASE = 10000.0
ROPE_AXIS_DIM = DA_HEAD_DIM // 2

FFN_HIDDEN = 2816

N_CTX = BATCH * SEQ
N_LAT = DEC_BATCH * DEC_SEQ
N_TOK = N_CTX + N_LAT
MOD_ROWS = 8
LANES = 128
SUBLANES = 8
VMEM_LIMIT = 48 * 1024 * 1024
BIG_VMEM_LIMIT = 56 * 1024 * 1024


def _params(*sem):
    return pltpu.CompilerParams(dimension_semantics=sem, vmem_limit_bytes=VMEM_LIMIT)


def _mod_row(i, tm):
    nctx = N_CTX // tm
    return jnp.where(i < nctx, 0, 1 + (i - nctx) // (DEC_SEQ // tm))


def _silu(x):
    return x * jax.nn.sigmoid(x)


def _softplus(x):
    return jnp.maximum(x, 0.0) + jnp.log1p(jnp.exp(-jnp.abs(x)))


def _split3(x):
    hi = x.astype(BF16)
    r1 = x - hi.astype(F32)
    mid = r1.astype(BF16)
    lo = (r1 - mid.astype(F32)).astype(BF16)
    return hi, mid, lo


def _dot(a, b):
    return jnp.dot(a, b, preferred_element_type=F32)


def _dot_nt(a, b):
    return lax.dot_general(a, b, (((1,), (1,)), ((), ())), preferred_element_type=F32)


def _norm_mod(x, g, sc, sh):
    r = x * lax.rsqrt(jnp.mean(x * x, axis=-1, keepdims=True) + EPS)
    return ((r * g) * (1.0 + sc) + sh).astype(BF16)


def _mod_kernel(c_ref, w_ref, b_ref, o_ref):
    s = _silu(c_ref[...]).astype(BF16)
    o_ref[...] = _dot(s, w_ref[...].astype(BF16)) + b_ref[...]


def _modulation(cpad, mod_w, mod_b):
    tn = 1024
    n = 6 * D_MODEL
    return pl.pallas_call(
        _mod_kernel,
        grid=(DEPTH, n // tn),
        in_specs=[
            pl.BlockSpec((MOD_ROWS, D_MODEL), lambda l, j: (0, 0)),
            pl.BlockSpec((None, D_MODEL, tn), lambda l, j: (l, 0, j)),
            pl.BlockSpec((None, 1, tn), lambda l, j: (l, 0, j)),
        ],
        out_specs=pl.BlockSpec((None, MOD_ROWS, tn), lambda l, j: (l, 0, j)),
        out_shape=jax.ShapeDtypeStruct((DEPTH, MOD_ROWS, n), F32),
        compiler_params=_params("parallel", "parallel"),
        name="modulation",
    )(cpad, mod_w, mod_b.reshape(DEPTH, 1, n))


def _mod_spec(layer, k, tm, ngrid):
    base = layer * MOD_ROWS * 6 + k
    if ngrid == 1:
        return pl.BlockSpec((None, 1, D_MODEL), lambda i: (base + 6 * _mod_row(i, tm), 0, 0))
    return pl.BlockSpec((None, 1, D_MODEL), lambda i, j: (base + 6 * _mod_row(i, tm), 0, 0))


def _nmm_kernel(x_ref, g_ref, sh_ref, sc_ref, w_ref, o_ref, h_ref):
    @pl.when(pl.program_id(1) == 0)
    def _():
        h_ref[...] = _norm_mod(x_ref[...], g_ref[...], sc_ref[...], sh_ref[...])

    o_ref[...] = _dot(h_ref[...], w_ref[...]).astype(o_ref.dtype)


def _nmm_in_specs(layer, kshift, kscale, tm, tn):
    return [
        pl.BlockSpec((tm, D_MODEL), lambda i, j: (i, 0)),
        pl.BlockSpec((1, D_MODEL), lambda i, j: (0, 0)),
        _mod_spec(layer, kshift, tm, 2),
        _mod_spec(layer, kscale, tm, 2),
        pl.BlockSpec((D_MODEL, tn), lambda i, j: (0, j)),
    ]


def _norm_mod_matmul(x, mod3, layer, kshift, kscale, g, w, *, tn, out_dtype, name):
    tm = 1024
    n = w.shape[1]
    return pl.pallas_call(
        _nmm_kernel,
        grid=(N_TOK // tm, n // tn),
        in_specs=_nmm_in_specs(layer, kshift, kscale, tm, tn),
        out_specs=pl.BlockSpec((tm, tn), lambda i, j: (i, j)),
        out_shape=jax.ShapeDtypeStruct((N_TOK, n), out_dtype),
        scratch_shapes=[pltpu.VMEM((tm, D_MODEL), BF16)],
        compiler_params=_params("parallel", "arbitrary"),
        name=name,
    )(x, g.reshape(1, D_MODEL), mod3, mod3, w)


def _nmm_join_kernel(xc_ref, xl_ref, g_ref, sh_ref, sc_ref, w_ref, o_ref, xo_ref, h_ref, *, n_ctx_blocks):
    @pl.when(pl.program_id(1) == 0)
    def _():
        x = jnp.where(pl.program_id(0) < n_ctx_blocks, xc_ref[...], xl_ref[...])
        xo_ref[...] = x
        h_ref[...] = _norm_mod(x, g_ref[...], sc_ref[...], sh_ref[...])

    o_ref[...] = _dot(h_ref[...], w_ref[...]).astype(o_ref.dtype)


def _norm_mod_matmul_join(x_ctx, x_lat, mod3, layer, kshift, kscale, g, w, *, tn, out_dtype, name):
    tm = 1024
    n = w.shape[1]
    nctx = N_CTX // tm
    specs = _nmm_in_specs(layer, kshift, kscale, tm, tn)
    return pl.pallas_call(
        functools.partial(_nmm_join_kernel, n_ctx_blocks=nctx),
        grid=(N_TOK // tm, n // tn),
        in_specs=[pl.BlockSpec((tm, D_MODEL), lambda i, j: (jnp.minimum(i, nctx - 1), 0)),
                  pl.BlockSpec((tm, D_MODEL), lambda i, j: (jnp.maximum(i - nctx, 0), 0))] + specs[1:],
        out_specs=[pl.BlockSpec((tm, tn), lambda i, j: (i, j)),
                   pl.BlockSpec((tm, D_MODEL), lambda i, j: (i, 0))],
        out_shape=[jax.ShapeDtypeStruct((N_TOK, n), out_dtype),
                   jax.ShapeDtypeStruct((N_TOK, D_MODEL), F32)],
        scratch_shapes=[pltpu.VMEM((tm, D_MODEL), BF16)],
        compiler_params=_params("parallel", "arbitrary"),
        name=name,
    )(x_ctx, x_lat, g.reshape(1, D_MODEL), mod3, mod3, w)


def _qkv_kernel(*refs, n_ctx_blocks, n_unused):
    x_ref, g_ref, sh_ref, sc_ref, w_ref, cos_ref, sin_ref = refs[:7]
    q_ref, k_ref, v_ref, kctx_ref, vctx_ref, h_ref = refs[7 + n_unused:]
    j = pl.program_id(1)
    is_ctx = pl.program_id(0) < n_ctx_blocks

    @pl.when(j == 0)
    def _():
        h_ref[...] = _norm_mod(x_ref[...], g_ref[...], sc_ref[...], sh_ref[...])

    u = _dot(h_ref[...], w_ref[...])

    def rope_into(o_refs, scale):
        cos = cos_ref[...] * scale
        sin = sin_ref[...] * scale
        lane = lax.broadcasted_iota(jnp.int32, cos.shape, 1)
        half = ROPE_AXIS_DIM // 2
        first = (lane % ROPE_AXIS_DIM) < half
        for k in range(u.shape[1] // HEAD_W):
            cols = slice(k * HEAD_W, (k + 1) * HEAD_W)
            uk = u[:, cols]
            partner = jnp.where(first, pltpu.roll(uk, HEAD_W - half, 1), pltpu.roll(uk, half, 1))
            r = uk * cos + partner * sin
            for o_ref in o_refs:
                if len(o_ref.shape) == 3:
                    o_ref[:, :, cols] = r.reshape(o_ref.shape[0], SEQ, HEAD_W)
                else:
                    o_ref[:, cols] = r.astype(o_ref.dtype)

    @pl.when(j == 0)
    def _():
        rope_into([q_ref], Q_PRESCALE)

    @pl.when(jnp.logical_and(j == 1, is_ctx))
    def _():
        rope_into([k_ref, kctx_ref], 1.0)

    @pl.when(jnp.logical_and(j == 1, jnp.logical_not(is_ctx)))
    def _():
        rope_into([k_ref], 1.0)

    @pl.when(j == 2)
    def _():
        v_ref[...] = u.astype(v_ref.dtype)

    @pl.when(jnp.logical_and(j == 2, is_ctx))
    def _():
        vctx_ref[...] = u.reshape(vctx_ref.shape)


def _qkv_proj(x, mod3, layer, g, w, cos_t, sin_t, slot, n_slots, caches):
    tm = 1024
    nctx = N_CTX // tm
    nseq = tm // SEQ
    out = pl.BlockSpec((tm, D_MODEL), lambda i, j: (i, 0))
    ctx_out = pl.BlockSpec((nseq, None, SEQ, D_MODEL), lambda i, j: (jnp.minimum(i, nctx - 1), slot, 0, 0))
    ctx_shape = jax.ShapeDtypeStruct((BATCH, n_slots, SEQ, D_MODEL), F32)
    in_specs = _nmm_in_specs(layer, 0, 1, tm, D_MODEL) + [pl.BlockSpec((tm, HEAD_W), lambda i, j: (i, 0))] * 2
    args = [x, g.reshape(1, D_MODEL), mod3, mod3, w, cos_t, sin_t]
    aliases = {}
    if caches is not None:
        in_specs += [pl.BlockSpec(memory_space=pl.ANY)] * 2
        aliases = {len(args): 3, len(args) + 1: 4}
        args += list(caches)
    return pl.pallas_call(
        functools.partial(_qkv_kernel, n_ctx_blocks=nctx, n_unused=len(aliases)),
        grid=(N_TOK // tm, 3),
        in_specs=in_specs,
        out_specs=[out, out, out, ctx_out, ctx_out],
        out_shape=[jax.ShapeDtypeStruct((N_TOK, D_MODEL), BF16),
                   jax.ShapeDtypeStruct((N_TOK, D_MODEL), BF16),
                   jax.ShapeDtypeStruct((N_TOK, D_MODEL), BF16),
                   ctx_shape, ctx_shape],
        scratch_shapes=[pltpu.VMEM((tm, D_MODEL), BF16)],
        input_output_aliases=aliases,
        compiler_params=pltpu.CompilerParams(dimension_semantics=("arbitrary", "arbitrary"),
                                             vmem_limit_bytes=BIG_VMEM_LIMIT),
        name="att_qkv",
    )(*args)


PAD = SUBLANES


def _block_seqlen(i, tm):
    return jnp.where(i < N_CTX // tm, SEQ, DEC_SEQ)


def _project_with_halo(h, hh, w, u_refs, tm):
    u_ref, up_ref, un_ref = u_refs
    u = _dot(h, w)
    uh = _dot(hh, w)
    u_ref[PAD:PAD + tm, :] = u
    up_ref[PAD + 1:PAD + 1 + tm, :] = u
    up_ref[PAD:PAD + 1, :] = uh[SUBLANES - 1:SUBLANES, :]
    un_ref[PAD - 1:PAD - 1 + tm, :] = u
    un_ref[PAD + tm - 1:PAD + tm, :] = uh[SUBLANES:SUBLANES + 1, :]


def _conv_chunks(u_refs, cws, cbs, row0, seqmask, tm, rows, emit):
    ridx = lax.broadcasted_iota(jnp.int32, (rows, 1), 0)
    for r0 in range(0, tm, rows):
        outs = []
        for (u_ref, up_ref, un_ref), cw, cb in zip(u_refs, cws, cbs):
            prev = up_ref[pl.ds(PAD + r0, rows), :]
            cur = u_ref[pl.ds(PAD + r0, rows), :]
            nxt = un_ref[pl.ds(PAD + r0, rows), :]
            if r0 % SEQ == 0:
                starts = ((row0 + r0) & seqmask) == 0
                prev = jnp.where(jnp.logical_and(ridx == 0, starts), 0.0, prev)
            if (r0 + rows) % SEQ == 0:
                ends = ((row0 + r0 + rows) & seqmask) == 0
                nxt = jnp.where(jnp.logical_and(ridx == rows - 1, ends), 0.0, nxt)
            outs.append(prev * cw[0:1, :] + cur * cw[1:2, :] + nxt * cw[2:3, :] + cb)
        emit(r0, outs)


def _halo_x_specs(tm):
    per = tm // SUBLANES
    last = N_TOK // SUBLANES - 1
    return [
        pl.BlockSpec((tm, D_MODEL), lambda i, j: (i, 0)),
        pl.BlockSpec((SUBLANES, D_MODEL), lambda i, j: (jnp.maximum(i * per - 1, 0), 0)),
        pl.BlockSpec((SUBLANES, D_MODEL), lambda i, j: (jnp.minimum((i + 1) * per, last), 0)),
    ]


def _xbc_kernel(x_ref, xp_ref, xn_ref, g_ref, sh_ref, sc_ref, w_ref, cw_ref, cb_ref, wdt_ref, bdt_ref,
                o_ref, dt_ref, h_ref, hh_ref, u_ref, up_ref, un_ref, *, tm, rows):
    i = pl.program_id(0)

    @pl.when(pl.program_id(1) == 0)
    def _():
        g, sc, sh = g_ref[...], sc_ref[...], sh_ref[...]
        h_ref[...] = _norm_mod(x_ref[...], g, sc, sh)
        hh_ref[...] = _norm_mod(jnp.concatenate([xp_ref[...], xn_ref[...]], axis=0), g, sc, sh)
        dt_ref[...] = _softplus(_dot(h_ref[...], wdt_ref[...]) + bdt_ref[...])

    u3 = (u_ref, up_ref, un_ref)
    _project_with_halo(h_ref[...], hh_ref[...], w_ref[...], u3, tm)

    def emit(r0, outs):
        o_ref[pl.ds(r0, rows), :] = _silu(outs[0])

    _conv_chunks([u3], [cw_ref[...]], [cb_ref[...]], i * tm, _block_seqlen(i, tm) - 1, tm, rows, emit)


def _ssd_xbc(x, mod3, layer, g, w_xbc, conv_w, conv_b, w_dt, dt_bias):
    tm, tc, rows = 1024, 512, 32
    ndt = w_dt.shape[1]
    return pl.pallas_call(
        functools.partial(_xbc_kernel, tm=tm, rows=rows),
        grid=(N_TOK // tm, SSD_CONV_CH // tc),
        in_specs=_halo_x_specs(tm) + [
            pl.BlockSpec((1, D_MODEL), lambda i, j: (0, 0)),
            _mod_spec(layer, 0, tm, 2),
            _mod_spec(layer, 1, tm, 2),
            pl.BlockSpec((D_MODEL, tc), lambda i, j: (0, j)),
            pl.BlockSpec((3, tc), lambda i, j: (0, j)),
            pl.BlockSpec((1, tc), lambda i, j: (0, j)),
            pl.BlockSpec((D_MODEL, ndt), lambda i, j: (0, 0)),
            pl.BlockSpec((1, ndt), lambda i, j: (0, 0)),
        ],
        out_specs=[
            pl.BlockSpec((tm, tc), lambda i, j: (i, j)),
            pl.BlockSpec((tm, ndt), lambda i, j: (i, 0)),
        ],
        out_shape=[
            jax.ShapeDtypeStruct((N_TOK, SSD_CONV_CH), F32),
            jax.ShapeDtypeStruct((N_TOK, ndt), F32),
        ],
        scratch_shapes=[pltpu.VMEM((tm, D_MODEL), BF16), pltpu.VMEM((2 * SUBLANES, D_MODEL), BF16),
                        *[pltpu.VMEM((tm + 2 * SUBLANES, tc), F32)] * 3],
        compiler_params=_params("parallel", "arbitrary"),
        name="ssd_xbc",
    )(x, x, x, g.reshape(1, D_MODEL), mod3, mod3, w_xbc, conv_w, conv_b.reshape(1, SSD_CONV_CH), w_dt, dt_bias)


def _ffn_kernel(*refs, tm, th, rows, n_ctx_blocks):
    (x_ref, xp_ref, xn_ref, g_ref, sh_ref, sc_ref, gate_ref, wa_ref, wv_ref,
     cwa_ref, cba_ref, cwv_ref, cbv_ref, wd_ref) = refs[:14]
    h_ref, hh_ref, act_ref = refs[-9:-6]
    ua3, uv3 = refs[-6:-3], refs[-3:]
    final = len(refs) == 14 + 3 + 9
    i = pl.program_id(0)
    j = pl.program_id(1)

    @pl.when(j == 0)
    def _():
        g, sc, sh = g_ref[...], sc_ref[...], sh_ref[...]
        h_ref[...] = _norm_mod(x_ref[...], g, sc, sh)
        hh_ref[...] = _norm_mod(jnp.concatenate([xp_ref[...], xn_ref[...]], axis=0), g, sc, sh)

    h = h_ref[...]
    hh = hh_ref[...]
    _project_with_halo(h, hh, wa_ref[...], ua3, tm)
    _project_with_halo(h, hh, wv_ref[...], uv3, tm)

    col0 = pl.multiple_of(j * th, th)

    def emit(r0, outs):
        act_ref[pl.ds(r0, rows), pl.ds(col0, th)] = (_silu(outs[0]) * outs[1]).astype(BF16)

    _conv_chunks([ua3, uv3], [cwa_ref[...], cwv_ref[...]], [cba_ref[...], cbv_ref[...]],
                 i * tm, _block_seqlen(i, tm) - 1, tm, rows, emit)

    @pl.when(j == pl.num_programs(1) - 1)
    def _():
        xo = x_ref[...] + gate_ref[...] * _dot(act_ref[...], wd_ref[...])
        if not final:
            refs[14][...] = xo
        else:
            fg_ref, yc_ref, yl_ref = refs[14:17]
            y = xo * lax.rsqrt(jnp.mean(xo * xo, axis=-1, keepdims=True) + EPS) * fg_ref[...]

            @pl.when(i < n_ctx_blocks)
            def _():
                yc_ref[...] = y

            @pl.when(i >= n_ctx_blocks)
            def _():
                yl_ref[...] = y


def _ffn(x, mod3, layer, g, up_w, conv_w, conv_b, down_w, final_g=None):
    tm, th, rows = 1024, 256, 64
    nb = FFN_HIDDEN // th
    nctx = N_CTX // tm
    cb = conv_b.reshape(1, 2 * FFN_HIDDEN)
    in_specs = _halo_x_specs(tm) + [
        pl.BlockSpec((1, D_MODEL), lambda i, j: (0, 0)),
        _mod_spec(layer, 3, tm, 2),
        _mod_spec(layer, 4, tm, 2),
        _mod_spec(layer, 5, tm, 2),
        pl.BlockSpec((D_MODEL, th), lambda i, j: (0, j)),
        pl.BlockSpec((D_MODEL, th), lambda i, j: (0, j + nb)),
        pl.BlockSpec((3, th), lambda i, j: (0, j)),
        pl.BlockSpec((1, th), lambda i, j: (0, j)),
        pl.BlockSpec((3, th), lambda i, j: (0, j + nb)),
        pl.BlockSpec((1, th), lambda i, j: (0, j + nb)),
        pl.BlockSpec((FFN_HIDDEN, D_MODEL), lambda i, j: (0, 0)),
    ]
    args = [x, x, x, g.reshape(1, D_MODEL), mod3, mod3, mod3, up_w, up_w, conv_w, cb, conv_w, cb, down_w]
    if final_g is None:
        out_specs = pl.BlockSpec((tm, D_MODEL), lambda i, j: (i, 0))
        out_shape = jax.ShapeDtypeStruct((N_TOK, D_MODEL), F32)
        params = _params("parallel", "arbitrary")
    else:
        in_specs.append(pl.BlockSpec((1, D_MODEL), lambda i, j: (0, 0)))
        args.append(final_g.reshape(1, D_MODEL))
        out_specs = [pl.BlockSpec((tm, D_MODEL), lambda i, j: (jnp.minimum(i, nctx - 1), 0)),
                     pl.BlockSpec((tm, D_MODEL), lambda i, j: (jnp.maximum(i - nctx, 0), 0))]
        out_shape = [jax.ShapeDtypeStruct((N_CTX, D_MODEL), F32), jax.ShapeDtypeStruct((N_LAT, D_MODEL), F32)]
        params = pltpu.CompilerParams(dimension_semantics=("arbitrary", "arbitrary"),
                                      vmem_limit_bytes=BIG_VMEM_LIMIT)
    return pl.pallas_call(
        functools.partial(_ffn_kernel, tm=tm, th=th, rows=rows, n_ctx_blocks=nctx),
        grid=(N_TOK // tm, nb),
        in_specs=in_specs,
        out_specs=out_specs,
        out_shape=out_shape,
        scratch_shapes=[pltpu.VMEM((tm, D_MODEL), BF16), pltpu.VMEM((2 * SUBLANES, D_MODEL), BF16),
                        pltpu.VMEM((tm, FFN_HIDDEN), BF16),
                        *[pltpu.VMEM((tm + 2 * SUBLANES, th), F32)] * 6],
        compiler_params=params,
        name="ffn",
    )(*args)


def _mm_res_kernel(ac_ref, al_ref, w_ref, x_ref, gate_ref, o_ref, *, n_ctx_blocks):
    a = jnp.where(pl.program_id(0) < n_ctx_blocks, ac_ref[...], al_ref[...])
    o_ref[...] = x_ref[...] + gate_ref[...] * _dot(a, w_ref[...])


def _two_source_specs(block, tm):
    nctx = N_CTX // tm
    return [pl.BlockSpec(block, lambda i: (jnp.minimum(i, nctx - 1), 0)),
            pl.BlockSpec(block, lambda i: (jnp.maximum(i - nctx, 0), 0))]


def _matmul_residual(a_ctx, a_lat, w, x, mod3, layer, kgate, *, name):
    tm = 512
    k = w.shape[0]
    return pl.pallas_call(
        functools.partial(_mm_res_kernel, n_ctx_blocks=N_CTX // tm),
        grid=(N_TOK // tm,),
        in_specs=_two_source_specs((tm, k), tm) + [
            pl.BlockSpec((k, D_MODEL), lambda i: (0, 0)),
            pl.BlockSpec((tm, D_MODEL), lambda i: (i, 0)),
            _mod_spec(layer, kgate, tm, 1),
        ],
        out_specs=pl.BlockSpec((tm, D_MODEL), lambda i: (i, 0)),
        out_shape=jax.ShapeDtypeStruct((N_TOK, D_MODEL), F32),
        compiler_params=_params("parallel"),
        name=name,
    )(a_ctx, a_lat, w, x, mod3)


def _scan_kernel(*refs, nc, zero_init, n_unused, fill_slot):
    x_ref, b_ref, c_ref, dt_ref, alog_ref, e_ref, dx_ref = refs[:7]
    h0_ref = None if zero_init else refs[7]
    y_ref, hl_ref, st_ref, yt_ref, yacc_ref = refs[7 + (not zero_init) + n_unused:]
    d = pl.program_id(1)
    c = pl.program_id(2)
    q = SSD_CHUNK

    @pl.when(c == 0)
    def _():
        if zero_init:
            st_ref[...] = jnp.zeros_like(st_ref)
        else:
            st_ref[...] = h0_ref[...].T

    half = SSD_HEAD_DIM
    sgn = 1 - 2 * d
    ii = lax.broadcasted_iota(jnp.int32, (q, q), 0)
    jj = lax.broadcasted_iota(jnp.int32, (q, q), 1)
    tri_b = jnp.where(((ii - jj) * sgn) >= 0, 1.0, 0.0).astype(BF16)
    jl = jj & (half - 1)
    tri_mix = [((ii - (half * jh + jl)) * sgn) >= 0 for jh in range(2)]
    lane = lax.broadcasted_iota(jnp.int32, (half, LANES), 1)
    low = lane < half

    def split_cat(v):
        return jnp.concatenate(_split3(v), axis=1)

    dtb = dt_ref[...]
    da = dtb * (-jnp.exp(alog_ref[...]) * LOG2E)
    cs3 = _dot(tri_b, split_cat(da))
    cs = cs3[:, 0:LANES] + cs3[:, LANES:2 * LANES] + cs3[:, 2 * LANES:]
    e3 = e_ref[...]
    dt_x = _dot(split_cat(dtb)[:, 0:2 * LANES], e3[0:2 * LANES, :])
    cs_x = _dot(split_cat(cs), e3)
    tot_x = jnp.where(d == 0, cs_x[q - 1:q, :], cs_x[0:1, :])
    cs_rows = [jnp.concatenate([cs[half * jh:half * (jh + 1), :]] * 2, axis=0).T for jh in range(2)]

    xs = x_ref[...]
    xd = xs * dt_x
    xw = (xd * jnp.exp2(tot_x - cs_x)).astype(BF16)
    dec_out = jnp.exp2(cs_x)
    dec_chunk = jnp.exp2(tot_x)

    for g in range(SSD_GROUPS):
        gs = slice(g * GROUP_W, (g + 1) * GROUP_W)
        bg = b_ref[:, g * SSD_STATE:(g + 1) * SSD_STATE]
        cg = c_ref[:, g * SSD_STATE:(g + 1) * SSD_STATE].astype(BF16)
        bgb = bg.astype(BF16)
        cb_mix = [_dot_nt(cg, jnp.concatenate([bgb[half * jh:half * (jh + 1), :]] * 2, axis=0)) for jh in range(2)]
        st_g = st_ref[:, gs]
        y_off = _dot(cg, st_g.astype(BF16))
        new_g = _dot(bg.T.astype(BF16), xw[:, gs])
        st_ref[:, gs] = st_g * dec_chunk[:, gs] + new_g
        for p in range(HEADS_PER_GROUP // 2):
            h = g * HEADS_PER_GROUP + 2 * p
            cols = slice(h * SSD_HEAD_DIM, (h + 2) * SSD_HEAD_DIM)
            cs_col = cs_x[:, cols]
            xd2 = xd[:, cols]
            ms, rs = [], []
            for jh in range(2):
                row = jnp.where(low[0:1, :], cs_rows[jh][h:h + 1, :], cs_rows[jh][h + 1:h + 2, :])
                decay = jnp.where(tri_mix[jh], jnp.exp2(cs_col - row), 0.0)
                ms.append((cb_mix[jh] * decay).astype(BF16))
                xj = xd2[half * jh:half * (jh + 1), :]
                rs += [jnp.where(low, xj, 0.0), jnp.where(low, 0.0, xj)]
            lhs = jnp.concatenate(ms, axis=1)
            rhs = jnp.concatenate(rs, axis=0).astype(BF16)
            loc = slice(2 * p * SSD_HEAD_DIM, (2 * p + 2) * SSD_HEAD_DIM)
            yt_ref[:, cols] = _dot(lhs, rhs) + y_off[:, loc] * dec_out[:, cols]

    r0 = pl.multiple_of((c + d * (nc - 1 - 2 * c)) * q, q)

    @pl.when(d == 0)
    def _():
        yacc_ref[pl.ds(r0, q), :] = yt_ref[...] + dx_ref[...] * xs

    @pl.when(d == 1)
    def _():
        yacc_ref[pl.ds(r0, q), :] += yt_ref[...]

    @pl.when(jnp.logical_and(d == 1, c == nc - 1))
    def _():
        y_ref[...] = yacc_ref[...].astype(y_ref.dtype)

    @pl.when(c == nc - 1)
    def _():
        if fill_slot is None:
            hl_ref[...] = st_ref[...].T
        else:
            for s in range(hl_ref.shape[0]):
                hl_ref[s] = st_ref[...].T if s == fill_slot else jnp.zeros(hl_ref.shape[1:], F32)


def _ssd_scan(xbc, dt, a_log_pad, expand, d_x, h0, slot, *, nseq, nc, chunk0, name, n_slots=1, states=None):
    q = SSD_CHUNK
    zero_init = h0 is None
    out_slot = slot if n_slots > 1 else 0

    def chunk(b, d, c):
        return chunk0 + b * nc + c + d * (nc - 1 - 2 * c)

    in_specs = [
        pl.BlockSpec((q, SSD_INNER), lambda b, d, c: (chunk(b, d, c), 0)),
        pl.BlockSpec((q, SSD_BC), lambda b, d, c: (chunk(b, d, c), SSD_INNER // SSD_BC)),
        pl.BlockSpec((q, SSD_BC), lambda b, d, c: (chunk(b, d, c), SSD_INNER // SSD_BC + 1)),
        pl.BlockSpec((q, LANES), lambda b, d, c: (chunk(b, d, c), d)),
        pl.BlockSpec((None, 1, LANES), lambda b, d, c: (d, 0, 0)),
        pl.BlockSpec((3 * LANES, SSD_INNER), lambda b, d, c: (0, 0)),
        pl.BlockSpec((1, SSD_INNER), lambda b, d, c: (0, 0)),
    ]
    args = [xbc, xbc, xbc, dt, a_log_pad, expand, d_x]
    if not zero_init:
        in_specs.append(pl.BlockSpec((None, None, None, SSD_INNER, SSD_STATE),
                                     lambda b, d, c: (b, slot, d, 0, 0)))
        args.append(h0)
    aliases = {}
    if states is not None:
        in_specs.append(pl.BlockSpec(memory_space=pl.ANY))
        args.append(states)
        aliases = {len(args) - 1: 1}
    fill_all = states is None and n_slots > 1
    if fill_all:
        state_spec = pl.BlockSpec((None, n_slots, None, SSD_INNER, SSD_STATE), lambda b, d, c: (b, 0, d, 0, 0))
    else:
        state_spec = pl.BlockSpec((None, None, None, SSD_INNER, SSD_STATE),
                                  lambda b, d, c: (b, out_slot, d, 0, 0))
    return pl.pallas_call(
        functools.partial(_scan_kernel, nc=nc, zero_init=zero_init, n_unused=len(aliases),
                          fill_slot=out_slot if fill_all else None),
        grid=(nseq, 2, nc),
        in_specs=in_specs,
        out_specs=[
            pl.BlockSpec((nc * q, SSD_INNER), lambda b, d, c: (b, 0)),
            state_spec,
        ],
        out_shape=[
            jax.ShapeDtypeStruct((nseq * nc * q, SSD_INNER), BF16),
            jax.ShapeDtypeStruct((nseq, n_slots, 2, SSD_INNER, SSD_STATE), F32),
        ],
        scratch_shapes=[pltpu.VMEM((SSD_STATE, SSD_INNER), F32), pltpu.VMEM((q, SSD_INNER), F32),
                        pltpu.VMEM((nc * q, SSD_INNER), F32)],
        input_output_aliases=aliases,
        compiler_params=pltpu.CompilerParams(dimension_semantics=("parallel", "arbitrary", "arbitrary"),
                                             vmem_limit_bytes=BIG_VMEM_LIMIT),
        name=name,
    )(*args)


def _ssd_out_kernel(yc_ref, yl_ref, z_ref, ng_ref, w_ref, x_ref, gate_ref, o_ref, *, n_ctx_blocks):
    y = jnp.where(pl.program_id(0) < n_ctx_blocks, yc_ref[...], yl_ref[...]).astype(F32)
    y = y * _silu(z_ref[...].astype(F32))
    gw = SSD_INNER // SSD_GROUPS
    parts = []
    for g in range(SSD_GROUPS):
        yg = y[:, g * gw:(g + 1) * gw]
        parts.append(yg * lax.rsqrt(jnp.mean(yg * yg, axis=-1, keepdims=True) + EPS))
    yn = (jnp.concatenate(parts, axis=1) * ng_ref[...]).astype(BF16)
    o_ref[...] = x_ref[...] + gate_ref[...] * _dot(yn, w_ref[...])


def _ssd_out(y_ctx, y_lat, z, norm_g, out_w, x, mod3, layer):
    tm = 512
    return pl.pallas_call(
        functools.partial(_ssd_out_kernel, n_ctx_blocks=N_CTX // tm),
        grid=(N_TOK // tm,),
        in_specs=_two_source_specs((tm, SSD_INNER), tm) + [
            pl.BlockSpec((tm, SSD_INNER), lambda i: (i, 0)),
            pl.BlockSpec((1, SSD_INNER), lambda i: (0, 0)),
            pl.BlockSpec((SSD_INNER, D_MODEL), lambda i: (0, 0)),
            pl.BlockSpec((tm, D_MODEL), lambda i: (i, 0)),
            _mod_spec(layer, 2, tm, 1),
        ],
        out_specs=pl.BlockSpec((tm, D_MODEL), lambda i: (i, 0)),
        out_shape=jax.ShapeDtypeStruct((N_TOK, D_MODEL), F32),
        compiler_params=_params("parallel"),
        name="ssd_out",
    )(y_ctx, y_lat, z, norm_g.reshape(1, SSD_INNER), out_w, x, mod3)


def _diff_lambda(lam_ref, lam_init):
    lv = lam_ref[...]
    return (jnp.exp(jnp.sum(lv[0:1] * lv[1:2], axis=-1, keepdims=True))
            - jnp.exp(jnp.sum(lv[2:3] * lv[3:4], axis=-1, keepdims=True)) + lam_init)


def _attend_head(q, kb, vb, lam, g, lam_init):
    lane = lax.broadcasted_iota(jnp.int32, q.shape, 1)
    zero = jnp.zeros_like(q)
    probs = []
    for qm in (jnp.where(lane < DA_HEAD_DIM, q, zero), jnp.where(lane >= DA_HEAD_DIM, q, zero)):
        s = _dot_nt(qm, kb)
        p = jnp.exp2(s - jnp.max(s, axis=-1, keepdims=True))
        probs.append((p, jnp.sum(p, axis=-1, keepdims=True)))
    (p1, l1), (p2, l2) = probs
    w = (p1 - p2 * (lam * l1 / l2)).astype(BF16)
    o = _dot(w, vb) * (1.0 / l1)
    o = o * lax.rsqrt(jnp.mean(o * o, axis=-1, keepdims=True) + EPS)
    return o * g * (1.0 - lam_init)


def _attn_ctx_kernel(q_ref, k_ref, v_ref, lam_ref, g_ref, o_ref, *, lam_init):
    lam = _diff_lambda(lam_ref, lam_init)
    g = g_ref[...]
    for h in range(DA_HEADS):
        cols = slice(h * HEAD_W, (h + 1) * HEAD_W)
        o = _attend_head(q_ref[:, cols], k_ref[:, cols], v_ref[:, cols], lam, g, lam_init)
        o_ref[:, cols] = o.astype(o_ref.dtype)


def _attn_lat_kernel(q_ref, k_ref, v_ref, kc_ref, vc_ref, lam_ref, g_ref, o_ref, kb_ref, vb_ref, *, lam_init):
    @pl.when(pl.program_id(2) == 0)
    def _():
        kb_ref[0:DEC_SEQ, :] = k_ref[...]
        kb_ref[DEC_SEQ:, :] = kc_ref[...].astype(BF16)
        vb_ref[0:DEC_SEQ, :] = v_ref[...]
        vb_ref[DEC_SEQ:, :] = vc_ref[...].astype(BF16)

    o = _attend_head(q_ref[...], kb_ref[...], vb_ref[...], _diff_lambda(lam_ref, lam_init), g_ref[...], lam_init)
    o_ref[...] = o.astype(o_ref.dtype)


def _attention(q, k, v, cache_k4, cache_v4, slot, lam_vecs, subln_g, lam_init):
    common = [pl.BlockSpec((4, DA_HEAD_DIM), lambda *_: (0, 0)),
              pl.BlockSpec((1, HEAD_W), lambda *_: (0, 0))]
    g2 = subln_g.reshape(1, HEAD_W)

    seq_block = pl.BlockSpec((SEQ, D_MODEL), lambda b: (b, 0))
    ctx = pl.pallas_call(
        functools.partial(_attn_ctx_kernel, lam_init=lam_init),
        grid=(BATCH,),
        in_specs=[seq_block, seq_block, seq_block] + common,
        out_specs=seq_block,
        out_shape=jax.ShapeDtypeStruct((N_CTX, D_MODEL), BF16),
        compiler_params=_params("parallel"),
        name="attn_ctx",
    )(q, k, v, lam_vecs, g2)

    tq = 256
    nq = DEC_SEQ // tq
    q0 = N_CTX // tq
    s0 = N_CTX // DEC_SEQ
    keys = DEC_SEQ + PAST_LEN
    lat = pl.pallas_call(
        functools.partial(_attn_lat_kernel, lam_init=lam_init),
        grid=(DEC_BATCH, DA_HEADS, nq),
        in_specs=[
            pl.BlockSpec((tq, HEAD_W), lambda b, h, i: (q0 + b * nq + i, h)),
            pl.BlockSpec((DEC_SEQ, HEAD_W), lambda b, h, i: (s0 + b, h)),
            pl.BlockSpec((DEC_SEQ, HEAD_W), lambda b, h, i: (s0 + b, h)),
            pl.BlockSpec((None, None, PAST_LEN, HEAD_W), lambda b, h, i: (b, slot, 0, h)),
            pl.BlockSpec((None, None, PAST_LEN, HEAD_W), lambda b, h, i: (b, slot, 0, h)),
        ] + common,
        out_specs=pl.BlockSpec((tq, HEAD_W), lambda b, h, i: (b * nq + i, h)),
        out_shape=jax.ShapeDtypeStruct((N_LAT, D_MODEL), BF16),
        scratch_shapes=[pltpu.VMEM((keys, HEAD_W), BF16), pltpu.VMEM((keys, HEAD_W), BF16)],
        compiler_params=_params("parallel", "parallel", "arbitrary"),
        name="attn_lat",
    )(q, k, v, cache_k4, cache_v4, lam_vecs, g2)
    return ctx, lat


def _rope_token_tables():
    rows = DEC_SEQ // GRID_W
    row = jnp.repeat(jnp.arange(rows, dtype=F32), GRID_W)
    col = jnp.tile(jnp.arange(GRID_W, dtype=F32), rows)
    inv = 1.0 / (ROPE_BASE ** (jnp.arange(0, ROPE_AXIS_DIM, 2, dtype=F32) / ROPE_AXIS_DIM))
    ang_r = row[:, None] * inv
    ang_c = col[:, None] * inv
    cr, sr, cc, sc = jnp.cos(ang_r), jnp.sin(ang_r), jnp.cos(ang_c), jnp.sin(ang_c)
    cos64 = jnp.concatenate([cr, cr, cc, cc], axis=1)
    sin64 = jnp.concatenate([-sr, sr, -sc, sc], axis=1)
    cos_lat = jnp.tile(cos64, (DEC_BATCH, 2))
    sin_lat = jnp.tile(sin64, (DEC_BATCH, 2))
    cos = jnp.concatenate([jnp.ones((N_CTX, HEAD_W), F32), cos_lat], axis=0)
    sin = jnp.concatenate([jnp.zeros((N_CTX, HEAD_W), F32), sin_lat], axis=0)
    return cos, sin


def _lambda_init(layer):
    return 0.8 - 0.6 * math.exp(-0.3 * layer)


def kernel(x_prompt, x_sample, state_ssd, cache_k, cache_v, c, c_ctx, mod_w, mod_b, norm_mix_g, norm_ffn_g, ssd_in_w, ssd_conv_w, ssd_conv_b, ssd_dt_bias, ssd_a_log, ssd_d, ssd_norm_g, ssd_out_w, att_qkv_w, att_lambda, att_subln_g, att_out_w, ffn_up_w, ffn_conv_w, ffn_conv_b, ffn_down_w, final_norm_g):
    n_ssd = ssd_in_w.shape[0]
    n_att = att_qkv_w.shape[0]
    x = None

    cpad = jnp.concatenate([c_ctx[None], c, jnp.zeros((MOD_ROWS - 1 - DEC_BATCH, D_MODEL), F32)], axis=0)
    mod = _modulation(cpad, mod_w, mod_b)
    mod3 = mod.reshape(DEPTH * MOD_ROWS * 6, 1, D_MODEL)

    cos_t, sin_t = _rope_token_tables()
    expand = (jnp.arange(LANES)[:, None] == (jnp.arange(SSD_INNER)[None, :] // SSD_HEAD_DIM)).astype(BF16)
    expand = jnp.concatenate([expand] * 3, axis=0)
    h0_lat = state_ssd.reshape(DEC_BATCH, n_ssd, 2, SSD_INNER, SSD_STATE)
    cache_k4 = cache_k.reshape(DEC_BATCH, n_att, PAST_LEN, D_MODEL)
    cache_v4 = cache_v.reshape(DEC_BATCH, n_att, PAST_LEN, D_MODEL)

    new_ssd = None
    new_kv = [jnp.zeros((BATCH, n_att, SEQ, D_MODEL), F32)] * 2
    for i in range(DEPTH):
        slot = i // 2
        if i % 2 == 0:
            in_w = ssd_in_w[slot]
            w_z = in_w[:, :SSD_INNER].astype(BF16)
            w_xbc = in_w[:, SSD_INNER:SSD_MAIN].astype(BF16)
            pad = jnp.zeros((D_MODEL, LANES - SSD_HEADS), F32)
            w_dt = jnp.concatenate([in_w[:, SSD_MAIN:SSD_MAIN + SSD_HEADS], pad,
                                    in_w[:, SSD_MAIN + SSD_HEADS:], pad], axis=1).astype(BF16)
            zpad = jnp.zeros((LANES - SSD_HEADS,), F32)
            dt_bias = jnp.concatenate([ssd_dt_bias[slot, 0], zpad, ssd_dt_bias[slot, 1], zpad]).reshape(1, 2 * LANES)
            a_log_pad = jnp.concatenate([ssd_a_log[slot], jnp.zeros((2, LANES - SSD_HEADS), F32)],
                                        axis=1).reshape(2, 1, LANES)
            d_x = jnp.repeat(ssd_d[slot], SSD_HEAD_DIM).reshape(1, SSD_INNER)
            if x is None:
                z, x = _norm_mod_matmul_join(x_prompt.reshape(N_CTX, D_MODEL), x_sample.reshape(N_LAT, D_MODEL),
                                             mod3, i, 0, 1, norm_mix_g[i], w_z, tn=1024, out_dtype=BF16,
                                             name="ssd_z_join")
            else:
                z = _norm_mod_matmul(x, mod3, i, 0, 1, norm_mix_g[i], w_z, tn=1024, out_dtype=BF16, name="ssd_z")
            xbc, dt = _ssd_xbc(x, mod3, i, norm_mix_g[i], w_xbc, ssd_conv_w[slot], ssd_conv_b[slot], w_dt, dt_bias)
            y_ctx, new_ssd = _ssd_scan(xbc, dt, a_log_pad, expand, d_x, None, slot,
                                       nseq=BATCH, nc=SEQ // SSD_CHUNK, chunk0=0, name="ssd_scan_ctx",
                                       n_slots=n_ssd, states=new_ssd)
            y_lat, _ = _ssd_scan(xbc, dt, a_log_pad, expand, d_x, h0_lat, slot,
                                 nseq=DEC_BATCH, nc=DEC_SEQ // SSD_CHUNK, chunk0=N_CTX // SSD_CHUNK,
                                 name="ssd_scan_lat")
            x = _ssd_out(y_ctx, y_lat, z, ssd_norm_g[slot], ssd_out_w[slot].astype(BF16), x, mod3, i)
        else:
            lam_init = _lambda_init(i)
            q, k, v, *new_kv = _qkv_proj(x, mod3, i, norm_mix_g[i], att_qkv_w[slot].astype(BF16),
                                         cos_t, sin_t, slot, n_att, new_kv)
            o_ctx, o_lat = _attention(q, k, v, cache_k4, cache_v4, slot, att_lambda[slot], att_subln_g[slot],
                                      lam_init)
            x = _matmul_residual(o_ctx, o_lat, att_out_w[slot].astype(BF16), x, mod3, i, 2, name="att_out")
        x = _ffn(x, mod3, i, norm_ffn_g[i], ffn_up_w[i].astype(BF16), ffn_conv_w[i], ffn_conv_b[i],
                 ffn_down_w[i].astype(BF16), final_g=final_norm_g if i == DEPTH - 1 else None)

    y_prompt = x[0].reshape(BATCH, SEQ, D_MODEL)
    y_sample = x[1].reshape(DEC_BATCH, DEC_SEQ, D_MODEL)
    return (y_prompt, y_sample,
            new_ssd.reshape(BATCH, n_ssd, 2, SSD_HEADS, SSD_HEAD_DIM, SSD_STATE),
            new_kv[0].reshape(BATCH, n_att, SEQ, DA_HEADS, 2, DA_HEAD_DIM),
            new_kv[1].reshape(BATCH, n_att, SEQ, DA_HEADS, HEAD_W))
```

```python
import functools
import math

import jax
import jax.numpy as jnp
from jax import lax
from jax.experimental import pallas as pl
from jax.experimental.pallas import tpu as pltpu

F32 = jnp.float32
BF16 = jnp.bfloat16

D_MODEL = 1024
BATCH = 16
SEQ = 256
DEPTH = 4
DEC_BATCH = 4
DEC_SEQ = 2048
PAST_LEN = 256
GRID_W = 64
EPS = 1e-6

SSD_INNER = 2048
SSD_HEAD_DIM = 64
SSD_HEADS = 32
SSD_GROUPS = 4
SSD_STATE = 128
SSD_CHUNK = 128
SSD_BC = SSD_GROUPS * SSD_STATE
SSD_CONV_CH = SSD_INNER + 2 * SSD_BC
SSD_MAIN = SSD_INNER + SSD_CONV_CH
HEADS_PER_GROUP = SSD_HEADS // SSD_GROUPS
GROUP_W = HEADS_PER_GROUP * SSD_HEAD_DIM

DA_HEAD_DIM = 64
DA_HEADS = 8
DA_SCALE = DA_HEAD_DIM ** -0.5
LOG2E = math.log2(math.e)
Q_PRESCALE = DA_SCALE * LOG2E
HEAD_W = 2 * DA_HEAD_DIM
ROPE_BASE = 10000.0
ROPE_AXIS_DIM = DA_HEAD_DIM // 2

FFN_HIDDEN = 2816

N_CTX = BATCH * SEQ
N_LAT = DEC_BATCH * DEC_SEQ
N_TOK = N_CTX + N_LAT
MOD_ROWS = 8
LANES = 128
SUBLANES = 8
VMEM_LIMIT = 48 * 1024 * 1024
BIG_VMEM_LIMIT = 56 * 1024 * 1024


def _params(*sem):
    return pltpu.CompilerParams(dimension_semantics=sem, vmem_limit_bytes=VMEM_LIMIT)


def _mod_row(i, tm):
    nctx = N_CTX // tm
    return jnp.where(i < nctx, 0, 1 + (i - nctx) // (DEC_SEQ // tm))


def _silu(x):
    return x * jax.nn.sigmoid(x)


def _softplus(x):
    return jnp.maximum(x, 0.0) + jnp.log1p(jnp.exp(-jnp.abs(x)))


def _split3(x):
    hi = x.astype(BF16)
    r1 = x - hi.astype(F32)
    mid = r1.astype(BF16)
    lo = (r1 - mid.astype(F32)).astype(BF16)
    return hi, mid, lo


def _dot(a, b):
    return jnp.dot(a, b, preferred_element_type=F32)


def _dot_nt(a, b):
    return lax.dot_general(a, b, (((1,), (1,)), ((), ())), preferred_element_type=F32)


def _norm_mod(x, g, sc, sh):
    r = x * lax.rsqrt(jnp.mean(x * x, axis=-1, keepdims=True) + EPS)
    return ((r * g) * (1.0 + sc) + sh).astype(BF16)


def _mod_kernel(c_ref, w_ref, b_ref, o_ref):
    s = _silu(c_ref[...]).astype(BF16)
    o_ref[...] = _dot(s, w_ref[...].astype(BF16)) + b_ref[...]


def _modulation(cpad, mod_w, mod_b):
    tn = 1024
    n = 6 * D_MODEL
    return pl.pallas_call(
        _mod_kernel,
        grid=(DEPTH, n // tn),
        in_specs=[
            pl.BlockSpec((MOD_ROWS, D_MODEL), lambda l, j: (0, 0)),
            pl.BlockSpec((None, D_MODEL, tn), lambda l, j: (l, 0, j)),
            pl.BlockSpec((None, 1, tn), lambda l, j: (l, 0, j)),
        ],
        out_specs=pl.BlockSpec((None, MOD_ROWS, tn), lambda l, j: (l, 0, j)),
        out_shape=jax.ShapeDtypeStruct((DEPTH, MOD_ROWS, n), F32),
        compiler_params=_params("parallel", "parallel"),
        name="modulation",
    )(cpad, mod_w, mod_b.reshape(DEPTH, 1, n))


def _mod_spec(layer, k, tm, ngrid):
    base = layer * MOD_ROWS * 6 + k
    if ngrid == 1:
        return pl.BlockSpec((None, 1, D_MODEL), lambda i: (base + 6 * _mod_row(i, tm), 0, 0))
    return pl.BlockSpec((None, 1, D_MODEL), lambda i, j: (base + 6 * _mod_row(i, tm), 0, 0))


def _nmm_kernel(x_ref, g_ref, sh_ref, sc_ref, w_ref, o_ref, h_ref):
    @pl.when(pl.program_id(1) == 0)
    def _():
        h_ref[...] = _norm_mod(x_ref[...], g_ref[...], sc_ref[...], sh_ref[...])

    o_ref[...] = _dot(h_ref[...], w_ref[...]).astype(o_ref.dtype)


def _nmm_in_specs(layer, kshift, kscale, tm, tn):
    return [
        pl.BlockSpec((tm, D_MODEL), lambda i, j: (i, 0)),
        pl.BlockSpec((1, D_MODEL), lambda i, j: (0, 0)),
        _mod_spec(layer, kshift, tm, 2),
        _mod_spec(layer, kscale, tm, 2),
        pl.BlockSpec((D_MODEL, tn), lambda i, j: (0, j)),
    ]


def _norm_mod_matmul(x, mod3, layer, kshift, kscale, g, w, *, tn, out_dtype, name):
    tm = 1024
    n = w.shape[1]
    return pl.pallas_call(
        _nmm_kernel,
        grid=(N_TOK // tm, n // tn),
        in_specs=_nmm_in_specs(layer, kshift, kscale, tm, tn),
        out_specs=pl.BlockSpec((tm, tn), lambda i, j: (i, j)),
        out_shape=jax.ShapeDtypeStruct((N_TOK, n), out_dtype),
        scratch_shapes=[pltpu.VMEM((tm, D_MODEL), BF16)],
        compiler_params=_params("parallel", "arbitrary"),
        name=name,
    )(x, g.reshape(1, D_MODEL), mod3, mod3, w)


def _nmm_join_kernel(xc_ref, xl_ref, g_ref, sh_ref, sc_ref, w_ref, o_ref, xo_ref, h_ref, *, n_ctx_blocks):
    @pl.when(pl.program_id(1) == 0)
    def _():
        x = jnp.where(pl.program_id(0) < n_ctx_blocks, xc_ref[...], xl_ref[...])
        xo_ref[...] = x
        h_ref[...] = _norm_mod(x, g_ref[...], sc_ref[...], sh_ref[...])

    o_ref[...] = _dot(h_ref[...], w_ref[...]).astype(o_ref.dtype)


def _norm_mod_matmul_join(x_ctx, x_lat, mod3, layer, kshift, kscale, g, w, *, tn, out_dtype, name):
    tm = 1024
    n = w.shape[1]
    nctx = N_CTX // tm
    specs = _nmm_in_specs(layer, kshift, kscale, tm, tn)
    return pl.pallas_call(
        functools.partial(_nmm_join_kernel, n_ctx_blocks=nctx),
        grid=(N_TOK // tm, n // tn),
        in_specs=[pl.BlockSpec((tm, D_MODEL), lambda i, j: (jnp.minimum(i, nctx - 1), 0)),
                  pl.BlockSpec((tm, D_MODEL), lambda i, j: (jnp.maximum(i - nctx, 0), 0))] + specs[1:],
        out_specs=[pl.BlockSpec((tm, tn), lambda i, j: (i, j)),
                   pl.BlockSpec((tm, D_MODEL), lambda i, j: (i, 0))],
        out_shape=[jax.ShapeDtypeStruct((N_TOK, n), out_dtype),
                   jax.ShapeDtypeStruct((N_TOK, D_MODEL), F32)],
        scratch_shapes=[pltpu.VMEM((tm, D_MODEL), BF16)],
        compiler_params=_params("parallel", "arbitrary"),
        name=name,
    )(x_ctx, x_lat, g.reshape(1, D_MODEL), mod3, mod3, w)


def _qkv_kernel(*refs, n_ctx_blocks, n_unused):
    x_ref, g_ref, sh_ref, sc_ref, w_ref, cos_ref, sin_ref = refs[:7]
    q_ref, k_ref, v_ref, kctx_ref, vctx_ref, h_ref = refs[7 + n_unused:]
    j = pl.program_id(1)
    is_ctx = pl.program_id(0) < n_ctx_blocks

    @pl.when(j == 0)
    def _():
        h_ref[...] = _norm_mod(x_ref[...], g_ref[...], sc_ref[...], sh_ref[...])

    u = _dot(h_ref[...], w_ref[...])

    def rope_into(o_refs, scale):
        cos = cos_ref[...] * scale
        sin = sin_ref[...] * scale
        lane = lax.broadcasted_iota(jnp.int32, cos.shape, 1)
        half = ROPE_AXIS_DIM // 2
        first = (lane % ROPE_AXIS_DIM) < half
        for k in range(u.shape[1] // HEAD_W):
            cols = slice(k * HEAD_W, (k + 1) * HEAD_W)
            uk = u[:, cols]
            partner = jnp.where(first, pltpu.roll(uk, HEAD_W - half, 1), pltpu.roll(uk, half, 1))
            r = uk * cos + partner * sin
            for o_ref in o_refs:
                if len(o_ref.shape) == 3:
                    o_ref[:, :, cols] = r.reshape(o_ref.shape[0], SEQ, HEAD_W)
                else:
                    o_ref[:, cols] = r.astype(o_ref.dtype)

    @pl.when(j == 0)
    def _():
        rope_into([q_ref], Q_PRESCALE)

    @pl.when(jnp.logical_and(j == 1, is_ctx))
    def _():
        rope_into([k_ref, kctx_ref], 1.0)

    @pl.when(jnp.logical_and(j == 1, jnp.logical_not(is_ctx)))
    def _():
        rope_into([k_ref], 1.0)

    @pl.when(j == 2)
    def _():
        v_ref[...] = u.astype(v_ref.dtype)

    @pl.when(jnp.logical_and(j == 2, is_ctx))
    def _():
        vctx_ref[...] = u.reshape(vctx_ref.shape)


def _qkv_proj(x, mod3, layer, g, w, cos_t, sin_t, slot, n_slots, caches):
    tm = 1024
    nctx = N_CTX // tm
    nseq = tm // SEQ
    out = pl.BlockSpec((tm, D_MODEL), lambda i, j: (i, 0))
    ctx_out = pl.BlockSpec((nseq, None, SEQ, D_MODEL), lambda i, j: (jnp.minimum(i, nctx - 1), slot, 0, 0))
    ctx_shape = jax.ShapeDtypeStruct((BATCH, n_slots, SEQ, D_MODEL), F32)
    in_specs = _nmm_in_specs(layer, 0, 1, tm, D_MODEL) + [pl.BlockSpec((tm, HEAD_W), lambda i, j: (i, 0))] * 2
    args = [x, g.reshape(1, D_MODEL), mod3, mod3, w, cos_t, sin_t]
    aliases = {}
    if caches is not None:
        in_specs += [pl.BlockSpec(memory_space=pl.ANY)] * 2
        aliases = {len(args): 3, len(args) + 1: 4}
        args += list(caches)
    return pl.pallas_call(
        functools.partial(_qkv_kernel, n_ctx_blocks=nctx, n_unused=len(aliases)),
        grid=(N_TOK // tm, 3),
        in_specs=in_specs,
        out_specs=[out, out, out, ctx_out, ctx_out],
        out_shape=[jax.ShapeDtypeStruct((N_TOK, D_MODEL), BF16),
                   jax.ShapeDtypeStruct((N_TOK, D_MODEL), BF16),
                   jax.ShapeDtypeStruct((N_TOK, D_MODEL), BF16),
                   ctx_shape, ctx_shape],
        scratch_shapes=[pltpu.VMEM((tm, D_MODEL), BF16)],
        input_output_aliases=aliases,
        compiler_params=pltpu.CompilerParams(dimension_semantics=("arbitrary", "arbitrary"),
                                             vmem_limit_bytes=BIG_VMEM_LIMIT),
        name="att_qkv",
    )(*args)


PAD = SUBLANES


def _block_seqlen(i, tm):
    return jnp.where(i < N_CTX // tm, SEQ, DEC_SEQ)


def _project_with_halo(h, hh, w, u_refs, tm):
    u_ref, up_ref, un_ref = u_refs
    u = _dot(h, w)
    uh = _dot(hh, w)
    u_ref[PAD:PAD + tm, :] = u
    up_ref[PAD + 1:PAD + 1 + tm, :] = u
    up_ref[PAD:PAD + 1, :] = uh[SUBLANES - 1:SUBLANES, :]
    un_ref[PAD - 1:PAD - 1 + tm, :] = u
    un_ref[PAD + tm - 1:PAD + tm, :] = uh[SUBLANES:SUBLANES + 1, :]


def _conv_chunks(u_refs, cws, cbs, row0, seqmask, tm, rows, emit):
    ridx = lax.broadcasted_iota(jnp.int32, (rows, 1), 0)
    for r0 in range(0, tm, rows):
        outs = []
        for (u_ref, up_ref, un_ref), cw, cb in zip(u_refs, cws, cbs):
            prev = up_ref[pl.ds(PAD + r0, rows), :]
            cur = u_ref[pl.ds(PAD + r0, rows), :]
            nxt = un_ref[pl.ds(PAD + r0, rows), :]
            if r0 % SEQ == 0:
                starts = ((row0 + r0) & seqmask) == 0
                prev = jnp.where(jnp.logical_and(ridx == 0, starts), 0.0, prev)
            if (r0 + rows) % SEQ == 0:
                ends = ((row0 + r0 + rows) & seqmask) == 0
                nxt = jnp.where(jnp.logical_and(ridx == rows - 1, ends), 0.0, nxt)
            outs.append(prev * cw[0:1, :] + cur * cw[1:2, :] + nxt * cw[2:3, :] + cb)
        emit(r0, outs)


def _halo_x_specs(tm):
    per = tm // SUBLANES
    last = N_TOK // SUBLANES - 1
    return [
        pl.BlockSpec((tm, D_MODEL), lambda i, j: (i, 0)),
        pl.BlockSpec((SUBLANES, D_MODEL), lambda i, j: (jnp.maximum(i * per - 1, 0), 0)),
        pl.BlockSpec((SUBLANES, D_MODEL), lambda i, j: (jnp.minimum((i + 1) * per, last), 0)),
    ]


def _xbc_kernel(x_ref, xp_ref, xn_ref, g_ref, sh_ref, sc_ref, w_ref, cw_ref, cb_ref, wdt_ref, bdt_ref,
                o_ref, dt_ref, h_ref, hh_ref, u_ref, up_ref, un_ref, *, tm, rows):
    i = pl.program_id(0)

    @pl.when(pl.program_id(1) == 0)
    def _():
        g, sc, sh = g_ref[...], sc_ref[...], sh_ref[...]
        h_ref[...] = _norm_mod(x_ref[...], g, sc, sh)
        hh_ref[...] = _norm_mod(jnp.concatenate([xp_ref[...], xn_ref[...]], axis=0), g, sc, sh)
        dt_ref[...] = _softplus(_dot(h_ref[...], wdt_ref[...]) + bdt_ref[...])

    u3 = (u_ref, up_ref, un_ref)
    _project_with_halo(h_ref[...], hh_ref[...], w_ref[...], u3, tm)

    def emit(r0, outs):
        o_ref[pl.ds(r0, rows), :] = _silu(outs[0]).astype(o_ref.dtype)

    _conv_chunks([u3], [cw_ref[...]], [cb_ref[...]], i * tm, _block_seqlen(i, tm) - 1, tm, rows, emit)


def _ssd_xbc(x, mod3, layer, g, w_xbc, conv_w, conv_b, w_dt, dt_bias):
    tm, tc, rows = 1024, 512, 32
    ndt = w_dt.shape[1]
    return pl.pallas_call(
        functools.partial(_xbc_kernel, tm=tm, rows=rows),
        grid=(N_TOK // tm, SSD_CONV_CH // tc),
        in_specs=_halo_x_specs(tm) + [
            pl.BlockSpec((1, D_MODEL), lambda i, j: (0, 0)),
            _mod_spec(layer, 0, tm, 2),
            _mod_spec(layer, 1, tm, 2),
            pl.BlockSpec((D_MODEL, tc), lambda i, j: (0, j)),
            pl.BlockSpec((3, tc), lambda i, j: (0, j)),
            pl.BlockSpec((1, tc), lambda i, j: (0, j)),
            pl.BlockSpec((D_MODEL, ndt), lambda i, j: (0, 0)),
            pl.BlockSpec((1, ndt), lambda i, j: (0, 0)),
        ],
        out_specs=[
            pl.BlockSpec((tm, tc), lambda i, j: (i, j)),
            pl.BlockSpec((tm, ndt), lambda i, j: (i, 0)),
        ],
        out_shape=[
            jax.ShapeDtypeStruct((N_TOK, SSD_CONV_CH), BF16),
            jax.ShapeDtypeStruct((N_TOK, ndt), F32),
        ],
        scratch_shapes=[pltpu.VMEM((tm, D_MODEL), BF16), pltpu.VMEM((2 * SUBLANES, D_MODEL), BF16),
                        *[pltpu.VMEM((tm + 2 * SUBLANES, tc), F32)] * 3],
        compiler_params=_params("parallel", "arbitrary"),
        name="ssd_xbc",
    )(x, x, x, g.reshape(1, D_MODEL), mod3, mod3, w_xbc, conv_w, conv_b.reshape(1, SSD_CONV_CH), w_dt, dt_bias)


def _ffn_kernel(*refs, tm, th, rows, n_ctx_blocks):
    (x_ref, xp_ref, xn_ref, g_ref, sh_ref, sc_ref, gate_ref, wa_ref, wv_ref,
     cwa_ref, cba_ref, cwv_ref, cbv_ref, wd_ref) = refs[:14]
    h_ref, hh_ref, act_ref = refs[-9:-6]
    ua3, uv3 = refs[-6:-3], refs[-3:]
    final = len(refs) == 14 + 3 + 9
    i = pl.program_id(0)
    j = pl.program_id(1)

    @pl.when(j == 0)
    def _():
        g, sc, sh = g_ref[...], sc_ref[...], sh_ref[...]
        h_ref[...] = _norm_mod(x_ref[...], g, sc, sh)
        hh_ref[...] = _norm_mod(jnp.concatenate([xp_ref[...], xn_ref[...]], axis=0), g, sc, sh)

    h = h_ref[...]
    hh = hh_ref[...]
    _project_with_halo(h, hh, wa_ref[...], ua3, tm)
    _project_with_halo(h, hh, wv_ref[...], uv3, tm)

    col0 = pl.multiple_of(j * th, th)

    def emit(r0, outs):
        act_ref[pl.ds(r0, rows), pl.ds(col0, th)] = (_silu(outs[0]) * outs[1]).astype(BF16)

    _conv_chunks([ua3, uv3], [cwa_ref[...], cwv_ref[...]], [cba_ref[...], cbv_ref[...]],
                 i * tm, _block_seqlen(i, tm) - 1, tm, rows, emit)

    @pl.when(j == pl.num_programs(1) - 1)
    def _():
        xo = x_ref[...] + gate_ref[...] * _dot(act_ref[...], wd_ref[...])
        if not final:
            refs[14][...] = xo
        else:
            fg_ref, yc_ref, yl_ref = refs[14:17]
            y = xo * lax.rsqrt(jnp.mean(xo * xo, axis=-1, keepdims=True) + EPS) * fg_ref[...]

            @pl.when(i < n_ctx_blocks)
            def _():
                yc_ref[...] = y

            @pl.when(i >= n_ctx_blocks)
            def _():
                yl_ref[...] = y


def _ffn(x, mod3, layer, g, up_w, conv_w, conv_b, down_w, final_g=None):
    tm, th, rows = 1024, 256, 64
    nb = FFN_HIDDEN // th
    nctx = N_CTX // tm
    cb = conv_b.reshape(1, 2 * FFN_HIDDEN)
    in_specs = _halo_x_specs(tm) + [
        pl.BlockSpec((1, D_MODEL), lambda i, j: (0, 0)),
        _mod_spec(layer, 3, tm, 2),
        _mod_spec(layer, 4, tm, 2),
        _mod_spec(layer, 5, tm, 2),
        pl.BlockSpec((D_MODEL, th), lambda i, j: (0, j)),
        pl.BlockSpec((D_MODEL, th), lambda i, j: (0, j + nb)),
        pl.BlockSpec((3, th), lambda i, j: (0, j)),
        pl.BlockSpec((1, th), lambda i, j: (0, j)),
        pl.BlockSpec((3, th), lambda i, j: (0, j + nb)),
        pl.BlockSpec((1, th), lambda i, j: (0, j + nb)),
        pl.BlockSpec((FFN_HIDDEN, D_MODEL), lambda i, j: (0, 0)),
    ]
    args = [x, x, x, g.reshape(1, D_MODEL), mod3, mod3, mod3, up_w, up_w, conv_w, cb, conv_w, cb, down_w]
    if final_g is None:
        out_specs = pl.BlockSpec((tm, D_MODEL), lambda i, j: (i, 0))
        out_shape = jax.ShapeDtypeStruct((N_TOK, D_MODEL), F32)
        params = _params("parallel", "arbitrary")
    else:
        in_specs.append(pl.BlockSpec((1, D_MODEL), lambda i, j: (0, 0)))
        args.append(final_g.reshape(1, D_MODEL))
        out_specs = [pl.BlockSpec((tm, D_MODEL), lambda i, j: (jnp.minimum(i, nctx - 1), 0)),
                     pl.BlockSpec((tm, D_MODEL), lambda i, j: (jnp.maximum(i - nctx, 0), 0))]
        out_shape = [jax.ShapeDtypeStruct((N_CTX, D_MODEL), F32), jax.ShapeDtypeStruct((N_LAT, D_MODEL), F32)]
        params = pltpu.CompilerParams(dimension_semantics=("arbitrary", "arbitrary"),
                                      vmem_limit_bytes=BIG_VMEM_LIMIT)
    return pl.pallas_call(
        functools.partial(_ffn_kernel, tm=tm, th=th, rows=rows, n_ctx_blocks=nctx),
        grid=(N_TOK // tm, nb),
        in_specs=in_specs,
        out_specs=out_specs,
        out_shape=out_shape,
        scratch_shapes=[pltpu.VMEM((tm, D_MODEL), BF16), pltpu.VMEM((2 * SUBLANES, D_MODEL), BF16),
                        pltpu.VMEM((tm, FFN_HIDDEN), BF16),
                        *[pltpu.VMEM((tm + 2 * SUBLANES, th), F32)] * 6],
        compiler_params=params,
        name="ffn",
    )(*args)


def _mm_res_kernel(ac_ref, al_ref, w_ref, x_ref, gate_ref, o_ref, *, n_ctx_blocks):
    a = jnp.where(pl.program_id(0) < n_ctx_blocks, ac_ref[...], al_ref[...])
    o_ref[...] = x_ref[...] + gate_ref[...] * _dot(a, w_ref[...])


def _two_source_specs(block, tm):
    nctx = N_CTX // tm
    return [pl.BlockSpec(block, lambda i: (jnp.minimum(i, nctx - 1), 0)),
            pl.BlockSpec(block, lambda i: (jnp.maximum(i - nctx, 0), 0))]


def _matmul_residual(a_ctx, a_lat, w, x, mod3, layer, kgate, *, name):
    tm = 512
    k = w.shape[0]
    return pl.pallas_call(
        functools.partial(_mm_res_kernel, n_ctx_blocks=N_CTX // tm),
        grid=(N_TOK // tm,),
        in_specs=_two_source_specs((tm, k), tm) + [
            pl.BlockSpec((k, D_MODEL), lambda i: (0, 0)),
            pl.BlockSpec((tm, D_MODEL), lambda i: (i, 0)),
            _mod_spec(layer, kgate, tm, 1),
        ],
        out_specs=pl.BlockSpec((tm, D_MODEL), lambda i: (i, 0)),
        out_shape=jax.ShapeDtypeStruct((N_TOK, D_MODEL), F32),
        compiler_params=_params("parallel"),
        name=name,
    )(a_ctx, a_lat, w, x, mod3)


def _scan_kernel(*refs, nc, zero_init, n_unused, fill_slot):
    x_ref, b_ref, c_ref, dt_ref, alog_ref, e_ref, dx_ref = refs[:7]
    h0_ref = None if zero_init else refs[7]
    y_ref, hl_ref, st_ref, yt_ref, yacc_ref = refs[7 + (not zero_init) + n_unused:]
    d = pl.program_id(1)
    c = pl.program_id(2)
    q = SSD_CHUNK

    @pl.when(c == 0)
    def _():
        if zero_init:
            st_ref[...] = jnp.zeros_like(st_ref)
        else:
            st_ref[...] = h0_ref[...].T

    half = SSD_HEAD_DIM
    sgn = 1 - 2 * d
    ii = lax.broadcasted_iota(jnp.int32, (q, q), 0)
    jj = lax.broadcasted_iota(jnp.int32, (q, q), 1)
    tri_b = jnp.where(((ii - jj) * sgn) >= 0, 1.0, 0.0).astype(BF16)
    jl = jj & (half - 1)
    tri_mix = [((ii - (half * jh + jl)) * sgn) >= 0 for jh in range(2)]
    lane = lax.broadcasted_iota(jnp.int32, (half, LANES), 1)
    low = lane < half

    def split_cat(v):
        return jnp.concatenate(_split3(v), axis=1)

    dtb = dt_ref[...]
    da = dtb * (-jnp.exp(alog_ref[...]) * LOG2E)
    cs3 = _dot(tri_b, split_cat(da))
    cs = cs3[:, 0:LANES] + cs3[:, LANES:2 * LANES] + cs3[:, 2 * LANES:]
    e3 = e_ref[...]
    dt_x = _dot(split_cat(dtb)[:, 0:2 * LANES], e3[0:2 * LANES, :])
    cs_x = _dot(split_cat(cs), e3)
    tot_x = jnp.where(d == 0, cs_x[q - 1:q, :], cs_x[0:1, :])
    cs_rows = [jnp.concatenate([cs[half * jh:half * (jh + 1), :]] * 2, axis=0).T for jh in range(2)]

    xs = x_ref[...].astype(F32)
    xd = xs * dt_x
    xw = (xd * jnp.exp2(tot_x - cs_x)).astype(BF16)
    dec_out = jnp.exp2(cs_x)
    dec_chunk = jnp.exp2(tot_x)

    for g in range(SSD_GROUPS):
        gs = slice(g * GROUP_W, (g + 1) * GROUP_W)
        bg = b_ref[:, g * SSD_STATE:(g + 1) * SSD_STATE].astype(F32)
        cg = c_ref[:, g * SSD_STATE:(g + 1) * SSD_STATE].astype(BF16)
        bgb = bg.astype(BF16)
        cb_mix = [_dot_nt(cg, jnp.concatenate([bgb[half * jh:half * (jh + 1), :]] * 2, axis=0)) for jh in range(2)]
        st_g = st_ref[:, gs]
        y_off = _dot(cg, st_g.astype(BF16))
        new_g = _dot(bg.T.astype(BF16), xw[:, gs])
        st_ref[:, gs] = st_g * dec_chunk[:, gs] + new_g
        for p in range(HEADS_PER_GROUP // 2):
            h = g * HEADS_PER_GROUP + 2 * p
            cols = slice(h * SSD_HEAD_DIM, (h + 2) * SSD_HEAD_DIM)
            cs_col = cs_x[:, cols]
            xd2 = xd[:, cols]
            ms, rs = [], []
            for jh in range(2):
                row = jnp.where(low[0:1, :], cs_rows[jh][h:h + 1, :], cs_rows[jh][h + 1:h + 2, :])
                decay = jnp.where(tri_mix[jh], jnp.exp2(cs_col - row), 0.0)
                ms.append((cb_mix[jh] * decay).astype(BF16))
                xj = xd2[half * jh:half * (jh + 1), :]
                rs += [jnp.where(low, xj, 0.0), jnp.where(low, 0.0, xj)]
            lhs = jnp.concatenate(ms, axis=1)
            rhs = jnp.concatenate(rs, axis=0).astype(BF16)
            loc = slice(2 * p * SSD_HEAD_DIM, (2 * p + 2) * SSD_HEAD_DIM)
            yt_ref[:, cols] = _dot(lhs, rhs) + y_off[:, loc] * dec_out[:, cols]

    r0 = pl.multiple_of((c + d * (nc - 1 - 2 * c)) * q, q)

    @pl.when(d == 0)
    def _():
        yacc_ref[pl.ds(r0, q), :] = yt_ref[...] + dx_ref[...] * xs

    @pl.when(d == 1)
    def _():
        yacc_ref[pl.ds(r0, q), :] += yt_ref[...]

    @pl.when(jnp.logical_and(d == 1, c == nc - 1))
    def _():
        y_ref[...] = yacc_ref[...].astype(y_ref.dtype)

    @pl.when(c == nc - 1)
    def _():
        if fill_slot is None:
            hl_ref[...] = st_ref[...].T
        else:
            for s in range(hl_ref.shape[0]):
                hl_ref[s] = st_ref[...].T if s == fill_slot else jnp.zeros(hl_ref.shape[1:], F32)


def _ssd_scan(xbc, dt, a_log_pad, expand, d_x, h0, slot, *, nseq, nc, chunk0, name, n_slots=1, states=None):
    q = SSD_CHUNK
    zero_init = h0 is None
    out_slot = slot if n_slots > 1 else 0

    def chunk(b, d, c):
        return chunk0 + b * nc + c + d * (nc - 1 - 2 * c)

    in_specs = [
        pl.BlockSpec((q, SSD_INNER), lambda b, d, c: (chunk(b, d, c), 0)),
        pl.BlockSpec((q, SSD_BC), lambda b, d, c: (chunk(b, d, c), SSD_INNER // SSD_BC)),
        pl.BlockSpec((q, SSD_BC), lambda b, d, c: (chunk(b, d, c), SSD_INNER // SSD_BC + 1)),
        pl.BlockSpec((q, LANES), lambda b, d, c: (chunk(b, d, c), d)),
        pl.BlockSpec((None, 1, LANES), lambda b, d, c: (d, 0, 0)),
        pl.BlockSpec((3 * LANES, SSD_INNER), lambda b, d, c: (0, 0)),
        pl.BlockSpec((1, SSD_INNER), lambda b, d, c: (0, 0)),
    ]
    args = [xbc, xbc, xbc, dt, a_log_pad, expand, d_x]
    if not zero_init:
        in_specs.append(pl.BlockSpec((None, None, None, SSD_INNER, SSD_STATE),
                                     lambda b, d, c: (b, slot, d, 0, 0)))
        args.append(h0)
    aliases = {}
    if states is not None:
        in_specs.append(pl.BlockSpec(memory_space=pl.ANY))
        args.append(states)
        aliases = {len(args) - 1: 1}
    fill_all = states is None and n_slots > 1
    if fill_all:
        state_spec = pl.BlockSpec((None, n_slots, None, SSD_INNER, SSD_STATE), lambda b, d, c: (b, 0, d, 0, 0))
    else:
        state_spec = pl.BlockSpec((None, None, None, SSD_INNER, SSD_STATE),
                                  lambda b, d, c: (b, out_slot, d, 0, 0))
    return pl.pallas_call(
        functools.partial(_scan_kernel, nc=nc, zero_init=zero_init, n_unused=len(aliases),
                          fill_slot=out_slot if fill_all else None),
        grid=(nseq, 2, nc),
        in_specs=in_specs,
        out_specs=[
            pl.BlockSpec((nc * q, SSD_INNER), lambda b, d, c: (b, 0)),
            state_spec,
        ],
        out_shape=[
            jax.ShapeDtypeStruct((nseq * nc * q, SSD_INNER), BF16),
            jax.ShapeDtypeStruct((nseq, n_slots, 2, SSD_INNER, SSD_STATE), F32),
        ],
        scratch_shapes=[pltpu.VMEM((SSD_STATE, SSD_INNER), F32), pltpu.VMEM((q, SSD_INNER), F32),
                        pltpu.VMEM((nc * q, SSD_INNER), F32)],
        input_output_aliases=aliases,
        compiler_params=pltpu.CompilerParams(dimension_semantics=("parallel", "arbitrary", "arbitrary"),
                                             vmem_limit_bytes=BIG_VMEM_LIMIT),
        name=name,
    )(*args)


def _ssd_out_kernel(yc_ref, yl_ref, z_ref, ng_ref, w_ref, x_ref, gate_ref, o_ref, *, n_ctx_blocks):
    y = jnp.where(pl.program_id(0) < n_ctx_blocks, yc_ref[...], yl_ref[...]).astype(F32)
    y = y * _silu(z_ref[...].astype(F32))
    gw = SSD_INNER // SSD_GROUPS
    parts = []
    for g in range(SSD_GROUPS):
        yg = y[:, g * gw:(g + 1) * gw]
        parts.append(yg * lax.rsqrt(jnp.mean(yg * yg, axis=-1, keepdims=True) + EPS))
    yn = (jnp.concatenate(parts, axis=1) * ng_ref[...]).astype(BF16)
    o_ref[...] = x_ref[...] + gate_ref[...] * _dot(yn, w_ref[...])


def _ssd_out(y_ctx, y_lat, z, norm_g, out_w, x, mod3, layer):
    tm = 512
    return pl.pallas_call(
        functools.partial(_ssd_out_kernel, n_ctx_blocks=N_CTX // tm),
        grid=(N_TOK // tm,),
        in_specs=_two_source_specs((tm, SSD_INNER), tm) + [
            pl.BlockSpec((tm, SSD_INNER), lambda i: (i, 0)),
            pl.BlockSpec((1, SSD_INNER), lambda i: (0, 0)),
            pl.BlockSpec((SSD_INNER, D_MODEL), lambda i: (0, 0)),
            pl.BlockSpec((tm, D_MODEL), lambda i: (i, 0)),
            _mod_spec(layer, 2, tm, 1),
        ],
        out_specs=pl.BlockSpec((tm, D_MODEL), lambda i: (i, 0)),
        out_shape=jax.ShapeDtypeStruct((N_TOK, D_MODEL), F32),
        compiler_params=_params("parallel"),
        name="ssd_out",
    )(y_ctx, y_lat, z, norm_g.reshape(1, SSD_INNER), out_w, x, mod3)


def _diff_lambda(lam_ref, lam_init):
    lv = lam_ref[...]
    return (jnp.exp(jnp.sum(lv[0:1] * lv[1:2], axis=-1, keepdims=True))
            - jnp.exp(jnp.sum(lv[2:3] * lv[3:4], axis=-1, keepdims=True)) + lam_init)


def _attend_head(q, kb, vb, lam, g, lam_init):
    lane = lax.broadcasted_iota(jnp.int32, q.shape, 1)
    zero = jnp.zeros_like(q)
    probs = []
    for qm in (jnp.where(lane < DA_HEAD_DIM, q, zero), jnp.where(lane >= DA_HEAD_DIM, q, zero)):
        s = _dot_nt(qm, kb)
        p = jnp.exp2(s - jnp.max(s, axis=-1, keepdims=True))
        probs.append((p, jnp.sum(p, axis=-1, keepdims=True)))
    (p1, l1), (p2, l2) = probs
    w = (p1 - p2 * (lam * l1 / l2)).astype(BF16)
    o = _dot(w, vb) * (1.0 / l1)
    o = o * lax.rsqrt(jnp.mean(o * o, axis=-1, keepdims=True) + EPS)
    return o * g * (1.0 - lam_init)


def _attn_ctx_kernel(q_ref, k_ref, v_ref, lam_ref, g_ref, o_ref, *, lam_init):
    lam = _diff_lambda(lam_ref, lam_init)
    g = g_ref[...]
    for h in range(DA_HEADS):
        cols = slice(h * HEAD_W, (h + 1) * HEAD_W)
        o = _attend_head(q_ref[:, cols], k_ref[:, cols], v_ref[:, cols], lam, g, lam_init)
        o_ref[:, cols] = o.astype(o_ref.dtype)


def _attn_lat_kernel(q_ref, k_ref, v_ref, kc_ref, vc_ref, lam_ref, g_ref, o_ref, kb_ref, vb_ref, *, lam_init):
    @pl.when(pl.program_id(2) == 0)
    def _():
        kb_ref[0:DEC_SEQ, :] = k_ref[...]
        kb_ref[DEC_SEQ:, :] = kc_ref[...].astype(BF16)
        vb_ref[0:DEC_SEQ, :] = v_ref[...]
        vb_ref[DEC_SEQ:, :] = vc_ref[...].astype(BF16)

    o = _attend_head(q_ref[...], kb_ref[...], vb_ref[...], _diff_lambda(lam_ref, lam_init), g_ref[...], lam_init)
    o_ref[...] = o.astype(o_ref.dtype)


def _attention(q, k, v, cache_k4, cache_v4, slot, lam_vecs, subln_g, lam_init):
    common = [pl.BlockSpec((4, DA_HEAD_DIM), lambda *_: (0, 0)),
              pl.BlockSpec((1, HEAD_W), lambda *_: (0, 0))]
    g2 = subln_g.reshape(1, HEAD_W)

    seq_block = pl.BlockSpec((SEQ, D_MODEL), lambda b: (b, 0))
    ctx = pl.pallas_call(
        functools.partial(_attn_ctx_kernel, lam_init=lam_init),
        grid=(BATCH,),
        in_specs=[seq_block, seq_block, seq_block] + common,
        out_specs=seq_block,
        out_shape=jax.ShapeDtypeStruct((N_CTX, D_MODEL), BF16),
        compiler_params=_params("parallel"),
        name="attn_ctx",
    )(q, k, v, lam_vecs, g2)

    tq = 256
    nq = DEC_SEQ // tq
    q0 = N_CTX // tq
    s0 = N_CTX // DEC_SEQ
    keys = DEC_SEQ + PAST_LEN
    lat = pl.pallas_call(
        functools.partial(_attn_lat_kernel, lam_init=lam_init),
        grid=(DEC_BATCH, DA_HEADS, nq),
        in_specs=[
            pl.BlockSpec((tq, HEAD_W), lambda b, h, i: (q0 + b * nq + i, h)),
            pl.BlockSpec((DEC_SEQ, HEAD_W), lambda b, h, i: (s0 + b, h)),
            pl.BlockSpec((DEC_SEQ, HEAD_W), lambda b, h, i: (s0 + b, h)),
            pl.BlockSpec((None, None, PAST_LEN, HEAD_W), lambda b, h, i: (b, slot, 0, h)),
            pl.BlockSpec((None, None, PAST_LEN, HEAD_W), lambda b, h, i: (b, slot, 0, h)),
        ] + common,
        out_specs=pl.BlockSpec((tq, HEAD_W), lambda b, h, i: (b * nq + i, h)),
        out_shape=jax.ShapeDtypeStruct((N_LAT, D_MODEL), BF16),
        scratch_shapes=[pltpu.VMEM((keys, HEAD_W), BF16), pltpu.VMEM((keys, HEAD_W), BF16)],
        compiler_params=_params("parallel", "parallel", "arbitrary"),
        name="attn_lat",
    )(q, k, v, cache_k4, cache_v4, lam_vecs, g2)
    return ctx, lat


def _rope_token_tables():
    rows = DEC_SEQ // GRID_W
    row = jnp.repeat(jnp.arange(rows, dtype=F32), GRID_W)
    col = jnp.tile(jnp.arange(GRID_W, dtype=F32), rows)
    inv = 1.0 / (ROPE_BASE ** (jnp.arange(0, ROPE_AXIS_DIM, 2, dtype=F32) / ROPE_AXIS_DIM))
    ang_r = row[:, None] * inv
    ang_c = col[:, None] * inv
    cr, sr, cc, sc = jnp.cos(ang_r), jnp.sin(ang_r), jnp.cos(ang_c), jnp.sin(ang_c)
    cos64 = jnp.concatenate([cr, cr, cc, cc], axis=1)
    sin64 = jnp.concatenate([-sr, sr, -sc, sc], axis=1)
    cos_lat = jnp.tile(cos64, (DEC_BATCH, 2))
    sin_lat = jnp.tile(sin64, (DEC_BATCH, 2))
    cos = jnp.concatenate([jnp.ones((N_CTX, HEAD_W), F32), cos_lat], axis=0)
    sin = jnp.concatenate([jnp.zeros((N_CTX, HEAD_W), F32), sin_lat], axis=0)
    return cos, sin


def _lambda_init(layer):
    return 0.8 - 0.6 * math.exp(-0.3 * layer)


def kernel(x_prompt, x_sample, state_ssd, cache_k, cache_v, c, c_ctx, mod_w, mod_b, norm_mix_g, norm_ffn_g, ssd_in_w, ssd_conv_w, ssd_conv_b, ssd_dt_bias, ssd_a_log, ssd_d, ssd_norm_g, ssd_out_w, att_qkv_w, att_lambda, att_subln_g, att_out_w, ffn_up_w, ffn_conv_w, ffn_conv_b, ffn_down_w, final_norm_g):
    n_ssd = ssd_in_w.shape[0]
    n_att = att_qkv_w.shape[0]
    x = None

    cpad = jnp.concatenate([c_ctx[None], c, jnp.zeros((MOD_ROWS - 1 - DEC_BATCH, D_MODEL), F32)], axis=0)
    mod = _modulation(cpad, mod_w, mod_b)
    mod3 = mod.reshape(DEPTH * MOD_ROWS * 6, 1, D_MODEL)

    cos_t, sin_t = _rope_token_tables()
    expand = (jnp.arange(LANES)[:, None] == (jnp.arange(SSD_INNER)[None, :] // SSD_HEAD_DIM)).astype(BF16)
    expand = jnp.concatenate([expand] * 3, axis=0)
    h0_lat = state_ssd.reshape(DEC_BATCH, n_ssd, 2, SSD_INNER, SSD_STATE)
    cache_k4 = cache_k.reshape(DEC_BATCH, n_att, PAST_LEN, D_MODEL)
    cache_v4 = cache_v.reshape(DEC_BATCH, n_att, PAST_LEN, D_MODEL)

    new_ssd = None
    new_kv = [jnp.zeros((BATCH, n_att, SEQ, D_MODEL), F32)] * 2
    for i in range(DEPTH):
        slot = i // 2
        if i % 2 == 0:
            in_w = ssd_in_w[slot]
            w_z = in_w[:, :SSD_INNER].astype(BF16)
            w_xbc = in_w[:, SSD_INNER:SSD_MAIN].astype(BF16)
            pad = jnp.zeros((D_MODEL, LANES - SSD_HEADS), F32)
            w_dt = jnp.concatenate([in_w[:, SSD_MAIN:SSD_MAIN + SSD_HEADS], pad,
                                    in_w[:, SSD_MAIN + SSD_HEADS:], pad], axis=1).astype(BF16)
            zpad = jnp.zeros((LANES - SSD_HEADS,), F32)
            dt_bias = jnp.concatenate([ssd_dt_bias[slot, 0], zpad, ssd_dt_bias[slot, 1], zpad]).reshape(1, 2 * LANES)
            a_log_pad = jnp.concatenate([ssd_a_log[slot], jnp.zeros((2, LANES - SSD_HEADS), F32)],
                                        axis=1).reshape(2, 1, LANES)
            d_x = jnp.repeat(ssd_d[slot], SSD_HEAD_DIM).reshape(1, SSD_INNER)
            if x is None:
                z, x = _norm_mod_matmul_join(x_prompt.reshape(N_CTX, D_MODEL), x_sample.reshape(N_LAT, D_MODEL),
                                             mod3, i, 0, 1, norm_mix_g[i], w_z, tn=1024, out_dtype=BF16,
                                             name="ssd_z_join")
            else:
                z = _norm_mod_matmul(x, mod3, i, 0, 1, norm_mix_g[i], w_z, tn=1024, out_dtype=BF16, name="ssd_z")
            xbc, dt = _ssd_xbc(x, mod3, i, norm_mix_g[i], w_xbc, ssd_conv_w[slot], ssd_conv_b[slot], w_dt, dt_bias)
            y_ctx, new_ssd = _ssd_scan(xbc, dt, a_log_pad, expand, d_x, None, slot,
                                       nseq=BATCH, nc=SEQ // SSD_CHUNK, chunk0=0, name="ssd_scan_ctx",
                                       n_slots=n_ssd, states=new_ssd)
            y_lat, _ = _ssd_scan(xbc, dt, a_log_pad, expand, d_x, h0_lat, slot,
                                 nseq=DEC_BATCH, nc=DEC_SEQ // SSD_CHUNK, chunk0=N_CTX // SSD_CHUNK,
                                 name="ssd_scan_lat")
            x = _ssd_out(y_ctx, y_lat, z, ssd_norm_g[slot], ssd_out_w[slot].astype(BF16), x, mod3, i)
        else:
            lam_init = _lambda_init(i)
            q, k, v, *new_kv = _qkv_proj(x, mod3, i, norm_mix_g[i], att_qkv_w[slot].astype(BF16),
                                         cos_t, sin_t, slot, n_att, new_kv)
            o_ctx, o_lat = _attention(q, k, v, cache_k4, cache_v4, slot, att_lambda[slot], att_subln_g[slot],
                                      lam_init)
            x = _matmul_residual(o_ctx, o_lat, att_out_w[slot].astype(BF16), x, mod3, i, 2, name="att_out")
        x = _ffn(x, mod3, i, norm_ffn_g[i], ffn_up_w[i].astype(BF16), ffn_conv_w[i], ffn_conv_b[i],
                 ffn_down_w[i].astype(BF16), final_g=final_norm_g if i == DEPTH - 1 else None)

    y_prompt = x[0].reshape(BATCH, SEQ, D_MODEL)
    y_sample = x[1].reshape(DEC_BATCH, DEC_SEQ, D_MODEL)
    return (y_prompt, y_sample,
            new_ssd.reshape(BATCH, n_ssd, 2, SSD_HEADS, SSD_HEAD_DIM, SSD_STATE),
            new_kv[0].reshape(BATCH, n_att, SEQ, DA_HEADS, 2, DA_HEAD_DIM),
            new_kv[1].reshape(BATCH, n_att, SEQ, DA_HEADS, HEAD_W))
```
